```python
import math
import jax
import jax.numpy as jnp
from jax import lax
import numpy as np


D_MODEL = 2048
BATCH = 16
SEQ = 2048
DEPTH = 1
DEC_BATCH = 16
DEC_SEQ = 64
PAST_LEN = 2048

CHUNK = 64
LEFT_CHUNKS = 8
ATT_WINDOW = LEFT_CHUNKS * CHUNK
BAND = ATT_WINDOW + CHUNK
D_ATT = D_MODEL // 2
D_SSM = D_MODEL - D_ATT
HEAD_DIM = 64
N_ATT_HEADS = D_ATT // HEAD_DIM
REL_CLIP = 2 * CHUNK
SSM_GROUP = 16
N_SSM_GROUPS = D_SSM // SSM_GROUP
SSM_STATE = 64
PEER_HEADS = 8
PEER_TOPK = 16
N_KEYS = 128
N_EXPERTS = N_KEYS * N_KEYS
PEER_DKEY = 256
PEER_DHALF = PEER_DKEY // 2
PEER_BLOCK = 128
DN_ALPHA = (2.0 * DEPTH) ** 0.25
DN_BETA = (8.0 * DEPTH) ** -0.25
LN_EPS = 1e-5
RMS_EPS = 1e-6

kernel_name = 'hybrid_stream_encoder_chunkattn_s5_peer_step'


def _layer_norm(x, g, b):
    xf = x.astype(jnp.float32)
    mu = jnp.mean(xf, axis=-1, keepdims=True)
    var = jnp.mean(jnp.square(xf - mu), axis=-1, keepdims=True)
    return ((xf - mu) * lax.rsqrt(var + LN_EPS) * g + b).astype(x.dtype)


def _rms_norm(x, g):
    xf = x.astype(jnp.float32)
    return (xf * lax.rsqrt(jnp.mean(jnp.square(xf), axis=-1, keepdims=True) + RMS_EPS) * g).astype(x.dtype)


def _rel_bias(rel_bias, q_pos, k_pos):
    idx = jnp.clip(k_pos[None, :] - q_pos[:, None], -REL_CLIP, REL_CLIP) + REL_CLIP
    return rel_bias[:, idx]


def _band_attention(q, k, v, bias, valid):
    s = jnp.einsum('bqhd,bkhd->bhqk', q, k).astype(jnp.float32) * (HEAD_DIM ** -0.5) + bias.astype(jnp.float32)
    s = jnp.where(valid[None, None, None, :], s, -1e30)
    p = jax.nn.softmax(s, axis=-1).astype(v.dtype)
    return jnp.einsum('bhqk,bkhd->bqhd', p, v)


def _prompt_attention(q, k, v, rel_bias):
    B, S, H, Dh = q.shape
    n_chunks = S // CHUNK
    pad = ((0, 0), (ATT_WINDOW, 0), (0, 0), (0, 0))
    kp = jnp.pad(k, pad)
    vp = jnp.pad(v, pad)
    q_pos = jnp.arange(CHUNK)
    k_pos = jnp.arange(BAND) - ATT_WINDOW
    bias = _rel_bias(rel_bias, q_pos, k_pos)

    def one_chunk(c):
        start = c * CHUNK
        qc = lax.dynamic_slice_in_dim(q, start, CHUNK, axis=1)
        kc = lax.dynamic_slice_in_dim(kp, start, BAND, axis=1)
        vc = lax.dynamic_slice_in_dim(vp, start, BAND, axis=1)
        valid = (start + k_pos) >= 0
        return _band_attention(qc, kc, vc, bias, valid)

    out = lax.map(one_chunk, jnp.arange(n_chunks))
    return jnp.moveaxis(out, 0, 1).reshape(B, S, H, Dh)


def _sample_attention(q, k_new, v_new, k_hist, v_hist, rel_bias):
    W = k_hist.shape[1]
    Ls = q.shape[1]
    k = jnp.concatenate([k_hist.astype(k_new.dtype), k_new], axis=1)
    v = jnp.concatenate([v_hist.astype(v_new.dtype), v_new], axis=1)
    q_pos = jnp.arange(Ls)
    k_pos = jnp.arange(W + Ls) - W
    bias = _rel_bias(rel_bias, q_pos, k_pos)
    valid = jnp.ones((W + Ls,), dtype=bool)
    return _band_attention(q, k, v, bias, valid)


def _s5(u, h0, a_re, a_im, log_dt, b_re, b_im, c_re, c_im, d, glu_w, glu_b):
    f32 = jnp.float32
    lam = lax.complex(a_re.astype(f32), a_im.astype(f32))
    dt = jnp.exp(log_dt.astype(f32))[:, None]
    a_bar = jnp.exp(lam * dt)
    b = lax.complex(b_re.astype(f32), b_im.astype(f32))
    b_bar = ((a_bar - 1.0) / lam)[..., None] * b
    uf = u.astype(f32)
    bu = jnp.einsum('gph,blgh->blgp', b_bar, uf.astype(jnp.complex64))
    bu = bu.at[:, 0].add(a_bar * h0)
    a_seq = jnp.broadcast_to(a_bar, bu.shape)

    def combine(e1, e2):
        a1, b1 = e1
        a2, b2 = e2
        return a1 * a2, a2 * b1 + b2

    _, h = lax.associative_scan(combine, (a_seq, bu), axis=1)
    c = lax.complex(c_re.astype(f32), c_im.astype(f32))
    y = jnp.einsum('ghp,blgp->blgh', c, h).real + d.astype(f32) * uf
    y = jax.nn.gelu(y, approximate=False)
    gate = jax.nn.sigmoid(jnp.einsum('blgh,ghk->blgk', y, glu_w.astype(f32)) + glu_b.astype(f32))
    return y * gate, h[:, -1]


def _peer(x, wq, keys, u_tab, v_tab):
    T, D = x.shape
    q = jnp.einsum('td,de->te', x, wq).reshape(T, PEER_HEADS, 2, PEER_DHALF)
    s = jnp.einsum('thcd,hcnd->thcn', q, keys).astype(jnp.float32)
    v1, i1 = lax.top_k(s[:, :, 0], PEER_TOPK)
    v2, i2 = lax.top_k(s[:, :, 1], PEER_TOPK)
    cand = (v1[..., :, None] + v2[..., None, :]).reshape(T, PEER_HEADS, PEER_TOPK * PEER_TOPK)
    cidx = (i1[..., :, None] * N_KEYS + i2[..., None, :]).reshape(T, PEER_HEADS, PEER_TOPK * PEER_TOPK)
    top_s, pos = lax.top_k(cand, PEER_TOPK)
    idx = jnp.take_along_axis(cidx, pos, axis=-1).reshape(T, PEER_HEADS * PEER_TOPK)
    g = jax.nn.softmax(top_s, axis=-1).reshape(T, PEER_HEADS * PEER_TOPK)
    n_blocks = -(-T // PEER_BLOCK)
    pad = n_blocks * PEER_BLOCK - T
    xb = jnp.pad(x, ((0, pad), (0, 0))).reshape(n_blocks, PEER_BLOCK, D)
    ib = jnp.pad(idx, ((0, pad), (0, 0))).reshape(n_blocks, PEER_BLOCK, PEER_HEADS * PEER_TOPK)
    gb = jnp.pad(g, ((0, pad), (0, 0))).reshape(n_blocks, PEER_BLOCK, PEER_HEADS * PEER_TOPK)

    def block(args):
        xt, it, gt = args
        a = jnp.einsum('td,tkd->tk', xt, u_tab[it])
        hid = (jax.nn.gelu(a.astype(jnp.float32), approximate=False) * gt).astype(xt.dtype)
        return jnp.einsum('tk,tkd->td', hid, v_tab[it])

    y = lax.map(block, (xb, ib, gb)).reshape(n_blocks * PEER_BLOCK, D)
    return y[:T]


def _encoder_layer(x, k_hist, v_hist, h0, w_in, rel_bias, norm_attn_g, ssm_a_re, ssm_a_im, ssm_log_dt,
                   ssm_b_re, ssm_b_im, ssm_c_re, ssm_c_im, ssm_d, ssm_glu_w, ssm_glu_b, norm_ssm_g, w_out,
                   ln1_g, ln1_b, peer_wq, peer_keys, peer_u, peer_v, ln2_g, ln2_b):
    B, L, _ = x.shape
    proj = jnp.einsum('bld,de->ble', x, w_in)
    q, k, v, u = jnp.split(proj, [D_ATT, 2 * D_ATT, 3 * D_ATT], axis=-1)
    q = q.reshape(B, L, N_ATT_HEADS, HEAD_DIM)
    k = k.reshape(B, L, N_ATT_HEADS, HEAD_DIM)
    v = v.reshape(B, L, N_ATT_HEADS, HEAD_DIM)
    if k_hist is None:
        att = _prompt_attention(q, k, v, rel_bias)
    else:
        att = _sample_attention(q, k, v, k_hist, v_hist, rel_bias)
    att = _rms_norm(att.reshape(B, L, D_ATT), norm_attn_g)
    ssm, h_last = _s5(u.reshape(B, L, N_SSM_GROUPS, SSM_GROUP), h0, ssm_a_re, ssm_a_im, ssm_log_dt,
                      ssm_b_re, ssm_b_im, ssm_c_re, ssm_c_im, ssm_d, ssm_glu_w, ssm_glu_b)
    ssm = _rms_norm(ssm.reshape(B, L, D_SSM).astype(x.dtype), norm_ssm_g)
    mix = jnp.einsum('ble,ed->bld', jnp.concatenate([att, ssm], axis=-1), w_out)
    x1 = _layer_norm(DN_ALPHA * x + mix, ln1_g, ln1_b)
    ffn = _peer(x1.reshape(B * L, D_MODEL), peer_wq, peer_keys, peer_u, peer_v).reshape(B, L, D_MODEL)
    x2 = _layer_norm(DN_ALPHA * x1 + ffn, ln2_g, ln2_b)
    return x2, k, v, h_last


def setup_inputs(seed: int = 0) -> dict:
    key = jax.random.key(seed)
    ks = jax.random.split(key, 32)
    f32 = jnp.float32
    Ld = DEPTH
    w_att = min(ATT_WINDOW, PAST_LEN)

    def nrm(k, shape, scale):
        return jax.random.normal(k, shape, f32) * scale

    inputs = {}
    inputs['x_prompt'] = nrm(ks[0], (BATCH, SEQ, D_MODEL), 1.0)
    inputs['x_sample'] = nrm(ks[1], (DEC_BATCH, DEC_SEQ, D_MODEL), 1.0)
    inputs['cache_attn_k'] = nrm(ks[2], (Ld, DEC_BATCH, w_att, N_ATT_HEADS, HEAD_DIM), 1.0)
    inputs['cache_attn_v'] = nrm(ks[3], (Ld, DEC_BATCH, w_att, N_ATT_HEADS, HEAD_DIM), 1.0)
    inputs['state_ssm_re'] = nrm(ks[4], (Ld, DEC_BATCH, N_SSM_GROUPS, SSM_STATE), 0.1)
    inputs['state_ssm_im'] = nrm(ks[5], (Ld, DEC_BATCH, N_SSM_GROUPS, SSM_STATE), 0.1)
    inputs['w_in'] = nrm(ks[6], (Ld, D_MODEL, 3 * D_ATT + D_SSM), D_MODEL ** -0.5)
    inputs['rel_bias'] = nrm(ks[7], (Ld, N_ATT_HEADS, 2 * REL_CLIP + 1), 0.1)
    inputs['norm_attn_g'] = 1.0 + nrm(ks[8], (Ld, D_ATT), 0.01)
    inputs['ssm_a_re'] = -0.5 + nrm(ks[9], (Ld, N_SSM_GROUPS, SSM_STATE), 0.01)
    inputs['ssm_a_im'] = jnp.pi * jnp.arange(SSM_STATE, dtype=f32)[None, None, :] + nrm(ks[10], (Ld, N_SSM_GROUPS, SSM_STATE), 0.01)
    inputs['ssm_log_dt'] = jax.random.uniform(ks[11], (Ld, N_SSM_GROUPS), f32, minval=math.log(1e-3), maxval=math.log(1e-1))
    inputs['ssm_b_re'] = nrm(ks[12], (Ld, N_SSM_GROUPS, SSM_STATE, SSM_GROUP), (2.0 * SSM_GROUP) ** -0.5)
    inputs['ssm_b_im'] = nrm(ks[13], (Ld, N_SSM_GROUPS, SSM_STATE, SSM_GROUP), (2.0 * SSM_GROUP) ** -0.5)
    inputs['ssm_c_re'] = nrm(ks[14], (Ld, N_SSM_GROUPS, SSM_GROUP, SSM_STATE), SSM_STATE ** -0.5)
    inputs['ssm_c_im'] = nrm(ks[15], (Ld, N_SSM_GROUPS, SSM_GROUP, SSM_STATE), SSM_STATE ** -0.5)
    inputs['ssm_d'] = nrm(ks[16], (Ld, N_SSM_GROUPS, SSM_GROUP), 1.0)
    inputs['ssm_glu_w'] = nrm(ks[17], (Ld, N_SSM_GROUPS, SSM_GROUP, SSM_GROUP), SSM_GROUP ** -0.5)
    inputs['ssm_glu_b'] = nrm(ks[18], (Ld, N_SSM_GROUPS, SSM_GROUP), 0.01)
    inputs['norm_ssm_g'] = 1.0 + nrm(ks[19], (Ld, D_SSM), 0.01)
    inputs['w_out'] = nrm(ks[20], (Ld, D_ATT + D_SSM, D_MODEL), (D_ATT + D_SSM) ** -0.5 * DN_BETA)
    inputs['ln1_g'] = 1.0 + nrm(ks[21], (Ld, D_MODEL), 0.01)
    inputs['ln1_b'] = nrm(ks[22], (Ld, D_MODEL), 0.01)
    inputs['peer_wq'] = nrm(ks[23], (Ld, D_MODEL, PEER_HEADS * PEER_DKEY), D_MODEL ** -0.5)
    inputs['peer_keys'] = nrm(ks[24], (Ld, PEER_HEADS, 2, N_KEYS, PEER_DHALF), PEER_DHALF ** -0.5)
    inputs['peer_u'] = nrm(ks[25], (Ld, N_EXPERTS, D_MODEL), D_MODEL ** -0.5)
    inputs['peer_v'] = nrm(ks[26], (Ld, N_EXPERTS, D_MODEL), DN_BETA)
    inputs['ln2_g'] = 1.0 + nrm(ks[27], (Ld, D_MODEL), 0.01)
    inputs['ln2_b'] = nrm(ks[28], (Ld, D_MODEL), 0.01)
    return inputs


def reference(x_prompt, x_sample, cache_attn_k, cache_attn_v, state_ssm_re, state_ssm_im, w_in, rel_bias,
              norm_attn_g, ssm_a_re, ssm_a_im, ssm_log_dt, ssm_b_re, ssm_b_im, ssm_c_re, ssm_c_im, ssm_d,
              ssm_glu_w, ssm_glu_b, norm_ssm_g, w_out, ln1_g, ln1_b, peer_wq, peer_keys, peer_u, peer_v,
              ln2_g, ln2_b):
    f32 = jnp.float32
    yp = x_prompt
    ys = x_sample
    keep = min(ATT_WINDOW, x_prompt.shape[1])
    kp_rows, vp_rows, hp_re, hp_im = [], [], [], []
    ks_rows, vs_rows, hs_re, hs_im = [], [], [], []
    for l in range(DEPTH):
        layer_w = (w_in[l], rel_bias[l], norm_attn_g[l], ssm_a_re[l], ssm_a_im[l], ssm_log_dt[l],
                   ssm_b_re[l], ssm_b_im[l], ssm_c_re[l], ssm_c_im[l], ssm_d[l], ssm_glu_w[l], ssm_glu_b[l],
                   norm_ssm_g[l], w_out[l], ln1_g[l], ln1_b[l], peer_wq[l], peer_keys[l], peer_u[l], peer_v[l],
                   ln2_g[l], ln2_b[l])
        h0p = jnp.zeros((x_prompt.shape[0], N_SSM_GROUPS, SSM_STATE), jnp.complex64)
        yp, kp, vp, hp = _encoder_layer(yp, None, None, h0p, *layer_w)
        h0s = lax.complex(state_ssm_re[l].astype(f32), state_ssm_im[l].astype(f32))
        ys, kn, vn, hs = _encoder_layer(ys, cache_attn_k[l], cache_attn_v[l], h0s, *layer_w)
        kp_rows.append(kp[:, -keep:])
        vp_rows.append(vp[:, -keep:])
        hp_re.append(hp.real)
        hp_im.append(hp.imag)
        ks_rows.append(kn)
        vs_rows.append(vn)
        hs_re.append(hs.real)
        hs_im.append(hs.imag)
    return (yp, ys, jnp.stack(kp_rows), jnp.stack(vp_rows), jnp.stack(hp_re), jnp.stack(hp_im),
            jnp.stack(ks_rows), jnp.stack(vs_rows), jnp.stack(hs_re), jnp.stack(hs_im))
```

```python
import functools

import jax
import jax.numpy as jnp
from jax import lax
from jax.experimental import pallas as pl
from jax.experimental.pallas import tpu as pltpu

F32 = jnp.float32
BF16 = jnp.bfloat16

CHUNK = 64
LEFT_CHUNKS = 8
ATT_WINDOW = LEFT_CHUNKS * CHUNK
BAND = ATT_WINDOW + CHUNK
HEAD_DIM = 64
REL_CLIP = 2 * CHUNK
SSM_GROUP = 16
SSM_STATE = 64
PEER_HEADS = 8
PEER_TOPK = 16
N_KEYS = 128
PEER_DHALF = 128
LN_EPS = 1e-5
RMS_EPS = 1e-6
MASK_VALUE = -1e30

LANES = 128
SUBLANES = 8
GROUPS_PER_TILE = LANES // SSM_GROUP
STATE_TILE = GROUPS_PER_TILE * SSM_STATE
VMEM_LIMIT = 56 * 1024 * 1024

SORT16 = (
    (0, 13), (1, 12), (2, 15), (3, 14), (4, 8), (5, 6), (7, 11), (9, 10),
    (0, 5), (1, 7), (2, 9), (3, 4), (6, 13), (8, 14), (10, 15), (11, 12),
    (0, 1), (2, 3), (4, 5), (6, 8), (7, 9), (10, 11), (12, 13), (14, 15),
    (0, 2), (1, 3), (4, 10), (5, 11), (6, 7), (8, 9), (12, 14), (13, 15),
    (1, 2), (3, 12), (4, 6), (5, 7), (8, 10), (9, 11), (13, 14),
    (1, 4), (2, 6), (5, 8), (7, 10), (9, 13), (11, 14),
    (2, 4), (3, 6), (9, 12), (11, 13),
    (3, 5), (6, 8), (7, 9), (10, 12),
    (3, 4), (5, 6), (7, 8), (9, 10), (11, 12),
    (6, 7), (8, 9),
)


def _params(*semantics):
    return pltpu.CompilerParams(dimension_semantics=semantics, vmem_limit_bytes=VMEM_LIMIT)


def _tile(n, pref):
    t = min(n, pref)
    while n % t:
        t //= 2
    return t


def _gelu(x):
    return 0.5 * x * (1.0 + lax.erf(x * 0.7071067811865476))


def _full(shape):
    return pl.BlockSpec(shape, lambda *_: (0,) * len(shape))


def _proj_kernel(x_ref, w_ref, qkv_ref, kvu_ref, *, width):
    xb = x_ref[...].astype(BF16)
    for c in range(4):
        acc = jnp.dot(xb, w_ref[:, c * width:(c + 1) * width], preferred_element_type=F32)
        if c == 0:
            qkv_ref[:, :width] = (acc * HEAD_DIM ** -0.5).astype(BF16)
        else:
            kvu_ref[:, (c - 1) * width:c * width] = acc
            if c < 3:
                qkv_ref[:, c * width:(c + 1) * width] = acc.astype(BF16)


def _proj(x2d, w_in_b, width):
    t, d = x2d.shape
    tm = _tile(t, 256)
    return pl.pallas_call(
        functools.partial(_proj_kernel, width=width),
        out_shape=(jax.ShapeDtypeStruct((t, 3 * width), BF16),
                   jax.ShapeDtypeStruct((t, 3 * width), F32)),
        grid=(t // tm,),
        in_specs=[pl.BlockSpec((tm, d), lambda i: (i, 0)),
                  pl.BlockSpec((d, 4 * width), lambda i: (0, 0))],
        out_specs=(pl.BlockSpec((tm, 3 * width), lambda i: (i, 0)),
                   pl.BlockSpec((tm, 3 * width), lambda i: (i, 0))),
        compiler_params=_params("parallel"),
        name="proj",
    )(x2d, w_in_b)


def _attn_kernel(q_ref, k_ref, v_ref, bias_ref, g_ref, o_ref, *, n_chunks, n_invalid, width):
    lo = lax.broadcasted_iota(jnp.int32, (CHUNK, LANES), 1) < HEAD_DIM
    kidx = lax.broadcasted_iota(jnp.int32, (CHUNK, BAND), 1)
    nt_dims = (((1,), (1,)), ((), ()))

    def chunk(c, carry):
        r0 = pl.multiple_of(c * CHUNK, CHUNK)
        pairs = []
        ssq = jnp.zeros((CHUNK, 1), F32)
        for hp in range(width // LANES):
            cols = slice(hp * LANES, (hp + 1) * LANES)
            qp = q_ref[pl.ds(r0, CHUNK), cols]
            kp = k_ref[pl.ds(r0, BAND), cols]
            vp = v_ref[pl.ds(r0, BAND), cols]
            halves = []
            for half in range(2):
                qh = jnp.where(lo, qp, 0) if half == 0 else jnp.where(lo, 0, qp)
                s = lax.dot_general(qh, kp, nt_dims, preferred_element_type=F32)
                s = s + bias_ref[2 * hp + half]
                if n_invalid > 0:
                    s = jnp.where(kidx + r0 >= n_invalid, s, MASK_VALUE)
                m = jnp.max(s, axis=-1, keepdims=True)
                e = jnp.exp(s - m)
                l = jnp.sum(e, axis=-1, keepdims=True)
                o = jnp.dot(e.astype(BF16), vp, preferred_element_type=F32)
                halves.append(o / l)
            o_pair = jnp.where(lo, halves[0], halves[1])
            ssq = ssq + jnp.sum(o_pair * o_pair, axis=-1, keepdims=True)
            pairs.append(o_pair)
        rinv = lax.rsqrt(ssq / width + RMS_EPS)
        for hp, o_pair in enumerate(pairs):
            cols = slice(hp * LANES, (hp + 1) * LANES)
            o_ref[pl.ds(r0, CHUNK), cols] = (o_pair * rinv * g_ref[:, cols]).astype(BF16)
        return carry

    lax.fori_loop(0, n_chunks, chunk, 0)


def _attention(qkv, kpad, vpad, bias, gain, nb, seq, width, n_invalid):
    lp = kpad.shape[1]
    n_heads = width // HEAD_DIM
    return pl.pallas_call(
        functools.partial(_attn_kernel, n_chunks=seq // CHUNK, n_invalid=n_invalid, width=width),
        out_shape=jax.ShapeDtypeStruct((nb * seq, width), BF16),
        grid=(nb,),
        in_specs=[pl.BlockSpec((seq, width), lambda b: (b, 0)),
                  pl.BlockSpec((None, lp, width), lambda b: (b, 0, 0)),
                  pl.BlockSpec((None, lp, width), lambda b: (b, 0, 0)),
                  _full((n_heads, CHUNK, BAND)),
                  _full((1, width))],
        out_specs=pl.BlockSpec((seq, width), lambda b: (b, 0)),
        compiler_params=_params("parallel"),
        name="attn",
    )(qkv, kpad, vpad, bias, gain)


def _ssm_prep_kernel(are_ref, aim_ref, ldt_ref, bre_ref, bim_ref, abr_ref, abi_ref, bbr_ref, bbi_ref):
    a_re = are_ref[...]
    a_im = aim_ref[...]
    dt = jnp.exp(ldt_ref[...])
    mag = jnp.exp(a_re * dt)
    ang = a_im * dt
    ab_re = mag * jnp.cos(ang)
    ab_im = mag * jnp.sin(ang)
    n_re = ab_re - 1.0
    den = a_re * a_re + a_im * a_im
    c_re = (n_re * a_re + ab_im * a_im) / den
    c_im = (ab_im * a_re - n_re * a_im) / den
    abr_ref[...] = ab_re
    abi_ref[...] = ab_im
    b_re = bre_ref[...]
    b_im = bim_ref[...]
    bbr_ref[...] = c_re[:, None, :] * b_re - c_im[:, None, :] * b_im
    bbi_ref[...] = c_re[:, None, :] * b_im + c_im[:, None, :] * b_re


def _ssm_prep(a_re, a_im, log_dt, bt_re, bt_im):
    g, p = a_re.shape
    h = bt_re.shape[1]
    return pl.pallas_call(
        _ssm_prep_kernel,
        out_shape=(jax.ShapeDtypeStruct((g, p), F32), jax.ShapeDtypeStruct((g, p), F32),
                   jax.ShapeDtypeStruct((g, h, p), F32), jax.ShapeDtypeStruct((g, h, p), F32)),
        name="ssm_prep",
    )(a_re, a_im, log_dt.reshape(g, 1), bt_re, bt_im)


def _ssm_kernel(u_ref, h0_ref, are_ref, aim_ref, bre_ref, bim_ref, cre_ref, cim_ref, d_ref, wg_ref, bg_ref,
                gn_ref, o_ref, hlast_ref, hst_ref, bu_ref, y_ref, *, nb, tl, n_gt, width):
    sw = STATE_TILE

    @pl.when(pl.program_id(0) == 0)
    def _():
        hst_ref[...] = h0_ref[...]

    for gt in range(n_gt):
        cols = slice(gt * LANES, (gt + 1) * LANES)
        uf = u_ref[:, cols]
        ub = uf.astype(BF16)
        bu_ref[:, :sw] = jnp.dot(ub, bre_ref[gt], preferred_element_type=F32)
        bu_ref[:, sw:] = jnp.dot(ub, bim_ref[gt], preferred_element_type=F32)
        a_re = jnp.broadcast_to(are_ref[gt], (nb, sw))
        a_im = jnp.broadcast_to(aim_ref[gt], (nb, sw))

        def step(t, carry, a_re=a_re, a_im=a_im):
            h_re, h_im = carry
            r0 = pl.multiple_of(t * nb, nb)
            n_re = a_re * h_re - a_im * h_im + bu_ref[pl.ds(r0, nb), :sw]
            n_im = a_re * h_im + a_im * h_re + bu_ref[pl.ds(r0, nb), sw:]
            bu_ref[pl.ds(r0, nb), :sw] = n_re
            bu_ref[pl.ds(r0, nb), sw:] = n_im
            return n_re, n_im

        h_re, h_im = lax.fori_loop(0, tl, step, (hst_ref[gt, :, :sw], hst_ref[gt, :, sw:]))
        hst_ref[gt, :, :sw] = h_re
        hst_ref[gt, :, sw:] = h_im

        y = (jnp.dot(bu_ref[:, :sw].astype(BF16), cre_ref[gt], preferred_element_type=F32)
             - jnp.dot(bu_ref[:, sw:].astype(BF16), cim_ref[gt], preferred_element_type=F32)
             + d_ref[:, cols] * uf)
        y = _gelu(y)
        z = jnp.dot(y.astype(BF16), wg_ref[gt], preferred_element_type=F32) + bg_ref[:, cols]
        y_ref[:, cols] = y * jax.nn.sigmoid(z)

    yy = y_ref[...]
    ms = jnp.sum(yy * yy, axis=-1, keepdims=True) / width
    o_ref[...] = (yy * lax.rsqrt(ms + RMS_EPS) * gn_ref[...]).astype(BF16)
    hlast_ref[...] = hst_ref[...]


def _ssm(u_tm, h0, sp, nb, seq, width):
    n_gt = width // LANES
    tl = _tile(seq, 64)
    rows = tl * nb
    sw = STATE_TILE
    return pl.pallas_call(
        functools.partial(_ssm_kernel, nb=nb, tl=tl, n_gt=n_gt, width=width),
        out_shape=(jax.ShapeDtypeStruct((seq * nb, width), BF16),
                   jax.ShapeDtypeStruct((n_gt, nb, 2 * sw), F32)),
        grid=(seq // tl,),
        in_specs=[pl.BlockSpec((rows, width), lambda s: (s, 0)),
                  _full((n_gt, nb, 2 * sw)),
                  _full((n_gt, 1, sw)), _full((n_gt, 1, sw)),
                  _full((n_gt, LANES, sw)), _full((n_gt, LANES, sw)),
                  _full((n_gt, sw, LANES)), _full((n_gt, sw, LANES)),
                  _full((1, width)),
                  _full((n_gt, LANES, LANES)),
                  _full((1, width)), _full((1, width))],
        out_specs=(pl.BlockSpec((rows, width), lambda s: (s, 0)),
                   _full((n_gt, nb, 2 * sw))),
        scratch_shapes=[pltpu.VMEM((n_gt, nb, 2 * sw), F32),
                        pltpu.VMEM((rows, 2 * sw), F32),
                        pltpu.VMEM((rows, width), F32)],
        compiler_params=_params("arbitrary"),
        name="ssm",
    )(u_tm, h0, sp["a_re"], sp["a_im"], sp["b_re"], sp["b_im"], sp["c_re"], sp["c_im"], sp["d"],
      sp["wg"], sp["bg"], sp["gn"])


def _block_diag(m, n_gt):
    _, r, c = m.shape
    m4 = m.reshape(n_gt, GROUPS_PER_TILE, r, 1, c)
    eye = jnp.eye(GROUPS_PER_TILE, dtype=bool)[None, :, None, :, None]
    return jnp.where(eye, m4, 0).reshape(n_gt, GROUPS_PER_TILE * r, GROUPS_PER_TILE * c)


def _ssm_params(a_re, a_im, log_dt, b_re, b_im, c_re, c_im, d, glu_w, glu_b, gain, width):
    n_gt = width // LANES
    ab_re, ab_im, bb_re, bb_im = _ssm_prep(a_re, a_im, log_dt, jnp.swapaxes(b_re, 1, 2),
                                           jnp.swapaxes(b_im, 1, 2))
    return dict(
        a_re=ab_re.reshape(n_gt, 1, STATE_TILE), a_im=ab_im.reshape(n_gt, 1, STATE_TILE),
        b_re=_block_diag(bb_re, n_gt).astype(BF16), b_im=_block_diag(bb_im, n_gt).astype(BF16),
        c_re=_block_diag(jnp.swapaxes(c_re, 1, 2), n_gt).astype(BF16),
        c_im=_block_diag(jnp.swapaxes(c_im, 1, 2), n_gt).astype(BF16),
        d=d.reshape(1, width), wg=_block_diag(glu_w, n_gt).astype(BF16), bg=glu_b.reshape(1, width),
        gn=gain.reshape(1, width))


def _state_to_tiles(s_re, s_im, n_gt):
    nb = s_re.shape[0]
    def one(s):
        return jnp.swapaxes(s.reshape(nb, n_gt, STATE_TILE), 0, 1)
    return jnp.concatenate([one(s_re), one(s_im)], axis=-1)


def _tiles_to_state(h, n_groups):
    n_gt, nb, _ = h.shape
    def one(s):
        return jnp.swapaxes(s, 0, 1).reshape(nb, n_groups, SSM_STATE)
    return one(h[:, :, :STATE_TILE]), one(h[:, :, STATE_TILE:])


def _layer_norm(z, g, b):
    mu = jnp.mean(z, axis=-1, keepdims=True)
    zc = z - mu
    var = jnp.mean(zc * zc, axis=-1, keepdims=True)
    return zc * lax.rsqrt(var + LN_EPS) * g + b


def _mix_kernel(a_ref, s_ref, x_ref, wo_ref, g_ref, b_ref, x1_ref, x1b_ref, *, alpha, width):
    mix = (jnp.dot(a_ref[...], wo_ref[:width], preferred_element_type=F32)
           + jnp.dot(s_ref[...], wo_ref[width:], preferred_element_type=F32))
    x1 = _layer_norm(alpha * x_ref[...] + mix, g_ref[...], b_ref[...])
    x1_ref[...] = x1
    x1b_ref[...] = x1.astype(BF16)


def _mix(att_n, ssm_n, x2d, w_out_b, g, b, alpha, width):
    t, d = x2d.shape
    tm = _tile(t, 512)
    row = lambda i: (i, 0)
    return pl.pallas_call(
        functools.partial(_mix_kernel, alpha=alpha, width=width),
        out_shape=(jax.ShapeDtypeStruct((t, d), F32), jax.ShapeDtypeStruct((t, d), BF16)),
        grid=(t // tm,),
        in_specs=[pl.BlockSpec((tm, width), row), pl.BlockSpec((tm, width), row),
                  pl.BlockSpec((tm, d), row), _full((2 * width, d)), _full((1, d)), _full((1, d))],
        out_specs=(pl.BlockSpec((tm, d), row), pl.BlockSpec((tm, d), row)),
        compiler_params=_params("parallel"),
        name="mix",
    )(att_n, ssm_n, x2d, w_out_b, g.reshape(1, d), b.reshape(1, d))


def _sort16(v):
    v = list(v)
    for i, j in SORT16:
        v[i], v[j] = jnp.maximum(v[i], v[j]), jnp.minimum(v[i], v[j])
    return v


def _merge_top16(a, b):
    v = [jnp.maximum(a[j], b[PEER_TOPK - 1 - j]) for j in range(PEER_TOPK)]
    for dist in (8, 4, 2, 1):
        for i in range(PEER_TOPK):
            if not i & dist:
                v[i], v[i + dist] = jnp.maximum(v[i], v[i + dist]), jnp.minimum(v[i], v[i + dist])
    return v


def _fold_sublanes(v):
    for shift in (4, 2, 1):
        v = _merge_top16(v, [pltpu.roll(x, shift, 0) for x in v])
    return v


def _route_kernel(x_ref, wq_ref, keys_ref, s1_ref, f_ref, s2_ref, e2_ref, tau_ref, *, tt, tc):
    qp = jnp.dot(x_ref[...], wq_ref[...], preferred_element_type=F32).astype(BF16)
    nt_dims = (((1,), (1,)), ((), ()))
    s_t = [lax.dot_general(keys_ref[0, c], qp[:, c * PEER_DHALF:(c + 1) * PEER_DHALF], nt_dims,
                           preferred_element_type=F32) for c in range(2)]
    for l0 in range(0, tt, tc):
        cols = slice(l0, l0 + tc)
        sub = lax.broadcasted_iota(jnp.int32, (SUBLANES, tc), 0)
        top = []
        for c in range(2):
            blocks = [s_t[c][SUBLANES * j:SUBLANES * (j + 1), cols] for j in range(N_KEYS // SUBLANES)]
            top.append(_fold_sublanes(_sort16(blocks)))
        v1, v2 = top
        v1_lo, v1_hi = v1[0], v1[SUBLANES]
        for r in range(1, SUBLANES):
            v1_lo = jnp.where(sub == r, v1[r], v1_lo)
            v1_hi = jnp.where(sub == r, v1[SUBLANES + r], v1_hi)
        cand_lo = [v1_lo + v2[b] for b in range(PEER_TOPK)]
        cand_hi = [v1_hi + v2[b] for b in range(PEER_TOPK)]
        top_s = _fold_sublanes(_merge_top16(cand_lo, cand_hi))
        z = jnp.ones((SUBLANES, tc), F32)
        for k in range(1, PEER_TOPK):
            z = z + jnp.exp(top_s[k] - top_s[0])
        tau_ref[0, :, cols] = top_s[PEER_TOPK - 1]
        s1 = s_t[0][:, cols]
        s2 = s_t[1][:, cols]
        s1_ref[0, :, cols] = s1
        s2_ref[0, :, cols] = s2
        f_ref[0, :, cols] = jnp.exp(s1 - v1[0][0:1, :]) / z[0:1, :]
        e2_ref[0, :, cols] = jnp.exp(s2 - v2[0][0:1, :])


def _route(x1b, wq_b, keys_b, tt):
    t, d = x1b.shape
    dk = 2 * PEER_DHALF
    tc = _tile(tt, 256)
    big = jax.ShapeDtypeStruct((PEER_HEADS, N_KEYS, t), F32)
    big_spec = pl.BlockSpec((1, N_KEYS, tt), lambda i, h: (h, 0, i))
    return pl.pallas_call(
        functools.partial(_route_kernel, tt=tt, tc=tc),
        out_shape=(big, big, big, big, jax.ShapeDtypeStruct((PEER_HEADS, SUBLANES, t), F32)),
        grid=(t // tt, PEER_HEADS),
        in_specs=[pl.BlockSpec((tt, d), lambda i, h: (i, 0)),
                  pl.BlockSpec((d, dk), lambda i, h: (0, h)),
                  pl.BlockSpec((1, 2, N_KEYS, PEER_DHALF), lambda i, h: (h, 0, 0, 0))],
        out_specs=(big_spec, big_spec, big_spec, big_spec,
                   pl.BlockSpec((1, SUBLANES, tt), lambda i, h: (h, 0, i))),
        compiler_params=_params("parallel", "parallel"),
        name="route",
    )(x1b, wq_b, keys_b)


def _peer_kernel(xt_ref, u_ref, vt_ref, s1_ref, f_ref, s2_ref, e2_ref, tau_ref, y_ref, a_ref, hid_ref,
                 *, eb, tt, rc):
    j = pl.program_id(1)

    @pl.when(j == 0)
    def _():
        y_ref[...] = jnp.zeros_like(y_ref)

    a_ref[...] = jnp.dot(u_ref[...], xt_ref[...], preferred_element_type=F32)
    for q in range(eb // N_KEYS):
        i1 = j * (eb // N_KEYS) + q
        for r0 in range(0, N_KEYS, rc):
            w = jnp.zeros((rc, tt), F32)
            for h in range(PEER_HEADS):
                s = s2_ref[h, r0:r0 + rc, :] + s1_ref[h, pl.ds(i1, 1), :]
                sel = jnp.where(s >= tau_ref[h, 0:1, :], e2_ref[h, r0:r0 + rc, :], 0.0)
                w = w + sel * f_ref[h, pl.ds(i1, 1), :]
            rows = slice(q * N_KEYS + r0, q * N_KEYS + r0 + rc)
            hid_ref[rows, :] = (_gelu(a_ref[rows, :]) * w).astype(BF16)
    y_ref[...] += jnp.dot(vt_ref[...], hid_ref[...], preferred_element_type=F32)


def _peer(x1t, u_b, vt_b, s1, f, s2, e2, tau, tt):
    d, t = x1t.shape
    n_exp = u_b.shape[0]
    eb = 512
    big_spec = pl.BlockSpec((PEER_HEADS, N_KEYS, tt), lambda i, j: (0, 0, i))
    return pl.pallas_call(
        functools.partial(_peer_kernel, eb=eb, tt=tt, rc=32),
        out_shape=jax.ShapeDtypeStruct((d, t), F32),
        grid=(t // tt, n_exp // eb),
        in_specs=[pl.BlockSpec((d, tt), lambda i, j: (0, i)),
                  pl.BlockSpec((eb, d), lambda i, j: (j, 0)),
                  pl.BlockSpec((d, eb), lambda i, j: (0, j)),
                  big_spec, big_spec, big_spec, big_spec,
                  pl.BlockSpec((PEER_HEADS, SUBLANES, tt), lambda i, j: (0, 0, i))],
        out_specs=pl.BlockSpec((d, tt), lambda i, j: (0, i)),
        scratch_shapes=[pltpu.VMEM((eb, tt), F32), pltpu.VMEM((eb, tt), BF16)],
        compiler_params=_params("parallel", "arbitrary"),
        name="peer",
    )(x1t, u_b, vt_b, s1, f, s2, e2, tau)


def _ln2_kernel(x1_ref, y_ref, g_ref, b_ref, o_ref, *, alpha):
    o_ref[...] = _layer_norm(alpha * x1_ref[...] + y_ref[...], g_ref[...], b_ref[...])


def _ln2(x1, y, g, b, alpha):
    t, d = x1.shape
    tm = _tile(t, 512)
    row = lambda i: (i, 0)
    return pl.pallas_call(
        functools.partial(_ln2_kernel, alpha=alpha),
        out_shape=jax.ShapeDtypeStruct((t, d), F32),
        grid=(t // tm,),
        in_specs=[pl.BlockSpec((tm, d), row), pl.BlockSpec((tm, d), row), _full((1, d)), _full((1, d))],
        out_specs=pl.BlockSpec((tm, d), row),
        compiler_params=_params("parallel"),
        name="ln2",
    )(x1, y, g.reshape(1, d), b.reshape(1, d))


def _rel_bias_table(rel_bias):
    q_pos = jnp.arange(CHUNK)
    k_pos = jnp.arange(BAND) - ATT_WINDOW
    idx = jnp.clip(k_pos[None, :] - q_pos[:, None], -REL_CLIP, REL_CLIP) + REL_CLIP
    return rel_bias[:, idx]


def _encoder_layer(x, hist_k, hist_v, h0, wts, alpha):
    nb, seq, d = x.shape
    width = d // 2
    t = nb * seq
    n_gt = width // LANES
    assert seq % CHUNK == 0 and width % LANES == 0 and nb % SUBLANES == 0
    assert hist_k is None or seq == CHUNK
    x2d = x.reshape(t, d)

    qkv, kvu = _proj(x2d, wts["w_in"], width)
    k_new = kvu[:, :width].reshape(nb, seq, width)
    v_new = kvu[:, width:2 * width].reshape(nb, seq, width)

    n_hist = 0 if hist_k is None else hist_k.shape[1]
    n_invalid = ATT_WINDOW - n_hist
    def padded(new_b, hist):
        parts = [jnp.zeros((nb, n_invalid, width), BF16)] if n_invalid else []
        if hist is not None:
            parts.append(hist.reshape(nb, n_hist, width).astype(BF16))
        return jnp.concatenate(parts + [new_b.reshape(nb, seq, width)], axis=1)
    kpad = padded(qkv[:, width:2 * width], hist_k)
    vpad = padded(qkv[:, 2 * width:], hist_v)
    att_n = _attention(qkv, kpad, vpad, wts["bias"], wts["norm_attn_g"], nb, seq, width, n_invalid)

    u_tm = jnp.swapaxes(kvu[:, 2 * width:].reshape(nb, seq, width), 0, 1).reshape(t, width)
    ssm_tm, h_last = _ssm(u_tm, h0, wts["ssm"], nb, seq, width)
    ssm_n = jnp.swapaxes(ssm_tm.reshape(seq, nb, width), 0, 1).reshape(t, width)

    x1, x1b = _mix(att_n, ssm_n, x2d, wts["w_out"], wts["ln1_g"], wts["ln1_b"], alpha, width)

    tt = _tile(t, 512)
    s1, f, s2, e2, tau = _route(x1b, wts["peer_wq"], wts["peer_keys"], tt)
    y_t = _peer(x1b.T, wts["peer_u"], wts["peer_vt"], s1, f, s2, e2, tau, tt)
    x2 = _ln2(x1, y_t.T, wts["ln2_g"], wts["ln2_b"], alpha)

    n_heads = width // HEAD_DIM
    return (x2.reshape(nb, seq, d), k_new.reshape(nb, seq, n_heads, HEAD_DIM),
            v_new.reshape(nb, seq, n_heads, HEAD_DIM), h_last)


def kernel(x_prompt, x_sample, cache_attn_k, cache_attn_v, state_ssm_re, state_ssm_im, w_in, rel_bias, norm_attn_g, ssm_a_re, ssm_a_im, ssm_log_dt, ssm_b_re, ssm_b_im, ssm_c_re, ssm_c_im, ssm_d, ssm_glu_w, ssm_glu_b, norm_ssm_g, w_out, ln1_g, ln1_b, peer_wq, peer_keys, peer_u, peer_v, ln2_g, ln2_b):
    depth = w_in.shape[0]
    alpha = (2.0 * depth) ** 0.25
    d = x_prompt.shape[-1]
    width = d // 2
    n_gt = width // LANES
    n_groups = width // SSM_GROUP
    keep = min(ATT_WINDOW, x_prompt.shape[1])

    yp, ys = x_prompt, x_sample
    outs = [[] for _ in range(8)]
    for l in range(depth):
        wts = dict(
            w_in=w_in[l].astype(BF16),
            bias=_rel_bias_table(rel_bias[l]),
            norm_attn_g=norm_attn_g[l].reshape(1, width),
            ssm=_ssm_params(ssm_a_re[l], ssm_a_im[l], ssm_log_dt[l], ssm_b_re[l], ssm_b_im[l], ssm_c_re[l],
                            ssm_c_im[l], ssm_d[l], ssm_glu_w[l], ssm_glu_b[l], norm_ssm_g[l], width),
            w_out=w_out[l].astype(BF16), ln1_g=ln1_g[l], ln1_b=ln1_b[l],
            peer_wq=peer_wq[l].astype(BF16), peer_keys=peer_keys[l].astype(BF16),
            peer_u=peer_u[l].astype(BF16), peer_vt=peer_v[l].astype(BF16).T,
            ln2_g=ln2_g[l], ln2_b=ln2_b[l])
        h0p = jnp.zeros((n_gt, x_prompt.shape[0], 2 * STATE_TILE), F32)
        yp, kp, vp, hp = _encoder_layer(yp, None, None, h0p, wts, alpha)
        h0s = _state_to_tiles(state_ssm_re[l].astype(F32), state_ssm_im[l].astype(F32), n_gt)
        ys, kn, vn, hs = _encoder_layer(ys, cache_attn_k[l], cache_attn_v[l], h0s, wts, alpha)
        hp_re, hp_im = _tiles_to_state(hp, n_groups)
        hs_re, hs_im = _tiles_to_state(hs, n_groups)
        for acc, val in zip(outs, (kp[:, -keep:], vp[:, -keep:], hp_re, hp_im, kn, vn, hs_re, hs_im)):
            acc.append(val)
    return (yp, ys) + tuple(jnp.stack(o) for o in outs)
```

```python
import functools

import jax
import jax.numpy as jnp
from jax import lax
from jax.experimental import pallas as pl
from jax.experimental.pallas import tpu as pltpu

F32 = jnp.float32
BF16 = jnp.bfloat16

CHUNK = 64
LEFT_CHUNKS = 8
ATT_WINDOW = LEFT_CHUNKS * CHUNK
BAND = ATT_WINDOW + CHUNK
HEAD_DIM = 64
REL_CLIP = 2 * CHUNK
SSM_GROUP = 16
SSM_STATE = 64
PEER_HEADS = 8
PEER_TOPK = 16
N_KEYS = 128
PEER_DHALF = 128
LN_EPS = 1e-5
RMS_EPS = 1e-6
MASK_VALUE = -1e30

LANES = 128
SUBLANES = 8
GROUPS_PER_TILE = LANES // SSM_GROUP
STATE_TILE = GROUPS_PER_TILE * SSM_STATE
VMEM_LIMIT = 56 * 1024 * 1024

SORT16 = (
    (0, 13), (1, 12), (2, 15), (3, 14), (4, 8), (5, 6), (7, 11), (9, 10),
    (0, 5), (1, 7), (2, 9), (3, 4), (6, 13), (8, 14), (10, 15), (11, 12),
    (0, 1), (2, 3), (4, 5), (6, 8), (7, 9), (10, 11), (12, 13), (14, 15),
    (0, 2), (1, 3), (4, 10), (5, 11), (6, 7), (8, 9), (12, 14), (13, 15),
    (1, 2), (3, 12), (4, 6), (5, 7), (8, 10), (9, 11), (13, 14),
    (1, 4), (2, 6), (5, 8), (7, 10), (9, 13), (11, 14),
    (2, 4), (3, 6), (9, 12), (11, 13),
    (3, 5), (6, 8), (7, 9), (10, 12),
    (3, 4), (5, 6), (7, 8), (9, 10), (11, 12),
    (6, 7), (8, 9),
)


def _params(*semantics):
    return pltpu.CompilerParams(dimension_semantics=semantics, vmem_limit_bytes=VMEM_LIMIT)


def _tile(n, pref):
    t = min(n, pref)
    while n % t:
        t //= 2
    return t


def _gelu(x):
    return 0.5 * x * (1.0 + lax.erf(x * 0.7071067811865476))


def _full(shape):
    return pl.BlockSpec(shape, lambda *_: (0,) * len(shape))


def _proj_kernel(x_ref, w_ref, qkv_ref, kvu_ref, *, width):
    xb = x_ref[...].astype(BF16)
    for c in range(4):
        acc = jnp.dot(xb, w_ref[:, c * width:(c + 1) * width], preferred_element_type=F32)
        if c == 0:
            qkv_ref[:, :width] = (acc * HEAD_DIM ** -0.5).astype(BF16)
        else:
            kvu_ref[:, (c - 1) * width:c * width] = acc
            if c < 3:
                qkv_ref[:, c * width:(c + 1) * width] = acc.astype(BF16)


def _proj(x2d, w_in_b, width):
    t, d = x2d.shape
    tm = _tile(t, 256)
    return pl.pallas_call(
        functools.partial(_proj_kernel, width=width),
        out_shape=(jax.ShapeDtypeStruct((t, 3 * width), BF16),
                   jax.ShapeDtypeStruct((t, 3 * width), F32)),
        grid=(t // tm,),
        in_specs=[pl.BlockSpec((tm, d), lambda i: (i, 0)),
                  pl.BlockSpec((d, 4 * width), lambda i: (0, 0))],
        out_specs=(pl.BlockSpec((tm, 3 * width), lambda i: (i, 0)),
                   pl.BlockSpec((tm, 3 * width), lambda i: (i, 0))),
        compiler_params=_params("parallel"),
        name="proj",
    )(x2d, w_in_b)


def _attn_kernel(q_ref, k_ref, v_ref, bias_ref, g_ref, o_ref, *, n_chunks, n_invalid, width):
    lo = lax.broadcasted_iota(jnp.int32, (CHUNK, LANES), 1) < HEAD_DIM
    kidx = lax.broadcasted_iota(jnp.int32, (CHUNK, BAND), 1)
    nt_dims = (((1,), (1,)), ((), ()))

    def chunk(c, carry):
        r0 = pl.multiple_of(c * CHUNK, CHUNK)
        pairs = []
        ssq = jnp.zeros((CHUNK, 1), F32)
        for hp in range(width // LANES):
            cols = slice(hp * LANES, (hp + 1) * LANES)
            qp = q_ref[pl.ds(r0, CHUNK), cols]
            kp = k_ref[pl.ds(r0, BAND), cols]
            vp = v_ref[pl.ds(r0, BAND), cols]
            halves = []
            for half in range(2):
                qh = jnp.where(lo, qp, 0) if half == 0 else jnp.where(lo, 0, qp)
                s = lax.dot_general(qh, kp, nt_dims, preferred_element_type=F32)
                s = s + bias_ref[2 * hp + half]
                if n_invalid > 0:
                    s = jnp.where(kidx + r0 >= n_invalid, s, MASK_VALUE)
                m = jnp.max(s, axis=-1, keepdims=True)
                e = jnp.exp(s - m)
                l = jnp.sum(e, axis=-1, keepdims=True)
                o = jnp.dot(e.astype(BF16), vp, preferred_element_type=F32)
                halves.append(o / l)
            o_pair = jnp.where(lo, halves[0], halves[1])
            ssq = ssq + jnp.sum(o_pair * o_pair, axis=-1, keepdims=True)
            pairs.append(o_pair)
        rinv = lax.rsqrt(ssq / width + RMS_EPS)
        for hp, o_pair in enumerate(pairs):
            cols = slice(hp * LANES, (hp + 1) * LANES)
            o_ref[pl.ds(r0, CHUNK), cols] = (o_pair * rinv * g_ref[:, cols]).astype(BF16)
        return carry

    lax.fori_loop(0, n_chunks, chunk, 0)


def _attention(qkv, kpad, vpad, bias, gain, nb, seq, width, n_invalid):
    lp = kpad.shape[1]
    n_heads = width // HEAD_DIM
    return pl.pallas_call(
        functools.partial(_attn_kernel, n_chunks=seq // CHUNK, n_invalid=n_invalid, width=width),
        out_shape=jax.ShapeDtypeStruct((nb * seq, width), BF16),
        grid=(nb,),
        in_specs=[pl.BlockSpec((seq, width), lambda b: (b, 0)),
                  pl.BlockSpec((None, lp, width), lambda b: (b, 0, 0)),
                  pl.BlockSpec((None, lp, width), lambda b: (b, 0, 0)),
                  _full((n_heads, CHUNK, BAND)),
                  _full((1, width))],
        out_specs=pl.BlockSpec((seq, width), lambda b: (b, 0)),
        compiler_params=_params("parallel"),
        name="attn",
    )(qkv, kpad, vpad, bias, gain)


def _ssm_prep_kernel(are_ref, aim_ref, ldt_ref, bre_ref, bim_ref, abr_ref, abi_ref, bbr_ref, bbi_ref):
    a_re = are_ref[...]
    a_im = aim_ref[...]
    dt = jnp.exp(ldt_ref[...])
    mag = jnp.exp(a_re * dt)
    ang = a_im * dt
    ab_re = mag * jnp.cos(ang)
    ab_im = mag * jnp.sin(ang)
    n_re = ab_re - 1.0
    den = a_re * a_re + a_im * a_im
    c_re = (n_re * a_re + ab_im * a_im) / den
    c_im = (ab_im * a_re - n_re * a_im) / den
    abr_ref[...] = ab_re
    abi_ref[...] = ab_im
    b_re = bre_ref[...]
    b_im = bim_ref[...]
    bbr_ref[...] = c_re[:, None, :] * b_re - c_im[:, None, :] * b_im
    bbi_ref[...] = c_re[:, None, :] * b_im + c_im[:, None, :] * b_re


def _ssm_prep(a_re, a_im, log_dt, bt_re, bt_im):
    g, p = a_re.shape
    h = bt_re.shape[1]
    return pl.pallas_call(
        _ssm_prep_kernel,
        out_shape=(jax.ShapeDtypeStruct((g, p), F32), jax.ShapeDtypeStruct((g, p), F32),
                   jax.ShapeDtypeStruct((g, h, p), F32), jax.ShapeDtypeStruct((g, h, p), F32)),
        name="ssm_prep",
    )(a_re, a_im, log_dt.reshape(g, 1), bt_re, bt_im)


def _ssm_kernel(u_ref, h0_ref, are_ref, aim_ref, bre_ref, bim_ref, cre_ref, cim_ref, d_ref, wg_ref, bg_ref,
                gn_ref, o_ref, hlast_ref, hst_ref, bu_ref, y_ref, *, nb, tl, n_gt, width):
    sw = STATE_TILE

    @pl.when(pl.program_id(0) == 0)
    def _():
        hst_ref[...] = h0_ref[...]

    for gt in range(n_gt):
        cols = slice(gt * LANES, (gt + 1) * LANES)
        uf = u_ref[:, cols]
        ub = uf.astype(BF16)
        bu_ref[:, :sw] = jnp.dot(ub, bre_ref[gt], preferred_element_type=F32)
        bu_ref[:, sw:] = jnp.dot(ub, bim_ref[gt], preferred_element_type=F32)
        a_re = jnp.broadcast_to(are_ref[gt], (nb, sw))
        a_im = jnp.broadcast_to(aim_ref[gt], (nb, sw))

        def step(t, carry, a_re=a_re, a_im=a_im):
            h_re, h_im = carry
            r0 = pl.multiple_of(t * nb, nb)
            n_re = a_re * h_re - a_im * h_im + bu_ref[pl.ds(r0, nb), :sw]
            n_im = a_re * h_im + a_im * h_re + bu_ref[pl.ds(r0, nb), sw:]
            bu_ref[pl.ds(r0, nb), :sw] = n_re
            bu_ref[pl.ds(r0, nb), sw:] = n_im
            return n_re, n_im

        h_re, h_im = lax.fori_loop(0, tl, step, (hst_ref[gt, :, :sw], hst_ref[gt, :, sw:]))
        hst_ref[gt, :, :sw] = h_re
        hst_ref[gt, :, sw:] = h_im

        y = (jnp.dot(bu_ref[:, :sw].astype(BF16), cre_ref[gt], preferred_element_type=F32)
             - jnp.dot(bu_ref[:, sw:].astype(BF16), cim_ref[gt], preferred_element_type=F32)
             + d_ref[:, cols] * uf)
        y = _gelu(y)
        z = jnp.dot(y.astype(BF16), wg_ref[gt], preferred_element_type=F32) + bg_ref[:, cols]
        y_ref[:, cols] = y * jax.nn.sigmoid(z)

    yy = y_ref[...]
    ms = jnp.sum(yy * yy, axis=-1, keepdims=True) / width
    o_ref[...] = (yy * lax.rsqrt(ms + RMS_EPS) * gn_ref[...]).astype(BF16)
    hlast_ref[...] = hst_ref[...]


def _ssm(u_tm, h0, sp, nb, seq, width):
    n_gt = width // LANES
    tl = _tile(seq, 64)
    rows = tl * nb
    sw = STATE_TILE
    return pl.pallas_call(
        functools.partial(_ssm_kernel, nb=nb, tl=tl, n_gt=n_gt, width=width),
        out_shape=(jax.ShapeDtypeStruct((seq * nb, width), BF16),
                   jax.ShapeDtypeStruct((n_gt, nb, 2 * sw), F32)),
        grid=(seq // tl,),
        in_specs=[pl.BlockSpec((rows, width), lambda s: (s, 0)),
                  _full((n_gt, nb, 2 * sw)),
                  _full((n_gt, 1, sw)), _full((n_gt, 1, sw)),
                  _full((n_gt, LANES, sw)), _full((n_gt, LANES, sw)),
                  _full((n_gt, sw, LANES)), _full((n_gt, sw, LANES)),
                  _full((1, width)),
                  _full((n_gt, LANES, LANES)),
                  _full((1, width)), _full((1, width))],
        out_specs=(pl.BlockSpec((rows, width), lambda s: (s, 0)),
                   _full((n_gt, nb, 2 * sw))),
        scratch_shapes=[pltpu.VMEM((n_gt, nb, 2 * sw), F32),
                        pltpu.VMEM((rows, 2 * sw), F32),
                        pltpu.VMEM((rows, width), F32)],
        compiler_params=_params("arbitrary"),
        name="ssm",
    )(u_tm, h0, sp["a_re"], sp["a_im"], sp["b_re"], sp["b_im"], sp["c_re"], sp["c_im"], sp["d"],
      sp["wg"], sp["bg"], sp["gn"])


def _block_diag(m, n_gt):
    _, r, c = m.shape
    m4 = m.reshape(n_gt, GROUPS_PER_TILE, r, 1, c)
    eye = jnp.eye(GROUPS_PER_TILE, dtype=bool)[None, :, None, :, None]
    return jnp.where(eye, m4, 0).reshape(n_gt, GROUPS_PER_TILE * r, GROUPS_PER_TILE * c)


def _ssm_params(a_re, a_im, log_dt, b_re, b_im, c_re, c_im, d, glu_w, glu_b, gain, width):
    n_gt = width // LANES
    ab_re, ab_im, bb_re, bb_im = _ssm_prep(a_re, a_im, log_dt, jnp.swapaxes(b_re, 1, 2),
                                           jnp.swapaxes(b_im, 1, 2))
    return dict(
        a_re=ab_re.reshape(n_gt, 1, STATE_TILE), a_im=ab_im.reshape(n_gt, 1, STATE_TILE),
        b_re=_block_diag(bb_re, n_gt).astype(BF16), b_im=_block_diag(bb_im, n_gt).astype(BF16),
        c_re=_block_diag(jnp.swapaxes(c_re, 1, 2), n_gt).astype(BF16),
        c_im=_block_diag(jnp.swapaxes(c_im, 1, 2), n_gt).astype(BF16),
        d=d.reshape(1, width), wg=_block_diag(glu_w, n_gt).astype(BF16), bg=glu_b.reshape(1, width),
        gn=gain.reshape(1, width))


def _state_to_tiles(s_re, s_im, n_gt):
    nb = s_re.shape[0]
    def one(s):
        return jnp.swapaxes(s.reshape(nb, n_gt, STATE_TILE), 0, 1)
    return jnp.concatenate([one(s_re), one(s_im)], axis=-1)


def _tiles_to_state(h, n_groups):
    n_gt, nb, _ = h.shape
    def one(s):
        return jnp.swapaxes(s, 0, 1).reshape(nb, n_groups, SSM_STATE)
    return one(h[:, :, :STATE_TILE]), one(h[:, :, STATE_TILE:])


def _layer_norm(z, g, b):
    mu = jnp.mean(z, axis=-1, keepdims=True)
    zc = z - mu
    var = jnp.mean(zc * zc, axis=-1, keepdims=True)
    return zc * lax.rsqrt(var + LN_EPS) * g + b


def _mix_kernel(a_ref, s_ref, x_ref, wo_ref, g_ref, b_ref, x1_ref, x1b_ref, *, alpha, width):
    mix = (jnp.dot(a_ref[...], wo_ref[:width], preferred_element_type=F32)
           + jnp.dot(s_ref[...], wo_ref[width:], preferred_element_type=F32))
    x1 = _layer_norm(alpha * x_ref[...] + mix, g_ref[...], b_ref[...])
    x1_ref[...] = x1
    x1b_ref[...] = x1.astype(BF16)


def _mix(att_n, ssm_n, x2d, w_out_b, g, b, alpha, width):
    t, d = x2d.shape
    tm = _tile(t, 512)
    row = lambda i: (i, 0)
    return pl.pallas_call(
        functools.partial(_mix_kernel, alpha=alpha, width=width),
        out_shape=(jax.ShapeDtypeStruct((t, d), F32), jax.ShapeDtypeStruct((t, d), BF16)),
        grid=(t // tm,),
        in_specs=[pl.BlockSpec((tm, width), row), pl.BlockSpec((tm, width), row),
                  pl.BlockSpec((tm, d), row), _full((2 * width, d)), _full((1, d)), _full((1, d))],
        out_specs=(pl.BlockSpec((tm, d), row), pl.BlockSpec((tm, d), row)),
        compiler_params=_params("parallel"),
        name="mix",
    )(att_n, ssm_n, x2d, w_out_b, g.reshape(1, d), b.reshape(1, d))


def _sort16(v):
    v = list(v)
    for i, j in SORT16:
        v[i], v[j] = jnp.maximum(v[i], v[j]), jnp.minimum(v[i], v[j])
    return v


def _merge_top16(a, b):
    v = [jnp.maximum(a[j], b[PEER_TOPK - 1 - j]) for j in range(PEER_TOPK)]
    for dist in (8, 4, 2, 1):
        for i in range(PEER_TOPK):
            if not i & dist:
                v[i], v[i + dist] = jnp.maximum(v[i], v[i + dist]), jnp.minimum(v[i], v[i + dist])
    return v


def _fold_sublanes(v):
    for shift in (4, 2, 1):
        v = _merge_top16(v, [pltpu.roll(x, shift, 0) for x in v])
    return v


def _route_kernel(x_ref, wq_ref, keys_ref, lim_ref, f_ref, r2_ref, e2_ref, *, tt, tc):
    qp = jnp.dot(x_ref[...], wq_ref[...], preferred_element_type=F32).astype(BF16)
    nt_dims = (((1,), (1,)), ((), ()))
    s_t = [lax.dot_general(keys_ref[0, c], qp[:, c * PEER_DHALF:(c + 1) * PEER_DHALF], nt_dims,
                           preferred_element_type=F32) for c in range(2)]
    n_blocks = N_KEYS // SUBLANES
    for l0 in range(0, tt, tc):
        cols = slice(l0, l0 + tc)
        sub = lax.broadcasted_iota(jnp.int32, (SUBLANES, tc), 0)
        blocks = [[s_t[c][SUBLANES * j:SUBLANES * (j + 1), cols] for j in range(n_blocks)] for c in range(2)]
        v1, v2 = (_fold_sublanes(_sort16(blocks[c])) for c in range(2))
        v1_lo, v1_hi = v1[0], v1[SUBLANES]
        for r in range(1, SUBLANES):
            v1_lo = jnp.where(sub == r, v1[r], v1_lo)
            v1_hi = jnp.where(sub == r, v1[SUBLANES + r], v1_hi)
        cand_lo = [v1_lo + v2[b] for b in range(PEER_TOPK)]
        cand_hi = [v1_hi + v2[b] for b in range(PEER_TOPK)]
        top_s = _fold_sublanes(_merge_top16(cand_lo, cand_hi))
        tau = top_s[PEER_TOPK - 1]
        z = jnp.ones((SUBLANES, tc), F32)
        for k in range(1, PEER_TOPK):
            z = z + jnp.exp(top_s[k] - top_s[0])
        zinv = 1.0 / z
        lim, f, r2, e2 = [], [], [], []
        for j in range(n_blocks):
            s1, s2 = blocks[0][j], blocks[1][j]
            cnt = jnp.zeros((SUBLANES, tc), F32)
            rank = jnp.zeros((SUBLANES, tc), F32)
            for b in range(PEER_TOPK):
                cnt = cnt + jnp.where(s1 + v2[b] >= tau, 1.0, 0.0)
                rank = rank + jnp.where(v2[b] > s2, 1.0, 0.0)
            lim.append(jnp.where(s1 >= v1[PEER_TOPK - 1], cnt, 0.0))
            r2.append(rank)
            f.append(jnp.exp(s1 - v1[0]) * zinv)
            e2.append(jnp.exp(s2 - v2[0]))
        lim_ref[0, :, cols] = jnp.concatenate(lim, axis=0)
        f_ref[0, :, cols] = jnp.concatenate(f, axis=0)
        r2_ref[0, :, cols] = jnp.concatenate(r2, axis=0).astype(BF16)
        e2_ref[0, :, cols] = jnp.concatenate(e2, axis=0).astype(BF16)


def _route(x1b, wq_b, keys_b, tt):
    t, d = x1b.shape
    dk = 2 * PEER_DHALF
    tc = _tile(tt, 256)
    big_spec = pl.BlockSpec((1, N_KEYS, tt), lambda i, h: (h, 0, i))
    return pl.pallas_call(
        functools.partial(_route_kernel, tt=tt, tc=tc),
        out_shape=(jax.ShapeDtypeStruct((PEER_HEADS, N_KEYS, t), F32),
                   jax.ShapeDtypeStruct((PEER_HEADS, N_KEYS, t), F32),
                   jax.ShapeDtypeStruct((PEER_HEADS, N_KEYS, t), BF16),
                   jax.ShapeDtypeStruct((PEER_HEADS, N_KEYS, t), BF16)),
        grid=(t // tt, PEER_HEADS),
        in_specs=[pl.BlockSpec((tt, d), lambda i, h: (i, 0)),
                  pl.BlockSpec((d, dk), lambda i, h: (0, h)),
                  pl.BlockSpec((1, 2, N_KEYS, PEER_DHALF), lambda i, h: (h, 0, 0, 0))],
        out_specs=(big_spec, big_spec, big_spec, big_spec),
        compiler_params=_params("parallel", "parallel"),
        name="route",
    )(x1b, wq_b, keys_b)


def _peer_kernel(xt_ref, u_ref, vt_ref, lim_ref, f_ref, r2_ref, e2_ref, y_ref, a_ref, hid_ref, bc_ref,
                 *, eb, tt):
    j = pl.program_id(1)
    slot = lax.rem(j, 2)
    rc = 2 * SUBLANES

    @pl.when(j == 0)
    def _():
        y_ref[...] = jnp.zeros_like(y_ref)
        hid_ref[1] = jnp.zeros((eb, tt), BF16)

    a_ref[...] = jnp.dot(u_ref[...], xt_ref[...], preferred_element_type=F32)
    y_ref[...] += jnp.dot(vt_ref[...], hid_ref[1 - slot], preferred_element_type=F32)
    for q in range(eb // N_KEYS):
        for h in range(PEER_HEADS):
            bc_ref[0, q, h] = jnp.broadcast_to(lim_ref[h, q:q + 1, :], (rc, tt)).astype(BF16)
            bc_ref[1, q, h] = jnp.broadcast_to(f_ref[h, q:q + 1, :], (rc, tt)).astype(BF16)
    zero = jnp.zeros((rc, tt), BF16)
    for q in range(eb // N_KEYS):
        for r0 in range(0, N_KEYS, rc):
            w = zero
            for h in range(PEER_HEADS):
                sel = jnp.where(r2_ref[h, r0:r0 + rc, :] < bc_ref[0, q, h], e2_ref[h, r0:r0 + rc, :], zero)
                w = w + sel * bc_ref[1, q, h]
            rows = slice(q * N_KEYS + r0, q * N_KEYS + r0 + rc)
            hid_ref[slot, rows, :] = _gelu(a_ref[rows, :]).astype(BF16) * w


def _peer(x1t, u_b, vt_b, lim, f, r2, e2, tt):
    d, t = x1t.shape
    n_exp = u_b.shape[0]
    eb = 512
    n_blocks = n_exp // eb
    keys_per_block = eb // N_KEYS
    lim = lim.reshape(PEER_HEADS, n_blocks, keys_per_block, t)
    f = f.reshape(PEER_HEADS, n_blocks, keys_per_block, t)
    row_spec = pl.BlockSpec((PEER_HEADS, None, keys_per_block, tt),
                            lambda i, j: (0, jnp.minimum(j, n_blocks - 1), 0, i))
    big_spec = pl.BlockSpec((PEER_HEADS, N_KEYS, tt), lambda i, j: (0, 0, i))
    return pl.pallas_call(
        functools.partial(_peer_kernel, eb=eb, tt=tt),
        out_shape=jax.ShapeDtypeStruct((d, t), F32),
        grid=(t // tt, n_blocks + 1),
        in_specs=[pl.BlockSpec((d, tt), lambda i, j: (0, i)),
                  pl.BlockSpec((eb, d), lambda i, j: (jnp.minimum(j, n_blocks - 1), 0)),
                  pl.BlockSpec((d, eb), lambda i, j: (0, jnp.maximum(j - 1, 0))),
                  row_spec, row_spec, big_spec, big_spec],
        out_specs=pl.BlockSpec((d, tt), lambda i, j: (0, i)),
        scratch_shapes=[pltpu.VMEM((eb, tt), F32), pltpu.VMEM((2, eb, tt), BF16),
                        pltpu.VMEM((2, keys_per_block, PEER_HEADS, 2 * SUBLANES, tt), BF16)],
        compiler_params=_params("parallel", "arbitrary"),
        name="peer",
    )(x1t, u_b, vt_b, lim, f, r2, e2)


def _ln2_kernel(x1_ref, y_ref, g_ref, b_ref, o_ref, *, alpha):
    o_ref[...] = _layer_norm(alpha * x1_ref[...] + y_ref[...], g_ref[...], b_ref[...])


def _ln2(x1, y, g, b, alpha):
    t, d = x1.shape
    tm = _tile(t, 512)
    row = lambda i: (i, 0)
    return pl.pallas_call(
        functools.partial(_ln2_kernel, alpha=alpha),
        out_shape=jax.ShapeDtypeStruct((t, d), F32),
        grid=(t // tm,),
        in_specs=[pl.BlockSpec((tm, d), row), pl.BlockSpec((tm, d), row), _full((1, d)), _full((1, d))],
        out_specs=pl.BlockSpec((tm, d), row),
        compiler_params=_params("parallel"),
        name="ln2",
    )(x1, y, g.reshape(1, d), b.reshape(1, d))


def _rel_bias_table(rel_bias):
    q_pos = jnp.arange(CHUNK)
    k_pos = jnp.arange(BAND) - ATT_WINDOW
    idx = jnp.clip(k_pos[None, :] - q_pos[:, None], -REL_CLIP, REL_CLIP) + REL_CLIP
    return rel_bias[:, idx]


def _encoder_layer(x, hist_k, hist_v, h0, wts, alpha):
    nb, seq, d = x.shape
    width = d // 2
    t = nb * seq
    n_gt = width // LANES
    assert seq % CHUNK == 0 and width % LANES == 0 and nb % SUBLANES == 0
    assert hist_k is None or seq == CHUNK
    x2d = x.reshape(t, d)

    qkv, kvu = _proj(x2d, wts["w_in"], width)
    k_new = kvu[:, :width].reshape(nb, seq, width)
    v_new = kvu[:, width:2 * width].reshape(nb, seq, width)

    n_hist = 0 if hist_k is None else hist_k.shape[1]
    n_invalid = ATT_WINDOW - n_hist
    def padded(new_b, hist):
        parts = [jnp.zeros((nb, n_invalid, width), BF16)] if n_invalid else []
        if hist is not None:
            parts.append(hist.reshape(nb, n_hist, width).astype(BF16))
        return jnp.concatenate(parts + [new_b.reshape(nb, seq, width)], axis=1)
    kpad = padded(qkv[:, width:2 * width], hist_k)
    vpad = padded(qkv[:, 2 * width:], hist_v)
    att_n = _attention(qkv, kpad, vpad, wts["bias"], wts["norm_attn_g"], nb, seq, width, n_invalid)

    u_tm = jnp.swapaxes(kvu[:, 2 * width:].reshape(nb, seq, width), 0, 1).reshape(t, width)
    ssm_tm, h_last = _ssm(u_tm, h0, wts["ssm"], nb, seq, width)
    ssm_n = jnp.swapaxes(ssm_tm.reshape(seq, nb, width), 0, 1).reshape(t, width)

    x1, x1b = _mix(att_n, ssm_n, x2d, wts["w_out"], wts["ln1_g"], wts["ln1_b"], alpha, width)

    tt = _tile(t, 512)
    lim, f, r2, e2 = _route(x1b, wts["peer_wq"], wts["peer_keys"], tt)
    y_t = _peer(x1b.T, wts["peer_u"], wts["peer_vt"], lim, f, r2, e2, tt)
    x2 = _ln2(x1, y_t.T, wts["ln2_g"], wts["ln2_b"], alpha)

    n_heads = width // HEAD_DIM
    return (x2.reshape(nb, seq, d), k_new.reshape(nb, seq, n_heads, HEAD_DIM),
            v_new.reshape(nb, seq, n_heads, HEAD_DIM), h_last)


def kernel(x_prompt, x_sample, cache_attn_k, cache_attn_v, state_ssm_re, state_ssm_im, w_in, rel_bias, norm_attn_g, ssm_a_re, ssm_a_im, ssm_log_dt, ssm_b_re, ssm_b_im, ssm_c_re, ssm_c_im, ssm_d, ssm_glu_w, ssm_glu_b, norm_ssm_g, w_out, ln1_g, ln1_b, peer_wq, peer_keys, peer_u, peer_v, ln2_g, ln2_b):
    depth = w_in.shape[0]
    alpha = (2.0 * depth) ** 0.25
    d = x_prompt.shape[-1]
    width = d // 2
    n_gt = width // LANES
    n_groups = width // SSM_GROUP
    keep = min(ATT_WINDOW, x_prompt.shape[1])

    yp, ys = x_prompt, x_sample
    outs = [[] for _ in range(8)]
    for l in range(depth):
        wts = dict(
            w_in=w_in[l].astype(BF16),
            bias=_rel_bias_table(rel_bias[l]),
            norm_attn_g=norm_attn_g[l].reshape(1, width),
            ssm=_ssm_params(ssm_a_re[l], ssm_a_im[l], ssm_log_dt[l], ssm_b_re[l], ssm_b_im[l], ssm_c_re[l],
                            ssm_c_im[l], ssm_d[l], ssm_glu_w[l], ssm_glu_b[l], norm_ssm_g[l], width),
            w_out=w_out[l].astype(BF16), ln1_g=ln1_g[l], ln1_b=ln1_b[l],
            peer_wq=peer_wq[l].astype(BF16), peer_keys=peer_keys[l].astype(BF16),
            peer_u=peer_u[l].astype(BF16), peer_vt=peer_v[l].astype(BF16).T,
            ln2_g=ln2_g[l], ln2_b=ln2_b[l])
        h0p = jnp.zeros((n_gt, x_prompt.shape[0], 2 * STATE_TILE), F32)
        yp, kp, vp, hp = _encoder_layer(yp, None, None, h0p, wts, alpha)
        h0s = _state_to_tiles(state_ssm_re[l].astype(F32), state_ssm_im[l].astype(F32), n_gt)
        ys, kn, vn, hs = _encoder_layer(ys, cache_attn_k[l], cache_attn_v[l], h0s, wts, alpha)
        hp_re, hp_im = _tiles_to_state(hp, n_groups)
        hs_re, hs_im = _tiles_to_state(hs, n_groups)
        for acc, val in zip(outs, (kp[:, -keep:], vp[:, -keep:], hp_re, hp_im, kn, vn, hs_re, hs_im)):
            acc.append(val)
    return (yp, ys) + tuple(jnp.stack(o) for o in outs)
```

```python
import functools

import jax
import jax.numpy as jnp
from jax import lax
from jax.experimental import pallas as pl
from jax.experimental.pallas import tpu as pltpu

F32 = jnp.float32
BF16 = jnp.bfloat16

CHUNK = 64
LEFT_CHUNKS = 8
ATT_WINDOW = LEFT_CHUNKS * CHUNK
BAND = ATT_WINDOW + CHUNK
HEAD_DIM = 64
REL_CLIP = 2 * CHUNK
SSM_GROUP = 16
SSM_STATE = 64
PEER_HEADS = 8
PEER_TOPK = 16
N_KEYS = 128
PEER_DHALF = 128
LN_EPS = 1e-5
RMS_EPS = 1e-6
MASK_VALUE = -1e30

LANES = 128
SUBLANES = 8
GROUPS_PER_TILE = LANES // SSM_GROUP
STATE_TILE = GROUPS_PER_TILE * SSM_STATE
VMEM_LIMIT = 56 * 1024 * 1024
EXPERT_BLOCK = 512
KEYS_PER_BLOCK = EXPERT_BLOCK // N_KEYS
PEER_TOKENS = 512

SORT16 = (
    (0, 13), (1, 12), (2, 15), (3, 14), (4, 8), (5, 6), (7, 11), (9, 10),
    (0, 5), (1, 7), (2, 9), (3, 4), (6, 13), (8, 14), (10, 15), (11, 12),
    (0, 1), (2, 3), (4, 5), (6, 8), (7, 9), (10, 11), (12, 13), (14, 15),
    (0, 2), (1, 3), (4, 10), (5, 11), (6, 7), (8, 9), (12, 14), (13, 15),
    (1, 2), (3, 12), (4, 6), (5, 7), (8, 10), (9, 11), (13, 14),
    (1, 4), (2, 6), (5, 8), (7, 10), (9, 13), (11, 14),
    (2, 4), (3, 6), (9, 12), (11, 13),
    (3, 5), (6, 8), (7, 9), (10, 12),
    (3, 4), (5, 6), (7, 8), (9, 10), (11, 12),
    (6, 7), (8, 9),
)


def _params(*semantics):
    return pltpu.CompilerParams(dimension_semantics=semantics, vmem_limit_bytes=VMEM_LIMIT)


def _tile(n, pref):
    t = min(n, pref)
    while n % t:
        t //= 2
    return t


def _gelu(x):
    return 0.5 * x * (1.0 + lax.erf(x * 0.7071067811865476))


def _full(shape):
    return pl.BlockSpec(shape, lambda *_: (0,) * len(shape))


def _proj_kernel(x_ref, w_ref, qkv_ref, kvu_ref, *, width):
    xb = x_ref[...].astype(BF16)
    for c in range(4):
        acc = jnp.dot(xb, w_ref[:, c * width:(c + 1) * width], preferred_element_type=F32)
        if c == 0:
            qkv_ref[:, :width] = (acc * HEAD_DIM ** -0.5).astype(BF16)
        else:
            kvu_ref[:, (c - 1) * width:c * width] = acc
            if c < 3:
                qkv_ref[:, c * width:(c + 1) * width] = acc.astype(BF16)


def _proj(x2d, w_in_b, width):
    t, d = x2d.shape
    tm = _tile(t, 256)
    return pl.pallas_call(
        functools.partial(_proj_kernel, width=width),
        out_shape=(jax.ShapeDtypeStruct((t, 3 * width), BF16),
                   jax.ShapeDtypeStruct((t, 3 * width), F32)),
        grid=(t // tm,),
        in_specs=[pl.BlockSpec((tm, d), lambda i: (i, 0)),
                  pl.BlockSpec((d, 4 * width), lambda i: (0, 0))],
        out_specs=(pl.BlockSpec((tm, 3 * width), lambda i: (i, 0)),
                   pl.BlockSpec((tm, 3 * width), lambda i: (i, 0))),
        compiler_params=_params("parallel"),
        name="proj",
    )(x2d, w_in_b)


def _attn_kernel(q_ref, k_ref, vt_ref, bias_ref, g_ref, o_ref, *, n_steps, cpb, n_invalid, width):
    nq = cpb * CHUNK
    nc = 2 * nq
    kb = ATT_WINDOW + nq
    lo = lax.broadcasted_iota(jnp.int32, (nq, LANES), 1) < HEAD_DIM
    krow = lax.broadcasted_iota(jnp.int32, (kb, nc), 0)
    nt_dims = (((1,), (1,)), ((), ()))

    def step(r0):
        tiles = []
        ssq = jnp.zeros((1, nq), F32)
        for hp in range(width // LANES):
            cols = slice(hp * LANES, (hp + 1) * LANES)
            q2 = q_ref[pl.ds(r0, nq), cols]
            qcat = jnp.concatenate([jnp.where(lo, q2, 0), jnp.where(lo, 0, q2)], axis=0)
            s = lax.dot_general(k_ref[pl.ds(r0, kb), cols], qcat, nt_dims, preferred_element_type=F32)
            s = s + bias_ref[hp]
            if n_invalid > 0:
                s = jnp.where(krow + r0 >= n_invalid, s, MASK_VALUE)
            m = jnp.max(s, axis=0, keepdims=True)
            e = jnp.exp(s - m)
            p = (e * (1.0 / jnp.sum(e, axis=0, keepdims=True))).astype(BF16)
            o_t = jnp.dot(vt_ref[cols, pl.ds(r0, kb)], p, preferred_element_type=F32)
            tile = jnp.concatenate([o_t[:HEAD_DIM, :nq], o_t[HEAD_DIM:, nq:]], axis=0)
            ssq = ssq + jnp.sum(tile * tile, axis=0, keepdims=True)
            tiles.append(tile)
        rinv = lax.rsqrt(ssq / width + RMS_EPS)
        for hp, tile in enumerate(tiles):
            rows = slice(hp * LANES, (hp + 1) * LANES)
            o_ref[rows, pl.ds(r0, nq)] = (tile * rinv * g_ref[rows, :nq]).astype(BF16)

    if n_steps == 1:
        step(0)
    else:
        def body(i, carry):
            step(pl.multiple_of(i * nq, nq))
            return carry
        lax.fori_loop(0, n_steps, body, 0)


def _attention(qkv, kpad, vtpad, bias, gain_b, nb, seq, width, cpb, n_invalid):
    lp = kpad.shape[1]
    nq = cpb * CHUNK
    return pl.pallas_call(
        functools.partial(_attn_kernel, n_steps=seq // nq, cpb=cpb, n_invalid=n_invalid, width=width),
        out_shape=jax.ShapeDtypeStruct((nb, width, seq), BF16),
        grid=(nb,),
        in_specs=[pl.BlockSpec((seq, width), lambda b: (b, 0)),
                  pl.BlockSpec((None, lp, width), lambda b: (b, 0, 0)),
                  pl.BlockSpec((None, width, lp), lambda b: (b, 0, 0)),
                  _full(bias.shape),
                  _full((width, LANES))],
        out_specs=pl.BlockSpec((None, width, seq), lambda b: (b, 0, 0)),
        compiler_params=_params("parallel"),
        name="attn",
    )(qkv, kpad, vtpad, bias, gain_b)


def _ssm_prep_kernel(are_ref, aim_ref, ldt_ref, bre_ref, bim_ref, abr_ref, abi_ref, bbr_ref, bbi_ref):
    a_re = are_ref[...]
    a_im = aim_ref[...]
    dt = jnp.exp(ldt_ref[...])
    mag = jnp.exp(a_re * dt)
    ang = a_im * dt
    ab_re = mag * jnp.cos(ang)
    ab_im = mag * jnp.sin(ang)
    n_re = ab_re - 1.0
    den = a_re * a_re + a_im * a_im
    c_re = (n_re * a_re + ab_im * a_im) / den
    c_im = (ab_im * a_re - n_re * a_im) / den
    abr_ref[...] = ab_re
    abi_ref[...] = ab_im
    b_re = bre_ref[...]
    b_im = bim_ref[...]
    bbr_ref[...] = c_re[:, None, :] * b_re - c_im[:, None, :] * b_im
    bbi_ref[...] = c_re[:, None, :] * b_im + c_im[:, None, :] * b_re


def _ssm_prep(a_re, a_im, log_dt, bt_re, bt_im):
    g, p = a_re.shape
    h = bt_re.shape[1]
    return pl.pallas_call(
        _ssm_prep_kernel,
        out_shape=(jax.ShapeDtypeStruct((g, p), F32), jax.ShapeDtypeStruct((g, p), F32),
                   jax.ShapeDtypeStruct((g, h, p), F32), jax.ShapeDtypeStruct((g, h, p), F32)),
        name="ssm_prep",
    )(a_re, a_im, log_dt.reshape(g, 1), bt_re, bt_im)


def _ssm_kernel(u_ref, h0_ref, are_ref, aim_ref, bre_ref, bim_ref, cre_ref, cim_ref, d_ref, wg_ref, bg_ref,
                gn_ref, o_ref, hlast_ref, hst_ref, bu_ref, y_ref, *, nb, tl, n_gt, width):
    sw = STATE_TILE

    @pl.when(pl.program_id(0) == 0)
    def _():
        hst_ref[...] = h0_ref[...]

    for gt in range(n_gt):
        cols = slice(gt * LANES, (gt + 1) * LANES)
        uf = u_ref[:, cols]
        ub = uf.astype(BF16)
        bu_ref[:, :sw] = jnp.dot(ub, bre_ref[gt], preferred_element_type=F32)
        bu_ref[:, sw:] = jnp.dot(ub, bim_ref[gt], preferred_element_type=F32)
        a_re = jnp.broadcast_to(are_ref[gt], (nb, sw))
        a_im = jnp.broadcast_to(aim_ref[gt], (nb, sw))

        def step(t, carry, a_re=a_re, a_im=a_im):
            h_re, h_im = carry
            r0 = pl.multiple_of(t * nb, nb)
            n_re = a_re * h_re - a_im * h_im + bu_ref[pl.ds(r0, nb), :sw]
            n_im = a_re * h_im + a_im * h_re + bu_ref[pl.ds(r0, nb), sw:]
            bu_ref[pl.ds(r0, nb), :sw] = n_re
            bu_ref[pl.ds(r0, nb), sw:] = n_im
            return n_re, n_im

        h_re, h_im = lax.fori_loop(0, tl, step, (hst_ref[gt, :, :sw], hst_ref[gt, :, sw:]))
        hst_ref[gt, :, :sw] = h_re
        hst_ref[gt, :, sw:] = h_im

        y = (jnp.dot(bu_ref[:, :sw].astype(BF16), cre_ref[gt], preferred_element_type=F32)
             - jnp.dot(bu_ref[:, sw:].astype(BF16), cim_ref[gt], preferred_element_type=F32)
             + d_ref[:, cols] * uf)
        y = _gelu(y)
        z = jnp.dot(y.astype(BF16), wg_ref[gt], preferred_element_type=F32) + bg_ref[:, cols]
        y_ref[:, cols] = y * jax.nn.sigmoid(z)

    yy = y_ref[...]
    ms = jnp.sum(yy * yy, axis=-1, keepdims=True) / width
    o_ref[...] = (yy * lax.rsqrt(ms + RMS_EPS) * gn_ref[...]).astype(BF16)
    hlast_ref[...] = hst_ref[...]


def _ssm(u_tm, h0, sp, nb, seq, width):
    n_gt = width // LANES
    tl = _tile(seq, 64)
    rows = tl * nb
    sw = STATE_TILE
    return pl.pallas_call(
        functools.partial(_ssm_kernel, nb=nb, tl=tl, n_gt=n_gt, width=width),
        out_shape=(jax.ShapeDtypeStruct((seq * nb, width), BF16),
                   jax.ShapeDtypeStruct((n_gt, nb, 2 * sw), F32)),
        grid=(seq // tl,),
        in_specs=[pl.BlockSpec((rows, width), lambda s: (s, 0)),
                  _full((n_gt, nb, 2 * sw)),
                  _full((n_gt, 1, sw)), _full((n_gt, 1, sw)),
                  _full((n_gt, LANES, sw)), _full((n_gt, LANES, sw)),
                  _full((n_gt, sw, LANES)), _full((n_gt, sw, LANES)),
                  _full((1, width)),
                  _full((n_gt, LANES, LANES)),
                  _full((1, width)), _full((1, width))],
        out_specs=(pl.BlockSpec((rows, width), lambda s: (s, 0)),
                   _full((n_gt, nb, 2 * sw))),
        scratch_shapes=[pltpu.VMEM((n_gt, nb, 2 * sw), F32),
                        pltpu.VMEM((rows, 2 * sw), F32),
                        pltpu.VMEM((rows, width), F32)],
        compiler_params=_params("arbitrary"),
        name="ssm",
    )(u_tm, h0, sp["a_re"], sp["a_im"], sp["b_re"], sp["b_im"], sp["c_re"], sp["c_im"], sp["d"],
      sp["wg"], sp["bg"], sp["gn"])


def _block_diag(m, n_gt):
    _, r, c = m.shape
    m4 = m.reshape(n_gt, GROUPS_PER_TILE, r, 1, c)
    eye = jnp.eye(GROUPS_PER_TILE, dtype=bool)[None, :, None, :, None]
    return jnp.where(eye, m4, 0).reshape(n_gt, GROUPS_PER_TILE * r, GROUPS_PER_TILE * c)


def _ssm_params(a_re, a_im, log_dt, b_re, b_im, c_re, c_im, d, glu_w, glu_b, gain, width):
    n_gt = width // LANES
    ab_re, ab_im, bb_re, bb_im = _ssm_prep(a_re, a_im, log_dt, jnp.swapaxes(b_re, 1, 2),
                                           jnp.swapaxes(b_im, 1, 2))
    return dict(
        a_re=ab_re.reshape(n_gt, 1, STATE_TILE), a_im=ab_im.reshape(n_gt, 1, STATE_TILE),
        b_re=_block_diag(bb_re, n_gt).astype(BF16), b_im=_block_diag(bb_im, n_gt).astype(BF16),
        c_re=_block_diag(jnp.swapaxes(c_re, 1, 2), n_gt).astype(BF16),
        c_im=_block_diag(jnp.swapaxes(c_im, 1, 2), n_gt).astype(BF16),
        d=d.reshape(1, width), wg=_block_diag(glu_w, n_gt).astype(BF16), bg=glu_b.reshape(1, width),
        gn=gain.reshape(1, width))


def _state_to_tiles(s_re, s_im, n_gt):
    nb = s_re.shape[0]
    def one(s):
        return jnp.swapaxes(s.reshape(nb, n_gt, STATE_TILE), 0, 1)
    return jnp.concatenate([one(s_re), one(s_im)], axis=-1)


def _tiles_to_state(h, n_groups):
    n_gt, nb, _ = h.shape
    def one(s):
        return jnp.swapaxes(s, 0, 1).reshape(nb, n_groups, SSM_STATE)
    return one(h[:, :, :STATE_TILE]), one(h[:, :, STATE_TILE:])


def _layer_norm(z, g, b):
    mu = jnp.mean(z, axis=-1, keepdims=True)
    zc = z - mu
    var = jnp.mean(zc * zc, axis=-1, keepdims=True)
    return zc * lax.rsqrt(var + LN_EPS) * g + b


def _mix_kernel(a_ref, s_ref, x_ref, wo_ref, g_ref, b_ref, x1_ref, x1b_ref, *, alpha, width):
    tn_dims = (((0,), (0,)), ((), ()))
    mix = (lax.dot_general(a_ref[...], wo_ref[:width], tn_dims, preferred_element_type=F32)
           + jnp.dot(s_ref[...], wo_ref[width:], preferred_element_type=F32))
    x1 = _layer_norm(alpha * x_ref[...] + mix, g_ref[...], b_ref[...])
    x1_ref[...] = x1
    x1b_ref[...] = x1.astype(BF16)


def _mix(att_t, ssm_n, x2d, w_out_b, g, b, alpha, width):
    t, d = x2d.shape
    nb, _, seq = att_t.shape
    tm = _tile(seq, 512)
    per_seq = seq // tm
    row = lambda i: (i, 0)
    return pl.pallas_call(
        functools.partial(_mix_kernel, alpha=alpha, width=width),
        out_shape=(jax.ShapeDtypeStruct((t, d), F32), jax.ShapeDtypeStruct((t, d), BF16)),
        grid=(t // tm,),
        in_specs=[pl.BlockSpec((None, width, tm), lambda i: (i // per_seq, 0, i % per_seq)),
                  pl.BlockSpec((tm, width), row),
                  pl.BlockSpec((tm, d), row), _full((2 * width, d)), _full((1, d)), _full((1, d))],
        out_specs=(pl.BlockSpec((tm, d), row), pl.BlockSpec((tm, d), row)),
        compiler_params=_params("parallel"),
        name="mix",
    )(att_t, ssm_n, x2d, w_out_b, g.reshape(1, d), b.reshape(1, d))


def _sort16(v):
    v = list(v)
    for i, j in SORT16:
        v[i], v[j] = jnp.maximum(v[i], v[j]), jnp.minimum(v[i], v[j])
    return v


def _merge_top16(a, b):
    v = [jnp.maximum(a[j], b[PEER_TOPK - 1 - j]) for j in range(PEER_TOPK)]
    for dist in (8, 4, 2, 1):
        for i in range(PEER_TOPK):
            if not i & dist:
                v[i], v[i + dist] = jnp.maximum(v[i], v[i + dist]), jnp.minimum(v[i], v[i + dist])
    return v


def _fold_sublanes(v):
    for shift in (4, 2, 1):
        v = _merge_top16(v, [pltpu.roll(x, shift, 0) for x in v])
    return v


def _route_kernel(x_ref, wq_ref, keys_ref, lim_ref, f_ref, r2_ref, e2_ref, *, tt, tc):
    qp = jnp.dot(x_ref[...], wq_ref[...], preferred_element_type=F32).astype(BF16)
    nt_dims = (((1,), (1,)), ((), ()))
    s_t = [lax.dot_general(keys_ref[0, c], qp[:, c * PEER_DHALF:(c + 1) * PEER_DHALF], nt_dims,
                           preferred_element_type=F32) for c in range(2)]
    n_blocks = N_KEYS // SUBLANES
    for l0 in range(0, tt, tc):
        cols = slice(l0, l0 + tc)
        sub = lax.broadcasted_iota(jnp.int32, (SUBLANES, tc), 0)
        blocks = [[s_t[c][SUBLANES * j:SUBLANES * (j + 1), cols] for j in range(n_blocks)] for c in range(2)]
        v1, v2 = (_fold_sublanes(_sort16(blocks[c])) for c in range(2))
        v1_lo, v1_hi = v1[0], v1[SUBLANES]
        for r in range(1, SUBLANES):
            v1_lo = jnp.where(sub == r, v1[r], v1_lo)
            v1_hi = jnp.where(sub == r, v1[SUBLANES + r], v1_hi)
        cand_lo = [v1_lo + v2[b] for b in range(PEER_TOPK)]
        cand_hi = [v1_hi + v2[b] for b in range(PEER_TOPK)]
        top_s = _fold_sublanes(_merge_top16(cand_lo, cand_hi))
        tau = top_s[PEER_TOPK - 1]
        z = jnp.ones((SUBLANES, tc), F32)
        for k in range(1, PEER_TOPK):
            z = z + jnp.exp(top_s[k] - top_s[0])
        zinv = 1.0 / z
        r2, e2 = [], []
        for j in range(n_blocks):
            s1, s2 = blocks[0][j], blocks[1][j]
            cnt = jnp.zeros((SUBLANES, tc), F32)
            rank = jnp.zeros((SUBLANES, tc), F32)
            for b in range(PEER_TOPK):
                cnt = cnt + jnp.where(s1 + v2[b] >= tau, 1.0, 0.0)
                rank = rank + jnp.where(v2[b] > s2, 1.0, 0.0)
            lim_j = jnp.where(s1 >= v1[PEER_TOPK - 1], cnt, 0.0)
            f_j = jnp.exp(s1 - v1[0]) * zinv
            for g in range(SUBLANES // KEYS_PER_BLOCK):
                rows = slice(g * KEYS_PER_BLOCK, (g + 1) * KEYS_PER_BLOCK)
                lim_ref[0, j * (SUBLANES // KEYS_PER_BLOCK) + g, :, cols] = lim_j[rows]
                f_ref[0, j * (SUBLANES // KEYS_PER_BLOCK) + g, :, cols] = f_j[rows]
            r2.append(rank)
            e2.append(jnp.exp(s2 - v2[0]))
        r2_ref[0, :, cols] = jnp.concatenate(r2, axis=0).astype(BF16)
        e2_ref[0, :, cols] = jnp.concatenate(e2, axis=0).astype(BF16)


def _route(x1b, wq_b, keys_b, tt):
    t, d = x1b.shape
    dk = 2 * PEER_DHALF
    tc = _tile(tt, 256)
    big_spec = pl.BlockSpec((1, N_KEYS, tt), lambda i, h: (h, 0, i))
    n_eb = N_KEYS // KEYS_PER_BLOCK
    row_shape = jax.ShapeDtypeStruct((PEER_HEADS, n_eb, KEYS_PER_BLOCK, t), F32)
    row_spec = pl.BlockSpec((1, n_eb, KEYS_PER_BLOCK, tt), lambda i, h: (h, 0, 0, i))
    return pl.pallas_call(
        functools.partial(_route_kernel, tt=tt, tc=tc),
        out_shape=(row_shape, row_shape,
                   jax.ShapeDtypeStruct((PEER_HEADS, N_KEYS, t), BF16),
                   jax.ShapeDtypeStruct((PEER_HEADS, N_KEYS, t), BF16)),
        grid=(t // tt, PEER_HEADS),
        in_specs=[pl.BlockSpec((tt, d), lambda i, h: (i, 0)),
                  pl.BlockSpec((d, dk), lambda i, h: (0, h)),
                  pl.BlockSpec((1, 2, N_KEYS, PEER_DHALF), lambda i, h: (h, 0, 0, 0))],
        out_specs=(row_spec, row_spec, big_spec, big_spec),
        compiler_params=_params("parallel", "parallel"),
        name="route",
    )(x1b, wq_b, keys_b)


def _peer_kernel(xt_ref, u_ref, vt_ref, lim_ref, f_ref, r2_ref, e2_ref, y_ref, a_ref, hid_ref, bc_ref,
                 *, tt):
    j = pl.program_id(1)
    slot = lax.rem(j, 2)
    rc = 2 * SUBLANES

    @pl.when(j == 0)
    def _():
        y_ref[...] = jnp.zeros_like(y_ref)
        hid_ref[1] = jnp.zeros(hid_ref.shape[1:], BF16)

    a_ref[...] = jnp.dot(u_ref[...], xt_ref[...], preferred_element_type=F32)
    y_ref[...] += jnp.dot(vt_ref[...], hid_ref[1 - slot], preferred_element_type=F32)
    for q in range(KEYS_PER_BLOCK):
        for h in range(PEER_HEADS):
            bc_ref[0, q, h] = jnp.broadcast_to(lim_ref[h, q:q + 1, :], (rc, tt)).astype(BF16)
            bc_ref[1, q, h] = jnp.broadcast_to(f_ref[h, q:q + 1, :], (rc, tt)).astype(BF16)
    zero = jnp.zeros((rc, tt), BF16)
    for q in range(KEYS_PER_BLOCK):
        for r0 in range(0, N_KEYS, rc):
            w = zero
            for h in range(PEER_HEADS):
                sel = jnp.where(r2_ref[h, r0:r0 + rc, :] < bc_ref[0, q, h], e2_ref[h, r0:r0 + rc, :], zero)
                w = w + sel * bc_ref[1, q, h]
            rows = slice(q * N_KEYS + r0, q * N_KEYS + r0 + rc)
            hid_ref[slot, rows, :] = _gelu(a_ref[rows, :]).astype(BF16) * w


def _peer(x1t, u_b, vt_b, lim, f, r2, e2, tt):
    d, t = x1t.shape
    eb = EXPERT_BLOCK
    n_blocks = u_b.shape[0] // eb
    row_spec = pl.BlockSpec((PEER_HEADS, None, KEYS_PER_BLOCK, tt),
                            lambda i, j: (0, jnp.minimum(j, n_blocks - 1), 0, i))
    big_spec = pl.BlockSpec((PEER_HEADS, N_KEYS, tt), lambda i, j: (0, 0, i))
    return pl.pallas_call(
        functools.partial(_peer_kernel, tt=tt),
        out_shape=jax.ShapeDtypeStruct((d, t), F32),
        grid=(t // tt, n_blocks + 1),
        in_specs=[pl.BlockSpec((d, tt), lambda i, j: (0, i)),
                  pl.BlockSpec((eb, d), lambda i, j: (jnp.minimum(j, n_blocks - 1), 0)),
                  pl.BlockSpec((None, d, eb), lambda i, j: (jnp.maximum(j - 1, 0), 0, 0)),
                  row_spec, row_spec, big_spec, big_spec],
        out_specs=pl.BlockSpec((d, tt), lambda i, j: (0, i)),
        scratch_shapes=[pltpu.VMEM((eb, tt), F32), pltpu.VMEM((2, eb, tt), BF16),
                        pltpu.VMEM((2, KEYS_PER_BLOCK, PEER_HEADS, 2 * SUBLANES, tt), BF16)],
        compiler_params=_params("parallel", "arbitrary"),
        name="peer",
    )(x1t, u_b, vt_b, lim, f, r2, e2)


def _ln2_kernel(x1_ref, y_ref, g_ref, b_ref, o_ref, *, alpha):
    o_ref[...] = _layer_norm(alpha * x1_ref[...] + y_ref[...], g_ref[...], b_ref[...])


def _ln2(x1, y, g, b, alpha):
    t, d = x1.shape
    tm = _tile(t, 512)
    row = lambda i: (i, 0)
    return pl.pallas_call(
        functools.partial(_ln2_kernel, alpha=alpha),
        out_shape=jax.ShapeDtypeStruct((t, d), F32),
        grid=(t // tm,),
        in_specs=[pl.BlockSpec((tm, d), row), pl.BlockSpec((tm, d), row), _full((1, d)), _full((1, d))],
        out_specs=pl.BlockSpec((tm, d), row),
        compiler_params=_params("parallel"),
        name="ln2",
    )(x1, y, g.reshape(1, d), b.reshape(1, d))


def _rel_bias_table(rel_bias):
    n_left = ATT_WINDOW - REL_CLIP + CHUNK - 1
    n_right = CHUNK - 1 - REL_CLIP
    ext = jnp.concatenate([jnp.repeat(rel_bias[:, :1], n_left, axis=1), rel_bias,
                           jnp.repeat(rel_bias[:, -1:], max(n_right, 0), axis=1)], axis=1)
    return jnp.stack([ext[:, CHUNK - 1 - i:CHUNK - 1 - i + BAND] for i in range(CHUNK)], axis=1)


def _step_bias(table, cpb):
    n_heads = table.shape[0]
    tab_t = jnp.swapaxes(table, 1, 2)
    per_chunk = [jnp.pad(tab_t, ((0, 0), (c * CHUNK, (cpb - 1 - c) * CHUNK), (0, 0)),
                         constant_values=MASK_VALUE) for c in range(cpb)]
    both = jnp.stack(per_chunk, axis=2)
    rows = both.shape[1]
    both = both.reshape(n_heads // 2, 2, rows, cpb * CHUNK)
    return jnp.swapaxes(both, 1, 2).reshape(n_heads // 2, rows, 2 * cpb * CHUNK)


def _encoder_layer(x, hist_k, hist_v, h0, wts, alpha):
    nb, seq, d = x.shape
    width = d // 2
    t = nb * seq
    n_gt = width // LANES
    assert seq % CHUNK == 0 and width % LANES == 0 and nb % SUBLANES == 0
    assert hist_k is None or seq == CHUNK
    x2d = x.reshape(t, d)

    qkv, kvu = _proj(x2d, wts["w_in"], width)
    k_new = kvu[:, :width].reshape(nb, seq, width)
    v_new = kvu[:, width:2 * width].reshape(nb, seq, width)

    n_hist = 0 if hist_k is None else hist_k.shape[1]
    n_invalid = ATT_WINDOW - n_hist
    def padded(new_b, hist):
        parts = [jnp.zeros((nb, n_invalid, width), BF16)] if n_invalid else []
        if hist is not None:
            parts.append(hist.reshape(nb, n_hist, width).astype(BF16))
        return jnp.concatenate(parts + [new_b.reshape(nb, seq, width)], axis=1)
    kpad = padded(qkv[:, width:2 * width], hist_k)
    vtpad = jnp.swapaxes(padded(qkv[:, 2 * width:], hist_v), 1, 2)
    n_chunks = seq // CHUNK
    cpb = 2 if n_chunks % 2 == 0 else 1
    assert cpb == 2 or n_chunks == 1
    att_t = _attention(qkv, kpad, vtpad, _step_bias(wts["bias"], cpb), wts["norm_attn_g"], nb, seq, width,
                       cpb, n_invalid)

    u_tm = jnp.swapaxes(kvu[:, 2 * width:].reshape(nb, seq, width), 0, 1).reshape(t, width)
    ssm_tm, h_last = _ssm(u_tm, h0, wts["ssm"], nb, seq, width)
    ssm_n = jnp.swapaxes(ssm_tm.reshape(seq, nb, width), 0, 1).reshape(t, width)

    x1, x1b = _mix(att_t, ssm_n, x2d, wts["w_out"], wts["ln1_g"], wts["ln1_b"], alpha, width)

    tt = _tile(t, PEER_TOKENS)
    lim, f, r2, e2 = _route(x1b, wts["peer_wq"], wts["peer_keys"], tt)
    y_t = _peer(x1b.T, wts["peer_u"], wts["peer_vt"], lim, f, r2, e2, tt)
    x2 = _ln2(x1, y_t.T, wts["ln2_g"], wts["ln2_b"], alpha)

    n_heads = width // HEAD_DIM
    return (x2.reshape(nb, seq, d), k_new.reshape(nb, seq, n_heads, HEAD_DIM),
            v_new.reshape(nb, seq, n_heads, HEAD_DIM), h_last)


def kernel(x_prompt, x_sample, cache_attn_k, cache_attn_v, state_ssm_re, state_ssm_im, w_in, rel_bias, norm_attn_g, ssm_a_re, ssm_a_im, ssm_log_dt, ssm_b_re, ssm_b_im, ssm_c_re, ssm_c_im, ssm_d, ssm_glu_w, ssm_glu_b, norm_ssm_g, w_out, ln1_g, ln1_b, peer_wq, peer_keys, peer_u, peer_v, ln2_g, ln2_b):
    depth = w_in.shape[0]
    alpha = (2.0 * depth) ** 0.25
    d = x_prompt.shape[-1]
    width = d // 2
    n_gt = width // LANES
    n_groups = width // SSM_GROUP
    keep = min(ATT_WINDOW, x_prompt.shape[1])

    yp, ys = x_prompt, x_sample
    outs = [[] for _ in range(8)]
    for l in range(depth):
        wts = dict(
            w_in=w_in[l].astype(BF16),
            bias=_rel_bias_table(rel_bias[l]),
            norm_attn_g=jnp.broadcast_to(norm_attn_g[l][:, None], (width, LANES)),
            ssm=_ssm_params(ssm_a_re[l], ssm_a_im[l], ssm_log_dt[l], ssm_b_re[l], ssm_b_im[l], ssm_c_re[l],
                            ssm_c_im[l], ssm_d[l], ssm_glu_w[l], ssm_glu_b[l], norm_ssm_g[l], width),
            w_out=w_out[l].astype(BF16), ln1_g=ln1_g[l], ln1_b=ln1_b[l],
            peer_wq=peer_wq[l].astype(BF16), peer_keys=peer_keys[l].astype(BF16),
            peer_u=peer_u[l].astype(BF16),
            peer_vt=jnp.swapaxes(peer_v[l].astype(BF16).reshape(-1, EXPERT_BLOCK, d), 1, 2),
            ln2_g=ln2_g[l], ln2_b=ln2_b[l])
        h0p = jnp.zeros((n_gt, x_prompt.shape[0], 2 * STATE_TILE), F32)
        yp, kp, vp, hp = _encoder_layer(yp, None, None, h0p, wts, alpha)
        h0s = _state_to_tiles(state_ssm_re[l].astype(F32), state_ssm_im[l].astype(F32), n_gt)
        ys, kn, vn, hs = _encoder_layer(ys, cache_attn_k[l], cache_attn_v[l], h0s, wts, alpha)
        hp_re, hp_im = _tiles_to_state(hp, n_groups)
        hs_re, hs_im = _tiles_to_state(hs, n_groups)
        for acc, val in zip(outs, (kp[:, -keep:], vp[:, -keep:], hp_re, hp_im, kn, vn, hs_re, hs_im)):
            acc.append(val)
    return (yp, ys) + tuple(jnp.stack(o) for o in outs)
```

```python
import functools

import jax
import jax.numpy as jnp
from jax import lax
from jax.experimental import pallas as pl
from jax.experimental.pallas import tpu as pltpu

F32 = jnp.float32
BF16 = jnp.bfloat16

CHUNK = 64
LEFT_CHUNKS = 8
ATT_WINDOW = LEFT_CHUNKS * CHUNK
BAND = ATT_WINDOW + CHUNK
HEAD_DIM = 64
REL_CLIP = 2 * CHUNK
SSM_GROUP = 16
SSM_STATE = 64
PEER_HEADS = 8
PEER_TOPK = 16
N_KEYS = 128
PEER_DHALF = 128
LN_EPS = 1e-5
RMS_EPS = 1e-6
MASK_VALUE = -1e30

LANES = 128
SUBLANES = 8
GROUPS_PER_TILE = LANES // SSM_GROUP
STATE_TILE = GROUPS_PER_TILE * SSM_STATE
VMEM_LIMIT = 56 * 1024 * 1024
EXPERT_BLOCK = 512
KEYS_PER_BLOCK = EXPERT_BLOCK // N_KEYS
PEER_TOKENS = 512

SORT16 = (
    (0, 13), (1, 12), (2, 15), (3, 14), (4, 8), (5, 6), (7, 11), (9, 10),
    (0, 5), (1, 7), (2, 9), (3, 4), (6, 13), (8, 14), (10, 15), (11, 12),
    (0, 1), (2, 3), (4, 5), (6, 8), (7, 9), (10, 11), (12, 13), (14, 15),
    (0, 2), (1, 3), (4, 10), (5, 11), (6, 7), (8, 9), (12, 14), (13, 15),
    (1, 2), (3, 12), (4, 6), (5, 7), (8, 10), (9, 11), (13, 14),
    (1, 4), (2, 6), (5, 8), (7, 10), (9, 13), (11, 14),
    (2, 4), (3, 6), (9, 12), (11, 13),
    (3, 5), (6, 8), (7, 9), (10, 12),
    (3, 4), (5, 6), (7, 8), (9, 10), (11, 12),
    (6, 7), (8, 9),
)


def _params(*semantics):
    return pltpu.CompilerParams(dimension_semantics=semantics, vmem_limit_bytes=VMEM_LIMIT)


def _tile(n, pref):
    t = min(n, pref)
    while n % t:
        t //= 2
    return t


def _gelu(x):
    return 0.5 * x * (1.0 + lax.erf(x * 0.7071067811865476))


def _full(shape):
    return pl.BlockSpec(shape, lambda *_: (0,) * len(shape))


def _proj_kernel(x_ref, w_ref, qkv_ref, kv_ref, u_ref, *, width):
    xb = x_ref[...].astype(BF16)
    for c in range(4):
        acc = jnp.dot(xb, w_ref[:, c * width:(c + 1) * width], preferred_element_type=F32)
        if c == 0:
            qkv_ref[:, :width] = (acc * HEAD_DIM ** -0.5).astype(BF16)
        elif c < 3:
            kv_ref[:, (c - 1) * width:c * width] = acc
            qkv_ref[:, c * width:(c + 1) * width] = acc.astype(BF16)
        else:
            u_ref[...] = acc


def _proj(x2d, w_in_b, nb, seq, width):
    t, d = x2d.shape
    tm = _tile(seq, 256)
    per_seq = seq // tm
    row = lambda i: (i, 0)
    return pl.pallas_call(
        functools.partial(_proj_kernel, width=width),
        out_shape=(jax.ShapeDtypeStruct((t, 3 * width), BF16),
                   jax.ShapeDtypeStruct((t, 2 * width), F32),
                   jax.ShapeDtypeStruct((seq, nb * width), F32)),
        grid=(t // tm,),
        in_specs=[pl.BlockSpec((tm, d), row),
                  pl.BlockSpec((d, 4 * width), lambda i: (0, 0))],
        out_specs=(pl.BlockSpec((tm, 3 * width), row),
                   pl.BlockSpec((tm, 2 * width), row),
                   pl.BlockSpec((tm, width), lambda i: (i % per_seq, i // per_seq))),
        compiler_params=_params("parallel"),
        name="proj",
    )(x2d, w_in_b)


def _attn_kernel(q_ref, k_ref, vt_ref, bias_ref, g_ref, o_ref, *, n_steps, cpb, n_invalid, width):
    nq = cpb * CHUNK
    nc = 2 * nq
    kb = ATT_WINDOW + nq
    lo = lax.broadcasted_iota(jnp.int32, (nq, LANES), 1) < HEAD_DIM
    krow = lax.broadcasted_iota(jnp.int32, (kb, nc), 0)
    nt_dims = (((1,), (1,)), ((), ()))

    def step(r0):
        tiles = []
        ssq = jnp.zeros((1, nq), F32)
        for hp in range(width // LANES):
            cols = slice(hp * LANES, (hp + 1) * LANES)
            q2 = q_ref[pl.ds(r0, nq), cols]
            qcat = jnp.concatenate([jnp.where(lo, q2, 0), jnp.where(lo, 0, q2)], axis=0)
            s = lax.dot_general(k_ref[pl.ds(r0, kb), cols], qcat, nt_dims, preferred_element_type=F32)
            s = s + bias_ref[hp]
            if n_invalid > 0:
                s = jnp.where(krow + r0 >= n_invalid, s, MASK_VALUE)
            m = jnp.max(s, axis=0, keepdims=True)
            e = jnp.exp(s - m)
            p = (e * (1.0 / jnp.sum(e, axis=0, keepdims=True))).astype(BF16)
            o_t = jnp.dot(vt_ref[cols, pl.ds(r0, kb)], p, preferred_element_type=F32)
            tile = jnp.concatenate([o_t[:HEAD_DIM, :nq], o_t[HEAD_DIM:, nq:]], axis=0)
            ssq = ssq + jnp.sum(tile * tile, axis=0, keepdims=True)
            tiles.append(tile)
        rinv = lax.rsqrt(ssq / width + RMS_EPS)
        for hp, tile in enumerate(tiles):
            rows = slice(hp * LANES, (hp + 1) * LANES)
            o_ref[rows, pl.ds(r0, nq)] = (tile * rinv * g_ref[rows, :nq]).astype(BF16)

    if n_steps == 1:
        step(0)
    else:
        def body(i, carry):
            step(pl.multiple_of(i * nq, nq))
            return carry
        lax.fori_loop(0, n_steps, body, 0)


def _attention(qkv, kpad, vtpad, bias, gain_b, nb, seq, width, cpb, n_invalid):
    lp = kpad.shape[1]
    nq = cpb * CHUNK
    return pl.pallas_call(
        functools.partial(_attn_kernel, n_steps=seq // nq, cpb=cpb, n_invalid=n_invalid, width=width),
        out_shape=jax.ShapeDtypeStruct((nb, width, seq), BF16),
        grid=(nb,),
        in_specs=[pl.BlockSpec((seq, width), lambda b: (b, 0)),
                  pl.BlockSpec((None, lp, width), lambda b: (b, 0, 0)),
                  pl.BlockSpec((None, width, lp), lambda b: (b, 0, 0)),
                  _full(bias.shape),
                  _full((width, LANES))],
        out_specs=pl.BlockSpec((None, width, seq), lambda b: (b, 0, 0)),
        compiler_params=_params("parallel"),
        name="attn",
    )(qkv, kpad, vtpad, bias, gain_b)


def _ssm_prep_kernel(are_ref, aim_ref, ldt_ref, bre_ref, bim_ref, abr_ref, abi_ref, bbr_ref, bbi_ref):
    a_re = are_ref[...]
    a_im = aim_ref[...]
    dt = jnp.exp(ldt_ref[...])
    mag = jnp.exp(a_re * dt)
    ang = a_im * dt
    ab_re = mag * jnp.cos(ang)
    ab_im = mag * jnp.sin(ang)
    n_re = ab_re - 1.0
    den = a_re * a_re + a_im * a_im
    c_re = (n_re * a_re + ab_im * a_im) / den
    c_im = (ab_im * a_re - n_re * a_im) / den
    abr_ref[...] = ab_re
    abi_ref[...] = ab_im
    b_re = bre_ref[...]
    b_im = bim_ref[...]
    bbr_ref[...] = c_re[:, None, :] * b_re - c_im[:, None, :] * b_im
    bbi_ref[...] = c_re[:, None, :] * b_im + c_im[:, None, :] * b_re


def _ssm_prep(a_re, a_im, log_dt, bt_re, bt_im):
    g, p = a_re.shape
    h = bt_re.shape[1]
    return pl.pallas_call(
        _ssm_prep_kernel,
        out_shape=(jax.ShapeDtypeStruct((g, p), F32), jax.ShapeDtypeStruct((g, p), F32),
                   jax.ShapeDtypeStruct((g, h, p), F32), jax.ShapeDtypeStruct((g, h, p), F32)),
        name="ssm_prep",
    )(a_re, a_im, log_dt.reshape(g, 1), bt_re, bt_im)


def _ssm_kernel(u_ref, h0_ref, are_ref, aim_ref, bre_ref, bim_ref, cre_ref, cim_ref, d_ref, wg_ref, bg_ref,
                gn_ref, o_ref, hlast_ref, hst_ref, bu_ref, y_ref, *, nb, tl, n_gt, width):
    sw = STATE_TILE

    @pl.when(pl.program_id(0) == 0)
    def _():
        hst_ref[...] = h0_ref[...]

    for gt in range(n_gt):
        cols = slice(gt * LANES, (gt + 1) * LANES)
        uf = u_ref[:, cols]
        ub = uf.astype(BF16)
        bu_ref[:, :sw] = jnp.dot(ub, bre_ref[gt], preferred_element_type=F32)
        bu_ref[:, sw:] = jnp.dot(ub, bim_ref[gt], preferred_element_type=F32)
        a_re = jnp.broadcast_to(are_ref[gt], (nb, sw))
        a_im = jnp.broadcast_to(aim_ref[gt], (nb, sw))

        def step(t, carry, a_re=a_re, a_im=a_im):
            h_re, h_im = carry
            r0 = pl.multiple_of(t * nb, nb)
            n_re = a_re * h_re - a_im * h_im + bu_ref[pl.ds(r0, nb), :sw]
            n_im = a_re * h_im + a_im * h_re + bu_ref[pl.ds(r0, nb), sw:]
            bu_ref[pl.ds(r0, nb), :sw] = n_re
            bu_ref[pl.ds(r0, nb), sw:] = n_im
            return n_re, n_im

        h_re, h_im = lax.fori_loop(0, tl, step, (hst_ref[gt, :, :sw], hst_ref[gt, :, sw:]))
        hst_ref[gt, :, :sw] = h_re
        hst_ref[gt, :, sw:] = h_im

        y = (jnp.dot(bu_ref[:, :sw].astype(BF16), cre_ref[gt], preferred_element_type=F32)
             - jnp.dot(bu_ref[:, sw:].astype(BF16), cim_ref[gt], preferred_element_type=F32)
             + d_ref[:, cols] * uf)
        y = _gelu(y)
        z = jnp.dot(y.astype(BF16), wg_ref[gt], preferred_element_type=F32) + bg_ref[:, cols]
        y_ref[:, cols] = y * jax.nn.sigmoid(z)

    yy = y_ref[...]
    ms = jnp.sum(yy * yy, axis=-1, keepdims=True) / width
    o_ref[...] = (yy * lax.rsqrt(ms + RMS_EPS) * gn_ref[...]).astype(BF16)
    hlast_ref[...] = hst_ref[...]


def _ssm(u_tm, h0, sp, nb, seq, width):
    n_gt = width // LANES
    tl = _tile(seq, 64)
    rows = tl * nb
    sw = STATE_TILE
    return pl.pallas_call(
        functools.partial(_ssm_kernel, nb=nb, tl=tl, n_gt=n_gt, width=width),
        out_shape=(jax.ShapeDtypeStruct((seq * nb, width), BF16),
                   jax.ShapeDtypeStruct((n_gt, nb, 2 * sw), F32)),
        grid=(seq // tl,),
        in_specs=[pl.BlockSpec((rows, width), lambda s: (s, 0)),
                  _full((n_gt, nb, 2 * sw)),
                  _full((n_gt, 1, sw)), _full((n_gt, 1, sw)),
                  _full((n_gt, LANES, sw)), _full((n_gt, LANES, sw)),
                  _full((n_gt, sw, LANES)), _full((n_gt, sw, LANES)),
                  _full((1, width)),
                  _full((n_gt, LANES, LANES)),
                  _full((1, width)), _full((1, width))],
        out_specs=(pl.BlockSpec((rows, width), lambda s: (s, 0)),
                   _full((n_gt, nb, 2 * sw))),
        scratch_shapes=[pltpu.VMEM((n_gt, nb, 2 * sw), F32),
                        pltpu.VMEM((rows, 2 * sw), F32),
                        pltpu.VMEM((rows, width), F32)],
        compiler_params=_params("arbitrary"),
        name="ssm",
    )(u_tm, h0, sp["a_re"], sp["a_im"], sp["b_re"], sp["b_im"], sp["c_re"], sp["c_im"], sp["d"],
      sp["wg"], sp["bg"], sp["gn"])


def _block_diag(m, n_gt):
    _, r, c = m.shape
    m4 = m.reshape(n_gt, GROUPS_PER_TILE, r, 1, c)
    eye = jnp.eye(GROUPS_PER_TILE, dtype=bool)[None, :, None, :, None]
    return jnp.where(eye, m4, 0).reshape(n_gt, GROUPS_PER_TILE * r, GROUPS_PER_TILE * c)


def _ssm_params(a_re, a_im, log_dt, b_re, b_im, c_re, c_im, d, glu_w, glu_b, gain, width):
    n_gt = width // LANES
    ab_re, ab_im, bb_re, bb_im = _ssm_prep(a_re, a_im, log_dt, jnp.swapaxes(b_re, 1, 2),
                                           jnp.swapaxes(b_im, 1, 2))
    return dict(
        a_re=ab_re.reshape(n_gt, 1, STATE_TILE), a_im=ab_im.reshape(n_gt, 1, STATE_TILE),
        b_re=_block_diag(bb_re, n_gt).astype(BF16), b_im=_block_diag(bb_im, n_gt).astype(BF16),
        c_re=_block_diag(jnp.swapaxes(c_re, 1, 2), n_gt).astype(BF16),
        c_im=_block_diag(jnp.swapaxes(c_im, 1, 2), n_gt).astype(BF16),
        d=d.reshape(1, width), wg=_block_diag(glu_w, n_gt).astype(BF16), bg=glu_b.reshape(1, width),
        gn=gain.reshape(1, width))


def _state_to_tiles(s_re, s_im, n_gt):
    nb = s_re.shape[0]
    def one(s):
        return jnp.swapaxes(s.reshape(nb, n_gt, STATE_TILE), 0, 1)
    return jnp.concatenate([one(s_re), one(s_im)], axis=-1)


def _tiles_to_state(h, n_groups):
    n_gt, nb, _ = h.shape
    def one(s):
        return jnp.swapaxes(s, 0, 1).reshape(nb, n_groups, SSM_STATE)
    return one(h[:, :, :STATE_TILE]), one(h[:, :, STATE_TILE:])


def _layer_norm(z, g, b):
    mu = jnp.mean(z, axis=-1, keepdims=True)
    zc = z - mu
    var = jnp.mean(zc * zc, axis=-1, keepdims=True)
    return zc * lax.rsqrt(var + LN_EPS) * g + b


def _mix_kernel(a_ref, s_ref, x_ref, wo_ref, g_ref, b_ref, x1_ref, x1b_ref, x1t_ref=None, *, alpha, width):
    tn_dims = (((0,), (0,)), ((), ()))
    mix = (lax.dot_general(a_ref[...], wo_ref[:width], tn_dims, preferred_element_type=F32)
           + jnp.dot(s_ref[...], wo_ref[width:], preferred_element_type=F32))
    x1 = _layer_norm(alpha * x_ref[...] + mix, g_ref[...], b_ref[...])
    x1_ref[...] = x1
    x1b_ref[...] = x1.astype(BF16)
    if x1t_ref is not None:
        x1t_ref[...] = x1.T.astype(BF16)


def _mix(att_t, ssm_tm, x2d, w_out_b, g, b, alpha, width):
    t, d = x2d.shape
    nb, _, seq = att_t.shape
    tm = _tile(seq, 512)
    per_seq = seq // tm
    row = lambda i: (i, 0)
    out_shape = [jax.ShapeDtypeStruct((t, d), F32), jax.ShapeDtypeStruct((t, d), BF16)]
    out_specs = [pl.BlockSpec((tm, d), row), pl.BlockSpec((tm, d), row)]
    if tm % LANES == 0:
        out_shape.append(jax.ShapeDtypeStruct((d, t), BF16))
        out_specs.append(pl.BlockSpec((d, tm), lambda i: (0, i)))
    outs = pl.pallas_call(
        functools.partial(_mix_kernel, alpha=alpha, width=width),
        out_shape=tuple(out_shape),
        grid=(t // tm,),
        in_specs=[pl.BlockSpec((None, width, tm), lambda i: (i // per_seq, 0, i % per_seq)),
                  pl.BlockSpec((tm, width), lambda i: (i % per_seq, i // per_seq)),
                  pl.BlockSpec((tm, d), row), _full((2 * width, d)), _full((1, d)), _full((1, d))],
        out_specs=tuple(out_specs),
        compiler_params=_params("parallel"),
        name="mix",
    )(att_t, ssm_tm, x2d, w_out_b, g.reshape(1, d), b.reshape(1, d))
    x1, x1b = outs[0], outs[1]
    return x1, x1b, (outs[2] if len(outs) == 3 else x1b.T)


def _sort16(v):
    v = list(v)
    for i, j in SORT16:
        v[i], v[j] = jnp.maximum(v[i], v[j]), jnp.minimum(v[i], v[j])
    return v


def _merge_top16(a, b):
    v = [jnp.maximum(a[j], b[PEER_TOPK - 1 - j]) for j in range(PEER_TOPK)]
    for dist in (8, 4, 2, 1):
        for i in range(PEER_TOPK):
            if not i & dist:
                v[i], v[i + dist] = jnp.maximum(v[i], v[i + dist]), jnp.minimum(v[i], v[i + dist])
    return v


def _fold_sublanes(v):
    for shift in (4, 2, 1):
        v = _merge_top16(v, [pltpu.roll(x, shift, 0) for x in v])
    return v


def _count_prefix(v, pred):
    assert len(v) == PEER_TOPK == 16
    full = pred(v[15])
    b3 = pred(v[7])
    b2 = pred(jnp.where(b3, v[11], v[3]))
    b1 = pred(jnp.where(b3, jnp.where(b2, v[13], v[9]), jnp.where(b2, v[5], v[1])))
    quad = [jnp.where(b1, v[4 * k + 2], v[4 * k]) for k in range(4)]
    b0 = pred(jnp.where(b3, jnp.where(b2, quad[3], quad[2]), jnp.where(b2, quad[1], quad[0])))
    n = (jnp.where(b3, 8.0, 0.0) + jnp.where(b2, 4.0, 0.0)) + (jnp.where(b1, 2.0, 0.0) + jnp.where(b0, 1.0, 0.0))
    return jnp.where(full, 16.0, n)


def _route_kernel(x_ref, wq_ref, keys_ref, lim_ref, f_ref, r2_ref, e2_ref, *, tt, tc):
    qp = jnp.dot(x_ref[...], wq_ref[...], preferred_element_type=F32).astype(BF16)
    nt_dims = (((1,), (1,)), ((), ()))
    s_t = [lax.dot_general(keys_ref[0, c], qp[:, c * PEER_DHALF:(c + 1) * PEER_DHALF], nt_dims,
                           preferred_element_type=F32) for c in range(2)]
    n_blocks = N_KEYS // SUBLANES
    for l0 in range(0, tt, tc):
        cols = slice(l0, l0 + tc)
        sub = lax.broadcasted_iota(jnp.int32, (SUBLANES, tc), 0)
        blocks = [[s_t[c][SUBLANES * j:SUBLANES * (j + 1), cols] for j in range(n_blocks)] for c in range(2)]
        v1, v2 = (_fold_sublanes(_sort16(blocks[c])) for c in range(2))
        v1_lo, v1_hi = v1[0], v1[SUBLANES]
        for r in range(1, SUBLANES):
            v1_lo = jnp.where(sub == r, v1[r], v1_lo)
            v1_hi = jnp.where(sub == r, v1[SUBLANES + r], v1_hi)
        cand_lo = [v1_lo + v2[b] for b in range(PEER_TOPK)]
        cand_hi = [v1_hi + v2[b] for b in range(PEER_TOPK)]
        top_s = _fold_sublanes(_merge_top16(cand_lo, cand_hi))
        tau = top_s[PEER_TOPK - 1]
        z = jnp.ones((SUBLANES, tc), F32)
        for k in range(1, PEER_TOPK):
            z = z + jnp.exp(top_s[k] - top_s[0])
        zinv = 1.0 / z
        r2, e2 = [], []
        for j in range(n_blocks):
            s1, s2 = blocks[0][j], blocks[1][j]
            cnt = _count_prefix(v2, lambda p, s1=s1: s1 + p >= tau)
            rank = _count_prefix(v2, lambda p, s2=s2: p > s2)
            lim_j = jnp.where(s1 >= v1[PEER_TOPK - 1], cnt, 0.0)
            f_j = jnp.exp(s1 - v1[0]) * zinv
            for g in range(SUBLANES // KEYS_PER_BLOCK):
                rows = slice(g * KEYS_PER_BLOCK, (g + 1) * KEYS_PER_BLOCK)
                lim_ref[0, j * (SUBLANES // KEYS_PER_BLOCK) + g, :, cols] = lim_j[rows]
                f_ref[0, j * (SUBLANES // KEYS_PER_BLOCK) + g, :, cols] = f_j[rows]
            r2.append(rank)
            e2.append(jnp.exp(s2 - v2[0]))
        r2_ref[0, :, cols] = jnp.concatenate(r2, axis=0).astype(BF16)
        e2_ref[0, :, cols] = jnp.concatenate(e2, axis=0).astype(BF16)


def _route(x1b, wq_b, keys_b, tt):
    t, d = x1b.shape
    dk = 2 * PEER_DHALF
    tc = _tile(tt, 256)
    big_spec = pl.BlockSpec((1, N_KEYS, tt), lambda i, h: (h, 0, i))
    n_eb = N_KEYS // KEYS_PER_BLOCK
    row_shape = jax.ShapeDtypeStruct((PEER_HEADS, n_eb, KEYS_PER_BLOCK, t), F32)
    row_spec = pl.BlockSpec((1, n_eb, KEYS_PER_BLOCK, tt), lambda i, h: (h, 0, 0, i))
    return pl.pallas_call(
        functools.partial(_route_kernel, tt=tt, tc=tc),
        out_shape=(row_shape, row_shape,
                   jax.ShapeDtypeStruct((PEER_HEADS, N_KEYS, t), BF16),
                   jax.ShapeDtypeStruct((PEER_HEADS, N_KEYS, t), BF16)),
        grid=(t // tt, PEER_HEADS),
        in_specs=[pl.BlockSpec((tt, d), lambda i, h: (i, 0)),
                  pl.BlockSpec((d, dk), lambda i, h: (0, h)),
                  pl.BlockSpec((1, 2, N_KEYS, PEER_DHALF), lambda i, h: (h, 0, 0, 0))],
        out_specs=(row_spec, row_spec, big_spec, big_spec),
        compiler_params=_params("parallel", "parallel"),
        name="route",
    )(x1b, wq_b, keys_b)


def _peer_kernel(xt_ref, u_ref, vt_ref, lim_ref, f_ref, r2_ref, e2_ref, y_ref, a_ref, hid_ref, bc_ref,
                 *, tt):
    j = pl.program_id(1)
    slot = lax.rem(j, 2)
    rc = 2 * SUBLANES

    @pl.when(j == 0)
    def _():
        y_ref[...] = jnp.zeros_like(y_ref)
        hid_ref[1] = jnp.zeros(hid_ref.shape[1:], BF16)

    a_ref[...] = jnp.dot(u_ref[...], xt_ref[...], preferred_element_type=F32)
    y_ref[...] += jnp.dot(vt_ref[...], hid_ref[1 - slot], preferred_element_type=F32)
    for q in range(KEYS_PER_BLOCK):
        for h in range(PEER_HEADS):
            bc_ref[0, q, h] = jnp.broadcast_to(lim_ref[h, q:q + 1, :], (rc, tt)).astype(BF16)
            bc_ref[1, q, h] = jnp.broadcast_to(f_ref[h, q:q + 1, :], (rc, tt)).astype(BF16)
    zero = jnp.zeros((rc, tt), BF16)
    for q in range(KEYS_PER_BLOCK):
        for r0 in range(0, N_KEYS, rc):
            w = zero
            for h in range(PEER_HEADS):
                sel = jnp.where(r2_ref[h, r0:r0 + rc, :] < bc_ref[0, q, h], e2_ref[h, r0:r0 + rc, :], zero)
                w = w + sel * bc_ref[1, q, h]
            rows = slice(q * N_KEYS + r0, q * N_KEYS + r0 + rc)
            hid_ref[slot, rows, :] = _gelu(a_ref[rows, :]).astype(BF16) * w


def _peer(x1t, u_b, vt_b, lim, f, r2, e2, tt):
    d, t = x1t.shape
    eb = EXPERT_BLOCK
    n_blocks = u_b.shape[0] // eb
    row_spec = pl.BlockSpec((PEER_HEADS, None, KEYS_PER_BLOCK, tt),
                            lambda i, j: (0, jnp.minimum(j, n_blocks - 1), 0, i))
    big_spec = pl.BlockSpec((PEER_HEADS, N_KEYS, tt), lambda i, j: (0, 0, i))
    return pl.pallas_call(
        functools.partial(_peer_kernel, tt=tt),
        out_shape=jax.ShapeDtypeStruct((d, t), F32),
        grid=(t // tt, n_blocks + 1),
        in_specs=[pl.BlockSpec((d, tt), lambda i, j: (0, i)),
                  pl.BlockSpec((eb, d), lambda i, j: (jnp.minimum(j, n_blocks - 1), 0)),
                  pl.BlockSpec((None, d, eb), lambda i, j: (jnp.maximum(j - 1, 0), 0, 0)),
                  row_spec, row_spec, big_spec, big_spec],
        out_specs=pl.BlockSpec((d, tt), lambda i, j: (0, i)),
        scratch_shapes=[pltpu.VMEM((eb, tt), F32), pltpu.VMEM((2, eb, tt), BF16),
                        pltpu.VMEM((2, KEYS_PER_BLOCK, PEER_HEADS, 2 * SUBLANES, tt), BF16)],
        compiler_params=_params("parallel", "arbitrary"),
        name="peer",
    )(x1t, u_b, vt_b, lim, f, r2, e2)


def _ln2_kernel(x1_ref, yt_ref, g_ref, b_ref, o_ref, *, alpha):
    o_ref[...] = _layer_norm(alpha * x1_ref[...] + yt_ref[...].T, g_ref[...], b_ref[...])


def _ln2(x1, y_t, g, b, alpha):
    t, d = x1.shape
    tm = _tile(t, 512)
    row = lambda i: (i, 0)
    return pl.pallas_call(
        functools.partial(_ln2_kernel, alpha=alpha),
        out_shape=jax.ShapeDtypeStruct((t, d), F32),
        grid=(t // tm,),
        in_specs=[pl.BlockSpec((tm, d), row), pl.BlockSpec((d, tm), lambda i: (0, i)), _full((1, d)),
                  _full((1, d))],
        out_specs=pl.BlockSpec((tm, d), row),
        compiler_params=_params("parallel"),
        name="ln2",
    )(x1, y_t, g.reshape(1, d), b.reshape(1, d))


def _rel_bias_table(rel_bias):
    n_left = ATT_WINDOW - REL_CLIP + CHUNK - 1
    n_right = CHUNK - 1 - REL_CLIP
    ext = jnp.concatenate([jnp.repeat(rel_bias[:, :1], n_left, axis=1), rel_bias,
                           jnp.repeat(rel_bias[:, -1:], max(n_right, 0), axis=1)], axis=1)
    return jnp.stack([ext[:, CHUNK - 1 - i:CHUNK - 1 - i + BAND] for i in range(CHUNK)], axis=1)


def _step_bias(table, cpb):
    n_heads = table.shape[0]
    tab_t = jnp.swapaxes(table, 1, 2)
    per_chunk = [jnp.pad(tab_t, ((0, 0), (c * CHUNK, (cpb - 1 - c) * CHUNK), (0, 0)),
                         constant_values=MASK_VALUE) for c in range(cpb)]
    both = jnp.stack(per_chunk, axis=2)
    rows = both.shape[1]
    both = both.reshape(n_heads // 2, 2, rows, cpb * CHUNK)
    return jnp.swapaxes(both, 1, 2).reshape(n_heads // 2, rows, 2 * cpb * CHUNK)


def _encoder_layer(x, hist_k, hist_v, h0, wts, alpha):
    nb, seq, d = x.shape
    width = d // 2
    t = nb * seq
    n_gt = width // LANES
    assert seq % CHUNK == 0 and width % LANES == 0 and nb % SUBLANES == 0
    assert hist_k is None or seq == CHUNK
    x2d = x.reshape(t, d)

    qkv, kv, u_tm = _proj(x2d, wts["w_in"], nb, seq, width)
    k_new = kv[:, :width].reshape(nb, seq, width)
    v_new = kv[:, width:].reshape(nb, seq, width)

    n_hist = 0 if hist_k is None else hist_k.shape[1]
    n_invalid = ATT_WINDOW - n_hist
    def padded(new_b, hist):
        parts = [jnp.zeros((nb, n_invalid, width), BF16)] if n_invalid else []
        if hist is not None:
            parts.append(hist.reshape(nb, n_hist, width).astype(BF16))
        return jnp.concatenate(parts + [new_b.reshape(nb, seq, width)], axis=1)
    kpad = padded(qkv[:, width:2 * width], hist_k)
    vtpad = jnp.swapaxes(padded(qkv[:, 2 * width:], hist_v), 1, 2)
    n_chunks = seq // CHUNK
    cpb = 2 if n_chunks % 2 == 0 else 1
    assert cpb == 2 or n_chunks == 1
    att_t = _attention(qkv, kpad, vtpad, _step_bias(wts["bias"], cpb), wts["norm_attn_g"], nb, seq, width,
                       cpb, n_invalid)

    ssm_tm, h_last = _ssm(u_tm.reshape(t, width), h0, wts["ssm"], nb, seq, width)

    x1, x1b, x1t = _mix(att_t, ssm_tm.reshape(seq, nb * width), x2d, wts["w_out"], wts["ln1_g"],
                        wts["ln1_b"], alpha, width)

    tt = _tile(t, PEER_TOKENS)
    lim, f, r2, e2 = _route(x1b, wts["peer_wq"], wts["peer_keys"], tt)
    y_t = _peer(x1t, wts["peer_u"], wts["peer_vt"], lim, f, r2, e2, tt)
    x2 = _ln2(x1, y_t, wts["ln2_g"], wts["ln2_b"], alpha)

    n_heads = width // HEAD_DIM
    return (x2.reshape(nb, seq, d), k_new.reshape(nb, seq, n_heads, HEAD_DIM),
            v_new.reshape(nb, seq, n_heads, HEAD_DIM), h_last)


def kernel(x_prompt, x_sample, cache_attn_k, cache_attn_v, state_ssm_re, state_ssm_im, w_in, rel_bias, norm_attn_g, ssm_a_re, ssm_a_im, ssm_log_dt, ssm_b_re, ssm_b_im, ssm_c_re, ssm_c_im, ssm_d, ssm_glu_w, ssm_glu_b, norm_ssm_g, w_out, ln1_g, ln1_b, peer_wq, peer_keys, peer_u, peer_v, ln2_g, ln2_b):
    depth = w_in.shape[0]
    alpha = (2.0 * depth) ** 0.25
    d = x_prompt.shape[-1]
    width = d // 2
    n_gt = width // LANES
    n_groups = width // SSM_GROUP
    keep = min(ATT_WINDOW, x_prompt.shape[1])

    yp, ys = x_prompt, x_sample
    outs = [[] for _ in range(8)]
    for l in range(depth):
        wts = dict(
            w_in=w_in[l].astype(BF16),
            bias=_rel_bias_table(rel_bias[l]),
            norm_attn_g=jnp.broadcast_to(norm_attn_g[l][:, None], (width, LANES)),
            ssm=_ssm_params(ssm_a_re[l], ssm_a_im[l], ssm_log_dt[l], ssm_b_re[l], ssm_b_im[l], ssm_c_re[l],
                            ssm_c_im[l], ssm_d[l], ssm_glu_w[l], ssm_glu_b[l], norm_ssm_g[l], width),
            w_out=w_out[l].astype(BF16), ln1_g=ln1_g[l], ln1_b=ln1_b[l],
            peer_wq=peer_wq[l].astype(BF16), peer_keys=peer_keys[l].astype(BF16),
            peer_u=peer_u[l].astype(BF16),
            peer_vt=jnp.swapaxes(peer_v[l].astype(BF16).reshape(-1, EXPERT_BLOCK, d), 1, 2),
            ln2_g=ln2_g[l], ln2_b=ln2_b[l])
        h0p = jnp.zeros((n_gt, x_prompt.shape[0], 2 * STATE_TILE), F32)
        yp, kp, vp, hp = _encoder_layer(yp, None, None, h0p, wts, alpha)
        h0s = _state_to_tiles(state_ssm_re[l].astype(F32), state_ssm_im[l].astype(F32), n_gt)
        ys, kn, vn, hs = _encoder_layer(ys, cache_attn_k[l], cache_attn_v[l], h0s, wts, alpha)
        hp_re, hp_im = _tiles_to_state(hp, n_groups)
        hs_re, hs_im = _tiles_to_state(hs, n_groups)
        for acc, val in zip(outs, (kp[:, -keep:], vp[:, -keep:], hp_re, hp_im, kn, vn, hs_re, hs_im)):
            acc.append(val)
    return (yp, ys) + tuple(jnp.stack(o) for o in outs)
```

```python
import functools

import jax
import jax.numpy as jnp
from jax import lax
from jax.experimental import pallas as pl
from jax.experimental.pallas import tpu as pltpu

F32 = jnp.float32
BF16 = jnp.bfloat16

CHUNK = 64
LEFT_CHUNKS = 8
ATT_WINDOW = LEFT_CHUNKS * CHUNK
BAND = ATT_WINDOW + CHUNK
HEAD_DIM = 64
REL_CLIP = 2 * CHUNK
SSM_GROUP = 16
SSM_STATE = 64
PEER_HEADS = 8
PEER_TOPK = 16
N_KEYS = 128
PEER_DHALF = 128
LN_EPS = 1e-5
RMS_EPS = 1e-6
MASK_VALUE = -1e30

LANES = 128
SUBLANES = 8
GROUPS_PER_TILE = LANES // SSM_GROUP
STATE_TILE = GROUPS_PER_TILE * SSM_STATE
VMEM_LIMIT = 56 * 1024 * 1024
EXPERT_BLOCK = 512
KEYS_PER_BLOCK = EXPERT_BLOCK // N_KEYS
PEER_TOKENS = 512

SORT16 = (
    (0, 13), (1, 12), (2, 15), (3, 14), (4, 8), (5, 6), (7, 11), (9, 10),
    (0, 5), (1, 7), (2, 9), (3, 4), (6, 13), (8, 14), (10, 15), (11, 12),
    (0, 1), (2, 3), (4, 5), (6, 8), (7, 9), (10, 11), (12, 13), (14, 15),
    (0, 2), (1, 3), (4, 10), (5, 11), (6, 7), (8, 9), (12, 14), (13, 15),
    (1, 2), (3, 12), (4, 6), (5, 7), (8, 10), (9, 11), (13, 14),
    (1, 4), (2, 6), (5, 8), (7, 10), (9, 13), (11, 14),
    (2, 4), (3, 6), (9, 12), (11, 13),
    (3, 5), (6, 8), (7, 9), (10, 12),
    (3, 4), (5, 6), (7, 8), (9, 10), (11, 12),
    (6, 7), (8, 9),
)


def _params(*semantics):
    return pltpu.CompilerParams(dimension_semantics=semantics, vmem_limit_bytes=VMEM_LIMIT)


def _tile(n, pref):
    t = min(n, pref)
    while n % t:
        t //= 2
    return t


def _gelu(x):
    return 0.5 * x * (1.0 + lax.erf(x * 0.7071067811865476))


def _full(shape):
    return pl.BlockSpec(shape, lambda *_: (0,) * len(shape))


def _proj_kernel(x_ref, w_ref, kbase_ref, vtbase_ref, q_ref, kpad_ref, vtpad_ref, kf_ref, vf_ref, u_ref,
                 *, width):
    del kbase_ref, vtbase_ref
    xb = x_ref[...].astype(BF16)
    acc = [jnp.dot(xb, w_ref[:, c * width:(c + 1) * width], preferred_element_type=F32) for c in range(4)]
    q_ref[...] = (acc[0] * HEAD_DIM ** -0.5).astype(BF16)
    kf_ref[...] = acc[1]
    kpad_ref[...] = acc[1].astype(BF16)
    vf_ref[...] = acc[2]
    vtpad_ref[...] = acc[2].T.astype(BF16)
    u_ref[...] = acc[3]


def _proj(x2d, w_in_b, kbase, vtbase, nb, seq, width):
    t, d = x2d.shape
    tm = _tile(seq, 256)
    assert tm % LANES == 0 and ATT_WINDOW % tm == 0
    per_seq = seq // tm
    skip = ATT_WINDOW // tm
    row = lambda i: (i, 0)
    rows_f32 = jax.ShapeDtypeStruct((t, width), F32)
    return pl.pallas_call(
        functools.partial(_proj_kernel, width=width),
        out_shape=(jax.ShapeDtypeStruct((t, width), BF16),
                   jax.ShapeDtypeStruct(kbase.shape, BF16), jax.ShapeDtypeStruct(vtbase.shape, BF16),
                   rows_f32, rows_f32, jax.ShapeDtypeStruct((seq, nb * width), F32)),
        grid=(t // tm,),
        in_specs=[pl.BlockSpec((tm, d), row),
                  pl.BlockSpec((d, 4 * width), lambda i: (0, 0)),
                  pl.BlockSpec(memory_space=pl.ANY), pl.BlockSpec(memory_space=pl.ANY)],
        out_specs=(pl.BlockSpec((tm, width), row),
                   pl.BlockSpec((None, tm, width), lambda i: (i // per_seq, skip + i % per_seq, 0)),
                   pl.BlockSpec((None, width, tm), lambda i: (i // per_seq, 0, skip + i % per_seq)),
                   pl.BlockSpec((tm, width), row), pl.BlockSpec((tm, width), row),
                   pl.BlockSpec((tm, width), lambda i: (i % per_seq, i // per_seq))),
        input_output_aliases={2: 1, 3: 2},
        compiler_params=_params("parallel"),
        name="proj",
    )(x2d, w_in_b, kbase, vtbase)


def _proj_short_kernel(x_ref, w_ref, q_ref, k_ref, v_ref, kf_ref, vf_ref, u_ref, *, width):
    xb = x_ref[...].astype(BF16)
    acc = [jnp.dot(xb, w_ref[:, c * width:(c + 1) * width], preferred_element_type=F32) for c in range(4)]
    q_ref[...] = (acc[0] * HEAD_DIM ** -0.5).astype(BF16)
    kf_ref[...] = acc[1]
    k_ref[...] = acc[1].astype(BF16)
    vf_ref[...] = acc[2]
    v_ref[...] = acc[2].astype(BF16)
    u_ref[...] = acc[3]


def _proj_short(x2d, w_in_b, nb, seq, width):
    t, d = x2d.shape
    tm = _tile(seq, 256)
    per_seq = seq // tm
    row = lambda i: (i, 0)
    rows_f32 = jax.ShapeDtypeStruct((t, width), F32)
    rows_b16 = jax.ShapeDtypeStruct((t, width), BF16)
    return pl.pallas_call(
        functools.partial(_proj_short_kernel, width=width),
        out_shape=(rows_b16, rows_b16, rows_b16, rows_f32, rows_f32,
                   jax.ShapeDtypeStruct((seq, nb * width), F32)),
        grid=(t // tm,),
        in_specs=[pl.BlockSpec((tm, d), row),
                  pl.BlockSpec((d, 4 * width), lambda i: (0, 0))],
        out_specs=tuple([pl.BlockSpec((tm, width), row)] * 5
                        + [pl.BlockSpec((tm, width), lambda i: (i % per_seq, i // per_seq))]),
        compiler_params=_params("parallel"),
        name="proj",
    )(x2d, w_in_b)


def _attn_kernel(q_ref, k_ref, vt_ref, bias_ref, g_ref, o_ref, *, n_steps, cpb, n_invalid, width):
    nq = cpb * CHUNK
    nc = 2 * nq
    kb = ATT_WINDOW + nq
    lo = lax.broadcasted_iota(jnp.int32, (nq, LANES), 1) < HEAD_DIM
    krow = lax.broadcasted_iota(jnp.int32, (kb, nc), 0)
    nt_dims = (((1,), (1,)), ((), ()))

    def step(r0, masked):
        tiles = []
        ssq = jnp.zeros((1, nq), F32)
        for hp in range(width // LANES):
            cols = slice(hp * LANES, (hp + 1) * LANES)
            q2 = q_ref[pl.ds(r0, nq), cols]
            qcat = jnp.concatenate([jnp.where(lo, q2, 0), jnp.where(lo, 0, q2)], axis=0)
            s = lax.dot_general(k_ref[pl.ds(r0, kb), cols], qcat, nt_dims, preferred_element_type=F32)
            s = s + bias_ref[hp]
            if masked:
                s = jnp.where(krow + r0 >= n_invalid, s, MASK_VALUE)
            m = jnp.max(s, axis=0, keepdims=True)
            e = jnp.exp(s - m)
            rl = 1.0 / jnp.sum(e, axis=0, keepdims=True)
            o_t = jnp.dot(vt_ref[cols, pl.ds(r0, kb)], e.astype(BF16), preferred_element_type=F32) * rl
            tile = jnp.concatenate([o_t[:HEAD_DIM, :nq], o_t[HEAD_DIM:, nq:]], axis=0)
            ssq = ssq + jnp.sum(tile * tile, axis=0, keepdims=True)
            tiles.append(tile)
        rinv = lax.rsqrt(ssq / width + RMS_EPS)
        for hp, tile in enumerate(tiles):
            rows = slice(hp * LANES, (hp + 1) * LANES)
            o_ref[rows, pl.ds(r0, nq)] = (tile * rinv * g_ref[rows, :nq]).astype(BF16)

    n_masked = min(n_steps, -(-n_invalid // nq))
    if n_steps == 1:
        step(0, n_masked > 0)
    else:
        def body(i, carry, masked):
            step(pl.multiple_of(i * nq, nq), masked)
            return carry
        lax.fori_loop(0, n_masked, functools.partial(body, masked=True), 0)
        lax.fori_loop(n_masked, n_steps, functools.partial(body, masked=False), 0)


def _attention(qkv, kpad, vtpad, bias, gain_b, nb, seq, width, cpb, n_invalid):
    lp = kpad.shape[1]
    nq = cpb * CHUNK
    return pl.pallas_call(
        functools.partial(_attn_kernel, n_steps=seq // nq, cpb=cpb, n_invalid=n_invalid, width=width),
        out_shape=jax.ShapeDtypeStruct((nb, width, seq), BF16),
        grid=(nb,),
        in_specs=[pl.BlockSpec((seq, width), lambda b: (b, 0)),
                  pl.BlockSpec((None, lp, width), lambda b: (b, 0, 0)),
                  pl.BlockSpec((None, width, lp), lambda b: (b, 0, 0)),
                  _full(bias.shape),
                  _full((width, LANES))],
        out_specs=pl.BlockSpec((None, width, seq), lambda b: (b, 0, 0)),
        compiler_params=_params("parallel"),
        name="attn",
    )(qkv, kpad, vtpad, bias, gain_b)


def _ssm_prep_kernel(are_ref, aim_ref, ldt_ref, bre_ref, bim_ref, abr_ref, abi_ref, bbr_ref, bbi_ref):
    a_re = are_ref[...]
    a_im = aim_ref[...]
    dt = jnp.exp(ldt_ref[...])
    mag = jnp.exp(a_re * dt)
    ang = a_im * dt
    ab_re = mag * jnp.cos(ang)
    ab_im = mag * jnp.sin(ang)
    n_re = ab_re - 1.0
    den = a_re * a_re + a_im * a_im
    c_re = (n_re * a_re + ab_im * a_im) / den
    c_im = (ab_im * a_re - n_re * a_im) / den
    abr_ref[...] = ab_re
    abi_ref[...] = ab_im
    b_re = bre_ref[...]
    b_im = bim_ref[...]
    bbr_ref[...] = c_re[:, None, :] * b_re - c_im[:, None, :] * b_im
    bbi_ref[...] = c_re[:, None, :] * b_im + c_im[:, None, :] * b_re


def _ssm_prep(a_re, a_im, log_dt, bt_re, bt_im):
    g, p = a_re.shape
    h = bt_re.shape[1]
    return pl.pallas_call(
        _ssm_prep_kernel,
        out_shape=(jax.ShapeDtypeStruct((g, p), F32), jax.ShapeDtypeStruct((g, p), F32),
                   jax.ShapeDtypeStruct((g, h, p), F32), jax.ShapeDtypeStruct((g, h, p), F32)),
        name="ssm_prep",
    )(a_re, a_im, log_dt.reshape(g, 1), bt_re, bt_im)


def _ssm_kernel(u_ref, h0_ref, are_ref, aim_ref, bre_ref, bim_ref, cre_ref, cim_ref, d_ref, wg_ref, bg_ref,
                gn_ref, o_ref, hlast_ref, hst_ref, bu_ref, y_ref, *, nb, tl, n_gt, width):
    sw = STATE_TILE

    @pl.when(pl.program_id(0) == 0)
    def _():
        hst_ref[...] = h0_ref[...]

    for gt in range(n_gt):
        cols = slice(gt * LANES, (gt + 1) * LANES)
        uf = u_ref[:, cols]
        ub = uf.astype(BF16)
        bu_ref[:, :sw] = jnp.dot(ub, bre_ref[gt], preferred_element_type=F32)
        bu_ref[:, sw:] = jnp.dot(ub, bim_ref[gt], preferred_element_type=F32)
        a_re = jnp.broadcast_to(are_ref[gt], (nb, sw))
        a_im = jnp.broadcast_to(aim_ref[gt], (nb, sw))

        def step(t, carry, a_re=a_re, a_im=a_im):
            h_re, h_im = carry
            r0 = pl.multiple_of(t * nb, nb)
            n_re = a_re * h_re - a_im * h_im + bu_ref[pl.ds(r0, nb), :sw]
            n_im = a_re * h_im + a_im * h_re + bu_ref[pl.ds(r0, nb), sw:]
            bu_ref[pl.ds(r0, nb), :sw] = n_re
            bu_ref[pl.ds(r0, nb), sw:] = n_im
            return n_re, n_im

        h_re, h_im = lax.fori_loop(0, tl, step, (hst_ref[gt, :, :sw], hst_ref[gt, :, sw:]))
        hst_ref[gt, :, :sw] = h_re
        hst_ref[gt, :, sw:] = h_im

        y = (jnp.dot(bu_ref[:, :sw].astype(BF16), cre_ref[gt], preferred_element_type=F32)
             - jnp.dot(bu_ref[:, sw:].astype(BF16), cim_ref[gt], preferred_element_type=F32)
             + d_ref[:, cols] * uf)
        y = _gelu(y)
        z = jnp.dot(y.astype(BF16), wg_ref[gt], preferred_element_type=F32) + bg_ref[:, cols]
        y_ref[:, cols] = y * jax.nn.sigmoid(z)

    yy = y_ref[...]
    ms = jnp.sum(yy * yy, axis=-1, keepdims=True) / width
    o_ref[...] = (yy * lax.rsqrt(ms + RMS_EPS) * gn_ref[...]).astype(BF16)
    hlast_ref[...] = hst_ref[...]


def _ssm(u_tm, h0, sp, nb, seq, width):
    n_gt = width // LANES
    tl = _tile(seq, 64)
    rows = tl * nb
    sw = STATE_TILE
    return pl.pallas_call(
        functools.partial(_ssm_kernel, nb=nb, tl=tl, n_gt=n_gt, width=width),
        out_shape=(jax.ShapeDtypeStruct((seq * nb, width), BF16),
                   jax.ShapeDtypeStruct((n_gt, nb, 2 * sw), F32)),
        grid=(seq // tl,),
        in_specs=[pl.BlockSpec((rows, width), lambda s: (s, 0)),
                  _full((n_gt, nb, 2 * sw)),
                  _full((n_gt, 1, sw)), _full((n_gt, 1, sw)),
                  _full((n_gt, LANES, sw)), _full((n_gt, LANES, sw)),
                  _full((n_gt, sw, LANES)), _full((n_gt, sw, LANES)),
                  _full((1, width)),
                  _full((n_gt, LANES, LANES)),
                  _full((1, width)), _full((1, width))],
        out_specs=(pl.BlockSpec((rows, width), lambda s: (s, 0)),
                   _full((n_gt, nb, 2 * sw))),
        scratch_shapes=[pltpu.VMEM((n_gt, nb, 2 * sw), F32),
                        pltpu.VMEM((rows, 2 * sw), F32),
                        pltpu.VMEM((rows, width), F32)],
        compiler_params=_params("arbitrary"),
        name="ssm",
    )(u_tm, h0, sp["a_re"], sp["a_im"], sp["b_re"], sp["b_im"], sp["c_re"], sp["c_im"], sp["d"],
      sp["wg"], sp["bg"], sp["gn"])


def _block_diag(m, n_gt):
    _, r, c = m.shape
    m4 = m.reshape(n_gt, GROUPS_PER_TILE, r, 1, c)
    eye = jnp.eye(GROUPS_PER_TILE, dtype=bool)[None, :, None, :, None]
    return jnp.where(eye, m4, 0).reshape(n_gt, GROUPS_PER_TILE * r, GROUPS_PER_TILE * c)


def _ssm_params(a_re, a_im, log_dt, b_re, b_im, c_re, c_im, d, glu_w, glu_b, gain, width):
    n_gt = width // LANES
    ab_re, ab_im, bb_re, bb_im = _ssm_prep(a_re, a_im, log_dt, jnp.swapaxes(b_re, 1, 2),
                                           jnp.swapaxes(b_im, 1, 2))
    return dict(
        a_re=ab_re.reshape(n_gt, 1, STATE_TILE), a_im=ab_im.reshape(n_gt, 1, STATE_TILE),
        b_re=_block_diag(bb_re, n_gt).astype(BF16), b_im=_block_diag(bb_im, n_gt).astype(BF16),
        c_re=_block_diag(jnp.swapaxes(c_re, 1, 2), n_gt).astype(BF16),
        c_im=_block_diag(jnp.swapaxes(c_im, 1, 2), n_gt).astype(BF16),
        d=d.reshape(1, width), wg=_block_diag(glu_w, n_gt).astype(BF16), bg=glu_b.reshape(1, width),
        gn=gain.reshape(1, width))


def _state_to_tiles(s_re, s_im, n_gt):
    nb = s_re.shape[0]
    def one(s):
        return jnp.swapaxes(s.reshape(nb, n_gt, STATE_TILE), 0, 1)
    return jnp.concatenate([one(s_re), one(s_im)], axis=-1)


def _tiles_to_state(h, n_groups):
    n_gt, nb, _ = h.shape
    def one(s):
        return jnp.swapaxes(s, 0, 1).reshape(nb, n_groups, SSM_STATE)
    return one(h[:, :, :STATE_TILE]), one(h[:, :, STATE_TILE:])


def _layer_norm(z, g, b):
    mu = jnp.mean(z, axis=-1, keepdims=True)
    zc = z - mu
    var = jnp.mean(zc * zc, axis=-1, keepdims=True)
    return zc * lax.rsqrt(var + LN_EPS) * g + b


def _mix_kernel(a_ref, s_ref, x_ref, wo_ref, g_ref, b_ref, x1_ref, x1b_ref, x1t_ref=None, *, alpha, width):
    tn_dims = (((0,), (0,)), ((), ()))
    mix = (lax.dot_general(a_ref[...], wo_ref[:width], tn_dims, preferred_element_type=F32)
           + jnp.dot(s_ref[...], wo_ref[width:], preferred_element_type=F32))
    x1 = _layer_norm(alpha * x_ref[...] + mix, g_ref[...], b_ref[...])
    x1_ref[...] = x1
    x1b_ref[...] = x1.astype(BF16)
    if x1t_ref is not None:
        x1t_ref[...] = x1.T.astype(BF16)


def _mix(att_t, ssm_tm, x2d, w_out_b, g, b, alpha, width):
    t, d = x2d.shape
    nb, _, seq = att_t.shape
    tm = _tile(seq, 512)
    per_seq = seq // tm
    row = lambda i: (i, 0)
    out_shape = [jax.ShapeDtypeStruct((t, d), F32), jax.ShapeDtypeStruct((t, d), BF16)]
    out_specs = [pl.BlockSpec((tm, d), row), pl.BlockSpec((tm, d), row)]
    if tm % LANES == 0:
        out_shape.append(jax.ShapeDtypeStruct((d, t), BF16))
        out_specs.append(pl.BlockSpec((d, tm), lambda i: (0, i)))
    outs = pl.pallas_call(
        functools.partial(_mix_kernel, alpha=alpha, width=width),
        out_shape=tuple(out_shape),
        grid=(t // tm,),
        in_specs=[pl.BlockSpec((None, width, tm), lambda i: (i // per_seq, 0, i % per_seq)),
                  pl.BlockSpec((tm, width), lambda i: (i % per_seq, i // per_seq)),
                  pl.BlockSpec((tm, d), row), _full((2 * width, d)), _full((1, d)), _full((1, d))],
        out_specs=tuple(out_specs),
        compiler_params=_params("parallel"),
        name="mix",
    )(att_t, ssm_tm, x2d, w_out_b, g.reshape(1, d), b.reshape(1, d))
    x1, x1b = outs[0], outs[1]
    return x1, x1b, (outs[2] if len(outs) == 3 else x1b.T)


def _sort16(v):
    v = list(v)
    for i, j in SORT16:
        v[i], v[j] = jnp.maximum(v[i], v[j]), jnp.minimum(v[i], v[j])
    return v


def _merge_top16(a, b):
    v = [jnp.maximum(a[j], b[PEER_TOPK - 1 - j]) for j in range(PEER_TOPK)]
    for dist in (8, 4, 2, 1):
        for i in range(PEER_TOPK):
            if not i & dist:
                v[i], v[i + dist] = jnp.maximum(v[i], v[i + dist]), jnp.minimum(v[i], v[i + dist])
    return v


def _fold_sublanes(v):
    for shift in (4, 2, 1):
        v = _merge_top16(v, [pltpu.roll(x, shift, 0) for x in v])
    return v


def _count_prefix(v, pred):
    assert len(v) == PEER_TOPK == 16
    full = pred(v[15])
    b3 = pred(v[7])
    b2 = pred(jnp.where(b3, v[11], v[3]))
    b1 = pred(jnp.where(b3, jnp.where(b2, v[13], v[9]), jnp.where(b2, v[5], v[1])))
    quad = [jnp.where(b1, v[4 * k + 2], v[4 * k]) for k in range(4)]
    b0 = pred(jnp.where(b3, jnp.where(b2, quad[3], quad[2]), jnp.where(b2, quad[1], quad[0])))
    n = (jnp.where(b3, 8.0, 0.0) + jnp.where(b2, 4.0, 0.0)) + (jnp.where(b1, 2.0, 0.0) + jnp.where(b0, 1.0, 0.0))
    return jnp.where(full, 16.0, n)


def _route_kernel(x_ref, wq_ref, keys_ref, lim_ref, f_ref, r2_ref, e2_ref, *, tt, tc):
    qp = jnp.dot(x_ref[...], wq_ref[...], preferred_element_type=F32).astype(BF16)
    nt_dims = (((1,), (1,)), ((), ()))
    s_t = [lax.dot_general(keys_ref[0, c], qp[:, c * PEER_DHALF:(c + 1) * PEER_DHALF], nt_dims,
                           preferred_element_type=F32) for c in range(2)]
    n_blocks = N_KEYS // SUBLANES
    for l0 in range(0, tt, tc):
        cols = slice(l0, l0 + tc)
        sub = lax.broadcasted_iota(jnp.int32, (SUBLANES, tc), 0)
        blocks = [[s_t[c][SUBLANES * j:SUBLANES * (j + 1), cols] for j in range(n_blocks)] for c in range(2)]
        v1, v2 = (_fold_sublanes(_sort16(blocks[c])) for c in range(2))
        v1_lo, v1_hi = v1[0], v1[SUBLANES]
        for r in range(1, SUBLANES):
            v1_lo = jnp.where(sub == r, v1[r], v1_lo)
            v1_hi = jnp.where(sub == r, v1[SUBLANES + r], v1_hi)
        cand_lo = [v1_lo + v2[b] for b in range(PEER_TOPK)]
        cand_hi = [v1_hi + v2[b] for b in range(PEER_TOPK)]
        top_s = _fold_sublanes(_merge_top16(cand_lo, cand_hi))
        tau = top_s[PEER_TOPK - 1]
        z = jnp.ones((SUBLANES, tc), F32)
        for k in range(1, PEER_TOPK):
            z = z + jnp.exp(top_s[k] - top_s[0])
        zinv = 1.0 / z
        r2, e2 = [], []
        for j in range(n_blocks):
            s1, s2 = blocks[0][j], blocks[1][j]
            cnt = _count_prefix(v2, lambda p, s1=s1: s1 + p >= tau)
            rank = _count_prefix(v2, lambda p, s2=s2: p > s2)
            lim_j = jnp.where(s1 >= v1[PEER_TOPK - 1], cnt, 0.0)
            f_j = jnp.exp(s1 - v1[0]) * zinv
            for g in range(SUBLANES // KEYS_PER_BLOCK):
                rows = slice(g * KEYS_PER_BLOCK, (g + 1) * KEYS_PER_BLOCK)
                lim_ref[0, j * (SUBLANES // KEYS_PER_BLOCK) + g, :, cols] = lim_j[rows]
                f_ref[0, j * (SUBLANES // KEYS_PER_BLOCK) + g, :, cols] = f_j[rows]
            r2.append(rank)
            e2.append(jnp.exp(s2 - v2[0]))
        r2_ref[0, :, cols] = jnp.concatenate(r2, axis=0).astype(BF16)
        e2_ref[0, :, cols] = jnp.concatenate(e2, axis=0).astype(BF16)


def _route(x1b, wq_b, keys_b, tt):
    t, d = x1b.shape
    dk = 2 * PEER_DHALF
    tc = _tile(tt, 256)
    big_spec = pl.BlockSpec((1, N_KEYS, tt), lambda i, h: (h, 0, i))
    n_eb = N_KEYS // KEYS_PER_BLOCK
    row_shape = jax.ShapeDtypeStruct((PEER_HEADS, n_eb, KEYS_PER_BLOCK, t), F32)
    row_spec = pl.BlockSpec((1, n_eb, KEYS_PER_BLOCK, tt), lambda i, h: (h, 0, 0, i))
    return pl.pallas_call(
        functools.partial(_route_kernel, tt=tt, tc=tc),
        out_shape=(row_shape, row_shape,
                   jax.ShapeDtypeStruct((PEER_HEADS, N_KEYS, t), BF16),
                   jax.ShapeDtypeStruct((PEER_HEADS, N_KEYS, t), BF16)),
        grid=(t // tt, PEER_HEADS),
        in_specs=[pl.BlockSpec((tt, d), lambda i, h: (i, 0)),
                  pl.BlockSpec((d, dk), lambda i, h: (0, h)),
                  pl.BlockSpec((1, 2, N_KEYS, PEER_DHALF), lambda i, h: (h, 0, 0, 0))],
        out_specs=(row_spec, row_spec, big_spec, big_spec),
        compiler_params=_params("parallel", "parallel"),
        name="route",
    )(x1b, wq_b, keys_b)


def _peer_kernel(xt_ref, u_ref, vt_ref, lim_ref, f_ref, r2_ref, e2_ref, y_ref, a_ref, hid_ref,
                 *, n_blocks, tt):
    s = pl.program_id(0)
    slot = lax.rem(s, 2)
    rc = 2 * SUBLANES

    @pl.when(s == 0)
    def _():
        hid_ref[1] = jnp.zeros(hid_ref.shape[1:], BF16)

    @pl.when(lax.rem(jnp.maximum(s - 1, 0), n_blocks) == 0)
    def _():
        y_ref[...] = jnp.zeros_like(y_ref)

    a_ref[...] = jnp.dot(u_ref[...], xt_ref[...], preferred_element_type=F32)
    y_ref[...] += jnp.dot(vt_ref[...], hid_ref[1 - slot], preferred_element_type=F32)
    zero = jnp.zeros((rc, tt), BF16)
    for q in range(KEYS_PER_BLOCK):
        w = [zero] * (N_KEYS // rc)
        for h in range(PEER_HEADS):
            lim_t = jnp.broadcast_to(lim_ref[h, q:q + 1, :], (rc, tt)).astype(BF16)
            f_t = jnp.broadcast_to(f_ref[h, q:q + 1, :], (rc, tt)).astype(BF16)
            for k in range(N_KEYS // rc):
                rows = slice(k * rc, (k + 1) * rc)
                w[k] = w[k] + jnp.where(r2_ref[h, rows, :] < lim_t, e2_ref[h, rows, :], zero) * f_t
        for k in range(N_KEYS // rc):
            rows = slice(q * N_KEYS + k * rc, q * N_KEYS + (k + 1) * rc)
            hid_ref[slot, rows, :] = _gelu(a_ref[rows, :]).astype(BF16) * w[k]


def _peer(x1t, u_b, vt_b, lim, f, r2, e2, tt):
    d, t = x1t.shape
    eb = EXPERT_BLOCK
    n_blocks = u_b.shape[0] // eb
    n_tiles = t // tt
    tile = lambda s: jnp.minimum(s // n_blocks, n_tiles - 1)
    prev = lambda s: jnp.maximum(s - 1, 0)
    row_spec = pl.BlockSpec((PEER_HEADS, None, KEYS_PER_BLOCK, tt), lambda s: (0, s % n_blocks, 0, tile(s)))
    big_spec = pl.BlockSpec((PEER_HEADS, N_KEYS, tt), lambda s: (0, 0, tile(s)))
    return pl.pallas_call(
        functools.partial(_peer_kernel, n_blocks=n_blocks, tt=tt),
        out_shape=jax.ShapeDtypeStruct((d, t), F32),
        grid=(n_tiles * n_blocks + 1,),
        in_specs=[pl.BlockSpec((d, tt), lambda s: (0, tile(s))),
                  pl.BlockSpec((eb, d), lambda s: (s % n_blocks, 0)),
                  pl.BlockSpec((None, d, eb), lambda s: (prev(s) % n_blocks, 0, 0)),
                  row_spec, row_spec, big_spec, big_spec],
        out_specs=pl.BlockSpec((d, tt), lambda s: (0, prev(s) // n_blocks)),
        scratch_shapes=[pltpu.VMEM((eb, tt), F32), pltpu.VMEM((2, eb, tt), BF16)],
        compiler_params=_params("arbitrary"),
        name="peer",
    )(x1t, u_b, vt_b, lim, f, r2, e2)


def _ln2_kernel(x1_ref, yt_ref, g_ref, b_ref, o_ref, *, alpha):
    o_ref[...] = _layer_norm(alpha * x1_ref[...] + yt_ref[...].T, g_ref[...], b_ref[...])


def _ln2(x1, y_t, g, b, alpha):
    t, d = x1.shape
    tm = _tile(t, 512)
    row = lambda i: (i, 0)
    return pl.pallas_call(
        functools.partial(_ln2_kernel, alpha=alpha),
        out_shape=jax.ShapeDtypeStruct((t, d), F32),
        grid=(t // tm,),
        in_specs=[pl.BlockSpec((tm, d), row), pl.BlockSpec((d, tm), lambda i: (0, i)), _full((1, d)),
                  _full((1, d))],
        out_specs=pl.BlockSpec((tm, d), row),
        compiler_params=_params("parallel"),
        name="ln2",
    )(x1, y_t, g.reshape(1, d), b.reshape(1, d))


def _rel_bias_table(rel_bias):
    n_left = ATT_WINDOW - REL_CLIP + CHUNK - 1
    n_right = CHUNK - 1 - REL_CLIP
    ext = jnp.concatenate([jnp.repeat(rel_bias[:, :1], n_left, axis=1), rel_bias,
                           jnp.repeat(rel_bias[:, -1:], max(n_right, 0), axis=1)], axis=1)
    return jnp.stack([ext[:, CHUNK - 1 - i:CHUNK - 1 - i + BAND] for i in range(CHUNK)], axis=1)


def _step_bias(table, cpb):
    n_heads = table.shape[0]
    tab_t = jnp.swapaxes(table, 1, 2)
    per_chunk = [jnp.pad(tab_t, ((0, 0), (c * CHUNK, (cpb - 1 - c) * CHUNK), (0, 0)),
                         constant_values=MASK_VALUE) for c in range(cpb)]
    both = jnp.stack(per_chunk, axis=2)
    rows = both.shape[1]
    both = both.reshape(n_heads // 2, 2, rows, cpb * CHUNK)
    return jnp.swapaxes(both, 1, 2).reshape(n_heads // 2, rows, 2 * cpb * CHUNK)


def _encoder_layer(x, hist_k, hist_v, h0, wts, alpha):
    nb, seq, d = x.shape
    width = d // 2
    t = nb * seq
    n_gt = width // LANES
    assert seq % CHUNK == 0 and width % LANES == 0 and nb % SUBLANES == 0
    assert hist_k is None or seq == CHUNK
    x2d = x.reshape(t, d)

    n_hist = 0 if hist_k is None else hist_k.shape[1]
    n_invalid = ATT_WINDOW - n_hist
    def history(hist, n_new):
        if hist is None:
            return jnp.zeros((nb, ATT_WINDOW + n_new, width), BF16)
        return jnp.pad(hist.reshape(nb, n_hist, width).astype(BF16), ((0, 0), (n_invalid, n_new), (0, 0)))
    if _tile(seq, 256) % LANES == 0:
        q_b, kpad, vtpad, k_new, v_new, u_tm = _proj(
            x2d, wts["w_in"], history(hist_k, seq), jnp.swapaxes(history(hist_v, seq), 1, 2), nb, seq, width)
    else:
        q_b, k_b, v_b, k_new, v_new, u_tm = _proj_short(x2d, wts["w_in"], nb, seq, width)
        kpad = jnp.concatenate([history(hist_k, 0), k_b.reshape(nb, seq, width)], axis=1)
        vtpad = jnp.swapaxes(jnp.concatenate([history(hist_v, 0), v_b.reshape(nb, seq, width)], axis=1), 1, 2)
    n_chunks = seq // CHUNK
    cpb = 2 if n_chunks % 2 == 0 else 1
    assert cpb == 2 or n_chunks == 1
    att_t = _attention(q_b, kpad, vtpad, _step_bias(wts["bias"], cpb), wts["norm_attn_g"], nb, seq, width,
                       cpb, n_invalid)

    ssm_tm, h_last = _ssm(u_tm.reshape(t, width), h0, wts["ssm"], nb, seq, width)

    x1, x1b, x1t = _mix(att_t, ssm_tm.reshape(seq, nb * width), x2d, wts["w_out"], wts["ln1_g"],
                        wts["ln1_b"], alpha, width)

    tt = _tile(t, PEER_TOKENS)
    lim, f, r2, e2 = _route(x1b, wts["peer_wq"], wts["peer_keys"], tt)
    y_t = _peer(x1t, wts["peer_u"], wts["peer_vt"], lim, f, r2, e2, tt)
    x2 = _ln2(x1, y_t, wts["ln2_g"], wts["ln2_b"], alpha)

    n_heads = width // HEAD_DIM
    return (x2.reshape(nb, seq, d), k_new.reshape(nb, seq, n_heads, HEAD_DIM),
            v_new.reshape(nb, seq, n_heads, HEAD_DIM), h_last)


def kernel(x_prompt, x_sample, cache_attn_k, cache_attn_v, state_ssm_re, state_ssm_im, w_in, rel_bias, norm_attn_g, ssm_a_re, ssm_a_im, ssm_log_dt, ssm_b_re, ssm_b_im, ssm_c_re, ssm_c_im, ssm_d, ssm_glu_w, ssm_glu_b, norm_ssm_g, w_out, ln1_g, ln1_b, peer_wq, peer_keys, peer_u, peer_v, ln2_g, ln2_b):
    depth = w_in.shape[0]
    alpha = (2.0 * depth) ** 0.25
    d = x_prompt.shape[-1]
    width = d // 2
    n_gt = width // LANES
    n_groups = width // SSM_GROUP
    keep = min(ATT_WINDOW, x_prompt.shape[1])

    yp, ys = x_prompt, x_sample
    outs = [[] for _ in range(8)]
    for l in range(depth):
        wts = dict(
            w_in=w_in[l].astype(BF16),
            bias=_rel_bias_table(rel_bias[l]),
            norm_attn_g=jnp.broadcast_to(norm_attn_g[l][:, None], (width, LANES)),
            ssm=_ssm_params(ssm_a_re[l], ssm_a_im[l], ssm_log_dt[l], ssm_b_re[l], ssm_b_im[l], ssm_c_re[l],
                            ssm_c_im[l], ssm_d[l], ssm_glu_w[l], ssm_glu_b[l], norm_ssm_g[l], width),
            w_out=w_out[l].astype(BF16), ln1_g=ln1_g[l], ln1_b=ln1_b[l],
            peer_wq=peer_wq[l].astype(BF16), peer_keys=peer_keys[l].astype(BF16),
            peer_u=peer_u[l].astype(BF16),
            peer_vt=jnp.swapaxes(peer_v[l].astype(BF16).reshape(-1, EXPERT_BLOCK, d), 1, 2),
            ln2_g=ln2_g[l], ln2_b=ln2_b[l])
        h0p = jnp.zeros((n_gt, x_prompt.shape[0], 2 * STATE_TILE), F32)
        yp, kp, vp, hp = _encoder_layer(yp, None, None, h0p, wts, alpha)
        h0s = _state_to_tiles(state_ssm_re[l].astype(F32), state_ssm_im[l].astype(F32), n_gt)
        ys, kn, vn, hs = _encoder_layer(ys, cache_attn_k[l], cache_attn_v[l], h0s, wts, alpha)
        hp_re, hp_im = _tiles_to_state(hp, n_groups)
        hs_re, hs_im = _tiles_to_state(hs, n_groups)
        for acc, val in zip(outs, (kp[:, -keep:], vp[:, -keep:], hp_re, hp_im, kn, vn, hs_re, hs_im)):
            acc.append(val)
    return (yp, ys) + tuple(jnp.stack(o) for o in outs)
```

```python
import functools

import jax
import jax.numpy as jnp
from jax import lax
from jax.experimental import pallas as pl
from jax.experimental.pallas import tpu as pltpu

F32 = jnp.float32
BF16 = jnp.bfloat16

CHUNK = 64
LEFT_CHUNKS = 8
ATT_WINDOW = LEFT_CHUNKS * CHUNK
BAND = ATT_WINDOW + CHUNK
HEAD_DIM = 64
REL_CLIP = 2 * CHUNK
SSM_GROUP = 16
SSM_STATE = 64
PEER_HEADS = 8
PEER_TOPK = 16
N_KEYS = 128
PEER_DHALF = 128
LN_EPS = 1e-5
RMS_EPS = 1e-6
MASK_VALUE = -1e30

LANES = 128
SUBLANES = 8
GROUPS_PER_TILE = LANES // SSM_GROUP
STATE_TILE = GROUPS_PER_TILE * SSM_STATE
VMEM_LIMIT = 56 * 1024 * 1024
EXPERT_BLOCK = 1024
KEYS_PER_BLOCK = EXPERT_BLOCK // N_KEYS
PEER_TOKENS = 512

SORT16 = (
    (0, 13), (1, 12), (2, 15), (3, 14), (4, 8), (5, 6), (7, 11), (9, 10),
    (0, 5), (1, 7), (2, 9), (3, 4), (6, 13), (8, 14), (10, 15), (11, 12),
    (0, 1), (2, 3), (4, 5), (6, 8), (7, 9), (10, 11), (12, 13), (14, 15),
    (0, 2), (1, 3), (4, 10), (5, 11), (6, 7), (8, 9), (12, 14), (13, 15),
    (1, 2), (3, 12), (4, 6), (5, 7), (8, 10), (9, 11), (13, 14),
    (1, 4), (2, 6), (5, 8), (7, 10), (9, 13), (11, 14),
    (2, 4), (3, 6), (9, 12), (11, 13),
    (3, 5), (6, 8), (7, 9), (10, 12),
    (3, 4), (5, 6), (7, 8), (9, 10), (11, 12),
    (6, 7), (8, 9),
)


def _params(*semantics):
    return pltpu.CompilerParams(dimension_semantics=semantics, vmem_limit_bytes=VMEM_LIMIT)


def _tile(n, pref):
    t = min(n, pref)
    while n % t:
        t //= 2
    return t


def _gelu(x):
    return 0.5 * x * (1.0 + lax.erf(x * 0.7071067811865476))


def _full(shape):
    return pl.BlockSpec(shape, lambda *_: (0,) * len(shape))


def _proj_kernel(x_ref, w_ref, kbase_ref, vtbase_ref, q_ref, kpad_ref, vtpad_ref, kf_ref, vf_ref, u_ref,
                 *, width):
    del kbase_ref, vtbase_ref
    xb = x_ref[...].astype(BF16)
    acc = [jnp.dot(xb, w_ref[:, c * width:(c + 1) * width], preferred_element_type=F32) for c in range(4)]
    q_ref[...] = (acc[0] * HEAD_DIM ** -0.5).astype(BF16)
    kf_ref[...] = acc[1]
    kpad_ref[...] = acc[1].astype(BF16)
    vf_ref[...] = acc[2]
    vtpad_ref[...] = acc[2].T.astype(BF16)
    u_ref[...] = acc[3]


def _proj(x2d, w_in_b, kbase, vtbase, nb, seq, width):
    t, d = x2d.shape
    tm = _tile(seq, 256)
    assert tm % LANES == 0 and ATT_WINDOW % tm == 0
    per_seq = seq // tm
    skip = ATT_WINDOW // tm
    row = lambda i: (i, 0)
    rows_f32 = jax.ShapeDtypeStruct((t, width), F32)
    return pl.pallas_call(
        functools.partial(_proj_kernel, width=width),
        out_shape=(jax.ShapeDtypeStruct((t, width), BF16),
                   jax.ShapeDtypeStruct(kbase.shape, BF16), jax.ShapeDtypeStruct(vtbase.shape, BF16),
                   rows_f32, rows_f32, jax.ShapeDtypeStruct((seq, nb * width), F32)),
        grid=(t // tm,),
        in_specs=[pl.BlockSpec((tm, d), row),
                  pl.BlockSpec((d, 4 * width), lambda i: (0, 0)),
                  pl.BlockSpec(memory_space=pl.ANY), pl.BlockSpec(memory_space=pl.ANY)],
        out_specs=(pl.BlockSpec((tm, width), row),
                   pl.BlockSpec((None, tm, width), lambda i: (i // per_seq, skip + i % per_seq, 0)),
                   pl.BlockSpec((None, width, tm), lambda i: (i // per_seq, 0, skip + i % per_seq)),
                   pl.BlockSpec((tm, width), row), pl.BlockSpec((tm, width), row),
                   pl.BlockSpec((tm, width), lambda i: (i % per_seq, i // per_seq))),
        input_output_aliases={2: 1, 3: 2},
        compiler_params=_params("parallel"),
        name="proj",
    )(x2d, w_in_b, kbase, vtbase)


def _proj_short_kernel(x_ref, w_ref, q_ref, k_ref, v_ref, kf_ref, vf_ref, u_ref, *, width):
    xb = x_ref[...].astype(BF16)
    acc = [jnp.dot(xb, w_ref[:, c * width:(c + 1) * width], preferred_element_type=F32) for c in range(4)]
    q_ref[...] = (acc[0] * HEAD_DIM ** -0.5).astype(BF16)
    kf_ref[...] = acc[1]
    k_ref[...] = acc[1].astype(BF16)
    vf_ref[...] = acc[2]
    v_ref[...] = acc[2].astype(BF16)
    u_ref[...] = acc[3]


def _proj_short(x2d, w_in_b, nb, seq, width):
    t, d = x2d.shape
    tm = _tile(seq, 256)
    per_seq = seq // tm
    row = lambda i: (i, 0)
    rows_f32 = jax.ShapeDtypeStruct((t, width), F32)
    rows_b16 = jax.ShapeDtypeStruct((t, width), BF16)
    return pl.pallas_call(
        functools.partial(_proj_short_kernel, width=width),
        out_shape=(rows_b16, rows_b16, rows_b16, rows_f32, rows_f32,
                   jax.ShapeDtypeStruct((seq, nb * width), F32)),
        grid=(t // tm,),
        in_specs=[pl.BlockSpec((tm, d), row),
                  pl.BlockSpec((d, 4 * width), lambda i: (0, 0))],
        out_specs=tuple([pl.BlockSpec((tm, width), row)] * 5
                        + [pl.BlockSpec((tm, width), lambda i: (i % per_seq, i // per_seq))]),
        compiler_params=_params("parallel"),
        name="proj",
    )(x2d, w_in_b)


def _attn_kernel(q_ref, k_ref, vt_ref, bias_ref, g_ref, o_ref, *, n_steps, cpb, n_invalid, width):
    nq = cpb * CHUNK
    nc = 2 * nq
    kb = ATT_WINDOW + nq
    lo = lax.broadcasted_iota(jnp.int32, (nq, LANES), 1) < HEAD_DIM
    krow = lax.broadcasted_iota(jnp.int32, (kb, nc), 0)
    nt_dims = (((1,), (1,)), ((), ()))
    k_split = -(-(kb // 2) // LANES) * LANES
    halves = ((0, k_split), (k_split, kb - k_split))

    def step(r0, masked):
        tiles = []
        ssq = jnp.zeros((1, nq), F32)
        for hp in range(width // LANES):
            cols = slice(hp * LANES, (hp + 1) * LANES)
            q2 = q_ref[pl.ds(r0, nq), cols]
            qcat = jnp.concatenate([jnp.where(lo, q2, 0), jnp.where(lo, 0, q2)], axis=0)
            s = jnp.concatenate(
                [lax.dot_general(k_ref[pl.ds(r0 + k0, kn), cols], qcat, nt_dims, preferred_element_type=F32)
                 for k0, kn in halves], axis=0)
            s = s + bias_ref[hp]
            if masked:
                s = jnp.where(krow + r0 >= n_invalid, s, MASK_VALUE)
            m = jnp.max(s, axis=0, keepdims=True)
            e = jnp.exp(s - m)
            rl = 1.0 / jnp.sum(e, axis=0, keepdims=True)
            p = e.astype(BF16)
            o_t = sum(jnp.dot(vt_ref[cols, pl.ds(r0 + k0, kn)], p[k0:k0 + kn], preferred_element_type=F32)
                      for k0, kn in halves) * rl
            tile = jnp.concatenate([o_t[:HEAD_DIM, :nq], o_t[HEAD_DIM:, nq:]], axis=0)
            ssq = ssq + jnp.sum(tile * tile, axis=0, keepdims=True)
            tiles.append(tile)
        rinv = lax.rsqrt(ssq / width + RMS_EPS)
        for hp, tile in enumerate(tiles):
            rows = slice(hp * LANES, (hp + 1) * LANES)
            o_ref[rows, pl.ds(r0, nq)] = (tile * rinv * g_ref[rows, :nq]).astype(BF16)

    n_masked = min(n_steps, -(-n_invalid // nq))
    if n_steps == 1:
        step(0, n_masked > 0)
    else:
        def body(i, carry, masked):
            step(pl.multiple_of(i * nq, nq), masked)
            return carry
        lax.fori_loop(0, n_masked, functools.partial(body, masked=True), 0)
        lax.fori_loop(n_masked, n_steps, functools.partial(body, masked=False), 0)


def _attention(qkv, kpad, vtpad, bias, gain_b, nb, seq, width, cpb, n_invalid):
    lp = kpad.shape[1]
    nq = cpb * CHUNK
    return pl.pallas_call(
        functools.partial(_attn_kernel, n_steps=seq // nq, cpb=cpb, n_invalid=n_invalid, width=width),
        out_shape=jax.ShapeDtypeStruct((nb, width, seq), BF16),
        grid=(nb,),
        in_specs=[pl.BlockSpec((seq, width), lambda b: (b, 0)),
                  pl.BlockSpec((None, lp, width), lambda b: (b, 0, 0)),
                  pl.BlockSpec((None, width, lp), lambda b: (b, 0, 0)),
                  _full(bias.shape),
                  _full((width, LANES))],
        out_specs=pl.BlockSpec((None, width, seq), lambda b: (b, 0, 0)),
        compiler_params=_params("parallel"),
        name="attn",
    )(qkv, kpad, vtpad, bias, gain_b)


def _ssm_prep_kernel(are_ref, aim_ref, ldt_ref, bre_ref, bim_ref, abr_ref, abi_ref, bbr_ref, bbi_ref):
    a_re = are_ref[...]
    a_im = aim_ref[...]
    dt = jnp.exp(ldt_ref[...])
    mag = jnp.exp(a_re * dt)
    ang = a_im * dt
    ab_re = mag * jnp.cos(ang)
    ab_im = mag * jnp.sin(ang)
    n_re = ab_re - 1.0
    den = a_re * a_re + a_im * a_im
    c_re = (n_re * a_re + ab_im * a_im) / den
    c_im = (ab_im * a_re - n_re * a_im) / den
    abr_ref[...] = ab_re
    abi_ref[...] = ab_im
    b_re = bre_ref[...]
    b_im = bim_ref[...]
    bbr_ref[...] = c_re[:, None, :] * b_re - c_im[:, None, :] * b_im
    bbi_ref[...] = c_re[:, None, :] * b_im + c_im[:, None, :] * b_re


def _ssm_prep(a_re, a_im, log_dt, bt_re, bt_im):
    g, p = a_re.shape
    h = bt_re.shape[1]
    return pl.pallas_call(
        _ssm_prep_kernel,
        out_shape=(jax.ShapeDtypeStruct((g, p), F32), jax.ShapeDtypeStruct((g, p), F32),
                   jax.ShapeDtypeStruct((g, h, p), F32), jax.ShapeDtypeStruct((g, h, p), F32)),
        name="ssm_prep",
    )(a_re, a_im, log_dt.reshape(g, 1), bt_re, bt_im)


def _ssm_kernel(u_ref, h0_ref, are_ref, aim_ref, bre_ref, bim_ref, cre_ref, cim_ref, d_ref, wg_ref, bg_ref,
                gn_ref, o_ref, hlast_ref, hst_ref, bu_ref, y_ref, *, nb, tl, n_gt, width):
    sw = STATE_TILE

    @pl.when(pl.program_id(0) == 0)
    def _():
        hst_ref[...] = h0_ref[...]

    for gt in range(n_gt):
        cols = slice(gt * LANES, (gt + 1) * LANES)
        uf = u_ref[:, cols]
        ub = uf.astype(BF16)
        bu_ref[:, :sw] = jnp.dot(ub, bre_ref[gt], preferred_element_type=F32)
        bu_ref[:, sw:] = jnp.dot(ub, bim_ref[gt], preferred_element_type=F32)
        a_re = jnp.broadcast_to(are_ref[gt], (nb, sw))
        a_im = jnp.broadcast_to(aim_ref[gt], (nb, sw))

        def step(t, carry, a_re=a_re, a_im=a_im):
            h_re, h_im = carry
            r0 = pl.multiple_of(t * nb, nb)
            n_re = a_re * h_re - a_im * h_im + bu_ref[pl.ds(r0, nb), :sw]
            n_im = a_re * h_im + a_im * h_re + bu_ref[pl.ds(r0, nb), sw:]
            bu_ref[pl.ds(r0, nb), :sw] = n_re
            bu_ref[pl.ds(r0, nb), sw:] = n_im
            return n_re, n_im

        h_re, h_im = lax.fori_loop(0, tl, step, (hst_ref[gt, :, :sw], hst_ref[gt, :, sw:]))
        hst_ref[gt, :, :sw] = h_re
        hst_ref[gt, :, sw:] = h_im

        y = (jnp.dot(bu_ref[:, :sw].astype(BF16), cre_ref[gt], preferred_element_type=F32)
             - jnp.dot(bu_ref[:, sw:].astype(BF16), cim_ref[gt], preferred_element_type=F32)
             + d_ref[:, cols] * uf)
        y = _gelu(y)
        z = jnp.dot(y.astype(BF16), wg_ref[gt], preferred_element_type=F32) + bg_ref[:, cols]
        y_ref[:, cols] = y * jax.nn.sigmoid(z)

    yy = y_ref[...]
    ms = jnp.sum(yy * yy, axis=-1, keepdims=True) / width
    o_ref[...] = (yy * lax.rsqrt(ms + RMS_EPS) * gn_ref[...]).astype(BF16)
    hlast_ref[...] = hst_ref[...]


def _ssm(u_tm, h0, sp, nb, seq, width):
    n_gt = width // LANES
    tl = _tile(seq, 64)
    rows = tl * nb
    sw = STATE_TILE
    return pl.pallas_call(
        functools.partial(_ssm_kernel, nb=nb, tl=tl, n_gt=n_gt, width=width),
        out_shape=(jax.ShapeDtypeStruct((seq * nb, width), BF16),
                   jax.ShapeDtypeStruct((n_gt, nb, 2 * sw), F32)),
        grid=(seq // tl,),
        in_specs=[pl.BlockSpec((rows, width), lambda s: (s, 0)),
                  _full((n_gt, nb, 2 * sw)),
                  _full((n_gt, 1, sw)), _full((n_gt, 1, sw)),
                  _full((n_gt, LANES, sw)), _full((n_gt, LANES, sw)),
                  _full((n_gt, sw, LANES)), _full((n_gt, sw, LANES)),
                  _full((1, width)),
                  _full((n_gt, LANES, LANES)),
                  _full((1, width)), _full((1, width))],
        out_specs=(pl.BlockSpec((rows, width), lambda s: (s, 0)),
                   _full((n_gt, nb, 2 * sw))),
        scratch_shapes=[pltpu.VMEM((n_gt, nb, 2 * sw), F32),
                        pltpu.VMEM((rows, 2 * sw), F32),
                        pltpu.VMEM((rows, width), F32)],
        compiler_params=_params("arbitrary"),
        name="ssm",
    )(u_tm, h0, sp["a_re"], sp["a_im"], sp["b_re"], sp["b_im"], sp["c_re"], sp["c_im"], sp["d"],
      sp["wg"], sp["bg"], sp["gn"])


def _block_diag(m, n_gt):
    _, r, c = m.shape
    m4 = m.reshape(n_gt, GROUPS_PER_TILE, r, 1, c)
    eye = jnp.eye(GROUPS_PER_TILE, dtype=bool)[None, :, None, :, None]
    return jnp.where(eye, m4, 0).reshape(n_gt, GROUPS_PER_TILE * r, GROUPS_PER_TILE * c)


def _ssm_params(a_re, a_im, log_dt, b_re, b_im, c_re, c_im, d, glu_w, glu_b, gain, width):
    n_gt = width // LANES
    ab_re, ab_im, bb_re, bb_im = _ssm_prep(a_re, a_im, log_dt, jnp.swapaxes(b_re, 1, 2),
                                           jnp.swapaxes(b_im, 1, 2))
    return dict(
        a_re=ab_re.reshape(n_gt, 1, STATE_TILE), a_im=ab_im.reshape(n_gt, 1, STATE_TILE),
        b_re=_block_diag(bb_re, n_gt).astype(BF16), b_im=_block_diag(bb_im, n_gt).astype(BF16),
        c_re=_block_diag(jnp.swapaxes(c_re, 1, 2), n_gt).astype(BF16),
        c_im=_block_diag(jnp.swapaxes(c_im, 1, 2), n_gt).astype(BF16),
        d=d.reshape(1, width), wg=_block_diag(glu_w, n_gt).astype(BF16), bg=glu_b.reshape(1, width),
        gn=gain.reshape(1, width))


def _state_to_tiles(s_re, s_im, n_gt):
    nb = s_re.shape[0]
    def one(s):
        return jnp.swapaxes(s.reshape(nb, n_gt, STATE_TILE), 0, 1)
    return jnp.concatenate([one(s_re), one(s_im)], axis=-1)


def _tiles_to_state(h, n_groups):
    n_gt, nb, _ = h.shape
    def one(s):
        return jnp.swapaxes(s, 0, 1).reshape(nb, n_groups, SSM_STATE)
    return one(h[:, :, :STATE_TILE]), one(h[:, :, STATE_TILE:])


def _layer_norm(z, g, b):
    mu = jnp.mean(z, axis=-1, keepdims=True)
    zc = z - mu
    var = jnp.mean(zc * zc, axis=-1, keepdims=True)
    return zc * lax.rsqrt(var + LN_EPS) * g + b


def _mix_kernel(a_ref, s_ref, x_ref, wo_ref, g_ref, b_ref, x1_ref, x1b_ref, x1t_ref=None, *, alpha, width):
    tn_dims = (((0,), (0,)), ((), ()))
    mix = (lax.dot_general(a_ref[...], wo_ref[:width], tn_dims, preferred_element_type=F32)
           + jnp.dot(s_ref[...], wo_ref[width:], preferred_element_type=F32))
    x1 = _layer_norm(alpha * x_ref[...] + mix, g_ref[...], b_ref[...])
    x1_ref[...] = x1
    x1b_ref[...] = x1.astype(BF16)
    if x1t_ref is not None:
        x1t_ref[...] = x1.T.astype(BF16)


def _mix(att_t, ssm_tm, x2d, w_out_b, g, b, alpha, width):
    t, d = x2d.shape
    nb, _, seq = att_t.shape
    tm = _tile(seq, 512)
    per_seq = seq // tm
    row = lambda i: (i, 0)
    out_shape = [jax.ShapeDtypeStruct((t, d), F32), jax.ShapeDtypeStruct((t, d), BF16)]
    out_specs = [pl.BlockSpec((tm, d), row), pl.BlockSpec((tm, d), row)]
    if tm % LANES == 0:
        out_shape.append(jax.ShapeDtypeStruct((d, t), BF16))
        out_specs.append(pl.BlockSpec((d, tm), lambda i: (0, i)))
    outs = pl.pallas_call(
        functools.partial(_mix_kernel, alpha=alpha, width=width),
        out_shape=tuple(out_shape),
        grid=(t // tm,),
        in_specs=[pl.BlockSpec((None, width, tm), lambda i: (i // per_seq, 0, i % per_seq)),
                  pl.BlockSpec((tm, width), lambda i: (i % per_seq, i // per_seq)),
                  pl.BlockSpec((tm, d), row), _full((2 * width, d)), _full((1, d)), _full((1, d))],
        out_specs=tuple(out_specs),
        compiler_params=_params("parallel"),
        name="mix",
    )(att_t, ssm_tm, x2d, w_out_b, g.reshape(1, d), b.reshape(1, d))
    x1, x1b = outs[0], outs[1]
    return x1, x1b, (outs[2] if len(outs) == 3 else x1b.T)


def _sort16(v):
    v = list(v)
    for i, j in SORT16:
        v[i], v[j] = jnp.maximum(v[i], v[j]), jnp.minimum(v[i], v[j])
    return v


def _merge_top16(a, b):
    v = [jnp.maximum(a[j], b[PEER_TOPK - 1 - j]) for j in range(PEER_TOPK)]
    for dist in (8, 4, 2, 1):
        for i in range(PEER_TOPK):
            if not i & dist:
                v[i], v[i + dist] = jnp.maximum(v[i], v[i + dist]), jnp.minimum(v[i], v[i + dist])
    return v


def _fold_sublanes(v):
    for shift in (4, 2, 1):
        v = _merge_top16(v, [pltpu.roll(x, shift, 0) for x in v])
    return v


def _count_prefix(v, pred):
    assert len(v) == PEER_TOPK == 16
    full = pred(v[15])
    b3 = pred(v[7])
    b2 = pred(jnp.where(b3, v[11], v[3]))
    b1 = pred(jnp.where(b3, jnp.where(b2, v[13], v[9]), jnp.where(b2, v[5], v[1])))
    quad = [jnp.where(b1, v[4 * k + 2], v[4 * k]) for k in range(4)]
    b0 = pred(jnp.where(b3, jnp.where(b2, quad[3], quad[2]), jnp.where(b2, quad[1], quad[0])))
    n = (jnp.where(b3, 8.0, 0.0) + jnp.where(b2, 4.0, 0.0)) + (jnp.where(b1, 2.0, 0.0) + jnp.where(b0, 1.0, 0.0))
    return jnp.where(full, 16.0, n)


def _route_kernel(x_ref, wq_ref, keys_ref, lim_ref, f_ref, r2_ref, e2_ref, *, tt, tc):
    qp = jnp.dot(x_ref[...], wq_ref[...], preferred_element_type=F32).astype(BF16)
    nt_dims = (((1,), (1,)), ((), ()))
    s_t = [lax.dot_general(keys_ref[0, c], qp[:, c * PEER_DHALF:(c + 1) * PEER_DHALF], nt_dims,
                           preferred_element_type=F32) for c in range(2)]
    n_blocks = N_KEYS // SUBLANES
    for l0 in range(0, tt, tc):
        cols = slice(l0, l0 + tc)
        sub = lax.broadcasted_iota(jnp.int32, (SUBLANES, tc), 0)
        blocks = [[s_t[c][SUBLANES * j:SUBLANES * (j + 1), cols] for j in range(n_blocks)] for c in range(2)]
        v1, v2 = (_fold_sublanes(_sort16(blocks[c])) for c in range(2))
        v1_lo, v1_hi = v1[0], v1[SUBLANES]
        for r in range(1, SUBLANES):
            v1_lo = jnp.where(sub == r, v1[r], v1_lo)
            v1_hi = jnp.where(sub == r, v1[SUBLANES + r], v1_hi)
        cand_lo = [v1_lo + v2[b] for b in range(PEER_TOPK)]
        cand_hi = [v1_hi + v2[b] for b in range(PEER_TOPK)]
        top_s = _fold_sublanes(_merge_top16(cand_lo, cand_hi))
        tau = top_s[PEER_TOPK - 1]
        z = jnp.ones((SUBLANES, tc), F32)
        for k in range(1, PEER_TOPK):
            z = z + jnp.exp(top_s[k] - top_s[0])
        zinv = 1.0 / z
        r2, e2 = [], []
        for j in range(n_blocks):
            s1, s2 = blocks[0][j], blocks[1][j]
            cnt = _count_prefix(v2, lambda p, s1=s1: s1 + p >= tau)
            rank = _count_prefix(v2, lambda p, s2=s2: p > s2)
            lim_j = jnp.where(s1 >= v1[PEER_TOPK - 1], cnt, 0.0)
            f_j = jnp.exp(s1 - v1[0]) * zinv
            for g in range(SUBLANES // KEYS_PER_BLOCK):
                rows = slice(g * KEYS_PER_BLOCK, (g + 1) * KEYS_PER_BLOCK)
                lim_ref[0, j * (SUBLANES // KEYS_PER_BLOCK) + g, :, cols] = lim_j[rows]
                f_ref[0, j * (SUBLANES // KEYS_PER_BLOCK) + g, :, cols] = f_j[rows]
            r2.append(rank)
            e2.append(jnp.exp(s2 - v2[0]))
        r2_ref[0, :, cols] = jnp.concatenate(r2, axis=0).astype(BF16)
        e2_ref[0, :, cols] = jnp.concatenate(e2, axis=0).astype(BF16)


def _route(x1b, wq_b, keys_b, tt):
    t, d = x1b.shape
    dk = 2 * PEER_DHALF
    tc = _tile(tt, 256)
    big_spec = pl.BlockSpec((1, N_KEYS, tt), lambda i, h: (h, 0, i))
    n_eb = N_KEYS // KEYS_PER_BLOCK
    row_shape = jax.ShapeDtypeStruct((PEER_HEADS, n_eb, KEYS_PER_BLOCK, t), F32)
    row_spec = pl.BlockSpec((1, n_eb, KEYS_PER_BLOCK, tt), lambda i, h: (h, 0, 0, i))
    return pl.pallas_call(
        functools.partial(_route_kernel, tt=tt, tc=tc),
        out_shape=(row_shape, row_shape,
                   jax.ShapeDtypeStruct((PEER_HEADS, N_KEYS, t), BF16),
                   jax.ShapeDtypeStruct((PEER_HEADS, N_KEYS, t), BF16)),
        grid=(t // tt, PEER_HEADS),
        in_specs=[pl.BlockSpec((tt, d), lambda i, h: (i, 0)),
                  pl.BlockSpec((d, dk), lambda i, h: (0, h)),
                  pl.BlockSpec((1, 2, N_KEYS, PEER_DHALF), lambda i, h: (h, 0, 0, 0))],
        out_specs=(row_spec, row_spec, big_spec, big_spec),
        compiler_params=_params("parallel", "parallel"),
        name="route",
    )(x1b, wq_b, keys_b)


def _peer_kernel(xt_ref, u_ref, vt_ref, lim_ref, f_ref, r2_ref, e2_ref, y_ref, a_ref, hid_ref,
                 *, n_blocks, tt):
    s = pl.program_id(0)
    slot = lax.rem(s, 2)
    rc = 2 * SUBLANES

    @pl.when(s == 0)
    def _():
        hid_ref[1] = jnp.zeros(hid_ref.shape[1:], BF16)

    @pl.when(lax.rem(jnp.maximum(s - 1, 0), n_blocks) == 0)
    def _():
        y_ref[...] = jnp.zeros_like(y_ref)

    a_ref[...] = jnp.dot(u_ref[...], xt_ref[...], preferred_element_type=F32)
    y_ref[...] += jnp.dot(vt_ref[...], hid_ref[1 - slot], preferred_element_type=F32)
    zero = jnp.zeros((rc, tt), BF16)
    for q in range(KEYS_PER_BLOCK):
        w = [zero] * (N_KEYS // rc)
        for h in range(PEER_HEADS):
            lim_t = jnp.broadcast_to(lim_ref[h, q:q + 1, :], (rc, tt)).astype(BF16)
            f_t = jnp.broadcast_to(f_ref[h, q:q + 1, :], (rc, tt)).astype(BF16)
            for k in range(N_KEYS // rc):
                rows = slice(k * rc, (k + 1) * rc)
                w[k] = w[k] + jnp.where(r2_ref[h, rows, :] < lim_t, e2_ref[h, rows, :], zero) * f_t
        for k in range(N_KEYS // rc):
            rows = slice(q * N_KEYS + k * rc, q * N_KEYS + (k + 1) * rc)
            hid_ref[slot, rows, :] = _gelu(a_ref[rows, :]).astype(BF16) * w[k]


def _peer(x1t, u_b, vt_b, lim, f, r2, e2, tt):
    d, t = x1t.shape
    eb = EXPERT_BLOCK
    n_blocks = u_b.shape[0] // eb
    n_tiles = t // tt
    tile = lambda s: jnp.minimum(s // n_blocks, n_tiles - 1)
    prev = lambda s: jnp.maximum(s - 1, 0)
    row_spec = pl.BlockSpec((PEER_HEADS, None, KEYS_PER_BLOCK, tt), lambda s: (0, s % n_blocks, 0, tile(s)))
    big_spec = pl.BlockSpec((PEER_HEADS, N_KEYS, tt), lambda s: (0, 0, tile(s)))
    return pl.pallas_call(
        functools.partial(_peer_kernel, n_blocks=n_blocks, tt=tt),
        out_shape=jax.ShapeDtypeStruct((d, t), F32),
        grid=(n_tiles * n_blocks + 1,),
        in_specs=[pl.BlockSpec((d, tt), lambda s: (0, tile(s))),
                  pl.BlockSpec((eb, d), lambda s: (s % n_blocks, 0)),
                  pl.BlockSpec((None, d, eb), lambda s: (prev(s) % n_blocks, 0, 0)),
                  row_spec, row_spec, big_spec, big_spec],
        out_specs=pl.BlockSpec((d, tt), lambda s: (0, prev(s) // n_blocks)),
        scratch_shapes=[pltpu.VMEM((eb, tt), F32), pltpu.VMEM((2, eb, tt), BF16)],
        compiler_params=_params("arbitrary"),
        name="peer",
    )(x1t, u_b, vt_b, lim, f, r2, e2)


def _ln2_kernel(x1_ref, yt_ref, g_ref, b_ref, o_ref, *, alpha):
    o_ref[...] = _layer_norm(alpha * x1_ref[...] + yt_ref[...].T, g_ref[...], b_ref[...])


def _ln2(x1, y_t, g, b, alpha):
    t, d = x1.shape
    tm = _tile(t, 512)
    row = lambda i: (i, 0)
    return pl.pallas_call(
        functools.partial(_ln2_kernel, alpha=alpha),
        out_shape=jax.ShapeDtypeStruct((t, d), F32),
        grid=(t // tm,),
        in_specs=[pl.BlockSpec((tm, d), row), pl.BlockSpec((d, tm), lambda i: (0, i)), _full((1, d)),
                  _full((1, d))],
        out_specs=pl.BlockSpec((tm, d), row),
        compiler_params=_params("parallel"),
        name="ln2",
    )(x1, y_t, g.reshape(1, d), b.reshape(1, d))


def _rel_bias_table(rel_bias):
    n_left = ATT_WINDOW - REL_CLIP + CHUNK - 1
    n_right = CHUNK - 1 - REL_CLIP
    ext = jnp.concatenate([jnp.repeat(rel_bias[:, :1], n_left, axis=1), rel_bias,
                           jnp.repeat(rel_bias[:, -1:], max(n_right, 0), axis=1)], axis=1)
    return jnp.stack([ext[:, CHUNK - 1 - i:CHUNK - 1 - i + BAND] for i in range(CHUNK)], axis=1)


def _step_bias(table, cpb):
    n_heads = table.shape[0]
    tab_t = jnp.swapaxes(table, 1, 2)
    per_chunk = [jnp.pad(tab_t, ((0, 0), (c * CHUNK, (cpb - 1 - c) * CHUNK), (0, 0)),
                         constant_values=MASK_VALUE) for c in range(cpb)]
    both = jnp.stack(per_chunk, axis=2)
    rows = both.shape[1]
    both = both.reshape(n_heads // 2, 2, rows, cpb * CHUNK)
    return jnp.swapaxes(both, 1, 2).reshape(n_heads // 2, rows, 2 * cpb * CHUNK)


def _encoder_layer(x, hist_k, hist_v, h0, wts, alpha):
    nb, seq, d = x.shape
    width = d // 2
    t = nb * seq
    n_gt = width // LANES
    assert seq % CHUNK == 0 and width % LANES == 0 and nb % SUBLANES == 0
    assert hist_k is None or seq == CHUNK
    x2d = x.reshape(t, d)

    n_hist = 0 if hist_k is None else hist_k.shape[1]
    n_invalid = ATT_WINDOW - n_hist
    def history(hist, n_new):
        if hist is None:
            return jnp.zeros((nb, ATT_WINDOW + n_new, width), BF16)
        return jnp.pad(hist.reshape(nb, n_hist, width).astype(BF16), ((0, 0), (n_invalid, n_new), (0, 0)))
    if _tile(seq, 256) % LANES == 0:
        q_b, kpad, vtpad, k_new, v_new, u_tm = _proj(
            x2d, wts["w_in"], history(hist_k, seq), jnp.swapaxes(history(hist_v, seq), 1, 2), nb, seq, width)
    else:
        q_b, k_b, v_b, k_new, v_new, u_tm = _proj_short(x2d, wts["w_in"], nb, seq, width)
        kpad = jnp.concatenate([history(hist_k, 0), k_b.reshape(nb, seq, width)], axis=1)
        vtpad = jnp.swapaxes(jnp.concatenate([history(hist_v, 0), v_b.reshape(nb, seq, width)], axis=1), 1, 2)
    n_chunks = seq // CHUNK
    cpb = 2 if n_chunks % 2 == 0 else 1
    assert cpb == 2 or n_chunks == 1
    att_t = _attention(q_b, kpad, vtpad, _step_bias(wts["bias"], cpb), wts["norm_attn_g"], nb, seq, width,
                       cpb, n_invalid)

    ssm_tm, h_last = _ssm(u_tm.reshape(t, width), h0, wts["ssm"], nb, seq, width)

    x1, x1b, x1t = _mix(att_t, ssm_tm.reshape(seq, nb * width), x2d, wts["w_out"], wts["ln1_g"],
                        wts["ln1_b"], alpha, width)

    tt = _tile(t, PEER_TOKENS)
    lim, f, r2, e2 = _route(x1b, wts["peer_wq"], wts["peer_keys"], tt)
    y_t = _peer(x1t, wts["peer_u"], wts["peer_vt"], lim, f, r2, e2, tt)
    x2 = _ln2(x1, y_t, wts["ln2_g"], wts["ln2_b"], alpha)

    n_heads = width // HEAD_DIM
    return (x2.reshape(nb, seq, d), k_new.reshape(nb, seq, n_heads, HEAD_DIM),
            v_new.reshape(nb, seq, n_heads, HEAD_DIM), h_last)


def kernel(x_prompt, x_sample, cache_attn_k, cache_attn_v, state_ssm_re, state_ssm_im, w_in, rel_bias, norm_attn_g, ssm_a_re, ssm_a_im, ssm_log_dt, ssm_b_re, ssm_b_im, ssm_c_re, ssm_c_im, ssm_d, ssm_glu_w, ssm_glu_b, norm_ssm_g, w_out, ln1_g, ln1_b, peer_wq, peer_keys, peer_u, peer_v, ln2_g, ln2_b):
    depth = w_in.shape[0]
    alpha = (2.0 * depth) ** 0.25
    d = x_prompt.shape[-1]
    width = d // 2
    n_gt = width // LANES
    n_groups = width // SSM_GROUP
    keep = min(ATT_WINDOW, x_prompt.shape[1])

    yp, ys = x_prompt, x_sample
    outs = [[] for _ in range(8)]
    for l in range(depth):
        wts = dict(
            w_in=w_in[l].astype(BF16),
            bias=_rel_bias_table(rel_bias[l]),
            norm_attn_g=jnp.broadcast_to(norm_attn_g[l][:, None], (width, LANES)),
            ssm=_ssm_params(ssm_a_re[l], ssm_a_im[l], ssm_log_dt[l], ssm_b_re[l], ssm_b_im[l], ssm_c_re[l],
                            ssm_c_im[l], ssm_d[l], ssm_glu_w[l], ssm_glu_b[l], norm_ssm_g[l], width),
            w_out=w_out[l].astype(BF16), ln1_g=ln1_g[l], ln1_b=ln1_b[l],
            peer_wq=peer_wq[l].astype(BF16), peer_keys=peer_keys[l].astype(BF16),
            peer_u=peer_u[l].astype(BF16),
            peer_vt=jnp.swapaxes(peer_v[l].astype(BF16).reshape(-1, EXPERT_BLOCK, d), 1, 2),
            ln2_g=ln2_g[l], ln2_b=ln2_b[l])
        h0p = jnp.zeros((n_gt, x_prompt.shape[0], 2 * STATE_TILE), F32)
        yp, kp, vp, hp = _encoder_layer(yp, None, None, h0p, wts, alpha)
        h0s = _state_to_tiles(state_ssm_re[l].astype(F32), state_ssm_im[l].astype(F32), n_gt)
        ys, kn, vn, hs = _encoder_layer(ys, cache_attn_k[l], cache_attn_v[l], h0s, wts, alpha)
        hp_re, hp_im = _tiles_to_state(hp, n_groups)
        hs_re, hs_im = _tiles_to_state(hs, n_groups)
        for acc, val in zip(outs, (kp[:, -keep:], vp[:, -keep:], hp_re, hp_im, kn, vn, hs_re, hs_im)):
            acc.append(val)
    return (yp, ys) + tuple(jnp.stack(o) for o in outs)
```

```python
import functools

import jax
import jax.numpy as jnp
from jax import lax
from jax.experimental import pallas as pl
from jax.experimental.pallas import tpu as pltpu

F32 = jnp.float32
BF16 = jnp.bfloat16

CHUNK = 64
LEFT_CHUNKS = 8
ATT_WINDOW = LEFT_CHUNKS * CHUNK
BAND = ATT_WINDOW + CHUNK
HEAD_DIM = 64
REL_CLIP = 2 * CHUNK
SSM_GROUP = 16
SSM_STATE = 64
PEER_HEADS = 8
PEER_TOPK = 16
N_KEYS = 128
PEER_DHALF = 128
LN_EPS = 1e-5
RMS_EPS = 1e-6
MASK_VALUE = -1e30

LANES = 128
SUBLANES = 8
GROUPS_PER_TILE = LANES // SSM_GROUP
STATE_TILE = GROUPS_PER_TILE * SSM_STATE
VMEM_LIMIT = 56 * 1024 * 1024
EXPERT_BLOCK = 1024
KEYS_PER_BLOCK = EXPERT_BLOCK // N_KEYS
PEER_TOKENS = 512

SORT16 = (
    (0, 13), (1, 12), (2, 15), (3, 14), (4, 8), (5, 6), (7, 11), (9, 10),
    (0, 5), (1, 7), (2, 9), (3, 4), (6, 13), (8, 14), (10, 15), (11, 12),
    (0, 1), (2, 3), (4, 5), (6, 8), (7, 9), (10, 11), (12, 13), (14, 15),
    (0, 2), (1, 3), (4, 10), (5, 11), (6, 7), (8, 9), (12, 14), (13, 15),
    (1, 2), (3, 12), (4, 6), (5, 7), (8, 10), (9, 11), (13, 14),
    (1, 4), (2, 6), (5, 8), (7, 10), (9, 13), (11, 14),
    (2, 4), (3, 6), (9, 12), (11, 13),
    (3, 5), (6, 8), (7, 9), (10, 12),
    (3, 4), (5, 6), (7, 8), (9, 10), (11, 12),
    (6, 7), (8, 9),
)


def _params(*semantics):
    return pltpu.CompilerParams(dimension_semantics=semantics, vmem_limit_bytes=VMEM_LIMIT)


def _tile(n, pref):
    t = min(n, pref)
    while n % t:
        t //= 2
    return t


def _gelu(x):
    return 0.5 * x * (1.0 + lax.erf(x * 0.7071067811865476))


def _full(shape):
    return pl.BlockSpec(shape, lambda *_: (0,) * len(shape))


def _proj_kernel(x_ref, w_ref, kbase_ref, vtbase_ref, q_ref, kpad_ref, vtpad_ref, kf_ref, vf_ref, u_ref,
                 *, width):
    del kbase_ref, vtbase_ref
    xb = x_ref[...].astype(BF16)
    acc = [jnp.dot(xb, w_ref[:, c * width:(c + 1) * width], preferred_element_type=F32) for c in range(4)]
    q_ref[...] = (acc[0] * HEAD_DIM ** -0.5).astype(BF16)
    kf_ref[...] = acc[1]
    kpad_ref[...] = acc[1].astype(BF16)
    vf_ref[...] = acc[2]
    vtpad_ref[...] = acc[2].T.astype(BF16)
    u_ref[...] = acc[3]


def _proj(x2d, w_in_b, kbase, vtbase, nb, seq, width):
    t, d = x2d.shape
    tm = _tile(seq, 256)
    assert tm % LANES == 0 and ATT_WINDOW % tm == 0
    per_seq = seq // tm
    skip = ATT_WINDOW // tm
    keep = min(ATT_WINDOW, seq)
    dropped = per_seq - keep // tm
    row = lambda i: (i, 0)
    tail = lambda i: (i // per_seq, jnp.maximum(i % per_seq - dropped, 0), 0)
    tail_f32 = jax.ShapeDtypeStruct((nb, keep, width), F32)
    return pl.pallas_call(
        functools.partial(_proj_kernel, width=width),
        out_shape=(jax.ShapeDtypeStruct((t, width), BF16),
                   jax.ShapeDtypeStruct(kbase.shape, BF16), jax.ShapeDtypeStruct(vtbase.shape, BF16),
                   tail_f32, tail_f32, jax.ShapeDtypeStruct((seq, nb * width), F32)),
        grid=(t // tm,),
        in_specs=[pl.BlockSpec((tm, d), row),
                  pl.BlockSpec((d, 4 * width), lambda i: (0, 0)),
                  pl.BlockSpec(memory_space=pl.ANY), pl.BlockSpec(memory_space=pl.ANY)],
        out_specs=(pl.BlockSpec((tm, width), row),
                   pl.BlockSpec((None, tm, width), lambda i: (i // per_seq, skip + i % per_seq, 0)),
                   pl.BlockSpec((None, width, tm), lambda i: (i // per_seq, 0, skip + i % per_seq)),
                   pl.BlockSpec((None, tm, width), tail), pl.BlockSpec((None, tm, width), tail),
                   pl.BlockSpec((tm, width), lambda i: (i % per_seq, i // per_seq))),
        input_output_aliases={2: 1, 3: 2},
        compiler_params=_params("arbitrary"),
        name="proj",
    )(x2d, w_in_b, kbase, vtbase)


def _proj_short_kernel(x_ref, w_ref, q_ref, k_ref, v_ref, kf_ref, vf_ref, u_ref, *, width):
    xb = x_ref[...].astype(BF16)
    acc = [jnp.dot(xb, w_ref[:, c * width:(c + 1) * width], preferred_element_type=F32) for c in range(4)]
    q_ref[...] = (acc[0] * HEAD_DIM ** -0.5).astype(BF16)
    kf_ref[...] = acc[1]
    k_ref[...] = acc[1].astype(BF16)
    vf_ref[...] = acc[2]
    v_ref[...] = acc[2].astype(BF16)
    u_ref[...] = acc[3]


def _proj_short(x2d, w_in_b, nb, seq, width):
    t, d = x2d.shape
    tm = _tile(seq, 256)
    per_seq = seq // tm
    row = lambda i: (i, 0)
    rows_f32 = jax.ShapeDtypeStruct((t, width), F32)
    rows_b16 = jax.ShapeDtypeStruct((t, width), BF16)
    return pl.pallas_call(
        functools.partial(_proj_short_kernel, width=width),
        out_shape=(rows_b16, rows_b16, rows_b16, rows_f32, rows_f32,
                   jax.ShapeDtypeStruct((seq, nb * width), F32)),
        grid=(t // tm,),
        in_specs=[pl.BlockSpec((tm, d), row),
                  pl.BlockSpec((d, 4 * width), lambda i: (0, 0))],
        out_specs=tuple([pl.BlockSpec((tm, width), row)] * 5
                        + [pl.BlockSpec((tm, width), lambda i: (i % per_seq, i // per_seq))]),
        compiler_params=_params("parallel"),
        name="proj",
    )(x2d, w_in_b)


def _attn_kernel(q_ref, k_ref, vt_ref, bias_ref, g_ref, o_ref, *, n_steps, cpb, n_invalid, width):
    nq = cpb * CHUNK
    nc = 2 * nq
    kb = ATT_WINDOW + nq
    lo = lax.broadcasted_iota(jnp.int32, (nq, LANES), 1) < HEAD_DIM
    krow = lax.broadcasted_iota(jnp.int32, (kb, nc), 0)
    nt_dims = (((1,), (1,)), ((), ()))
    k_split = -(-(kb // 2) // LANES) * LANES
    halves = ((0, k_split), (k_split, kb - k_split))

    def step(r0, masked):
        tiles = []
        ssq = jnp.zeros((1, nq), F32)
        for hp in range(width // LANES):
            cols = slice(hp * LANES, (hp + 1) * LANES)
            q2 = q_ref[pl.ds(r0, nq), cols]
            qcat = jnp.concatenate([jnp.where(lo, q2, 0), jnp.where(lo, 0, q2)], axis=0)
            s = jnp.concatenate(
                [lax.dot_general(k_ref[pl.ds(r0 + k0, kn), cols], qcat, nt_dims, preferred_element_type=F32)
                 for k0, kn in halves], axis=0)
            s = s + bias_ref[hp]
            if masked:
                s = jnp.where(krow + r0 >= n_invalid, s, MASK_VALUE)
            m = jnp.max(s, axis=0, keepdims=True)
            e = jnp.exp(s - m)
            rl = 1.0 / jnp.sum(e, axis=0, keepdims=True)
            p = e.astype(BF16)
            o_t = sum(jnp.dot(vt_ref[cols, pl.ds(r0 + k0, kn)], p[k0:k0 + kn], preferred_element_type=F32)
                      for k0, kn in halves) * rl
            tile = jnp.concatenate([o_t[:HEAD_DIM, :nq], o_t[HEAD_DIM:, nq:]], axis=0)
            ssq = ssq + jnp.sum(tile * tile, axis=0, keepdims=True)
            tiles.append(tile)
        rinv = lax.rsqrt(ssq / width + RMS_EPS)
        for hp, tile in enumerate(tiles):
            rows = slice(hp * LANES, (hp + 1) * LANES)
            o_ref[rows, pl.ds(r0, nq)] = (tile * rinv * g_ref[rows, :nq]).astype(BF16)

    n_masked = min(n_steps, -(-n_invalid // nq))
    if n_steps == 1:
        step(0, n_masked > 0)
    else:
        def body(i, carry, masked):
            step(pl.multiple_of(i * nq, nq), masked)
            return carry
        lax.fori_loop(0, n_masked, functools.partial(body, masked=True), 0)
        lax.fori_loop(n_masked, n_steps, functools.partial(body, masked=False), 0)


def _attention(qkv, kpad, vtpad, bias, gain_b, nb, seq, width, cpb, n_invalid):
    lp = kpad.shape[1]
    nq = cpb * CHUNK
    return pl.pallas_call(
        functools.partial(_attn_kernel, n_steps=seq // nq, cpb=cpb, n_invalid=n_invalid, width=width),
        out_shape=jax.ShapeDtypeStruct((nb, width, seq), BF16),
        grid=(nb,),
        in_specs=[pl.BlockSpec((seq, width), lambda b: (b, 0)),
                  pl.BlockSpec((None, lp, width), lambda b: (b, 0, 0)),
                  pl.BlockSpec((None, width, lp), lambda b: (b, 0, 0)),
                  _full(bias.shape),
                  _full((width, LANES))],
        out_specs=pl.BlockSpec((None, width, seq), lambda b: (b, 0, 0)),
        compiler_params=_params("parallel"),
        name="attn",
    )(qkv, kpad, vtpad, bias, gain_b)


def _ssm_prep_kernel(are_ref, aim_ref, ldt_ref, bre_ref, bim_ref, abr_ref, abi_ref, bbr_ref, bbi_ref):
    a_re = are_ref[...]
    a_im = aim_ref[...]
    dt = jnp.exp(ldt_ref[...])
    mag = jnp.exp(a_re * dt)
    ang = a_im * dt
    ab_re = mag * jnp.cos(ang)
    ab_im = mag * jnp.sin(ang)
    n_re = ab_re - 1.0
    den = a_re * a_re + a_im * a_im
    c_re = (n_re * a_re + ab_im * a_im) / den
    c_im = (ab_im * a_re - n_re * a_im) / den
    abr_ref[...] = ab_re
    abi_ref[...] = ab_im
    b_re = bre_ref[...]
    b_im = bim_ref[...]
    bbr_ref[...] = c_re[:, None, :] * b_re - c_im[:, None, :] * b_im
    bbi_ref[...] = c_re[:, None, :] * b_im + c_im[:, None, :] * b_re


def _ssm_prep(a_re, a_im, log_dt, bt_re, bt_im):
    g, p = a_re.shape
    h = bt_re.shape[1]
    return pl.pallas_call(
        _ssm_prep_kernel,
        out_shape=(jax.ShapeDtypeStruct((g, p), F32), jax.ShapeDtypeStruct((g, p), F32),
                   jax.ShapeDtypeStruct((g, h, p), F32), jax.ShapeDtypeStruct((g, h, p), F32)),
        name="ssm_prep",
    )(a_re, a_im, log_dt.reshape(g, 1), bt_re, bt_im)


def _ssm_kernel(u_ref, h0_ref, are_ref, aim_ref, bre_ref, bim_ref, cre_ref, cim_ref, d_ref, wg_ref, bg_ref,
                gn_ref, o_ref, hlast_ref, hst_ref, bu_ref, y_ref, *, nb, tl, n_gt, width):
    sw = STATE_TILE

    @pl.when(pl.program_id(0) == 0)
    def _():
        hst_ref[...] = h0_ref[...]

    for gt in range(n_gt):
        cols = slice(gt * LANES, (gt + 1) * LANES)
        uf = u_ref[:, cols]
        ub = uf.astype(BF16)
        bu_ref[:, :sw] = jnp.dot(ub, bre_ref[gt], preferred_element_type=F32)
        bu_ref[:, sw:] = jnp.dot(ub, bim_ref[gt], preferred_element_type=F32)
        a_re = jnp.broadcast_to(are_ref[gt], (nb, sw))
        a_im = jnp.broadcast_to(aim_ref[gt], (nb, sw))

        def step(t, carry, a_re=a_re, a_im=a_im):
            h_re, h_im = carry
            r0 = pl.multiple_of(t * nb, nb)
            n_re = a_re * h_re - a_im * h_im + bu_ref[pl.ds(r0, nb), :sw]
            n_im = a_re * h_im + a_im * h_re + bu_ref[pl.ds(r0, nb), sw:]
            bu_ref[pl.ds(r0, nb), :sw] = n_re
            bu_ref[pl.ds(r0, nb), sw:] = n_im
            return n_re, n_im

        h_re, h_im = lax.fori_loop(0, tl, step, (hst_ref[gt, :, :sw], hst_ref[gt, :, sw:]))
        hst_ref[gt, :, :sw] = h_re
        hst_ref[gt, :, sw:] = h_im

        y = (jnp.dot(bu_ref[:, :sw].astype(BF16), cre_ref[gt], preferred_element_type=F32)
             - jnp.dot(bu_ref[:, sw:].astype(BF16), cim_ref[gt], preferred_element_type=F32)
             + d_ref[:, cols] * uf)
        y = _gelu(y)
        z = jnp.dot(y.astype(BF16), wg_ref[gt], preferred_element_type=F32) + bg_ref[:, cols]
        y_ref[:, cols] = y * jax.nn.sigmoid(z)

    yy = y_ref[...]
    ms = jnp.sum(yy * yy, axis=-1, keepdims=True) / width
    o_ref[...] = (yy * lax.rsqrt(ms + RMS_EPS) * gn_ref[...]).astype(BF16)
    hlast_ref[...] = hst_ref[...]


def _ssm(u_tm, h0, sp, nb, seq, width):
    n_gt = width // LANES
    tl = _tile(seq, 64)
    rows = tl * nb
    sw = STATE_TILE
    return pl.pallas_call(
        functools.partial(_ssm_kernel, nb=nb, tl=tl, n_gt=n_gt, width=width),
        out_shape=(jax.ShapeDtypeStruct((seq * nb, width), BF16),
                   jax.ShapeDtypeStruct((n_gt, nb, 2 * sw), F32)),
        grid=(seq // tl,),
        in_specs=[pl.BlockSpec((rows, width), lambda s: (s, 0)),
                  _full((n_gt, nb, 2 * sw)),
                  _full((n_gt, 1, sw)), _full((n_gt, 1, sw)),
                  _full((n_gt, LANES, sw)), _full((n_gt, LANES, sw)),
                  _full((n_gt, sw, LANES)), _full((n_gt, sw, LANES)),
                  _full((1, width)),
                  _full((n_gt, LANES, LANES)),
                  _full((1, width)), _full((1, width))],
        out_specs=(pl.BlockSpec((rows, width), lambda s: (s, 0)),
                   _full((n_gt, nb, 2 * sw))),
        scratch_shapes=[pltpu.VMEM((n_gt, nb, 2 * sw), F32),
                        pltpu.VMEM((rows, 2 * sw), F32),
                        pltpu.VMEM((rows, width), F32)],
        compiler_params=_params("arbitrary"),
        name="ssm",
    )(u_tm, h0, sp["a_re"], sp["a_im"], sp["b_re"], sp["b_im"], sp["c_re"], sp["c_im"], sp["d"],
      sp["wg"], sp["bg"], sp["gn"])


def _block_diag(m, n_gt):
    _, r, c = m.shape
    m4 = m.reshape(n_gt, GROUPS_PER_TILE, r, 1, c)
    eye = jnp.eye(GROUPS_PER_TILE, dtype=bool)[None, :, None, :, None]
    return jnp.where(eye, m4, 0).reshape(n_gt, GROUPS_PER_TILE * r, GROUPS_PER_TILE * c)


def _ssm_params(a_re, a_im, log_dt, b_re, b_im, c_re, c_im, d, glu_w, glu_b, gain, width):
    n_gt = width // LANES
    ab_re, ab_im, bb_re, bb_im = _ssm_prep(a_re, a_im, log_dt, jnp.swapaxes(b_re, 1, 2),
                                           jnp.swapaxes(b_im, 1, 2))
    return dict(
        a_re=ab_re.reshape(n_gt, 1, STATE_TILE), a_im=ab_im.reshape(n_gt, 1, STATE_TILE),
        b_re=_block_diag(bb_re, n_gt).astype(BF16), b_im=_block_diag(bb_im, n_gt).astype(BF16),
        c_re=_block_diag(jnp.swapaxes(c_re, 1, 2), n_gt).astype(BF16),
        c_im=_block_diag(jnp.swapaxes(c_im, 1, 2), n_gt).astype(BF16),
        d=d.reshape(1, width), wg=_block_diag(glu_w, n_gt).astype(BF16), bg=glu_b.reshape(1, width),
        gn=gain.reshape(1, width))


def _state_to_tiles(s_re, s_im, n_gt):
    nb = s_re.shape[0]
    def one(s):
        return jnp.swapaxes(s.reshape(nb, n_gt, STATE_TILE), 0, 1)
    return jnp.concatenate([one(s_re), one(s_im)], axis=-1)


def _tiles_to_state(h, n_groups):
    n_gt, nb, _ = h.shape
    def one(s):
        return jnp.swapaxes(s, 0, 1).reshape(nb, n_groups, SSM_STATE)
    return one(h[:, :, :STATE_TILE]), one(h[:, :, STATE_TILE:])


def _layer_norm(z, g, b):
    mu = jnp.mean(z, axis=-1, keepdims=True)
    zc = z - mu
    var = jnp.mean(zc * zc, axis=-1, keepdims=True)
    return zc * lax.rsqrt(var + LN_EPS) * g + b


def _mix_kernel(a_ref, s_ref, x_ref, wo_ref, g_ref, b_ref, x1_ref, x1b_ref, x1t_ref=None, *, alpha, width):
    tn_dims = (((0,), (0,)), ((), ()))
    mix = (lax.dot_general(a_ref[...], wo_ref[:width], tn_dims, preferred_element_type=F32)
           + jnp.dot(s_ref[...], wo_ref[width:], preferred_element_type=F32))
    x1 = _layer_norm(alpha * x_ref[...] + mix, g_ref[...], b_ref[...])
    x1_ref[...] = x1
    x1b_ref[...] = x1.astype(BF16)
    if x1t_ref is not None:
        x1t_ref[...] = x1.T.astype(BF16)


def _mix(att_t, ssm_tm, x2d, w_out_b, g, b, alpha, width):
    t, d = x2d.shape
    nb, _, seq = att_t.shape
    tm = _tile(seq, 512)
    per_seq = seq // tm
    row = lambda i: (i, 0)
    out_shape = [jax.ShapeDtypeStruct((t, d), F32), jax.ShapeDtypeStruct((t, d), BF16)]
    out_specs = [pl.BlockSpec((tm, d), row), pl.BlockSpec((tm, d), row)]
    if tm % LANES == 0:
        out_shape.append(jax.ShapeDtypeStruct((d, t), BF16))
        out_specs.append(pl.BlockSpec((d, tm), lambda i: (0, i)))
    outs = pl.pallas_call(
        functools.partial(_mix_kernel, alpha=alpha, width=width),
        out_shape=tuple(out_shape),
        grid=(t // tm,),
        in_specs=[pl.BlockSpec((None, width, tm), lambda i: (i // per_seq, 0, i % per_seq)),
                  pl.BlockSpec((tm, width), lambda i: (i % per_seq, i // per_seq)),
                  pl.BlockSpec((tm, d), row), _full((2 * width, d)), _full((1, d)), _full((1, d))],
        out_specs=tuple(out_specs),
        compiler_params=_params("parallel"),
        name="mix",
    )(att_t, ssm_tm, x2d, w_out_b, g.reshape(1, d), b.reshape(1, d))
    x1, x1b = outs[0], outs[1]
    return x1, x1b, (outs[2] if len(outs) == 3 else x1b.T)


def _sort16(v):
    v = list(v)
    for i, j in SORT16:
        v[i], v[j] = jnp.maximum(v[i], v[j]), jnp.minimum(v[i], v[j])
    return v


def _merge_top16(a, b):
    v = [jnp.maximum(a[j], b[PEER_TOPK - 1 - j]) for j in range(PEER_TOPK)]
    for dist in (8, 4, 2, 1):
        for i in range(PEER_TOPK):
            if not i & dist:
                v[i], v[i + dist] = jnp.maximum(v[i], v[i + dist]), jnp.minimum(v[i], v[i + dist])
    return v


def _fold_sublanes(v):
    for shift in (4, 2, 1):
        v = _merge_top16(v, [pltpu.roll(x, shift, 0) for x in v])
    return v


def _count_prefix(v, pred):
    assert len(v) == PEER_TOPK == 16
    full = pred(v[15])
    b3 = pred(v[7])
    b2 = pred(jnp.where(b3, v[11], v[3]))
    b1 = pred(jnp.where(b3, jnp.where(b2, v[13], v[9]), jnp.where(b2, v[5], v[1])))
    quad = [jnp.where(b1, v[4 * k + 2], v[4 * k]) for k in range(4)]
    b0 = pred(jnp.where(b3, jnp.where(b2, quad[3], quad[2]), jnp.where(b2, quad[1], quad[0])))
    n = (jnp.where(b3, 8.0, 0.0) + jnp.where(b2, 4.0, 0.0)) + (jnp.where(b1, 2.0, 0.0) + jnp.where(b0, 1.0, 0.0))
    return jnp.where(full, 16.0, n)


def _route_kernel(x_ref, wq_ref, keys_ref, lim_ref, f_ref, r2_ref, e2_ref, *, tt, tc):
    qp = jnp.dot(x_ref[...], wq_ref[...], preferred_element_type=F32).astype(BF16)
    nt_dims = (((1,), (1,)), ((), ()))
    s_t = [lax.dot_general(keys_ref[0, c], qp[:, c * PEER_DHALF:(c + 1) * PEER_DHALF], nt_dims,
                           preferred_element_type=F32) for c in range(2)]
    n_blocks = N_KEYS // SUBLANES
    for l0 in range(0, tt, tc):
        cols = slice(l0, l0 + tc)
        sub = lax.broadcasted_iota(jnp.int32, (SUBLANES, tc), 0)
        blocks = [[s_t[c][SUBLANES * j:SUBLANES * (j + 1), cols] for j in range(n_blocks)] for c in range(2)]
        v1, v2 = (_fold_sublanes(_sort16(blocks[c])) for c in range(2))
        v1_lo, v1_hi = v1[0], v1[SUBLANES]
        for r in range(1, SUBLANES):
            v1_lo = jnp.where(sub == r, v1[r], v1_lo)
            v1_hi = jnp.where(sub == r, v1[SUBLANES + r], v1_hi)
        cand_lo = [v1_lo + v2[b] for b in range(PEER_TOPK)]
        cand_hi = [v1_hi + v2[b] for b in range(PEER_TOPK)]
        top_s = _fold_sublanes(_merge_top16(cand_lo, cand_hi))
        tau = top_s[PEER_TOPK - 1]
        z = jnp.ones((SUBLANES, tc), F32)
        for k in range(1, PEER_TOPK):
            z = z + jnp.exp(top_s[k] - top_s[0])
        zinv = 1.0 / z
        r2, e2 = [], []
        for j in range(n_blocks):
            s1, s2 = blocks[0][j], blocks[1][j]
            cnt = _count_prefix(v2, lambda p, s1=s1: s1 + p >= tau)
            rank = _count_prefix(v2, lambda p, s2=s2: p > s2)
            lim_j = jnp.where(s1 >= v1[PEER_TOPK - 1], cnt, 0.0)
            f_j = jnp.exp(s1 - v1[0]) * zinv
            for g in range(SUBLANES // KEYS_PER_BLOCK):
                rows = slice(g * KEYS_PER_BLOCK, (g + 1) * KEYS_PER_BLOCK)
                lim_ref[0, j * (SUBLANES // KEYS_PER_BLOCK) + g, :, cols] = lim_j[rows]
                f_ref[0, j * (SUBLANES // KEYS_PER_BLOCK) + g, :, cols] = f_j[rows]
            r2.append(rank)
            e2.append(jnp.exp(s2 - v2[0]))
        r2_ref[0, :, cols] = jnp.concatenate(r2, axis=0).astype(BF16)
        e2_ref[0, :, cols] = jnp.concatenate(e2, axis=0).astype(BF16)


def _route(x1b, wq_b, keys_b, tt):
    t, d = x1b.shape
    dk = 2 * PEER_DHALF
    tc = _tile(tt, 256)
    big_spec = pl.BlockSpec((1, N_KEYS, tt), lambda i, h: (h, 0, i))
    n_eb = N_KEYS // KEYS_PER_BLOCK
    row_shape = jax.ShapeDtypeStruct((PEER_HEADS, n_eb, KEYS_PER_BLOCK, t), F32)
    row_spec = pl.BlockSpec((1, n_eb, KEYS_PER_BLOCK, tt), lambda i, h: (h, 0, 0, i))
    return pl.pallas_call(
        functools.partial(_route_kernel, tt=tt, tc=tc),
        out_shape=(row_shape, row_shape,
                   jax.ShapeDtypeStruct((PEER_HEADS, N_KEYS, t), BF16),
                   jax.ShapeDtypeStruct((PEER_HEADS, N_KEYS, t), BF16)),
        grid=(t // tt, PEER_HEADS),
        in_specs=[pl.BlockSpec((tt, d), lambda i, h: (i, 0)),
                  pl.BlockSpec((d, dk), lambda i, h: (0, h)),
                  pl.BlockSpec((1, 2, N_KEYS, PEER_DHALF), lambda i, h: (h, 0, 0, 0))],
        out_specs=(row_spec, row_spec, big_spec, big_spec),
        compiler_params=_params("parallel", "parallel"),
        name="route",
    )(x1b, wq_b, keys_b)


def _peer_kernel(xt_ref, u_ref, vt_ref, lim_ref, f_ref, r2_ref, e2_ref, y_ref, a_ref, hid_ref,
                 *, n_blocks, tt):
    s = pl.program_id(0)
    slot = lax.rem(s, 2)
    rc = 2 * SUBLANES

    @pl.when(s == 0)
    def _():
        hid_ref[1] = jnp.zeros(hid_ref.shape[1:], BF16)

    @pl.when(lax.rem(jnp.maximum(s - 1, 0), n_blocks) == 0)
    def _():
        y_ref[...] = jnp.zeros_like(y_ref)

    a_ref[...] = jnp.dot(u_ref[...], xt_ref[...], preferred_element_type=F32)
    y_ref[...] += jnp.dot(vt_ref[...], hid_ref[1 - slot], preferred_element_type=F32)
    zero = jnp.zeros((rc, tt), BF16)
    for q in range(KEYS_PER_BLOCK):
        w = [zero] * (N_KEYS // rc)
        for h in range(PEER_HEADS):
            lim_t = jnp.broadcast_to(lim_ref[h, q:q + 1, :], (rc, tt)).astype(BF16)
            f_t = jnp.broadcast_to(f_ref[h, q:q + 1, :], (rc, tt)).astype(BF16)
            for k in range(N_KEYS // rc):
                rows = slice(k * rc, (k + 1) * rc)
                w[k] = w[k] + jnp.where(r2_ref[h, rows, :] < lim_t, e2_ref[h, rows, :], zero) * f_t
        for k in range(N_KEYS // rc):
            rows = slice(q * N_KEYS + k * rc, q * N_KEYS + (k + 1) * rc)
            hid_ref[slot, rows, :] = _gelu(a_ref[rows, :]).astype(BF16) * w[k]


def _peer(x1t, u_b, vt_b, lim, f, r2, e2, tt):
    d, t = x1t.shape
    eb = EXPERT_BLOCK
    n_blocks = u_b.shape[0] // eb
    n_tiles = t // tt
    tile = lambda s: jnp.minimum(s // n_blocks, n_tiles - 1)
    prev = lambda s: jnp.maximum(s - 1, 0)
    row_spec = pl.BlockSpec((PEER_HEADS, None, KEYS_PER_BLOCK, tt), lambda s: (0, s % n_blocks, 0, tile(s)))
    big_spec = pl.BlockSpec((PEER_HEADS, N_KEYS, tt), lambda s: (0, 0, tile(s)))
    return pl.pallas_call(
        functools.partial(_peer_kernel, n_blocks=n_blocks, tt=tt),
        out_shape=jax.ShapeDtypeStruct((d, t), F32),
        grid=(n_tiles * n_blocks + 1,),
        in_specs=[pl.BlockSpec((d, tt), lambda s: (0, tile(s))),
                  pl.BlockSpec((eb, d), lambda s: (s % n_blocks, 0)),
                  pl.BlockSpec((None, d, eb), lambda s: (prev(s) % n_blocks, 0, 0)),
                  row_spec, row_spec, big_spec, big_spec],
        out_specs=pl.BlockSpec((d, tt), lambda s: (0, prev(s) // n_blocks)),
        scratch_shapes=[pltpu.VMEM((eb, tt), F32), pltpu.VMEM((2, eb, tt), BF16)],
        compiler_params=_params("arbitrary"),
        name="peer",
    )(x1t, u_b, vt_b, lim, f, r2, e2)


def _ln2_kernel(x1_ref, yt_ref, g_ref, b_ref, o_ref, *, alpha):
    o_ref[...] = _layer_norm(alpha * x1_ref[...] + yt_ref[...].T, g_ref[...], b_ref[...])


def _ln2(x1, y_t, g, b, alpha):
    t, d = x1.shape
    tm = _tile(t, 512)
    row = lambda i: (i, 0)
    return pl.pallas_call(
        functools.partial(_ln2_kernel, alpha=alpha),
        out_shape=jax.ShapeDtypeStruct((t, d), F32),
        grid=(t // tm,),
        in_specs=[pl.BlockSpec((tm, d), row), pl.BlockSpec((d, tm), lambda i: (0, i)), _full((1, d)),
                  _full((1, d))],
        out_specs=pl.BlockSpec((tm, d), row),
        compiler_params=_params("parallel"),
        name="ln2",
    )(x1, y_t, g.reshape(1, d), b.reshape(1, d))


def _rel_bias_table(rel_bias):
    n_left = ATT_WINDOW - REL_CLIP + CHUNK - 1
    n_right = CHUNK - 1 - REL_CLIP
    ext = jnp.concatenate([jnp.repeat(rel_bias[:, :1], n_left, axis=1), rel_bias,
                           jnp.repeat(rel_bias[:, -1:], max(n_right, 0), axis=1)], axis=1)
    return jnp.stack([ext[:, CHUNK - 1 - i:CHUNK - 1 - i + BAND] for i in range(CHUNK)], axis=1)


def _step_bias(table, cpb):
    n_heads = table.shape[0]
    tab_t = jnp.swapaxes(table, 1, 2)
    per_chunk = [jnp.pad(tab_t, ((0, 0), (c * CHUNK, (cpb - 1 - c) * CHUNK), (0, 0)),
                         constant_values=MASK_VALUE) for c in range(cpb)]
    both = jnp.stack(per_chunk, axis=2)
    rows = both.shape[1]
    both = both.reshape(n_heads // 2, 2, rows, cpb * CHUNK)
    return jnp.swapaxes(both, 1, 2).reshape(n_heads // 2, rows, 2 * cpb * CHUNK)


def _encoder_layer(x, hist_k, hist_v, h0, wts, alpha):
    nb, seq, d = x.shape
    width = d // 2
    t = nb * seq
    n_gt = width // LANES
    assert seq % CHUNK == 0 and width % LANES == 0 and nb % SUBLANES == 0
    assert hist_k is None or seq == CHUNK
    x2d = x.reshape(t, d)

    n_hist = 0 if hist_k is None else hist_k.shape[1]
    n_invalid = ATT_WINDOW - n_hist
    def history(hist, n_new):
        if hist is None:
            return jnp.zeros((nb, ATT_WINDOW + n_new, width), BF16)
        return jnp.pad(hist.reshape(nb, n_hist, width).astype(BF16), ((0, 0), (n_invalid, n_new), (0, 0)))
    if _tile(seq, 256) % LANES == 0:
        q_b, kpad, vtpad, k_new, v_new, u_tm = _proj(
            x2d, wts["w_in"], history(hist_k, seq), jnp.swapaxes(history(hist_v, seq), 1, 2), nb, seq, width)
    else:
        q_b, k_b, v_b, k_new, v_new, u_tm = _proj_short(x2d, wts["w_in"], nb, seq, width)
        kpad = jnp.concatenate([history(hist_k, 0), k_b.reshape(nb, seq, width)], axis=1)
        vtpad = jnp.swapaxes(jnp.concatenate([history(hist_v, 0), v_b.reshape(nb, seq, width)], axis=1), 1, 2)
    n_chunks = seq // CHUNK
    cpb = 2 if n_chunks % 2 == 0 else 1
    assert cpb == 2 or n_chunks == 1
    att_t = _attention(q_b, kpad, vtpad, _step_bias(wts["bias"], cpb), wts["norm_attn_g"], nb, seq, width,
                       cpb, n_invalid)

    ssm_tm, h_last = _ssm(u_tm.reshape(t, width), h0, wts["ssm"], nb, seq, width)

    x1, x1b, x1t = _mix(att_t, ssm_tm.reshape(seq, nb * width), x2d, wts["w_out"], wts["ln1_g"],
                        wts["ln1_b"], alpha, width)

    tt = _tile(t, PEER_TOKENS)
    lim, f, r2, e2 = _route(x1b, wts["peer_wq"], wts["peer_keys"], tt)
    y_t = _peer(x1t, wts["peer_u"], wts["peer_vt"], lim, f, r2, e2, tt)
    x2 = _ln2(x1, y_t, wts["ln2_g"], wts["ln2_b"], alpha)

    n_heads = width // HEAD_DIM
    return (x2.reshape(nb, seq, d), k_new.reshape(nb, -1, n_heads, HEAD_DIM),
            v_new.reshape(nb, -1, n_heads, HEAD_DIM), h_last)


def kernel(x_prompt, x_sample, cache_attn_k, cache_attn_v, state_ssm_re, state_ssm_im, w_in, rel_bias, norm_attn_g, ssm_a_re, ssm_a_im, ssm_log_dt, ssm_b_re, ssm_b_im, ssm_c_re, ssm_c_im, ssm_d, ssm_glu_w, ssm_glu_b, norm_ssm_g, w_out, ln1_g, ln1_b, peer_wq, peer_keys, peer_u, peer_v, ln2_g, ln2_b):
    depth = w_in.shape[0]
    alpha = (2.0 * depth) ** 0.25
    d = x_prompt.shape[-1]
    width = d // 2
    n_gt = width // LANES
    n_groups = width // SSM_GROUP
    keep = min(ATT_WINDOW, x_prompt.shape[1])

    yp, ys = x_prompt, x_sample
    outs = [[] for _ in range(8)]
    for l in range(depth):
        wts = dict(
            w_in=w_in[l].astype(BF16),
            bias=_rel_bias_table(rel_bias[l]),
            norm_attn_g=jnp.broadcast_to(norm_attn_g[l][:, None], (width, LANES)),
            ssm=_ssm_params(ssm_a_re[l], ssm_a_im[l], ssm_log_dt[l], ssm_b_re[l], ssm_b_im[l], ssm_c_re[l],
                            ssm_c_im[l], ssm_d[l], ssm_glu_w[l], ssm_glu_b[l], norm_ssm_g[l], width),
            w_out=w_out[l].astype(BF16), ln1_g=ln1_g[l], ln1_b=ln1_b[l],
            peer_wq=peer_wq[l].astype(BF16), peer_keys=peer_keys[l].astype(BF16),
            peer_u=peer_u[l].astype(BF16),
            peer_vt=jnp.swapaxes(peer_v[l].astype(BF16).reshape(-1, EXPERT_BLOCK, d), 1, 2),
            ln2_g=ln2_g[l], ln2_b=ln2_b[l])
        h0p = jnp.zeros((n_gt, x_prompt.shape[0], 2 * STATE_TILE), F32)
        yp, kp, vp, hp = _encoder_layer(yp, None, None, h0p, wts, alpha)
        h0s = _state_to_tiles(state_ssm_re[l].astype(F32), state_ssm_im[l].astype(F32), n_gt)
        ys, kn, vn, hs = _encoder_layer(ys, cache_attn_k[l], cache_attn_v[l], h0s, wts, alpha)
        hp_re, hp_im = _tiles_to_state(hp, n_groups)
        hs_re, hs_im = _tiles_to_state(hs, n_groups)
        assert kp.shape[1] == keep
        for acc, val in zip(outs, (kp, vp, hp_re, hp_im, kn, vn, hs_re, hs_im)):
            acc.append(val)
    return (yp, ys) + tuple(jnp.stack(o) for o in outs)
```

```python
import functools

import jax
import jax.numpy as jnp
from jax import lax
from jax.experimental import pallas as pl
from jax.experimental.pallas import tpu as pltpu

F32 = jnp.float32
BF16 = jnp.bfloat16

CHUNK = 64
LEFT_CHUNKS = 8
ATT_WINDOW = LEFT_CHUNKS * CHUNK
BAND = ATT_WINDOW + CHUNK
HEAD_DIM = 64
REL_CLIP = 2 * CHUNK
SSM_GROUP = 16
SSM_STATE = 64
PEER_HEADS = 8
PEER_TOPK = 16
N_KEYS = 128
PEER_DHALF = 128
LN_EPS = 1e-5
RMS_EPS = 1e-6
MASK_VALUE = -1e30
LOG2E = 1.4426950408889634
Q_SCALE = HEAD_DIM ** -0.5 * LOG2E

LANES = 128
SUBLANES = 8
GROUPS_PER_TILE = LANES // SSM_GROUP
STATE_TILE = GROUPS_PER_TILE * SSM_STATE
VMEM_LIMIT = 56 * 1024 * 1024
EXPERT_BLOCK = 1024
KEYS_PER_BLOCK = EXPERT_BLOCK // N_KEYS
PEER_TOKENS = 512
ATTN_PAIR_GROUP = 8

SORT16 = (
    (0, 13), (1, 12), (2, 15), (3, 14), (4, 8), (5, 6), (7, 11), (9, 10),
    (0, 5), (1, 7), (2, 9), (3, 4), (6, 13), (8, 14), (10, 15), (11, 12),
    (0, 1), (2, 3), (4, 5), (6, 8), (7, 9), (10, 11), (12, 13), (14, 15),
    (0, 2), (1, 3), (4, 10), (5, 11), (6, 7), (8, 9), (12, 14), (13, 15),
    (1, 2), (3, 12), (4, 6), (5, 7), (8, 10), (9, 11), (13, 14),
    (1, 4), (2, 6), (5, 8), (7, 10), (9, 13), (11, 14),
    (2, 4), (3, 6), (9, 12), (11, 13),
    (3, 5), (6, 8), (7, 9), (10, 12),
    (3, 4), (5, 6), (7, 8), (9, 10), (11, 12),
    (6, 7), (8, 9),
)


def _params(*semantics):
    return pltpu.CompilerParams(dimension_semantics=semantics, vmem_limit_bytes=VMEM_LIMIT)


def _tile(n, pref):
    t = min(n, pref)
    while n % t:
        t //= 2
    return t


def _gelu(x):
    return 0.5 * x * (1.0 + lax.erf(x * 0.7071067811865476))


def _full(shape):
    return pl.BlockSpec(shape, lambda *_: (0,) * len(shape))


def _proj_kernel(x_ref, w_ref, kbase_ref, vtbase_ref, q_ref, kpad_ref, vtpad_ref, kf_ref, vf_ref, u_ref,
                 *, width):
    del kbase_ref, vtbase_ref
    xb = x_ref[...].astype(BF16)
    acc = [jnp.dot(xb, w_ref[:, c * width:(c + 1) * width], preferred_element_type=F32) for c in range(4)]
    q_ref[...] = (acc[0] * Q_SCALE).astype(BF16)
    kf_ref[...] = acc[1]
    kpad_ref[...] = acc[1].astype(BF16)
    vf_ref[...] = acc[2]
    vtpad_ref[...] = acc[2].T.astype(BF16)
    u_ref[...] = acc[3]


def _proj(x2d, w_in_b, kbase, vtbase, nb, seq, width):
    t, d = x2d.shape
    tm = _tile(seq, 256)
    assert tm % LANES == 0 and ATT_WINDOW % tm == 0
    per_seq = seq // tm
    skip = ATT_WINDOW // tm
    keep = min(ATT_WINDOW, seq)
    dropped = per_seq - keep // tm
    row = lambda i: (i, 0)
    tail = lambda i: (i // per_seq, jnp.maximum(i % per_seq - dropped, 0), 0)
    tail_f32 = jax.ShapeDtypeStruct((nb, keep, width), F32)
    return pl.pallas_call(
        functools.partial(_proj_kernel, width=width),
        out_shape=(jax.ShapeDtypeStruct((t, width), BF16),
                   jax.ShapeDtypeStruct(kbase.shape, BF16), jax.ShapeDtypeStruct(vtbase.shape, BF16),
                   tail_f32, tail_f32, jax.ShapeDtypeStruct((seq, nb * width), F32)),
        grid=(t // tm,),
        in_specs=[pl.BlockSpec((tm, d), row),
                  pl.BlockSpec((d, 4 * width), lambda i: (0, 0)),
                  pl.BlockSpec(memory_space=pl.ANY), pl.BlockSpec(memory_space=pl.ANY)],
        out_specs=(pl.BlockSpec((tm, width), row),
                   pl.BlockSpec((None, tm, width), lambda i: (i // per_seq, skip + i % per_seq, 0)),
                   pl.BlockSpec((None, width, tm), lambda i: (i // per_seq, 0, skip + i % per_seq)),
                   pl.BlockSpec((None, tm, width), tail), pl.BlockSpec((None, tm, width), tail),
                   pl.BlockSpec((tm, width), lambda i: (i % per_seq, i // per_seq))),
        input_output_aliases={2: 1, 3: 2},
        compiler_params=_params("arbitrary"),
        name="proj",
    )(x2d, w_in_b, kbase, vtbase)


def _proj_short_kernel(x_ref, w_ref, q_ref, k_ref, v_ref, kf_ref, vf_ref, u_ref, *, width):
    xb = x_ref[...].astype(BF16)
    acc = [jnp.dot(xb, w_ref[:, c * width:(c + 1) * width], preferred_element_type=F32) for c in range(4)]
    q_ref[...] = (acc[0] * Q_SCALE).astype(BF16)
    kf_ref[...] = acc[1]
    k_ref[...] = acc[1].astype(BF16)
    vf_ref[...] = acc[2]
    v_ref[...] = acc[2].astype(BF16)
    u_ref[...] = acc[3]


def _proj_short(x2d, w_in_b, nb, seq, width):
    t, d = x2d.shape
    tm = _tile(seq, 256)
    per_seq = seq // tm
    row = lambda i: (i, 0)
    rows_f32 = jax.ShapeDtypeStruct((t, width), F32)
    rows_b16 = jax.ShapeDtypeStruct((t, width), BF16)
    return pl.pallas_call(
        functools.partial(_proj_short_kernel, width=width),
        out_shape=(rows_b16, rows_b16, rows_b16, rows_f32, rows_f32,
                   jax.ShapeDtypeStruct((seq, nb * width), F32)),
        grid=(t // tm,),
        in_specs=[pl.BlockSpec((tm, d), row),
                  pl.BlockSpec((d, 4 * width), lambda i: (0, 0))],
        out_specs=tuple([pl.BlockSpec((tm, width), row)] * 5
                        + [pl.BlockSpec((tm, width), lambda i: (i % per_seq, i // per_seq))]),
        compiler_params=_params("parallel"),
        name="proj",
    )(x2d, w_in_b)


def _attn_kernel(q_ref, k_ref, vt_ref, bias_ref, g_ref, o_ref, *, n_steps, cpb, n_invalid, width):
    nq = cpb * CHUNK
    nc = 2 * nq
    kb = ATT_WINDOW + nq
    lo = lax.broadcasted_iota(jnp.int32, (nq, LANES), 1) < HEAD_DIM
    nt_dims = (((1,), (1,)), ((), ()))
    group = ATTN_PAIR_GROUP
    k_split = -(-(kb // 2) // LANES) * LANES
    halves = ((0, k_split), (k_split, kb - k_split))
    z0, z1 = (cpb - 1) * CHUNK, ATT_WINDOW - REL_CLIP

    def pair_scores(r0, hp):
        cols = slice(hp * LANES, (hp + 1) * LANES)
        q2 = q_ref[pl.ds(r0, nq), cols]
        qcat = jnp.concatenate([jnp.where(lo, q2, 0), jnp.where(lo, 0, q2)], axis=0)
        return jnp.concatenate(
            [lax.dot_general(k_ref[pl.ds(r0 + k0, kn), cols], qcat, nt_dims, preferred_element_type=F32)
             for k0, kn in halves], axis=0)

    def step(r0, masked):
        tiles = []
        ssq = jnp.zeros((1, nq), F32)
        n_pairs = width // LANES
        for g0 in range(0, n_pairs, group):
            pairs = range(g0, min(g0 + group, n_pairs))
            s = jnp.concatenate([pair_scores(r0, hp) for hp in pairs], axis=1)
            parts = [s[z0:z1], s[z1:] + jnp.concatenate([bias_ref[hp, z1:, :] for hp in pairs], axis=1)]
            if z0:
                parts.insert(0, s[:z0] + jnp.concatenate([bias_ref[hp, :z0, :] for hp in pairs], axis=1))
            s = jnp.concatenate(parts, axis=0)
            if masked:
                s = jnp.where(lax.broadcasted_iota(jnp.int32, s.shape, 0) + r0 >= n_invalid, s, MASK_VALUE)
            m = jnp.max(s, axis=0, keepdims=True)
            e = jnp.exp2(s - m)
            rl = 1.0 / jnp.sum(e, axis=0, keepdims=True)
            p = e.astype(BF16)
            for i, hp in enumerate(pairs):
                cols = slice(hp * LANES, (hp + 1) * LANES)
                pc = slice(i * nc, (i + 1) * nc)
                o_t = sum(jnp.dot(vt_ref[cols, pl.ds(r0 + k0, kn)], p[k0:k0 + kn, pc],
                                  preferred_element_type=F32) for k0, kn in halves) * rl[:, pc]
                tile = jnp.concatenate([o_t[:HEAD_DIM, :nq], o_t[HEAD_DIM:, nq:]], axis=0)
                ssq = ssq + jnp.sum(tile * tile, axis=0, keepdims=True)
                tiles.append(tile)
        rinv = lax.rsqrt(ssq / width + RMS_EPS)
        for hp, tile in enumerate(tiles):
            rows = slice(hp * LANES, (hp + 1) * LANES)
            o_ref[rows, pl.ds(r0, nq)] = (tile * rinv * g_ref[rows, :nq]).astype(BF16)

    n_masked = min(n_steps, -(-n_invalid // nq))
    if n_steps == 1:
        step(0, n_masked > 0)
    else:
        def body(i, carry, masked):
            step(pl.multiple_of(i * nq, nq), masked)
            return carry
        lax.fori_loop(0, n_masked, functools.partial(body, masked=True), 0)
        lax.fori_loop(n_masked, n_steps, functools.partial(body, masked=False), 0)


def _attention(qkv, kpad, vtpad, bias, gain_b, nb, seq, width, cpb, n_invalid):
    lp = kpad.shape[1]
    nq = cpb * CHUNK
    return pl.pallas_call(
        functools.partial(_attn_kernel, n_steps=seq // nq, cpb=cpb, n_invalid=n_invalid, width=width),
        out_shape=jax.ShapeDtypeStruct((nb, width, seq), BF16),
        grid=(nb,),
        in_specs=[pl.BlockSpec((seq, width), lambda b: (b, 0)),
                  pl.BlockSpec((None, lp, width), lambda b: (b, 0, 0)),
                  pl.BlockSpec((None, width, lp), lambda b: (b, 0, 0)),
                  _full(bias.shape),
                  _full((width, LANES))],
        out_specs=pl.BlockSpec((None, width, seq), lambda b: (b, 0, 0)),
        compiler_params=_params("parallel"),
        name="attn",
    )(qkv, kpad, vtpad, bias, gain_b)


def _ssm_prep_kernel(are_ref, aim_ref, ldt_ref, bre_ref, bim_ref, abr_ref, abi_ref, bbr_ref, bbi_ref):
    a_re = are_ref[...]
    a_im = aim_ref[...]
    dt = jnp.exp(ldt_ref[...])
    mag = jnp.exp(a_re * dt)
    ang = a_im * dt
    ab_re = mag * jnp.cos(ang)
    ab_im = mag * jnp.sin(ang)
    n_re = ab_re - 1.0
    den = a_re * a_re + a_im * a_im
    c_re = (n_re * a_re + ab_im * a_im) / den
    c_im = (ab_im * a_re - n_re * a_im) / den
    abr_ref[...] = ab_re
    abi_ref[...] = ab_im
    b_re = bre_ref[...]
    b_im = bim_ref[...]
    bbr_ref[...] = c_re[:, None, :] * b_re - c_im[:, None, :] * b_im
    bbi_ref[...] = c_re[:, None, :] * b_im + c_im[:, None, :] * b_re


def _ssm_prep(a_re, a_im, log_dt, bt_re, bt_im):
    g, p = a_re.shape
    h = bt_re.shape[1]
    return pl.pallas_call(
        _ssm_prep_kernel,
        out_shape=(jax.ShapeDtypeStruct((g, p), F32), jax.ShapeDtypeStruct((g, p), F32),
                   jax.ShapeDtypeStruct((g, h, p), F32), jax.ShapeDtypeStruct((g, h, p), F32)),
        name="ssm_prep",
    )(a_re, a_im, log_dt.reshape(g, 1), bt_re, bt_im)


def _ssm_kernel(u_ref, h0_ref, are_ref, aim_ref, bre_ref, bim_ref, cre_ref, cim_ref, d_ref, wg_ref, bg_ref,
                gn_ref, o_ref, hlast_ref, hst_ref, bu_ref, y_ref, *, nb, tl, n_gt, width):
    sw = STATE_TILE

    @pl.when(pl.program_id(0) == 0)
    def _():
        hst_ref[...] = h0_ref[...]

    for gt in range(n_gt):
        cols = slice(gt * LANES, (gt + 1) * LANES)
        uf = u_ref[:, cols]
        ub = uf.astype(BF16)
        bu_ref[:, :sw] = jnp.dot(ub, bre_ref[gt], preferred_element_type=F32)
        bu_ref[:, sw:] = jnp.dot(ub, bim_ref[gt], preferred_element_type=F32)
        a_re = jnp.broadcast_to(are_ref[gt], (nb, sw))
        a_im = jnp.broadcast_to(aim_ref[gt], (nb, sw))

        def step(t, carry, a_re=a_re, a_im=a_im):
            h_re, h_im = carry
            r0 = pl.multiple_of(t * nb, nb)
            n_re = a_re * h_re - a_im * h_im + bu_ref[pl.ds(r0, nb), :sw]
            n_im = a_re * h_im + a_im * h_re + bu_ref[pl.ds(r0, nb), sw:]
            bu_ref[pl.ds(r0, nb), :sw] = n_re
            bu_ref[pl.ds(r0, nb), sw:] = n_im
            return n_re, n_im

        h_re, h_im = lax.fori_loop(0, tl, step, (hst_ref[gt, :, :sw], hst_ref[gt, :, sw:]))
        hst_ref[gt, :, :sw] = h_re
        hst_ref[gt, :, sw:] = h_im

        y = (jnp.dot(bu_ref[:, :sw].astype(BF16), cre_ref[gt], preferred_element_type=F32)
             - jnp.dot(bu_ref[:, sw:].astype(BF16), cim_ref[gt], preferred_element_type=F32)
             + d_ref[:, cols] * uf)
        y = _gelu(y)
        z = jnp.dot(y.astype(BF16), wg_ref[gt], preferred_element_type=F32) + bg_ref[:, cols]
        y_ref[:, cols] = y * jax.nn.sigmoid(z)

    yy = y_ref[...]
    ms = jnp.sum(yy * yy, axis=-1, keepdims=True) / width
    o_ref[...] = (yy * lax.rsqrt(ms + RMS_EPS) * gn_ref[...]).astype(BF16)
    hlast_ref[...] = hst_ref[...]


def _ssm(u_tm, h0, sp, nb, seq, width):
    n_gt = width // LANES
    tl = _tile(seq, 64)
    rows = tl * nb
    sw = STATE_TILE
    return pl.pallas_call(
        functools.partial(_ssm_kernel, nb=nb, tl=tl, n_gt=n_gt, width=width),
        out_shape=(jax.ShapeDtypeStruct((seq * nb, width), BF16),
                   jax.ShapeDtypeStruct((n_gt, nb, 2 * sw), F32)),
        grid=(seq // tl,),
        in_specs=[pl.BlockSpec((rows, width), lambda s: (s, 0)),
                  _full((n_gt, nb, 2 * sw)),
                  _full((n_gt, 1, sw)), _full((n_gt, 1, sw)),
                  _full((n_gt, LANES, sw)), _full((n_gt, LANES, sw)),
                  _full((n_gt, sw, LANES)), _full((n_gt, sw, LANES)),
                  _full((1, width)),
                  _full((n_gt, LANES, LANES)),
                  _full((1, width)), _full((1, width))],
        out_specs=(pl.BlockSpec((rows, width), lambda s: (s, 0)),
                   _full((n_gt, nb, 2 * sw))),
        scratch_shapes=[pltpu.VMEM((n_gt, nb, 2 * sw), F32),
                        pltpu.VMEM((rows, 2 * sw), F32),
                        pltpu.VMEM((rows, width), F32)],
        compiler_params=_params("arbitrary"),
        name="ssm",
    )(u_tm, h0, sp["a_re"], sp["a_im"], sp["b_re"], sp["b_im"], sp["c_re"], sp["c_im"], sp["d"],
      sp["wg"], sp["bg"], sp["gn"])


def _block_diag(m, n_gt):
    _, r, c = m.shape
    m4 = m.reshape(n_gt, GROUPS_PER_TILE, r, 1, c)
    eye = jnp.eye(GROUPS_PER_TILE, dtype=bool)[None, :, None, :, None]
    return jnp.where(eye, m4, 0).reshape(n_gt, GROUPS_PER_TILE * r, GROUPS_PER_TILE * c)


def _ssm_params(a_re, a_im, log_dt, b_re, b_im, c_re, c_im, d, glu_w, glu_b, gain, width):
    n_gt = width // LANES
    ab_re, ab_im, bb_re, bb_im = _ssm_prep(a_re, a_im, log_dt, jnp.swapaxes(b_re, 1, 2),
                                           jnp.swapaxes(b_im, 1, 2))
    return dict(
        a_re=ab_re.reshape(n_gt, 1, STATE_TILE), a_im=ab_im.reshape(n_gt, 1, STATE_TILE),
        b_re=_block_diag(bb_re, n_gt).astype(BF16), b_im=_block_diag(bb_im, n_gt).astype(BF16),
        c_re=_block_diag(jnp.swapaxes(c_re, 1, 2), n_gt).astype(BF16),
        c_im=_block_diag(jnp.swapaxes(c_im, 1, 2), n_gt).astype(BF16),
        d=d.reshape(1, width), wg=_block_diag(glu_w, n_gt).astype(BF16), bg=glu_b.reshape(1, width),
        gn=gain.reshape(1, width))


def _state_to_tiles(s_re, s_im, n_gt):
    nb = s_re.shape[0]
    def one(s):
        return jnp.swapaxes(s.reshape(nb, n_gt, STATE_TILE), 0, 1)
    return jnp.concatenate([one(s_re), one(s_im)], axis=-1)


def _tiles_to_state(h, n_groups):
    n_gt, nb, _ = h.shape
    def one(s):
        return jnp.swapaxes(s, 0, 1).reshape(nb, n_groups, SSM_STATE)
    return one(h[:, :, :STATE_TILE]), one(h[:, :, STATE_TILE:])


def _layer_norm(z, g, b):
    mu = jnp.mean(z, axis=-1, keepdims=True)
    zc = z - mu
    var = jnp.mean(zc * zc, axis=-1, keepdims=True)
    return zc * lax.rsqrt(var + LN_EPS) * g + b


def _mix_kernel(a_ref, s_ref, x_ref, wo_ref, g_ref, b_ref, x1_ref, x1b_ref, x1t_ref=None, *, alpha, width):
    tn_dims = (((0,), (0,)), ((), ()))
    mix = (lax.dot_general(a_ref[...], wo_ref[:width], tn_dims, preferred_element_type=F32)
           + jnp.dot(s_ref[...], wo_ref[width:], preferred_element_type=F32))
    x1 = _layer_norm(alpha * x_ref[...] + mix, g_ref[...], b_ref[...])
    x1_ref[...] = x1
    x1b_ref[...] = x1.astype(BF16)
    if x1t_ref is not None:
        x1t_ref[...] = x1.T.astype(BF16)


def _mix(att_t, ssm_tm, x2d, w_out_b, g, b, alpha, width):
    t, d = x2d.shape
    nb, _, seq = att_t.shape
    tm = _tile(seq, 512)
    per_seq = seq // tm
    row = lambda i: (i, 0)
    out_shape = [jax.ShapeDtypeStruct((t, d), F32), jax.ShapeDtypeStruct((t, d), BF16)]
    out_specs = [pl.BlockSpec((tm, d), row), pl.BlockSpec((tm, d), row)]
    if tm % LANES == 0:
        out_shape.append(jax.ShapeDtypeStruct((d, t), BF16))
        out_specs.append(pl.BlockSpec((d, tm), lambda i: (0, i)))
    outs = pl.pallas_call(
        functools.partial(_mix_kernel, alpha=alpha, width=width),
        out_shape=tuple(out_shape),
        grid=(t // tm,),
        in_specs=[pl.BlockSpec((None, width, tm), lambda i: (i // per_seq, 0, i % per_seq)),
                  pl.BlockSpec((tm, width), lambda i: (i % per_seq, i // per_seq)),
                  pl.BlockSpec((tm, d), row), _full((2 * width, d)), _full((1, d)), _full((1, d))],
        out_specs=tuple(out_specs),
        compiler_params=_params("parallel"),
        name="mix",
    )(att_t, ssm_tm, x2d, w_out_b, g.reshape(1, d), b.reshape(1, d))
    x1, x1b = outs[0], outs[1]
    return x1, x1b, (outs[2] if len(outs) == 3 else x1b.T)


def _sort16(v):
    v = list(v)
    for i, j in SORT16:
        v[i], v[j] = jnp.maximum(v[i], v[j]), jnp.minimum(v[i], v[j])
    return v


def _merge_top16(a, b):
    v = [jnp.maximum(a[j], b[PEER_TOPK - 1 - j]) for j in range(PEER_TOPK)]
    for dist in (8, 4, 2, 1):
        for i in range(PEER_TOPK):
            if not i & dist:
                v[i], v[i + dist] = jnp.maximum(v[i], v[i + dist]), jnp.minimum(v[i], v[i + dist])
    return v


def _fold_sublanes(v):
    for shift in (4, 2, 1):
        v = _merge_top16(v, [pltpu.roll(x, shift, 0) for x in v])
    return v


def _count_prefix(v, pred):
    assert len(v) == PEER_TOPK == 16
    full = pred(v[15])
    b3 = pred(v[7])
    b2 = pred(jnp.where(b3, v[11], v[3]))
    b1 = pred(jnp.where(b3, jnp.where(b2, v[13], v[9]), jnp.where(b2, v[5], v[1])))
    quad = [jnp.where(b1, v[4 * k + 2], v[4 * k]) for k in range(4)]
    b0 = pred(jnp.where(b3, jnp.where(b2, quad[3], quad[2]), jnp.where(b2, quad[1], quad[0])))
    n = (jnp.where(b3, 8.0, 0.0) + jnp.where(b2, 4.0, 0.0)) + (jnp.where(b1, 2.0, 0.0) + jnp.where(b0, 1.0, 0.0))
    return jnp.where(full, 16.0, n)


def _route_kernel(x_ref, wq_ref, keys_ref, lim_ref, f_ref, r2_ref, e2_ref, *, tt, tc):
    qp = jnp.dot(x_ref[...], wq_ref[...], preferred_element_type=F32).astype(BF16)
    nt_dims = (((1,), (1,)), ((), ()))
    s_t = [lax.dot_general(keys_ref[0, c], qp[:, c * PEER_DHALF:(c + 1) * PEER_DHALF], nt_dims,
                           preferred_element_type=F32) for c in range(2)]
    n_blocks = N_KEYS // SUBLANES
    for l0 in range(0, tt, tc):
        cols = slice(l0, l0 + tc)
        sub = lax.broadcasted_iota(jnp.int32, (SUBLANES, tc), 0)
        blocks = [[s_t[c][SUBLANES * j:SUBLANES * (j + 1), cols] for j in range(n_blocks)] for c in range(2)]
        v1, v2 = (_fold_sublanes(_sort16(blocks[c])) for c in range(2))
        v1_lo, v1_hi = v1[0], v1[SUBLANES]
        for r in range(1, SUBLANES):
            v1_lo = jnp.where(sub == r, v1[r], v1_lo)
            v1_hi = jnp.where(sub == r, v1[SUBLANES + r], v1_hi)
        cand_lo = [v1_lo + v2[b] for b in range(PEER_TOPK)]
        cand_hi = [v1_hi + v2[b] for b in range(PEER_TOPK)]
        top_s = _fold_sublanes(_merge_top16(cand_lo, cand_hi))
        tau = top_s[PEER_TOPK - 1]
        z = jnp.ones((SUBLANES, tc), F32)
        for k in range(1, PEER_TOPK):
            z = z + jnp.exp(top_s[k] - top_s[0])
        zinv = 1.0 / z
        r2, e2 = [], []
        for j in range(n_blocks):
            s1, s2 = blocks[0][j], blocks[1][j]
            cnt = _count_prefix(v2, lambda p, s1=s1: s1 + p >= tau)
            rank = _count_prefix(v2, lambda p, s2=s2: p > s2)
            lim_j = jnp.where(s1 >= v1[PEER_TOPK - 1], cnt, 0.0)
            f_j = jnp.exp(s1 - v1[0]) * zinv
            for g in range(SUBLANES // KEYS_PER_BLOCK):
                rows = slice(g * KEYS_PER_BLOCK, (g + 1) * KEYS_PER_BLOCK)
                lim_ref[0, j * (SUBLANES // KEYS_PER_BLOCK) + g, :, cols] = lim_j[rows]
                f_ref[0, j * (SUBLANES // KEYS_PER_BLOCK) + g, :, cols] = f_j[rows]
            r2.append(rank)
            e2.append(jnp.exp(s2 - v2[0]))
        r2_ref[0, :, cols] = jnp.concatenate(r2, axis=0).astype(BF16)
        e2_ref[0, :, cols] = jnp.concatenate(e2, axis=0).astype(BF16)


def _route(x1b, wq_b, keys_b, tt):
    t, d = x1b.shape
    dk = 2 * PEER_DHALF
    tc = _tile(tt, 256)
    big_spec = pl.BlockSpec((1, N_KEYS, tt), lambda i, h: (h, 0, i))
    n_eb = N_KEYS // KEYS_PER_BLOCK
    row_shape = jax.ShapeDtypeStruct((PEER_HEADS, n_eb, KEYS_PER_BLOCK, t), F32)
    row_spec = pl.BlockSpec((1, n_eb, KEYS_PER_BLOCK, tt), lambda i, h: (h, 0, 0, i))
    return pl.pallas_call(
        functools.partial(_route_kernel, tt=tt, tc=tc),
        out_shape=(row_shape, row_shape,
                   jax.ShapeDtypeStruct((PEER_HEADS, N_KEYS, t), BF16),
                   jax.ShapeDtypeStruct((PEER_HEADS, N_KEYS, t), BF16)),
        grid=(t // tt, PEER_HEADS),
        in_specs=[pl.BlockSpec((tt, d), lambda i, h: (i, 0)),
                  pl.BlockSpec((d, dk), lambda i, h: (0, h)),
                  pl.BlockSpec((1, 2, N_KEYS, PEER_DHALF), lambda i, h: (h, 0, 0, 0))],
        out_specs=(row_spec, row_spec, big_spec, big_spec),
        compiler_params=_params("parallel", "parallel"),
        name="route",
    )(x1b, wq_b, keys_b)


def _peer_kernel(xt_ref, u_ref, vt_ref, lim_ref, f_ref, r2_ref, e2_ref, y_ref, a_ref, hid_ref,
                 *, n_blocks, tt):
    s = pl.program_id(0)
    slot = lax.rem(s, 2)
    rc = 2 * SUBLANES

    @pl.when(s == 0)
    def _():
        hid_ref[1] = jnp.zeros(hid_ref.shape[1:], BF16)

    @pl.when(lax.rem(jnp.maximum(s - 1, 0), n_blocks) == 0)
    def _():
        y_ref[...] = jnp.zeros_like(y_ref)

    a_ref[...] = jnp.dot(u_ref[...], xt_ref[...], preferred_element_type=F32)
    y_ref[...] += jnp.dot(vt_ref[...], hid_ref[1 - slot], preferred_element_type=F32)
    zero = jnp.zeros((rc, tt), BF16)
    for q in range(KEYS_PER_BLOCK):
        w = [zero] * (N_KEYS // rc)
        for h in range(PEER_HEADS):
            lim_t = jnp.broadcast_to(lim_ref[h, q:q + 1, :], (rc, tt)).astype(BF16)
            f_t = jnp.broadcast_to(f_ref[h, q:q + 1, :], (rc, tt)).astype(BF16)
            for k in range(N_KEYS // rc):
                rows = slice(k * rc, (k + 1) * rc)
                w[k] = w[k] + jnp.where(r2_ref[h, rows, :] < lim_t, e2_ref[h, rows, :], zero) * f_t
        for k in range(N_KEYS // rc):
            rows = slice(q * N_KEYS + k * rc, q * N_KEYS + (k + 1) * rc)
            hid_ref[slot, rows, :] = _gelu(a_ref[rows, :]).astype(BF16) * w[k]


def _peer(x1t, u_b, vt_b, lim, f, r2, e2, tt):
    d, t = x1t.shape
    eb = EXPERT_BLOCK
    n_blocks = u_b.shape[0] // eb
    n_tiles = t // tt
    tile = lambda s: jnp.minimum(s // n_blocks, n_tiles - 1)
    prev = lambda s: jnp.maximum(s - 1, 0)
    row_spec = pl.BlockSpec((PEER_HEADS, None, KEYS_PER_BLOCK, tt), lambda s: (0, s % n_blocks, 0, tile(s)))
    big_spec = pl.BlockSpec((PEER_HEADS, N_KEYS, tt), lambda s: (0, 0, tile(s)))
    return pl.pallas_call(
        functools.partial(_peer_kernel, n_blocks=n_blocks, tt=tt),
        out_shape=jax.ShapeDtypeStruct((d, t), F32),
        grid=(n_tiles * n_blocks + 1,),
        in_specs=[pl.BlockSpec((d, tt), lambda s: (0, tile(s))),
                  pl.BlockSpec((eb, d), lambda s: (s % n_blocks, 0)),
                  pl.BlockSpec((None, d, eb), lambda s: (prev(s) % n_blocks, 0, 0)),
                  row_spec, row_spec, big_spec, big_spec],
        out_specs=pl.BlockSpec((d, tt), lambda s: (0, prev(s) // n_blocks)),
        scratch_shapes=[pltpu.VMEM((eb, tt), F32), pltpu.VMEM((2, eb, tt), BF16)],
        compiler_params=_params("arbitrary"),
        name="peer",
    )(x1t, u_b, vt_b, lim, f, r2, e2)


def _ln2_kernel(x1_ref, yt_ref, g_ref, b_ref, o_ref, *, alpha):
    o_ref[...] = _layer_norm(alpha * x1_ref[...] + yt_ref[...].T, g_ref[...], b_ref[...])


def _ln2(x1, y_t, g, b, alpha):
    t, d = x1.shape
    tm = _tile(t, 512)
    row = lambda i: (i, 0)
    return pl.pallas_call(
        functools.partial(_ln2_kernel, alpha=alpha),
        out_shape=jax.ShapeDtypeStruct((t, d), F32),
        grid=(t // tm,),
        in_specs=[pl.BlockSpec((tm, d), row), pl.BlockSpec((d, tm), lambda i: (0, i)), _full((1, d)),
                  _full((1, d))],
        out_specs=pl.BlockSpec((tm, d), row),
        compiler_params=_params("parallel"),
        name="ln2",
    )(x1, y_t, g.reshape(1, d), b.reshape(1, d))


def _rel_bias_table(rel_bias):
    n_left = ATT_WINDOW - REL_CLIP + CHUNK - 1
    n_right = CHUNK - 1 - REL_CLIP
    ext = jnp.concatenate([jnp.repeat(rel_bias[:, :1], n_left, axis=1), rel_bias,
                           jnp.repeat(rel_bias[:, -1:], max(n_right, 0), axis=1)], axis=1)
    return jnp.stack([ext[:, CHUNK - 1 - i:CHUNK - 1 - i + BAND] for i in range(CHUNK)], axis=1)


def _step_bias(table, cpb):
    n_heads = table.shape[0]
    table = (table - table[:, :1, :1]) * LOG2E
    tab_t = jnp.swapaxes(table, 1, 2)
    per_chunk = [jnp.pad(tab_t, ((0, 0), (c * CHUNK, (cpb - 1 - c) * CHUNK), (0, 0)),
                         constant_values=MASK_VALUE) for c in range(cpb)]
    both = jnp.stack(per_chunk, axis=2)
    rows = both.shape[1]
    both = both.reshape(n_heads // 2, 2, rows, cpb * CHUNK)
    return jnp.swapaxes(both, 1, 2).reshape(n_heads // 2, rows, 2 * cpb * CHUNK)


def _encoder_layer(x, hist_k, hist_v, h0, wts, alpha):
    nb, seq, d = x.shape
    width = d // 2
    t = nb * seq
    n_gt = width // LANES
    assert seq % CHUNK == 0 and width % LANES == 0 and nb % SUBLANES == 0
    assert hist_k is None or seq == CHUNK
    x2d = x.reshape(t, d)

    n_hist = 0 if hist_k is None else hist_k.shape[1]
    n_invalid = ATT_WINDOW - n_hist
    def history(hist, n_new):
        if hist is None:
            return jnp.zeros((nb, ATT_WINDOW + n_new, width), BF16)
        return jnp.pad(hist.reshape(nb, n_hist, width).astype(BF16), ((0, 0), (n_invalid, n_new), (0, 0)))
    if _tile(seq, 256) % LANES == 0:
        q_b, kpad, vtpad, k_new, v_new, u_tm = _proj(
            x2d, wts["w_in"], history(hist_k, seq), jnp.swapaxes(history(hist_v, seq), 1, 2), nb, seq, width)
    else:
        q_b, k_b, v_b, k_new, v_new, u_tm = _proj_short(x2d, wts["w_in"], nb, seq, width)
        kpad = jnp.concatenate([history(hist_k, 0), k_b.reshape(nb, seq, width)], axis=1)
        vtpad = jnp.swapaxes(jnp.concatenate([history(hist_v, 0), v_b.reshape(nb, seq, width)], axis=1), 1, 2)
    n_chunks = seq // CHUNK
    cpb = 2 if n_chunks % 2 == 0 else 1
    assert cpb == 2 or n_chunks == 1
    att_t = _attention(q_b, kpad, vtpad, _step_bias(wts["bias"], cpb), wts["norm_attn_g"], nb, seq, width,
                       cpb, n_invalid)

    ssm_tm, h_last = _ssm(u_tm.reshape(t, width), h0, wts["ssm"], nb, seq, width)

    x1, x1b, x1t = _mix(att_t, ssm_tm.reshape(seq, nb * width), x2d, wts["w_out"], wts["ln1_g"],
                        wts["ln1_b"], alpha, width)

    tt = _tile(t, PEER_TOKENS)
    lim, f, r2, e2 = _route(x1b, wts["peer_wq"], wts["peer_keys"], tt)
    y_t = _peer(x1t, wts["peer_u"], wts["peer_vt"], lim, f, r2, e2, tt)
    x2 = _ln2(x1, y_t, wts["ln2_g"], wts["ln2_b"], alpha)

    n_heads = width // HEAD_DIM
    return (x2.reshape(nb, seq, d), k_new.reshape(nb, -1, n_heads, HEAD_DIM),
            v_new.reshape(nb, -1, n_heads, HEAD_DIM), h_last)


def kernel(x_prompt, x_sample, cache_attn_k, cache_attn_v, state_ssm_re, state_ssm_im, w_in, rel_bias, norm_attn_g, ssm_a_re, ssm_a_im, ssm_log_dt, ssm_b_re, ssm_b_im, ssm_c_re, ssm_c_im, ssm_d, ssm_glu_w, ssm_glu_b, norm_ssm_g, w_out, ln1_g, ln1_b, peer_wq, peer_keys, peer_u, peer_v, ln2_g, ln2_b):
    depth = w_in.shape[0]
    alpha = (2.0 * depth) ** 0.25
    d = x_prompt.shape[-1]
    width = d // 2
    n_gt = width // LANES
    n_groups = width // SSM_GROUP
    keep = min(ATT_WINDOW, x_prompt.shape[1])

    yp, ys = x_prompt, x_sample
    outs = [[] for _ in range(8)]
    for l in range(depth):
        wts = dict(
            w_in=w_in[l].astype(BF16),
            bias=_rel_bias_table(rel_bias[l]),
            norm_attn_g=jnp.broadcast_to(norm_attn_g[l][:, None], (width, LANES)),
            ssm=_ssm_params(ssm_a_re[l], ssm_a_im[l], ssm_log_dt[l], ssm_b_re[l], ssm_b_im[l], ssm_c_re[l],
                            ssm_c_im[l], ssm_d[l], ssm_glu_w[l], ssm_glu_b[l], norm_ssm_g[l], width),
            w_out=w_out[l].astype(BF16), ln1_g=ln1_g[l], ln1_b=ln1_b[l],
            peer_wq=peer_wq[l].astype(BF16), peer_keys=peer_keys[l].astype(BF16),
            peer_u=peer_u[l].astype(BF16),
            peer_vt=jnp.swapaxes(peer_v[l].astype(BF16).reshape(-1, EXPERT_BLOCK, d), 1, 2),
            ln2_g=ln2_g[l], ln2_b=ln2_b[l])
        h0p = jnp.zeros((n_gt, x_prompt.shape[0], 2 * STATE_TILE), F32)
        yp, kp, vp, hp = _encoder_layer(yp, None, None, h0p, wts, alpha)
        h0s = _state_to_tiles(state_ssm_re[l].astype(F32), state_ssm_im[l].astype(F32), n_gt)
        ys, kn, vn, hs = _encoder_layer(ys, cache_attn_k[l], cache_attn_v[l], h0s, wts, alpha)
        hp_re, hp_im = _tiles_to_state(hp, n_groups)
        hs_re, hs_im = _tiles_to_state(hs, n_groups)
        assert kp.shape[1] == keep
        for acc, val in zip(outs, (kp, vp, hp_re, hp_im, kn, vn, hs_re, hs_im)):
            acc.append(val)
    return (yp, ys) + tuple(jnp.stack(o) for o in outs)
```

```python
import functools

import jax
import jax.numpy as jnp
from jax import lax
from jax.experimental import pallas as pl
from jax.experimental.pallas import tpu as pltpu

F32 = jnp.float32
BF16 = jnp.bfloat16

CHUNK = 64
LEFT_CHUNKS = 8
ATT_WINDOW = LEFT_CHUNKS * CHUNK
BAND = ATT_WINDOW + CHUNK
HEAD_DIM = 64
REL_CLIP = 2 * CHUNK
SSM_GROUP = 16
SSM_STATE = 64
PEER_HEADS = 8
PEER_TOPK = 16
N_KEYS = 128
PEER_DHALF = 128
LN_EPS = 1e-5
RMS_EPS = 1e-6
MASK_VALUE = -1e30
LOG2E = 1.4426950408889634
Q_SCALE = HEAD_DIM ** -0.5 * LOG2E

LANES = 128
SUBLANES = 8
GROUPS_PER_TILE = LANES // SSM_GROUP
STATE_TILE = GROUPS_PER_TILE * SSM_STATE
VMEM_LIMIT = 56 * 1024 * 1024
EXPERT_BLOCK = 1024
KEYS_PER_BLOCK = EXPERT_BLOCK // N_KEYS
PEER_TOKENS = 512
ATTN_PAIR_GROUP = 8

SORT16 = (
    (0, 13), (1, 12), (2, 15), (3, 14), (4, 8), (5, 6), (7, 11), (9, 10),
    (0, 5), (1, 7), (2, 9), (3, 4), (6, 13), (8, 14), (10, 15), (11, 12),
    (0, 1), (2, 3), (4, 5), (6, 8), (7, 9), (10, 11), (12, 13), (14, 15),
    (0, 2), (1, 3), (4, 10), (5, 11), (6, 7), (8, 9), (12, 14), (13, 15),
    (1, 2), (3, 12), (4, 6), (5, 7), (8, 10), (9, 11), (13, 14),
    (1, 4), (2, 6), (5, 8), (7, 10), (9, 13), (11, 14),
    (2, 4), (3, 6), (9, 12), (11, 13),
    (3, 5), (6, 8), (7, 9), (10, 12),
    (3, 4), (5, 6), (7, 8), (9, 10), (11, 12),
    (6, 7), (8, 9),
)


def _params(*semantics):
    return pltpu.CompilerParams(dimension_semantics=semantics, vmem_limit_bytes=VMEM_LIMIT)


def _tile(n, pref):
    t = min(n, pref)
    while n % t:
        t //= 2
    return t


def _gelu(x):
    return 0.5 * x * (1.0 + lax.erf(x * 0.7071067811865476))


def _full(shape):
    return pl.BlockSpec(shape, lambda *_: (0,) * len(shape))


def _proj_kernel(x_ref, w_ref, kbase_ref, vtbase_ref, q_ref, kpad_ref, vtpad_ref, kf_ref, vf_ref, u_ref,
                 *, width):
    del kbase_ref, vtbase_ref
    xb = x_ref[...].astype(BF16)
    acc = [jnp.dot(xb, w_ref[:, c * width:(c + 1) * width], preferred_element_type=F32) for c in range(4)]
    q_ref[...] = (acc[0] * Q_SCALE).astype(BF16)
    kf_ref[...] = acc[1]
    kpad_ref[...] = acc[1].astype(BF16)
    vf_ref[...] = acc[2]
    vtpad_ref[...] = acc[2].T.astype(BF16)
    u_ref[...] = acc[3]


def _proj(x2d, w_in_b, kbase, vtbase, nb, seq, width):
    t, d = x2d.shape
    tm = _tile(seq, 256)
    assert tm % LANES == 0 and ATT_WINDOW % tm == 0
    per_seq = seq // tm
    skip = ATT_WINDOW // tm
    keep = min(ATT_WINDOW, seq)
    dropped = per_seq - keep // tm
    row = lambda i: (i, 0)
    tail = lambda i: (i // per_seq, jnp.maximum(i % per_seq - dropped, 0), 0)
    tail_f32 = jax.ShapeDtypeStruct((nb, keep, width), F32)
    return pl.pallas_call(
        functools.partial(_proj_kernel, width=width),
        out_shape=(jax.ShapeDtypeStruct((t, width), BF16),
                   jax.ShapeDtypeStruct(kbase.shape, BF16), jax.ShapeDtypeStruct(vtbase.shape, BF16),
                   tail_f32, tail_f32, jax.ShapeDtypeStruct((seq, nb * width), F32)),
        grid=(t // tm,),
        in_specs=[pl.BlockSpec((tm, d), row),
                  pl.BlockSpec((d, 4 * width), lambda i: (0, 0)),
                  pl.BlockSpec(memory_space=pl.ANY), pl.BlockSpec(memory_space=pl.ANY)],
        out_specs=(pl.BlockSpec((tm, width), row),
                   pl.BlockSpec((None, tm, width), lambda i: (i // per_seq, skip + i % per_seq, 0)),
                   pl.BlockSpec((None, width, tm), lambda i: (i // per_seq, 0, skip + i % per_seq)),
                   pl.BlockSpec((None, tm, width), tail), pl.BlockSpec((None, tm, width), tail),
                   pl.BlockSpec((tm, width), lambda i: (i % per_seq, i // per_seq))),
        input_output_aliases={2: 1, 3: 2},
        compiler_params=_params("arbitrary"),
        name="proj",
    )(x2d, w_in_b, kbase, vtbase)


def _proj_short_kernel(x_ref, w_ref, q_ref, k_ref, v_ref, kf_ref, vf_ref, u_ref, *, width):
    xb = x_ref[...].astype(BF16)
    acc = [jnp.dot(xb, w_ref[:, c * width:(c + 1) * width], preferred_element_type=F32) for c in range(4)]
    q_ref[...] = (acc[0] * Q_SCALE).astype(BF16)
    kf_ref[...] = acc[1]
    k_ref[...] = acc[1].astype(BF16)
    vf_ref[...] = acc[2]
    v_ref[...] = acc[2].astype(BF16)
    u_ref[...] = acc[3]


def _proj_short(x2d, w_in_b, nb, seq, width):
    t, d = x2d.shape
    tm = _tile(seq, 256)
    per_seq = seq // tm
    row = lambda i: (i, 0)
    rows_f32 = jax.ShapeDtypeStruct((t, width), F32)
    rows_b16 = jax.ShapeDtypeStruct((t, width), BF16)
    return pl.pallas_call(
        functools.partial(_proj_short_kernel, width=width),
        out_shape=(rows_b16, rows_b16, rows_b16, rows_f32, rows_f32,
                   jax.ShapeDtypeStruct((seq, nb * width), F32)),
        grid=(t // tm,),
        in_specs=[pl.BlockSpec((tm, d), row),
                  pl.BlockSpec((d, 4 * width), lambda i: (0, 0))],
        out_specs=tuple([pl.BlockSpec((tm, width), row)] * 5
                        + [pl.BlockSpec((tm, width), lambda i: (i % per_seq, i // per_seq))]),
        compiler_params=_params("parallel"),
        name="proj",
    )(x2d, w_in_b)


def _attn_kernel(q_ref, k_ref, vt_ref, bias_ref, g_ref, o_ref, *, n_steps, cpb, n_invalid, width):
    nq = cpb * CHUNK
    nc = 2 * nq
    kb = ATT_WINDOW + nq
    lo = lax.broadcasted_iota(jnp.int32, (nq, LANES), 1) < HEAD_DIM
    nt_dims = (((1,), (1,)), ((), ()))
    group = ATTN_PAIR_GROUP
    k_split = -(-(kb // 2) // LANES) * LANES
    halves = ((0, k_split), (k_split, kb - k_split))
    z0, z1 = (cpb - 1) * CHUNK, ATT_WINDOW - REL_CLIP

    def pair_scores(r0, hp):
        cols = slice(hp * LANES, (hp + 1) * LANES)
        q2 = q_ref[pl.ds(r0, nq), cols]
        qcat = jnp.concatenate([jnp.where(lo, q2, 0), jnp.where(lo, 0, q2)], axis=0)
        return jnp.concatenate(
            [lax.dot_general(k_ref[pl.ds(r0 + k0, kn), cols], qcat, nt_dims, preferred_element_type=F32)
             for k0, kn in halves], axis=0)

    def step(r0, masked):
        tiles = []
        ssq = jnp.zeros((1, nq), F32)
        n_pairs = width // LANES
        for g0 in range(0, n_pairs, group):
            pairs = range(g0, min(g0 + group, n_pairs))
            s = jnp.concatenate([pair_scores(r0, hp) for hp in pairs], axis=1)
            parts = [s[z0:z1], s[z1:] + jnp.concatenate([bias_ref[hp, z1:, :] for hp in pairs], axis=1)]
            if z0:
                parts.insert(0, s[:z0] + jnp.concatenate([bias_ref[hp, :z0, :] for hp in pairs], axis=1))
            s = jnp.concatenate(parts, axis=0)
            if masked:
                s = jnp.where(lax.broadcasted_iota(jnp.int32, s.shape, 0) + r0 >= n_invalid, s, MASK_VALUE)
            m = jnp.max(s, axis=0, keepdims=True)
            e = jnp.exp2(s - m)
            rl = 1.0 / jnp.sum(e, axis=0, keepdims=True)
            p = e.astype(BF16)
            for i, hp in enumerate(pairs):
                cols = slice(hp * LANES, (hp + 1) * LANES)
                pc = slice(i * nc, (i + 1) * nc)
                o_t = sum(jnp.dot(vt_ref[cols, pl.ds(r0 + k0, kn)], p[k0:k0 + kn, pc],
                                  preferred_element_type=F32) for k0, kn in halves) * rl[:, pc]
                tile = jnp.concatenate([o_t[:HEAD_DIM, :nq], o_t[HEAD_DIM:, nq:]], axis=0)
                ssq = ssq + jnp.sum(tile * tile, axis=0, keepdims=True)
                tiles.append(tile)
        rinv = lax.rsqrt(ssq / width + RMS_EPS)
        for hp, tile in enumerate(tiles):
            rows = slice(hp * LANES, (hp + 1) * LANES)
            o_ref[rows, pl.ds(r0, nq)] = (tile * rinv * g_ref[rows, :nq]).astype(BF16)

    n_masked = min(n_steps, -(-n_invalid // nq))
    if n_steps == 1:
        step(0, n_masked > 0)
    else:
        def body(i, carry, masked):
            step(pl.multiple_of(i * nq, nq), masked)
            return carry
        lax.fori_loop(0, n_masked, functools.partial(body, masked=True), 0)
        lax.fori_loop(n_masked, n_steps, functools.partial(body, masked=False), 0)


def _attention(qkv, kpad, vtpad, bias, gain_b, nb, seq, width, cpb, n_invalid):
    lp = kpad.shape[1]
    nq = cpb * CHUNK
    return pl.pallas_call(
        functools.partial(_attn_kernel, n_steps=seq // nq, cpb=cpb, n_invalid=n_invalid, width=width),
        out_shape=jax.ShapeDtypeStruct((nb, width, seq), BF16),
        grid=(nb,),
        in_specs=[pl.BlockSpec((seq, width), lambda b: (b, 0)),
                  pl.BlockSpec((None, lp, width), lambda b: (b, 0, 0)),
                  pl.BlockSpec((None, width, lp), lambda b: (b, 0, 0)),
                  _full(bias.shape),
                  _full((width, LANES))],
        out_specs=pl.BlockSpec((None, width, seq), lambda b: (b, 0, 0)),
        compiler_params=_params("parallel"),
        name="attn",
    )(qkv, kpad, vtpad, bias, gain_b)


def _ssm_prep_kernel(are_ref, aim_ref, ldt_ref, bre_ref, bim_ref, abr_ref, abi_ref, bbr_ref, bbi_ref):
    a_re = are_ref[...]
    a_im = aim_ref[...]
    dt = jnp.exp(ldt_ref[...])
    mag = jnp.exp(a_re * dt)
    ang = a_im * dt
    ab_re = mag * jnp.cos(ang)
    ab_im = mag * jnp.sin(ang)
    n_re = ab_re - 1.0
    den = a_re * a_re + a_im * a_im
    c_re = (n_re * a_re + ab_im * a_im) / den
    c_im = (ab_im * a_re - n_re * a_im) / den
    abr_ref[...] = ab_re
    abi_ref[...] = ab_im
    b_re = bre_ref[...]
    b_im = bim_ref[...]
    bbr_ref[...] = c_re[:, None, :] * b_re - c_im[:, None, :] * b_im
    bbi_ref[...] = c_re[:, None, :] * b_im + c_im[:, None, :] * b_re


def _ssm_prep(a_re, a_im, log_dt, bt_re, bt_im):
    g, p = a_re.shape
    h = bt_re.shape[1]
    return pl.pallas_call(
        _ssm_prep_kernel,
        out_shape=(jax.ShapeDtypeStruct((g, p), F32), jax.ShapeDtypeStruct((g, p), F32),
                   jax.ShapeDtypeStruct((g, h, p), F32), jax.ShapeDtypeStruct((g, h, p), F32)),
        name="ssm_prep",
    )(a_re, a_im, log_dt.reshape(g, 1), bt_re, bt_im)


def _ssm_kernel(u_ref, h0_ref, are_ref, aim_ref, bre_ref, bim_ref, cre_ref, cim_ref, d_ref, wg_ref, bg_ref,
                gn_ref, o_ref, hlast_ref, hst_ref, bu_ref, y_ref, *, nb, tl, n_gt, width):
    sw = STATE_TILE

    @pl.when(pl.program_id(0) == 0)
    def _():
        hst_ref[...] = h0_ref[...]

    for gt in range(n_gt):
        cols = slice(gt * LANES, (gt + 1) * LANES)
        uf = u_ref[:, cols]
        ub = uf.astype(BF16)
        bu_ref[:, :sw] = jnp.dot(ub, bre_ref[gt], preferred_element_type=F32)
        bu_ref[:, sw:] = jnp.dot(ub, bim_ref[gt], preferred_element_type=F32)
        a_re = jnp.broadcast_to(are_ref[gt], (nb, sw))
        a_im = jnp.broadcast_to(aim_ref[gt], (nb, sw))

        def step(t, carry, a_re=a_re, a_im=a_im):
            h_re, h_im = carry
            r0 = pl.multiple_of(t * nb, nb)
            n_re = a_re * h_re - a_im * h_im + bu_ref[pl.ds(r0, nb), :sw]
            n_im = a_re * h_im + a_im * h_re + bu_ref[pl.ds(r0, nb), sw:]
            bu_ref[pl.ds(r0, nb), :sw] = n_re
            bu_ref[pl.ds(r0, nb), sw:] = n_im
            return n_re, n_im

        h_re, h_im = lax.fori_loop(0, tl, step, (hst_ref[gt, :, :sw], hst_ref[gt, :, sw:]), unroll=True)
        hst_ref[gt, :, :sw] = h_re
        hst_ref[gt, :, sw:] = h_im

        y = (jnp.dot(bu_ref[:, :sw].astype(BF16), cre_ref[gt], preferred_element_type=F32)
             - jnp.dot(bu_ref[:, sw:].astype(BF16), cim_ref[gt], preferred_element_type=F32)
             + d_ref[:, cols] * uf)
        y = _gelu(y)
        z = jnp.dot(y.astype(BF16), wg_ref[gt], preferred_element_type=F32) + bg_ref[:, cols]
        y_ref[:, cols] = y * jax.nn.sigmoid(z)

    yy = y_ref[...]
    ms = jnp.sum(yy * yy, axis=-1, keepdims=True) / width
    o_ref[...] = (yy * lax.rsqrt(ms + RMS_EPS) * gn_ref[...]).astype(BF16)
    hlast_ref[...] = hst_ref[...]


def _ssm(u_tm, h0, sp, nb, seq, width):
    n_gt = width // LANES
    tl = _tile(seq, 64)
    rows = tl * nb
    sw = STATE_TILE
    return pl.pallas_call(
        functools.partial(_ssm_kernel, nb=nb, tl=tl, n_gt=n_gt, width=width),
        out_shape=(jax.ShapeDtypeStruct((seq * nb, width), BF16),
                   jax.ShapeDtypeStruct((n_gt, nb, 2 * sw), F32)),
        grid=(seq // tl,),
        in_specs=[pl.BlockSpec((rows, width), lambda s: (s, 0)),
                  _full((n_gt, nb, 2 * sw)),
                  _full((n_gt, 1, sw)), _full((n_gt, 1, sw)),
                  _full((n_gt, LANES, sw)), _full((n_gt, LANES, sw)),
                  _full((n_gt, sw, LANES)), _full((n_gt, sw, LANES)),
                  _full((1, width)),
                  _full((n_gt, LANES, LANES)),
                  _full((1, width)), _full((1, width))],
        out_specs=(pl.BlockSpec((rows, width), lambda s: (s, 0)),
                   _full((n_gt, nb, 2 * sw))),
        scratch_shapes=[pltpu.VMEM((n_gt, nb, 2 * sw), F32),
                        pltpu.VMEM((rows, 2 * sw), F32),
                        pltpu.VMEM((rows, width), F32)],
        compiler_params=_params("arbitrary"),
        name="ssm",
    )(u_tm, h0, sp["a_re"], sp["a_im"], sp["b_re"], sp["b_im"], sp["c_re"], sp["c_im"], sp["d"],
      sp["wg"], sp["bg"], sp["gn"])


def _block_diag(m, n_gt):
    _, r, c = m.shape
    m4 = m.reshape(n_gt, GROUPS_PER_TILE, r, 1, c)
    eye = jnp.eye(GROUPS_PER_TILE, dtype=bool)[None, :, None, :, None]
    return jnp.where(eye, m4, 0).reshape(n_gt, GROUPS_PER_TILE * r, GROUPS_PER_TILE * c)


def _ssm_params(a_re, a_im, log_dt, b_re, b_im, c_re, c_im, d, glu_w, glu_b, gain, width):
    n_gt = width // LANES
    ab_re, ab_im, bb_re, bb_im = _ssm_prep(a_re, a_im, log_dt, jnp.swapaxes(b_re, 1, 2),
                                           jnp.swapaxes(b_im, 1, 2))
    return dict(
        a_re=ab_re.reshape(n_gt, 1, STATE_TILE), a_im=ab_im.reshape(n_gt, 1, STATE_TILE),
        b_re=_block_diag(bb_re, n_gt).astype(BF16), b_im=_block_diag(bb_im, n_gt).astype(BF16),
        c_re=_block_diag(jnp.swapaxes(c_re, 1, 2), n_gt).astype(BF16),
        c_im=_block_diag(jnp.swapaxes(c_im, 1, 2), n_gt).astype(BF16),
        d=d.reshape(1, width), wg=_block_diag(glu_w, n_gt).astype(BF16), bg=glu_b.reshape(1, width),
        gn=gain.reshape(1, width))


def _state_to_tiles(s_re, s_im, n_gt):
    nb = s_re.shape[0]
    def one(s):
        return jnp.swapaxes(s.reshape(nb, n_gt, STATE_TILE), 0, 1)
    return jnp.concatenate([one(s_re), one(s_im)], axis=-1)


def _tiles_to_state(h, n_groups):
    n_gt, nb, _ = h.shape
    def one(s):
        return jnp.swapaxes(s, 0, 1).reshape(nb, n_groups, SSM_STATE)
    return one(h[:, :, :STATE_TILE]), one(h[:, :, STATE_TILE:])


def _layer_norm(z, g, b):
    mu = jnp.mean(z, axis=-1, keepdims=True)
    zc = z - mu
    var = jnp.mean(zc * zc, axis=-1, keepdims=True)
    return zc * lax.rsqrt(var + LN_EPS) * g + b


def _mix_kernel(a_ref, s_ref, x_ref, wo_ref, g_ref, b_ref, x1_ref, x1b_ref, x1t_ref=None, *, alpha, width):
    tn_dims = (((0,), (0,)), ((), ()))
    tm = x_ref.shape[0]
    n_split = 2 if tm % (2 * LANES) == 0 else 1
    for r0 in range(0, tm, tm // n_split):
        rows = slice(r0, r0 + tm // n_split)
        mix = (lax.dot_general(a_ref[:, rows], wo_ref[:width], tn_dims, preferred_element_type=F32)
               + jnp.dot(s_ref[rows, :], wo_ref[width:], preferred_element_type=F32))
        x1 = _layer_norm(alpha * x_ref[rows, :] + mix, g_ref[...], b_ref[...])
        x1_ref[rows, :] = x1
        x1b_ref[rows, :] = x1.astype(BF16)
        if x1t_ref is not None:
            x1t_ref[:, rows] = x1.T.astype(BF16)


def _mix(att_t, ssm_tm, x2d, w_out_b, g, b, alpha, width):
    t, d = x2d.shape
    nb, _, seq = att_t.shape
    tm = _tile(seq, 512)
    per_seq = seq // tm
    row = lambda i: (i, 0)
    out_shape = [jax.ShapeDtypeStruct((t, d), F32), jax.ShapeDtypeStruct((t, d), BF16)]
    out_specs = [pl.BlockSpec((tm, d), row), pl.BlockSpec((tm, d), row)]
    if tm % LANES == 0:
        out_shape.append(jax.ShapeDtypeStruct((d, t), BF16))
        out_specs.append(pl.BlockSpec((d, tm), lambda i: (0, i)))
    outs = pl.pallas_call(
        functools.partial(_mix_kernel, alpha=alpha, width=width),
        out_shape=tuple(out_shape),
        grid=(t // tm,),
        in_specs=[pl.BlockSpec((None, width, tm), lambda i: (i // per_seq, 0, i % per_seq)),
                  pl.BlockSpec((tm, width), lambda i: (i % per_seq, i // per_seq)),
                  pl.BlockSpec((tm, d), row), _full((2 * width, d)), _full((1, d)), _full((1, d))],
        out_specs=tuple(out_specs),
        compiler_params=_params("parallel"),
        name="mix",
    )(att_t, ssm_tm, x2d, w_out_b, g.reshape(1, d), b.reshape(1, d))
    x1, x1b = outs[0], outs[1]
    return x1, x1b, (outs[2] if len(outs) == 3 else x1b.T)


def _sort16(v):
    v = list(v)
    for i, j in SORT16:
        v[i], v[j] = jnp.maximum(v[i], v[j]), jnp.minimum(v[i], v[j])
    return v


def _merge_top16(a, b):
    v = [jnp.maximum(a[j], b[PEER_TOPK - 1 - j]) for j in range(PEER_TOPK)]
    for dist in (8, 4, 2, 1):
        for i in range(PEER_TOPK):
            if not i & dist:
                v[i], v[i + dist] = jnp.maximum(v[i], v[i + dist]), jnp.minimum(v[i], v[i + dist])
    return v


def _fold_sublanes(v):
    for shift in (4, 2, 1):
        v = _merge_top16(v, [pltpu.roll(x, shift, 0) for x in v])
    return v


def _count_prefix(v, pred):
    assert len(v) == PEER_TOPK == 16
    full = pred(v[15])
    b3 = pred(v[7])
    b2 = pred(jnp.where(b3, v[11], v[3]))
    b1 = pred(jnp.where(b3, jnp.where(b2, v[13], v[9]), jnp.where(b2, v[5], v[1])))
    quad = [jnp.where(b1, v[4 * k + 2], v[4 * k]) for k in range(4)]
    b0 = pred(jnp.where(b3, jnp.where(b2, quad[3], quad[2]), jnp.where(b2, quad[1], quad[0])))
    n = (jnp.where(b3, 8.0, 0.0) + jnp.where(b2, 4.0, 0.0)) + (jnp.where(b1, 2.0, 0.0) + jnp.where(b0, 1.0, 0.0))
    return jnp.where(full, 16.0, n)


def _route_kernel(x_ref, wq_ref, keys_ref, lim_ref, f_ref, r2_ref, e2_ref, *, tt, tc):
    qp = jnp.dot(x_ref[...], wq_ref[...], preferred_element_type=F32).astype(BF16)
    nt_dims = (((1,), (1,)), ((), ()))
    s_t = [lax.dot_general(keys_ref[0, c], qp[:, c * PEER_DHALF:(c + 1) * PEER_DHALF], nt_dims,
                           preferred_element_type=F32) for c in range(2)]
    n_blocks = N_KEYS // SUBLANES
    for l0 in range(0, tt, tc):
        cols = slice(l0, l0 + tc)
        sub = lax.broadcasted_iota(jnp.int32, (SUBLANES, tc), 0)
        blocks = [[s_t[c][SUBLANES * j:SUBLANES * (j + 1), cols] for j in range(n_blocks)] for c in range(2)]
        v1, v2 = (_fold_sublanes(_sort16(blocks[c])) for c in range(2))
        v1_lo, v1_hi = v1[0], v1[SUBLANES]
        for r in range(1, SUBLANES):
            v1_lo = jnp.where(sub == r, v1[r], v1_lo)
            v1_hi = jnp.where(sub == r, v1[SUBLANES + r], v1_hi)
        cand_lo = [v1_lo + v2[b] for b in range(PEER_TOPK)]
        cand_hi = [v1_hi + v2[b] for b in range(PEER_TOPK)]
        top_s = _fold_sublanes(_merge_top16(cand_lo, cand_hi))
        tau = top_s[PEER_TOPK - 1]
        z = jnp.ones((SUBLANES, tc), F32)
        for k in range(1, PEER_TOPK):
            z = z + jnp.exp(top_s[k] - top_s[0])
        zinv = 1.0 / z
        r2, e2 = [], []
        for j in range(n_blocks):
            s1, s2 = blocks[0][j], blocks[1][j]
            cnt = _count_prefix(v2, lambda p, s1=s1: s1 + p >= tau)
            rank = _count_prefix(v2, lambda p, s2=s2: p > s2)
            lim_j = jnp.where(s1 >= v1[PEER_TOPK - 1], cnt, 0.0)
            f_j = jnp.exp(s1 - v1[0]) * zinv
            for g in range(SUBLANES // KEYS_PER_BLOCK):
                rows = slice(g * KEYS_PER_BLOCK, (g + 1) * KEYS_PER_BLOCK)
                lim_ref[0, j * (SUBLANES // KEYS_PER_BLOCK) + g, :, cols] = lim_j[rows]
                f_ref[0, j * (SUBLANES // KEYS_PER_BLOCK) + g, :, cols] = f_j[rows]
            r2.append(rank)
            e2.append(jnp.exp(s2 - v2[0]))
        r2_ref[0, :, cols] = jnp.concatenate(r2, axis=0).astype(BF16)
        e2_ref[0, :, cols] = jnp.concatenate(e2, axis=0).astype(BF16)


def _route(x1b, wq_b, keys_b, tt):
    t, d = x1b.shape
    dk = 2 * PEER_DHALF
    tc = _tile(tt, 256)
    big_spec = pl.BlockSpec((1, N_KEYS, tt), lambda i, h: (h, 0, i))
    n_eb = N_KEYS // KEYS_PER_BLOCK
    row_shape = jax.ShapeDtypeStruct((PEER_HEADS, n_eb, KEYS_PER_BLOCK, t), F32)
    row_spec = pl.BlockSpec((1, n_eb, KEYS_PER_BLOCK, tt), lambda i, h: (h, 0, 0, i))
    return pl.pallas_call(
        functools.partial(_route_kernel, tt=tt, tc=tc),
        out_shape=(row_shape, row_shape,
                   jax.ShapeDtypeStruct((PEER_HEADS, N_KEYS, t), BF16),
                   jax.ShapeDtypeStruct((PEER_HEADS, N_KEYS, t), BF16)),
        grid=(t // tt, PEER_HEADS),
        in_specs=[pl.BlockSpec((tt, d), lambda i, h: (i, 0)),
                  pl.BlockSpec((d, dk), lambda i, h: (0, h)),
                  pl.BlockSpec((1, 2, N_KEYS, PEER_DHALF), lambda i, h: (h, 0, 0, 0))],
        out_specs=(row_spec, row_spec, big_spec, big_spec),
        compiler_params=_params("parallel", "parallel"),
        name="route",
    )(x1b, wq_b, keys_b)


def _peer_kernel(xt_ref, u_ref, vt_ref, lim_ref, f_ref, r2_ref, e2_ref, y_ref, a_ref, hid_ref,
                 *, n_blocks, tt):
    s = pl.program_id(0)
    slot = lax.rem(s, 2)
    rc = 2 * SUBLANES

    @pl.when(s == 0)
    def _():
        hid_ref[1] = jnp.zeros(hid_ref.shape[1:], BF16)

    @pl.when(lax.rem(jnp.maximum(s - 1, 0), n_blocks) == 0)
    def _():
        y_ref[...] = jnp.zeros_like(y_ref)

    a_ref[...] = jnp.dot(u_ref[...], xt_ref[...], preferred_element_type=F32)
    y_ref[...] += jnp.dot(vt_ref[...], hid_ref[1 - slot], preferred_element_type=F32)
    zero = jnp.zeros((rc, tt), BF16)
    for q in range(KEYS_PER_BLOCK):
        w = [zero] * (N_KEYS // rc)
        for h in range(PEER_HEADS):
            lim_t = jnp.broadcast_to(lim_ref[h, q:q + 1, :], (rc, tt)).astype(BF16)
            f_t = jnp.broadcast_to(f_ref[h, q:q + 1, :], (rc, tt)).astype(BF16)
            for k in range(N_KEYS // rc):
                rows = slice(k * rc, (k + 1) * rc)
                w[k] = w[k] + jnp.where(r2_ref[h, rows, :] < lim_t, e2_ref[h, rows, :], zero) * f_t
        for k in range(N_KEYS // rc):
            rows = slice(q * N_KEYS + k * rc, q * N_KEYS + (k + 1) * rc)
            hid_ref[slot, rows, :] = _gelu(a_ref[rows, :]).astype(BF16) * w[k]


def _peer(x1t, u_b, vt_b, lim, f, r2, e2, tt):
    d, t = x1t.shape
    eb = EXPERT_BLOCK
    n_blocks = u_b.shape[0] // eb
    n_tiles = t // tt
    tile = lambda s: jnp.minimum(s // n_blocks, n_tiles - 1)
    prev = lambda s: jnp.maximum(s - 1, 0)
    row_spec = pl.BlockSpec((PEER_HEADS, None, KEYS_PER_BLOCK, tt), lambda s: (0, s % n_blocks, 0, tile(s)))
    big_spec = pl.BlockSpec((PEER_HEADS, N_KEYS, tt), lambda s: (0, 0, tile(s)))
    return pl.pallas_call(
        functools.partial(_peer_kernel, n_blocks=n_blocks, tt=tt),
        out_shape=jax.ShapeDtypeStruct((d, t), F32),
        grid=(n_tiles * n_blocks + 1,),
        in_specs=[pl.BlockSpec((d, tt), lambda s: (0, tile(s))),
                  pl.BlockSpec((eb, d), lambda s: (s % n_blocks, 0)),
                  pl.BlockSpec((None, d, eb), lambda s: (prev(s) % n_blocks, 0, 0)),
                  row_spec, row_spec, big_spec, big_spec],
        out_specs=pl.BlockSpec((d, tt), lambda s: (0, prev(s) // n_blocks)),
        scratch_shapes=[pltpu.VMEM((eb, tt), F32), pltpu.VMEM((2, eb, tt), BF16)],
        compiler_params=_params("arbitrary"),
        name="peer",
    )(x1t, u_b, vt_b, lim, f, r2, e2)


def _ln2_kernel(x1_ref, yt_ref, g_ref, b_ref, o_ref, *, alpha):
    o_ref[...] = _layer_norm(alpha * x1_ref[...] + yt_ref[...].T, g_ref[...], b_ref[...])


def _ln2(x1, y_t, g, b, alpha):
    t, d = x1.shape
    tm = _tile(t, 512)
    row = lambda i: (i, 0)
    return pl.pallas_call(
        functools.partial(_ln2_kernel, alpha=alpha),
        out_shape=jax.ShapeDtypeStruct((t, d), F32),
        grid=(t // tm,),
        in_specs=[pl.BlockSpec((tm, d), row), pl.BlockSpec((d, tm), lambda i: (0, i)), _full((1, d)),
                  _full((1, d))],
        out_specs=pl.BlockSpec((tm, d), row),
        compiler_params=_params("parallel"),
        name="ln2",
    )(x1, y_t, g.reshape(1, d), b.reshape(1, d))


def _rel_bias_table(rel_bias):
    n_left = ATT_WINDOW - REL_CLIP + CHUNK - 1
    n_right = CHUNK - 1 - REL_CLIP
    ext = jnp.concatenate([jnp.repeat(rel_bias[:, :1], n_left, axis=1), rel_bias,
                           jnp.repeat(rel_bias[:, -1:], max(n_right, 0), axis=1)], axis=1)
    return jnp.stack([ext[:, CHUNK - 1 - i:CHUNK - 1 - i + BAND] for i in range(CHUNK)], axis=1)


def _step_bias(table, cpb):
    n_heads = table.shape[0]
    table = (table - table[:, :1, :1]) * LOG2E
    tab_t = jnp.swapaxes(table, 1, 2)
    per_chunk = [jnp.pad(tab_t, ((0, 0), (c * CHUNK, (cpb - 1 - c) * CHUNK), (0, 0)),
                         constant_values=MASK_VALUE) for c in range(cpb)]
    both = jnp.stack(per_chunk, axis=2)
    rows = both.shape[1]
    both = both.reshape(n_heads // 2, 2, rows, cpb * CHUNK)
    return jnp.swapaxes(both, 1, 2).reshape(n_heads // 2, rows, 2 * cpb * CHUNK)


def _encoder_layer(x, hist_k, hist_v, h0, wts, alpha):
    nb, seq, d = x.shape
    width = d // 2
    t = nb * seq
    n_gt = width // LANES
    assert seq % CHUNK == 0 and width % LANES == 0 and nb % SUBLANES == 0
    assert hist_k is None or seq == CHUNK
    x2d = x.reshape(t, d)

    n_hist = 0 if hist_k is None else hist_k.shape[1]
    n_invalid = ATT_WINDOW - n_hist
    def history(hist, n_new):
        if hist is None:
            return jnp.zeros((nb, ATT_WINDOW + n_new, width), BF16)
        return jnp.pad(hist.reshape(nb, n_hist, width).astype(BF16), ((0, 0), (n_invalid, n_new), (0, 0)))
    if _tile(seq, 256) % LANES == 0:
        q_b, kpad, vtpad, k_new, v_new, u_tm = _proj(
            x2d, wts["w_in"], history(hist_k, seq), jnp.swapaxes(history(hist_v, seq), 1, 2), nb, seq, width)
    else:
        q_b, k_b, v_b, k_new, v_new, u_tm = _proj_short(x2d, wts["w_in"], nb, seq, width)
        kpad = jnp.concatenate([history(hist_k, 0), k_b.reshape(nb, seq, width)], axis=1)
        vtpad = jnp.swapaxes(jnp.concatenate([history(hist_v, 0), v_b.reshape(nb, seq, width)], axis=1), 1, 2)
    n_chunks = seq // CHUNK
    cpb = 2 if n_chunks % 2 == 0 else 1
    assert cpb == 2 or n_chunks == 1
    att_t = _attention(q_b, kpad, vtpad, _step_bias(wts["bias"], cpb), wts["norm_attn_g"], nb, seq, width,
                       cpb, n_invalid)

    ssm_tm, h_last = _ssm(u_tm.reshape(t, width), h0, wts["ssm"], nb, seq, width)

    x1, x1b, x1t = _mix(att_t, ssm_tm.reshape(seq, nb * width), x2d, wts["w_out"], wts["ln1_g"],
                        wts["ln1_b"], alpha, width)

    tt = _tile(t, PEER_TOKENS)
    lim, f, r2, e2 = _route(x1b, wts["peer_wq"], wts["peer_keys"], tt)
    y_t = _peer(x1t, wts["peer_u"], wts["peer_vt"], lim, f, r2, e2, tt)
    x2 = _ln2(x1, y_t, wts["ln2_g"], wts["ln2_b"], alpha)

    n_heads = width // HEAD_DIM
    return (x2.reshape(nb, seq, d), k_new.reshape(nb, -1, n_heads, HEAD_DIM),
            v_new.reshape(nb, -1, n_heads, HEAD_DIM), h_last)


def kernel(x_prompt, x_sample, cache_attn_k, cache_attn_v, state_ssm_re, state_ssm_im, w_in, rel_bias, norm_attn_g, ssm_a_re, ssm_a_im, ssm_log_dt, ssm_b_re, ssm_b_im, ssm_c_re, ssm_c_im, ssm_d, ssm_glu_w, ssm_glu_b, norm_ssm_g, w_out, ln1_g, ln1_b, peer_wq, peer_keys, peer_u, peer_v, ln2_g, ln2_b):
    depth = w_in.shape[0]
    alpha = (2.0 * depth) ** 0.25
    d = x_prompt.shape[-1]
    width = d // 2
    n_gt = width // LANES
    n_groups = width // SSM_GROUP
    keep = min(ATT_WINDOW, x_prompt.shape[1])

    yp, ys = x_prompt, x_sample
    outs = [[] for _ in range(8)]
    for l in range(depth):
        wts = dict(
            w_in=w_in[l].astype(BF16),
            bias=_rel_bias_table(rel_bias[l]),
            norm_attn_g=jnp.broadcast_to(norm_attn_g[l][:, None], (width, LANES)),
            ssm=_ssm_params(ssm_a_re[l], ssm_a_im[l], ssm_log_dt[l], ssm_b_re[l], ssm_b_im[l], ssm_c_re[l],
                            ssm_c_im[l], ssm_d[l], ssm_glu_w[l], ssm_glu_b[l], norm_ssm_g[l], width),
            w_out=w_out[l].astype(BF16), ln1_g=ln1_g[l], ln1_b=ln1_b[l],
            peer_wq=peer_wq[l].astype(BF16), peer_keys=peer_keys[l].astype(BF16),
            peer_u=peer_u[l].astype(BF16),
            peer_vt=jnp.swapaxes(peer_v[l].astype(BF16).reshape(-1, EXPERT_BLOCK, d), 1, 2),
            ln2_g=ln2_g[l], ln2_b=ln2_b[l])
        h0p = jnp.zeros((n_gt, x_prompt.shape[0], 2 * STATE_TILE), F32)
        yp, kp, vp, hp = _encoder_layer(yp, None, None, h0p, wts, alpha)
        h0s = _state_to_tiles(state_ssm_re[l].astype(F32), state_ssm_im[l].astype(F32), n_gt)
        ys, kn, vn, hs = _encoder_layer(ys, cache_attn_k[l], cache_attn_v[l], h0s, wts, alpha)
        hp_re, hp_im = _tiles_to_state(hp, n_groups)
        hs_re, hs_im = _tiles_to_state(hs, n_groups)
        assert kp.shape[1] == keep
        for acc, val in zip(outs, (kp, vp, hp_re, hp_im, kn, vn, hs_re, hs_im)):
            acc.append(val)
    return (yp, ys) + tuple(jnp.stack(o) for o in outs)
```

```python
import functools

import jax
import jax.numpy as jnp
from jax import lax
from jax.experimental import pallas as pl
from jax.experimental.pallas import tpu as pltpu

F32 = jnp.float32
BF16 = jnp.bfloat16

CHUNK = 64
LEFT_CHUNKS = 8
ATT_WINDOW = LEFT_CHUNKS * CHUNK
BAND = ATT_WINDOW + CHUNK
HEAD_DIM = 64
REL_CLIP = 2 * CHUNK
SSM_GROUP = 16
SSM_STATE = 64
PEER_HEADS = 8
PEER_TOPK = 16
N_KEYS = 128
PEER_DHALF = 128
LN_EPS = 1e-5
RMS_EPS = 1e-6
MASK_VALUE = -1e30
LOG2E = 1.4426950408889634
Q_SCALE = HEAD_DIM ** -0.5 * LOG2E

LANES = 128
SUBLANES = 8
GROUPS_PER_TILE = LANES // SSM_GROUP
STATE_TILE = GROUPS_PER_TILE * SSM_STATE
VMEM_LIMIT = 56 * 1024 * 1024
EXPERT_BLOCK = 1024
KEYS_PER_BLOCK = EXPERT_BLOCK // N_KEYS
PEER_TOKENS = 512
ATTN_PAIR_GROUP = 8
CHAIN_EVERY = 2

SORT16 = (
    (0, 13), (1, 12), (2, 15), (3, 14), (4, 8), (5, 6), (7, 11), (9, 10),
    (0, 5), (1, 7), (2, 9), (3, 4), (6, 13), (8, 14), (10, 15), (11, 12),
    (0, 1), (2, 3), (4, 5), (6, 8), (7, 9), (10, 11), (12, 13), (14, 15),
    (0, 2), (1, 3), (4, 10), (5, 11), (6, 7), (8, 9), (12, 14), (13, 15),
    (1, 2), (3, 12), (4, 6), (5, 7), (8, 10), (9, 11), (13, 14),
    (1, 4), (2, 6), (5, 8), (7, 10), (9, 13), (11, 14),
    (2, 4), (3, 6), (9, 12), (11, 13),
    (3, 5), (6, 8), (7, 9), (10, 12),
    (3, 4), (5, 6), (7, 8), (9, 10), (11, 12),
    (6, 7), (8, 9),
)


def _params(*semantics):
    return pltpu.CompilerParams(dimension_semantics=semantics, vmem_limit_bytes=VMEM_LIMIT)


def _tile(n, pref):
    t = min(n, pref)
    while n % t:
        t //= 2
    return t


def _gelu(x):
    return 0.5 * x * (1.0 + lax.erf(x * 0.7071067811865476))


def _full(shape):
    return pl.BlockSpec(shape, lambda *_: (0,) * len(shape))


def _proj_kernel(x_ref, w_ref, kbase_ref, vtbase_ref, q_ref, kpad_ref, vtpad_ref, kf_ref, vf_ref, u_ref,
                 *, width):
    del kbase_ref, vtbase_ref
    xb = x_ref[...].astype(BF16)
    acc = [jnp.dot(xb, w_ref[:, c * width:(c + 1) * width], preferred_element_type=F32) for c in range(4)]
    q_ref[...] = (acc[0] * Q_SCALE).astype(BF16)
    kf_ref[...] = acc[1]
    kpad_ref[...] = acc[1].astype(BF16)
    vf_ref[...] = acc[2]
    vtpad_ref[...] = acc[2].T.astype(BF16)
    u_ref[...] = acc[3]


def _proj(x2d, w_in_b, kbase, vtbase, nb, seq, width):
    t, d = x2d.shape
    tm = _tile(seq, 256)
    assert tm % LANES == 0 and ATT_WINDOW % tm == 0
    per_seq = seq // tm
    skip = ATT_WINDOW // tm
    keep = min(ATT_WINDOW, seq)
    dropped = per_seq - keep // tm
    row = lambda i: (i, 0)
    tail = lambda i: (i // per_seq, jnp.maximum(i % per_seq - dropped, 0), 0)
    tail_f32 = jax.ShapeDtypeStruct((nb, keep, width), F32)
    return pl.pallas_call(
        functools.partial(_proj_kernel, width=width),
        out_shape=(jax.ShapeDtypeStruct((t, width), BF16),
                   jax.ShapeDtypeStruct(kbase.shape, BF16), jax.ShapeDtypeStruct(vtbase.shape, BF16),
                   tail_f32, tail_f32, jax.ShapeDtypeStruct((seq, nb * width), F32)),
        grid=(t // tm,),
        in_specs=[pl.BlockSpec((tm, d), row),
                  pl.BlockSpec((d, 4 * width), lambda i: (0, 0)),
                  pl.BlockSpec(memory_space=pl.ANY), pl.BlockSpec(memory_space=pl.ANY)],
        out_specs=(pl.BlockSpec((tm, width), row),
                   pl.BlockSpec((None, tm, width), lambda i: (i // per_seq, skip + i % per_seq, 0)),
                   pl.BlockSpec((None, width, tm), lambda i: (i // per_seq, 0, skip + i % per_seq)),
                   pl.BlockSpec((None, tm, width), tail), pl.BlockSpec((None, tm, width), tail),
                   pl.BlockSpec((tm, width), lambda i: (i % per_seq, i // per_seq))),
        input_output_aliases={2: 1, 3: 2},
        compiler_params=_params("arbitrary"),
        name="proj",
    )(x2d, w_in_b, kbase, vtbase)


def _proj_short_kernel(x_ref, w_ref, q_ref, k_ref, v_ref, kf_ref, vf_ref, u_ref, *, width):
    xb = x_ref[...].astype(BF16)
    acc = [jnp.dot(xb, w_ref[:, c * width:(c + 1) * width], preferred_element_type=F32) for c in range(4)]
    q_ref[...] = (acc[0] * Q_SCALE).astype(BF16)
    kf_ref[...] = acc[1]
    k_ref[...] = acc[1].astype(BF16)
    vf_ref[...] = acc[2]
    v_ref[...] = acc[2].astype(BF16)
    u_ref[...] = acc[3]


def _proj_short(x2d, w_in_b, nb, seq, width):
    t, d = x2d.shape
    tm = _tile(seq, 256)
    per_seq = seq // tm
    row = lambda i: (i, 0)
    rows_f32 = jax.ShapeDtypeStruct((t, width), F32)
    rows_b16 = jax.ShapeDtypeStruct((t, width), BF16)
    return pl.pallas_call(
        functools.partial(_proj_short_kernel, width=width),
        out_shape=(rows_b16, rows_b16, rows_b16, rows_f32, rows_f32,
                   jax.ShapeDtypeStruct((seq, nb * width), F32)),
        grid=(t // tm,),
        in_specs=[pl.BlockSpec((tm, d), row),
                  pl.BlockSpec((d, 4 * width), lambda i: (0, 0))],
        out_specs=tuple([pl.BlockSpec((tm, width), row)] * 5
                        + [pl.BlockSpec((tm, width), lambda i: (i % per_seq, i // per_seq))]),
        compiler_params=_params("parallel"),
        name="proj",
    )(x2d, w_in_b)


def _attn_kernel(q_ref, k_ref, vt_ref, bias_ref, g_ref, o_ref, *, n_steps, cpb, n_invalid, width):
    nq = cpb * CHUNK
    nc = 2 * nq
    kb = ATT_WINDOW + nq
    lo = lax.broadcasted_iota(jnp.int32, (nq, LANES), 1) < HEAD_DIM
    nt_dims = (((1,), (1,)), ((), ()))
    group = ATTN_PAIR_GROUP
    k_split = -(-(kb // 2) // LANES) * LANES
    halves = ((0, k_split), (k_split, kb - k_split))
    z0, z1 = (cpb - 1) * CHUNK, ATT_WINDOW - REL_CLIP

    def pair_scores(r0, hp):
        cols = slice(hp * LANES, (hp + 1) * LANES)
        q2 = q_ref[pl.ds(r0, nq), cols]
        qcat = jnp.concatenate([jnp.where(lo, q2, 0), jnp.where(lo, 0, q2)], axis=0)
        return jnp.concatenate(
            [lax.dot_general(k_ref[pl.ds(r0 + k0, kn), cols], qcat, nt_dims, preferred_element_type=F32)
             for k0, kn in halves], axis=0)

    def step(r0, masked):
        tiles = []
        ssq = jnp.zeros((1, nq), F32)
        n_pairs = width // LANES
        for g0 in range(0, n_pairs, group):
            pairs = range(g0, min(g0 + group, n_pairs))
            s = jnp.concatenate([pair_scores(r0, hp) for hp in pairs], axis=1)
            parts = [s[z0:z1], s[z1:] + jnp.concatenate([bias_ref[hp, z1:, :] for hp in pairs], axis=1)]
            if z0:
                parts.insert(0, s[:z0] + jnp.concatenate([bias_ref[hp, :z0, :] for hp in pairs], axis=1))
            s = jnp.concatenate(parts, axis=0)
            if masked:
                s = jnp.where(lax.broadcasted_iota(jnp.int32, s.shape, 0) + r0 >= n_invalid, s, MASK_VALUE)
            m = jnp.max(s, axis=0, keepdims=True)
            e = jnp.exp2(s - m)
            rl = 1.0 / jnp.sum(e, axis=0, keepdims=True)
            p = e.astype(BF16)
            for i, hp in enumerate(pairs):
                cols = slice(hp * LANES, (hp + 1) * LANES)
                pc = slice(i * nc, (i + 1) * nc)
                o_t = sum(jnp.dot(vt_ref[cols, pl.ds(r0 + k0, kn)], p[k0:k0 + kn, pc],
                                  preferred_element_type=F32) for k0, kn in halves) * rl[:, pc]
                tile = jnp.concatenate([o_t[:HEAD_DIM, :nq], o_t[HEAD_DIM:, nq:]], axis=0)
                ssq = ssq + jnp.sum(tile * tile, axis=0, keepdims=True)
                tiles.append(tile)
        rinv = lax.rsqrt(ssq / width + RMS_EPS)
        for hp, tile in enumerate(tiles):
            rows = slice(hp * LANES, (hp + 1) * LANES)
            o_ref[rows, pl.ds(r0, nq)] = (tile * rinv * g_ref[rows, :nq]).astype(BF16)

    n_masked = min(n_steps, -(-n_invalid // nq))
    if n_steps == 1:
        step(0, n_masked > 0)
    else:
        def body(i, carry, masked):
            step(pl.multiple_of(i * nq, nq), masked)
            return carry
        lax.fori_loop(0, n_masked, functools.partial(body, masked=True), 0)
        lax.fori_loop(n_masked, n_steps, functools.partial(body, masked=False), 0)


def _attention(qkv, kpad, vtpad, bias, gain_b, nb, seq, width, cpb, n_invalid):
    lp = kpad.shape[1]
    nq = cpb * CHUNK
    return pl.pallas_call(
        functools.partial(_attn_kernel, n_steps=seq // nq, cpb=cpb, n_invalid=n_invalid, width=width),
        out_shape=jax.ShapeDtypeStruct((nb, width, seq), BF16),
        grid=(nb,),
        in_specs=[pl.BlockSpec((seq, width), lambda b: (b, 0)),
                  pl.BlockSpec((None, lp, width), lambda b: (b, 0, 0)),
                  pl.BlockSpec((None, width, lp), lambda b: (b, 0, 0)),
                  _full(bias.shape),
                  _full((width, LANES))],
        out_specs=pl.BlockSpec((None, width, seq), lambda b: (b, 0, 0)),
        compiler_params=_params("parallel"),
        name="attn",
    )(qkv, kpad, vtpad, bias, gain_b)


def _ssm_prep_kernel(are_ref, aim_ref, ldt_ref, bre_ref, bim_ref, abr_ref, abi_ref, bbr_ref, bbi_ref):
    a_re = are_ref[...]
    a_im = aim_ref[...]
    dt = jnp.exp(ldt_ref[...])
    mag = jnp.exp(a_re * dt)
    ang = a_im * dt
    ab_re = mag * jnp.cos(ang)
    ab_im = mag * jnp.sin(ang)
    n_re = ab_re - 1.0
    den = a_re * a_re + a_im * a_im
    c_re = (n_re * a_re + ab_im * a_im) / den
    c_im = (ab_im * a_re - n_re * a_im) / den
    abr_ref[...] = ab_re
    abi_ref[...] = ab_im
    b_re = bre_ref[...]
    b_im = bim_ref[...]
    bbr_ref[...] = c_re[:, None, :] * b_re - c_im[:, None, :] * b_im
    bbi_ref[...] = c_re[:, None, :] * b_im + c_im[:, None, :] * b_re


def _ssm_prep(a_re, a_im, log_dt, bt_re, bt_im):
    g, p = a_re.shape
    h = bt_re.shape[1]
    return pl.pallas_call(
        _ssm_prep_kernel,
        out_shape=(jax.ShapeDtypeStruct((g, p), F32), jax.ShapeDtypeStruct((g, p), F32),
                   jax.ShapeDtypeStruct((g, h, p), F32), jax.ShapeDtypeStruct((g, h, p), F32)),
        name="ssm_prep",
    )(a_re, a_im, log_dt.reshape(g, 1), bt_re, bt_im)


def _ssm_kernel(u_ref, h0_ref, are_ref, aim_ref, bre_ref, bim_ref, cre_ref, cim_ref, d_ref, wg_ref, bg_ref,
                gn_ref, o_ref, hlast_ref, hst_ref, bu_ref, y_ref, *, nb, tl, n_gt, width):
    sw = STATE_TILE

    @pl.when(pl.program_id(0) == 0)
    def _():
        hst_ref[...] = h0_ref[...]

    for gt in range(n_gt):
        cols = slice(gt * LANES, (gt + 1) * LANES)
        uf = u_ref[:, cols]
        ub = uf.astype(BF16)
        bu_ref[:, :sw] = jnp.dot(ub, bre_ref[gt], preferred_element_type=F32)
        bu_ref[:, sw:] = jnp.dot(ub, bim_ref[gt], preferred_element_type=F32)
        a_re = jnp.broadcast_to(are_ref[gt], (nb, sw))
        a_im = jnp.broadcast_to(aim_ref[gt], (nb, sw))

        def step(t, carry, a_re=a_re, a_im=a_im):
            h_re, h_im = carry
            r0 = pl.multiple_of(t * nb, nb)
            n_re = a_re * h_re - a_im * h_im + bu_ref[pl.ds(r0, nb), :sw]
            n_im = a_re * h_im + a_im * h_re + bu_ref[pl.ds(r0, nb), sw:]
            bu_ref[pl.ds(r0, nb), :sw] = n_re
            bu_ref[pl.ds(r0, nb), sw:] = n_im
            return n_re, n_im

        h_re, h_im = lax.fori_loop(0, tl, step, (hst_ref[gt, :, :sw], hst_ref[gt, :, sw:]), unroll=True)
        hst_ref[gt, :, :sw] = h_re
        hst_ref[gt, :, sw:] = h_im

        y = (jnp.dot(bu_ref[:, :sw].astype(BF16), cre_ref[gt], preferred_element_type=F32)
             - jnp.dot(bu_ref[:, sw:].astype(BF16), cim_ref[gt], preferred_element_type=F32)
             + d_ref[:, cols] * uf)
        y = _gelu(y)
        z = jnp.dot(y.astype(BF16), wg_ref[gt], preferred_element_type=F32) + bg_ref[:, cols]
        y_ref[:, cols] = y * jax.nn.sigmoid(z)

    yy = y_ref[...]
    ms = jnp.sum(yy * yy, axis=-1, keepdims=True) / width
    o_ref[...] = (yy * lax.rsqrt(ms + RMS_EPS) * gn_ref[...]).astype(BF16)
    hlast_ref[...] = hst_ref[...]


def _ssm(u_tm, h0, sp, nb, seq, width):
    n_gt = width // LANES
    tl = _tile(seq, 64)
    rows = tl * nb
    sw = STATE_TILE
    return pl.pallas_call(
        functools.partial(_ssm_kernel, nb=nb, tl=tl, n_gt=n_gt, width=width),
        out_shape=(jax.ShapeDtypeStruct((seq * nb, width), BF16),
                   jax.ShapeDtypeStruct((n_gt, nb, 2 * sw), F32)),
        grid=(seq // tl,),
        in_specs=[pl.BlockSpec((rows, width), lambda s: (s, 0)),
                  _full((n_gt, nb, 2 * sw)),
                  _full((n_gt, 1, sw)), _full((n_gt, 1, sw)),
                  _full((n_gt, LANES, sw)), _full((n_gt, LANES, sw)),
                  _full((n_gt, sw, LANES)), _full((n_gt, sw, LANES)),
                  _full((1, width)),
                  _full((n_gt, LANES, LANES)),
                  _full((1, width)), _full((1, width))],
        out_specs=(pl.BlockSpec((rows, width), lambda s: (s, 0)),
                   _full((n_gt, nb, 2 * sw))),
        scratch_shapes=[pltpu.VMEM((n_gt, nb, 2 * sw), F32),
                        pltpu.VMEM((rows, 2 * sw), F32),
                        pltpu.VMEM((rows, width), F32)],
        compiler_params=_params("arbitrary"),
        name="ssm",
    )(u_tm, h0, sp["a_re"], sp["a_im"], sp["b_re"], sp["b_im"], sp["c_re"], sp["c_im"], sp["d"],
      sp["wg"], sp["bg"], sp["gn"])


def _block_diag(m, n_gt):
    _, r, c = m.shape
    m4 = m.reshape(n_gt, GROUPS_PER_TILE, r, 1, c)
    eye = jnp.eye(GROUPS_PER_TILE, dtype=bool)[None, :, None, :, None]
    return jnp.where(eye, m4, 0).reshape(n_gt, GROUPS_PER_TILE * r, GROUPS_PER_TILE * c)


def _ssm_params(a_re, a_im, log_dt, b_re, b_im, c_re, c_im, d, glu_w, glu_b, gain, width):
    n_gt = width // LANES
    ab_re, ab_im, bb_re, bb_im = _ssm_prep(a_re, a_im, log_dt, jnp.swapaxes(b_re, 1, 2),
                                           jnp.swapaxes(b_im, 1, 2))
    return dict(
        a_re=ab_re.reshape(n_gt, 1, STATE_TILE), a_im=ab_im.reshape(n_gt, 1, STATE_TILE),
        b_re=_block_diag(bb_re, n_gt).astype(BF16), b_im=_block_diag(bb_im, n_gt).astype(BF16),
        c_re=_block_diag(jnp.swapaxes(c_re, 1, 2), n_gt).astype(BF16),
        c_im=_block_diag(jnp.swapaxes(c_im, 1, 2), n_gt).astype(BF16),
        d=d.reshape(1, width), wg=_block_diag(glu_w, n_gt).astype(BF16), bg=glu_b.reshape(1, width),
        gn=gain.reshape(1, width))


def _state_to_tiles(s_re, s_im, n_gt):
    nb = s_re.shape[0]
    def one(s):
        return jnp.swapaxes(s.reshape(nb, n_gt, STATE_TILE), 0, 1)
    return jnp.concatenate([one(s_re), one(s_im)], axis=-1)


def _tiles_to_state(h, n_groups):
    n_gt, nb, _ = h.shape
    def one(s):
        return jnp.swapaxes(s, 0, 1).reshape(nb, n_groups, SSM_STATE)
    return one(h[:, :, :STATE_TILE]), one(h[:, :, STATE_TILE:])


def _layer_norm(z, g, b):
    mu = jnp.mean(z, axis=-1, keepdims=True)
    zc = z - mu
    var = jnp.mean(zc * zc, axis=-1, keepdims=True)
    return zc * lax.rsqrt(var + LN_EPS) * g + b


def _mix_kernel(a_ref, s_ref, x_ref, wo_ref, g_ref, b_ref, x1_ref, x1b_ref, x1t_ref=None, *, alpha, width):
    tn_dims = (((0,), (0,)), ((), ()))
    tm = x_ref.shape[0]
    n_split = 2 if tm % (2 * LANES) == 0 else 1
    for r0 in range(0, tm, tm // n_split):
        rows = slice(r0, r0 + tm // n_split)
        mix = (lax.dot_general(a_ref[:, rows], wo_ref[:width], tn_dims, preferred_element_type=F32)
               + jnp.dot(s_ref[rows, :], wo_ref[width:], preferred_element_type=F32))
        x1 = _layer_norm(alpha * x_ref[rows, :] + mix, g_ref[...], b_ref[...])
        x1_ref[rows, :] = x1
        x1b_ref[rows, :] = x1.astype(BF16)
        if x1t_ref is not None:
            x1t_ref[:, rows] = x1.T.astype(BF16)


def _mix(att_t, ssm_tm, x2d, w_out_b, g, b, alpha, width):
    t, d = x2d.shape
    nb, _, seq = att_t.shape
    tm = _tile(seq, 512)
    per_seq = seq // tm
    row = lambda i: (i, 0)
    out_shape = [jax.ShapeDtypeStruct((t, d), F32), jax.ShapeDtypeStruct((t, d), BF16)]
    out_specs = [pl.BlockSpec((tm, d), row), pl.BlockSpec((tm, d), row)]
    if tm % LANES == 0:
        out_shape.append(jax.ShapeDtypeStruct((d, t), BF16))
        out_specs.append(pl.BlockSpec((d, tm), lambda i: (0, i)))
    outs = pl.pallas_call(
        functools.partial(_mix_kernel, alpha=alpha, width=width),
        out_shape=tuple(out_shape),
        grid=(t // tm,),
        in_specs=[pl.BlockSpec((None, width, tm), lambda i: (i // per_seq, 0, i % per_seq)),
                  pl.BlockSpec((tm, width), lambda i: (i % per_seq, i // per_seq)),
                  pl.BlockSpec((tm, d), row), _full((2 * width, d)), _full((1, d)), _full((1, d))],
        out_specs=tuple(out_specs),
        compiler_params=_params("parallel"),
        name="mix",
    )(att_t, ssm_tm, x2d, w_out_b, g.reshape(1, d), b.reshape(1, d))
    x1, x1b = outs[0], outs[1]
    return x1, x1b, (outs[2] if len(outs) == 3 else x1b.T)


def _sort16(v):
    v = list(v)
    for i, j in SORT16:
        v[i], v[j] = jnp.maximum(v[i], v[j]), jnp.minimum(v[i], v[j])
    return v


def _merge_top16(a, b):
    v = [jnp.maximum(a[j], b[PEER_TOPK - 1 - j]) for j in range(PEER_TOPK)]
    for dist in (8, 4, 2, 1):
        for i in range(PEER_TOPK):
            if not i & dist:
                v[i], v[i + dist] = jnp.maximum(v[i], v[i + dist]), jnp.minimum(v[i], v[i + dist])
    return v


def _fold_sublanes(v):
    for shift in (4, 2, 1):
        v = _merge_top16(v, [pltpu.roll(x, shift, 0) for x in v])
    return v


def _count_prefix(v, pred):
    assert len(v) == PEER_TOPK == 16
    full = pred(v[15])
    b3 = pred(v[7])
    b2 = pred(jnp.where(b3, v[11], v[3]))
    b1 = pred(jnp.where(b3, jnp.where(b2, v[13], v[9]), jnp.where(b2, v[5], v[1])))
    quad = [jnp.where(b1, v[4 * k + 2], v[4 * k]) for k in range(4)]
    b0 = pred(jnp.where(b3, jnp.where(b2, quad[3], quad[2]), jnp.where(b2, quad[1], quad[0])))
    n = (jnp.where(b3, 8.0, 0.0) + jnp.where(b2, 4.0, 0.0)) + (jnp.where(b1, 2.0, 0.0) + jnp.where(b0, 1.0, 0.0))
    return jnp.where(full, 16.0, n)


def _route_kernel(x_ref, wq_ref, keys_ref, lim_ref, f_ref, r2_ref, e2_ref, *, tt, tc):
    qp = jnp.dot(x_ref[...], wq_ref[...], preferred_element_type=F32).astype(BF16)
    nt_dims = (((1,), (1,)), ((), ()))
    s_t = [lax.dot_general(keys_ref[0, c], qp[:, c * PEER_DHALF:(c + 1) * PEER_DHALF], nt_dims,
                           preferred_element_type=F32) for c in range(2)]
    n_blocks = N_KEYS // SUBLANES
    for l0 in range(0, tt, tc):
        cols = slice(l0, l0 + tc)
        sub = lax.broadcasted_iota(jnp.int32, (SUBLANES, tc), 0)
        blocks = [[s_t[c][SUBLANES * j:SUBLANES * (j + 1), cols] for j in range(n_blocks)] for c in range(2)]
        v1, v2 = (_fold_sublanes(_sort16(blocks[c])) for c in range(2))
        v1_lo, v1_hi = v1[0], v1[SUBLANES]
        for r in range(1, SUBLANES):
            v1_lo = jnp.where(sub == r, v1[r], v1_lo)
            v1_hi = jnp.where(sub == r, v1[SUBLANES + r], v1_hi)
        cand_lo = [v1_lo + v2[b] for b in range(PEER_TOPK)]
        cand_hi = [v1_hi + v2[b] for b in range(PEER_TOPK)]
        top_s = _fold_sublanes(_merge_top16(cand_lo, cand_hi))
        tau = top_s[PEER_TOPK - 1]
        z = jnp.ones((SUBLANES, tc), F32)
        for k in range(1, PEER_TOPK):
            z = z + jnp.exp(top_s[k] - top_s[0])
        zinv = 1.0 / z
        r2, e2 = [], []
        for j in range(n_blocks):
            s1, s2 = blocks[0][j], blocks[1][j]
            cnt = _count_prefix(v2, lambda p, s1=s1: s1 + p >= tau)
            rank = _count_prefix(v2, lambda p, s2=s2: p > s2)
            lim_j = jnp.where(s1 >= v1[PEER_TOPK - 1], cnt, 0.0)
            f_j = jnp.exp(s1 - v1[0]) * zinv
            for g in range(SUBLANES // KEYS_PER_BLOCK):
                rows = slice(g * KEYS_PER_BLOCK, (g + 1) * KEYS_PER_BLOCK)
                lim_ref[0, j * (SUBLANES // KEYS_PER_BLOCK) + g, :, cols] = lim_j[rows]
                f_ref[0, j * (SUBLANES // KEYS_PER_BLOCK) + g, :, cols] = f_j[rows]
            r2.append(rank)
            e2.append(jnp.exp(s2 - v2[0]))
        r2_ref[0, :, cols] = jnp.concatenate(r2, axis=0).astype(BF16)
        e2_ref[0, :, cols] = jnp.concatenate(e2, axis=0).astype(BF16)


def _route(x1b, wq_b, keys_b, tt):
    t, d = x1b.shape
    dk = 2 * PEER_DHALF
    tc = _tile(tt, 256)
    big_spec = pl.BlockSpec((1, N_KEYS, tt), lambda i, h: (h, 0, i))
    n_eb = N_KEYS // KEYS_PER_BLOCK
    row_shape = jax.ShapeDtypeStruct((PEER_HEADS, n_eb, KEYS_PER_BLOCK, t), F32)
    row_spec = pl.BlockSpec((1, n_eb, KEYS_PER_BLOCK, tt), lambda i, h: (h, 0, 0, i))
    return pl.pallas_call(
        functools.partial(_route_kernel, tt=tt, tc=tc),
        out_shape=(row_shape, row_shape,
                   jax.ShapeDtypeStruct((PEER_HEADS, N_KEYS, t), BF16),
                   jax.ShapeDtypeStruct((PEER_HEADS, N_KEYS, t), BF16)),
        grid=(t // tt, PEER_HEADS),
        in_specs=[pl.BlockSpec((tt, d), lambda i, h: (i, 0)),
                  pl.BlockSpec((d, dk), lambda i, h: (0, h)),
                  pl.BlockSpec((1, 2, N_KEYS, PEER_DHALF), lambda i, h: (h, 0, 0, 0))],
        out_specs=(row_spec, row_spec, big_spec, big_spec),
        compiler_params=_params("parallel", "parallel"),
        name="route",
    )(x1b, wq_b, keys_b)


def _peer_kernel(xt_ref, u_ref, vt_ref, lim_ref, f_ref, r2_ref, e2_ref, y_ref, a_ref, hid_ref, bc_ref, w_ref,
                 *, n_blocks, tt):
    s = pl.program_id(0)
    slot = lax.rem(s, 2)
    rc = 2 * SUBLANES

    @pl.when(s == 0)
    def _():
        hid_ref[1] = jnp.zeros(hid_ref.shape[1:], BF16)

    @pl.when(lax.rem(jnp.maximum(s - 1, 0), n_blocks) == 0)
    def _():
        y_ref[...] = jnp.zeros_like(y_ref)

    a_ref[...] = jnp.dot(u_ref[...], xt_ref[...], preferred_element_type=F32)
    y_ref[...] += jnp.dot(vt_ref[...], hid_ref[1 - slot], preferred_element_type=F32)
    zero = jnp.zeros((rc, tt), BF16)
    for q in range(KEYS_PER_BLOCK):
        for h in range(PEER_HEADS):
            bc_ref[0, q, h] = jnp.broadcast_to(lim_ref[h, q:q + 1, :], (rc, tt)).astype(BF16)
            bc_ref[1, q, h] = jnp.broadcast_to(f_ref[h, q:q + 1, :], (rc, tt)).astype(BF16)
    never = r2_ref[0, 0:rc, :] < -1.0
    dep = zero
    for q in range(KEYS_PER_BLOCK):
        for k in range(N_KEYS // rc):
            rows = slice(k * rc, (k + 1) * rc)
            w = dep
            for h in range(PEER_HEADS):
                lim_t = bc_ref[0, q, h]
                if h % CHAIN_EVERY == 0:
                    lim_t = lim_t + dep
                w = w + jnp.where(r2_ref[h, rows, :] < lim_t, e2_ref[h, rows, :], zero) * bc_ref[1, q, h]
                if h % CHAIN_EVERY == CHAIN_EVERY - 1:
                    dep = jnp.where(never, w, zero)
            w_ref[q * N_KEYS + k * rc:q * N_KEYS + (k + 1) * rc, :] = w
    hid_ref[slot] = _gelu(a_ref[...]).astype(BF16) * w_ref[...]


def _peer(x1t, u_b, vt_b, lim, f, r2, e2, tt):
    d, t = x1t.shape
    eb = EXPERT_BLOCK
    n_blocks = u_b.shape[0] // eb
    n_tiles = t // tt
    tile = lambda s: jnp.minimum(s // n_blocks, n_tiles - 1)
    prev = lambda s: jnp.maximum(s - 1, 0)
    row_spec = pl.BlockSpec((PEER_HEADS, None, KEYS_PER_BLOCK, tt), lambda s: (0, s % n_blocks, 0, tile(s)))
    big_spec = pl.BlockSpec((PEER_HEADS, N_KEYS, tt), lambda s: (0, 0, tile(s)))
    return pl.pallas_call(
        functools.partial(_peer_kernel, n_blocks=n_blocks, tt=tt),
        out_shape=jax.ShapeDtypeStruct((d, t), F32),
        grid=(n_tiles * n_blocks + 1,),
        in_specs=[pl.BlockSpec((d, tt), lambda s: (0, tile(s))),
                  pl.BlockSpec((eb, d), lambda s: (s % n_blocks, 0)),
                  pl.BlockSpec((None, d, eb), lambda s: (prev(s) % n_blocks, 0, 0)),
                  row_spec, row_spec, big_spec, big_spec],
        out_specs=pl.BlockSpec((d, tt), lambda s: (0, prev(s) // n_blocks)),
        scratch_shapes=[pltpu.VMEM((eb, tt), F32), pltpu.VMEM((2, eb, tt), BF16),
                        pltpu.VMEM((2, KEYS_PER_BLOCK, PEER_HEADS, 2 * SUBLANES, tt), BF16),
                        pltpu.VMEM((eb, tt), BF16)],
        compiler_params=_params("arbitrary"),
        name="peer",
    )(x1t, u_b, vt_b, lim, f, r2, e2)


def _ln2_kernel(x1_ref, yt_ref, g_ref, b_ref, o_ref, *, alpha):
    o_ref[...] = _layer_norm(alpha * x1_ref[...] + yt_ref[...].T, g_ref[...], b_ref[...])


def _ln2(x1, y_t, g, b, alpha):
    t, d = x1.shape
    tm = _tile(t, 512)
    row = lambda i: (i, 0)
    return pl.pallas_call(
        functools.partial(_ln2_kernel, alpha=alpha),
        out_shape=jax.ShapeDtypeStruct((t, d), F32),
        grid=(t // tm,),
        in_specs=[pl.BlockSpec((tm, d), row), pl.BlockSpec((d, tm), lambda i: (0, i)), _full((1, d)),
                  _full((1, d))],
        out_specs=pl.BlockSpec((tm, d), row),
        compiler_params=_params("parallel"),
        name="ln2",
    )(x1, y_t, g.reshape(1, d), b.reshape(1, d))


def _rel_bias_table(rel_bias):
    n_left = ATT_WINDOW - REL_CLIP + CHUNK - 1
    n_right = CHUNK - 1 - REL_CLIP
    ext = jnp.concatenate([jnp.repeat(rel_bias[:, :1], n_left, axis=1), rel_bias,
                           jnp.repeat(rel_bias[:, -1:], max(n_right, 0), axis=1)], axis=1)
    return jnp.stack([ext[:, CHUNK - 1 - i:CHUNK - 1 - i + BAND] for i in range(CHUNK)], axis=1)


def _step_bias(table, cpb):
    n_heads = table.shape[0]
    table = (table - table[:, :1, :1]) * LOG2E
    tab_t = jnp.swapaxes(table, 1, 2)
    per_chunk = [jnp.pad(tab_t, ((0, 0), (c * CHUNK, (cpb - 1 - c) * CHUNK), (0, 0)),
                         constant_values=MASK_VALUE) for c in range(cpb)]
    both = jnp.stack(per_chunk, axis=2)
    rows = both.shape[1]
    both = both.reshape(n_heads // 2, 2, rows, cpb * CHUNK)
    return jnp.swapaxes(both, 1, 2).reshape(n_heads // 2, rows, 2 * cpb * CHUNK)


def _encoder_layer(x, hist_k, hist_v, h0, wts, alpha):
    nb, seq, d = x.shape
    width = d // 2
    t = nb * seq
    n_gt = width // LANES
    assert seq % CHUNK == 0 and width % LANES == 0 and nb % SUBLANES == 0
    assert hist_k is None or seq == CHUNK
    x2d = x.reshape(t, d)

    n_hist = 0 if hist_k is None else hist_k.shape[1]
    n_invalid = ATT_WINDOW - n_hist
    def history(hist, n_new):
        if hist is None:
            return jnp.zeros((nb, ATT_WINDOW + n_new, width), BF16)
        return jnp.pad(hist.reshape(nb, n_hist, width).astype(BF16), ((0, 0), (n_invalid, n_new), (0, 0)))
    if _tile(seq, 256) % LANES == 0:
        q_b, kpad, vtpad, k_new, v_new, u_tm = _proj(
            x2d, wts["w_in"], history(hist_k, seq), jnp.swapaxes(history(hist_v, seq), 1, 2), nb, seq, width)
    else:
        q_b, k_b, v_b, k_new, v_new, u_tm = _proj_short(x2d, wts["w_in"], nb, seq, width)
        kpad = jnp.concatenate([history(hist_k, 0), k_b.reshape(nb, seq, width)], axis=1)
        vtpad = jnp.swapaxes(jnp.concatenate([history(hist_v, 0), v_b.reshape(nb, seq, width)], axis=1), 1, 2)
    n_chunks = seq // CHUNK
    cpb = 2 if n_chunks % 2 == 0 else 1
    assert cpb == 2 or n_chunks == 1
    att_t = _attention(q_b, kpad, vtpad, _step_bias(wts["bias"], cpb), wts["norm_attn_g"], nb, seq, width,
                       cpb, n_invalid)

    ssm_tm, h_last = _ssm(u_tm.reshape(t, width), h0, wts["ssm"], nb, seq, width)

    x1, x1b, x1t = _mix(att_t, ssm_tm.reshape(seq, nb * width), x2d, wts["w_out"], wts["ln1_g"],
                        wts["ln1_b"], alpha, width)

    tt = _tile(t, PEER_TOKENS)
    lim, f, r2, e2 = _route(x1b, wts["peer_wq"], wts["peer_keys"], tt)
    y_t = _peer(x1t, wts["peer_u"], wts["peer_vt"], lim, f, r2, e2, tt)
    x2 = _ln2(x1, y_t, wts["ln2_g"], wts["ln2_b"], alpha)

    n_heads = width // HEAD_DIM
    return (x2.reshape(nb, seq, d), k_new.reshape(nb, -1, n_heads, HEAD_DIM),
            v_new.reshape(nb, -1, n_heads, HEAD_DIM), h_last)


def kernel(x_prompt, x_sample, cache_attn_k, cache_attn_v, state_ssm_re, state_ssm_im, w_in, rel_bias, norm_attn_g, ssm_a_re, ssm_a_im, ssm_log_dt, ssm_b_re, ssm_b_im, ssm_c_re, ssm_c_im, ssm_d, ssm_glu_w, ssm_glu_b, norm_ssm_g, w_out, ln1_g, ln1_b, peer_wq, peer_keys, peer_u, peer_v, ln2_g, ln2_b):
    depth = w_in.shape[0]
    alpha = (2.0 * depth) ** 0.25
    d = x_prompt.shape[-1]
    width = d // 2
    n_gt = width // LANES
    n_groups = width // SSM_GROUP
    keep = min(ATT_WINDOW, x_prompt.shape[1])

    yp, ys = x_prompt, x_sample
    outs = [[] for _ in range(8)]
    for l in range(depth):
        wts = dict(
            w_in=w_in[l].astype(BF16),
            bias=_rel_bias_table(rel_bias[l]),
            norm_attn_g=jnp.broadcast_to(norm_attn_g[l][:, None], (width, LANES)),
            ssm=_ssm_params(ssm_a_re[l], ssm_a_im[l], ssm_log_dt[l], ssm_b_re[l], ssm_b_im[l], ssm_c_re[l],
                            ssm_c_im[l], ssm_d[l], ssm_glu_w[l], ssm_glu_b[l], norm_ssm_g[l], width),
            w_out=w_out[l].astype(BF16), ln1_g=ln1_g[l], ln1_b=ln1_b[l],
            peer_wq=peer_wq[l].astype(BF16), peer_keys=peer_keys[l].astype(BF16),
            peer_u=peer_u[l].astype(BF16),
            peer_vt=jnp.swapaxes(peer_v[l].astype(BF16).reshape(-1, EXPERT_BLOCK, d), 1, 2),
            ln2_g=ln2_g[l], ln2_b=ln2_b[l])
        h0p = jnp.zeros((n_gt, x_prompt.shape[0], 2 * STATE_TILE), F32)
        yp, kp, vp, hp = _encoder_layer(yp, None, None, h0p, wts, alpha)
        h0s = _state_to_tiles(state_ssm_re[l].astype(F32), state_ssm_im[l].astype(F32), n_gt)
        ys, kn, vn, hs = _encoder_layer(ys, cache_attn_k[l], cache_attn_v[l], h0s, wts, alpha)
        hp_re, hp_im = _tiles_to_state(hp, n_groups)
        hs_re, hs_im = _tiles_to_state(hs, n_groups)
        assert kp.shape[1] == keep
        for acc, val in zip(outs, (kp, vp, hp_re, hp_im, kn, vn, hs_re, hs_im)):
            acc.append(val)
    return (yp, ys) + tuple(jnp.stack(o) for o in outs)
```

```python
import functools

import jax
import jax.numpy as jnp
from jax import lax
from jax.experimental import pallas as pl
from jax.experimental.pallas import tpu as pltpu

F32 = jnp.float32
BF16 = jnp.bfloat16

CHUNK = 64
LEFT_CHUNKS = 8
ATT_WINDOW = LEFT_CHUNKS * CHUNK
BAND = ATT_WINDOW + CHUNK
HEAD_DIM = 64
REL_CLIP = 2 * CHUNK
SSM_GROUP = 16
SSM_STATE = 64
PEER_HEADS = 8
PEER_TOPK = 16
N_KEYS = 128
PEER_DHALF = 128
LN_EPS = 1e-5
RMS_EPS = 1e-6
MASK_VALUE = -1e30
LOG2E = 1.4426950408889634
Q_SCALE = HEAD_DIM ** -0.5 * LOG2E

LANES = 128
SUBLANES = 8
GROUPS_PER_TILE = LANES // SSM_GROUP
STATE_TILE = GROUPS_PER_TILE * SSM_STATE
VMEM_LIMIT = 56 * 1024 * 1024
EXPERT_BLOCK = 1024
KEYS_PER_BLOCK = EXPERT_BLOCK // N_KEYS
PEER_TOKENS = 512
ATTN_PAIR_GROUP = 8
CHAIN_EVERY = 2
ROUTE_HEADS = 1

SORT16 = (
    (0, 13), (1, 12), (2, 15), (3, 14), (4, 8), (5, 6), (7, 11), (9, 10),
    (0, 5), (1, 7), (2, 9), (3, 4), (6, 13), (8, 14), (10, 15), (11, 12),
    (0, 1), (2, 3), (4, 5), (6, 8), (7, 9), (10, 11), (12, 13), (14, 15),
    (0, 2), (1, 3), (4, 10), (5, 11), (6, 7), (8, 9), (12, 14), (13, 15),
    (1, 2), (3, 12), (4, 6), (5, 7), (8, 10), (9, 11), (13, 14),
    (1, 4), (2, 6), (5, 8), (7, 10), (9, 13), (11, 14),
    (2, 4), (3, 6), (9, 12), (11, 13),
    (3, 5), (6, 8), (7, 9), (10, 12),
    (3, 4), (5, 6), (7, 8), (9, 10), (11, 12),
    (6, 7), (8, 9),
)


def _params(*semantics):
    return pltpu.CompilerParams(dimension_semantics=semantics, vmem_limit_bytes=VMEM_LIMIT)


def _tile(n, pref):
    t = min(n, pref)
    while n % t:
        t //= 2
    return t


def _gelu(x):
    return 0.5 * x * (1.0 + lax.erf(x * 0.7071067811865476))


def _full(shape):
    return pl.BlockSpec(shape, lambda *_: (0,) * len(shape))


def _proj_kernel(x_ref, w_ref, kbase_ref, vtbase_ref, q_ref, kpad_ref, vtpad_ref, kf_ref, vf_ref, u_ref,
                 *, width):
    del kbase_ref, vtbase_ref
    xb = x_ref[...].astype(BF16)
    acc = [jnp.dot(xb, w_ref[:, c * width:(c + 1) * width], preferred_element_type=F32) for c in range(4)]
    q_ref[...] = (acc[0] * Q_SCALE).astype(BF16)
    kf_ref[...] = acc[1]
    kpad_ref[...] = acc[1].astype(BF16)
    vf_ref[...] = acc[2]
    vtpad_ref[...] = acc[2].T.astype(BF16)
    u_ref[...] = acc[3]


def _proj(x2d, w_in_b, kbase, vtbase, nb, seq, width):
    t, d = x2d.shape
    tm = _tile(seq, 256)
    assert tm % LANES == 0 and ATT_WINDOW % tm == 0
    per_seq = seq // tm
    skip = ATT_WINDOW // tm
    keep = min(ATT_WINDOW, seq)
    dropped = per_seq - keep // tm
    row = lambda i: (i, 0)
    tail = lambda i: (i // per_seq, jnp.maximum(i % per_seq - dropped, 0), 0)
    tail_f32 = jax.ShapeDtypeStruct((nb, keep, width), F32)
    return pl.pallas_call(
        functools.partial(_proj_kernel, width=width),
        out_shape=(jax.ShapeDtypeStruct((t, width), BF16),
                   jax.ShapeDtypeStruct(kbase.shape, BF16), jax.ShapeDtypeStruct(vtbase.shape, BF16),
                   tail_f32, tail_f32, jax.ShapeDtypeStruct((seq, nb * width), F32)),
        grid=(t // tm,),
        in_specs=[pl.BlockSpec((tm, d), row),
                  pl.BlockSpec((d, 4 * width), lambda i: (0, 0)),
                  pl.BlockSpec(memory_space=pl.ANY), pl.BlockSpec(memory_space=pl.ANY)],
        out_specs=(pl.BlockSpec((tm, width), row),
                   pl.BlockSpec((None, tm, width), lambda i: (i // per_seq, skip + i % per_seq, 0)),
                   pl.BlockSpec((None, width, tm), lambda i: (i // per_seq, 0, skip + i % per_seq)),
                   pl.BlockSpec((None, tm, width), tail), pl.BlockSpec((None, tm, width), tail),
                   pl.BlockSpec((tm, width), lambda i: (i % per_seq, i // per_seq))),
        input_output_aliases={2: 1, 3: 2},
        compiler_params=_params("arbitrary"),
        name="proj",
    )(x2d, w_in_b, kbase, vtbase)


def _proj_short_kernel(x_ref, w_ref, q_ref, k_ref, v_ref, kf_ref, vf_ref, u_ref, *, width):
    xb = x_ref[...].astype(BF16)
    acc = [jnp.dot(xb, w_ref[:, c * width:(c + 1) * width], preferred_element_type=F32) for c in range(4)]
    q_ref[...] = (acc[0] * Q_SCALE).astype(BF16)
    kf_ref[...] = acc[1]
    k_ref[...] = acc[1].astype(BF16)
    vf_ref[...] = acc[2]
    v_ref[...] = acc[2].astype(BF16)
    u_ref[...] = acc[3]


def _proj_short(x2d, w_in_b, nb, seq, width):
    t, d = x2d.shape
    tm = _tile(seq, 256)
    per_seq = seq // tm
    row = lambda i: (i, 0)
    rows_f32 = jax.ShapeDtypeStruct((t, width), F32)
    rows_b16 = jax.ShapeDtypeStruct((t, width), BF16)
    return pl.pallas_call(
        functools.partial(_proj_short_kernel, width=width),
        out_shape=(rows_b16, rows_b16, rows_b16, rows_f32, rows_f32,
                   jax.ShapeDtypeStruct((seq, nb * width), F32)),
        grid=(t // tm,),
        in_specs=[pl.BlockSpec((tm, d), row),
                  pl.BlockSpec((d, 4 * width), lambda i: (0, 0))],
        out_specs=tuple([pl.BlockSpec((tm, width), row)] * 5
                        + [pl.BlockSpec((tm, width), lambda i: (i % per_seq, i // per_seq))]),
        compiler_params=_params("parallel"),
        name="proj",
    )(x2d, w_in_b)


def _attn_kernel(q_ref, k_ref, vt_ref, bias_ref, g_ref, o_ref, *, n_steps, cpb, n_invalid, width):
    nq = cpb * CHUNK
    nc = 2 * nq
    kb = ATT_WINDOW + nq
    lo = lax.broadcasted_iota(jnp.int32, (nq, LANES), 1) < HEAD_DIM
    nt_dims = (((1,), (1,)), ((), ()))
    group = ATTN_PAIR_GROUP
    k_split = -(-(kb // 2) // LANES) * LANES
    halves = ((0, k_split), (k_split, kb - k_split))
    z0, z1 = (cpb - 1) * CHUNK, ATT_WINDOW - REL_CLIP

    def pair_scores(r0, hp):
        cols = slice(hp * LANES, (hp + 1) * LANES)
        q2 = q_ref[pl.ds(r0, nq), cols]
        qcat = jnp.concatenate([jnp.where(lo, q2, 0), jnp.where(lo, 0, q2)], axis=0)
        return jnp.concatenate(
            [lax.dot_general(k_ref[pl.ds(r0 + k0, kn), cols], qcat, nt_dims, preferred_element_type=F32)
             for k0, kn in halves], axis=0)

    def step(r0, masked):
        tiles = []
        ssq = jnp.zeros((1, nq), F32)
        n_pairs = width // LANES
        for g0 in range(0, n_pairs, group):
            pairs = range(g0, min(g0 + group, n_pairs))
            s = jnp.concatenate([pair_scores(r0, hp) for hp in pairs], axis=1)
            parts = [s[z0:z1], s[z1:] + jnp.concatenate([bias_ref[hp, z1:, :] for hp in pairs], axis=1)]
            if z0:
                parts.insert(0, s[:z0] + jnp.concatenate([bias_ref[hp, :z0, :] for hp in pairs], axis=1))
            s = jnp.concatenate(parts, axis=0)
            if masked:
                s = jnp.where(lax.broadcasted_iota(jnp.int32, s.shape, 0) + r0 >= n_invalid, s, MASK_VALUE)
            m = jnp.max(s, axis=0, keepdims=True)
            e = jnp.exp2(s - m)
            rl = 1.0 / jnp.sum(e, axis=0, keepdims=True)
            p = e.astype(BF16)
            for i, hp in enumerate(pairs):
                cols = slice(hp * LANES, (hp + 1) * LANES)
                pc = slice(i * nc, (i + 1) * nc)
                o_t = sum(jnp.dot(vt_ref[cols, pl.ds(r0 + k0, kn)], p[k0:k0 + kn, pc],
                                  preferred_element_type=F32) for k0, kn in halves) * rl[:, pc]
                tile = jnp.concatenate([o_t[:HEAD_DIM, :nq], o_t[HEAD_DIM:, nq:]], axis=0)
                ssq = ssq + jnp.sum(tile * tile, axis=0, keepdims=True)
                tiles.append(tile)
        rinv = lax.rsqrt(ssq / width + RMS_EPS)
        for hp, tile in enumerate(tiles):
            rows = slice(hp * LANES, (hp + 1) * LANES)
            o_ref[rows, pl.ds(r0, nq)] = (tile * rinv * g_ref[rows, :nq]).astype(BF16)

    n_masked = min(n_steps, -(-n_invalid // nq))
    if n_steps == 1:
        step(0, n_masked > 0)
    else:
        def body(i, carry, masked):
            step(pl.multiple_of(i * nq, nq), masked)
            return carry
        lax.fori_loop(0, n_masked, functools.partial(body, masked=True), 0)
        lax.fori_loop(n_masked, n_steps, functools.partial(body, masked=False), 0)


def _attention(qkv, kpad, vtpad, bias, gain_b, nb, seq, width, cpb, n_invalid):
    lp = kpad.shape[1]
    nq = cpb * CHUNK
    return pl.pallas_call(
        functools.partial(_attn_kernel, n_steps=seq // nq, cpb=cpb, n_invalid=n_invalid, width=width),
        out_shape=jax.ShapeDtypeStruct((nb, width, seq), BF16),
        grid=(nb,),
        in_specs=[pl.BlockSpec((seq, width), lambda b: (b, 0)),
                  pl.BlockSpec((None, lp, width), lambda b: (b, 0, 0)),
                  pl.BlockSpec((None, width, lp), lambda b: (b, 0, 0)),
                  _full(bias.shape),
                  _full((width, LANES))],
        out_specs=pl.BlockSpec((None, width, seq), lambda b: (b, 0, 0)),
        compiler_params=_params("parallel"),
        name="attn",
    )(qkv, kpad, vtpad, bias, gain_b)


def _ssm_prep_kernel(are_ref, aim_ref, ldt_ref, bre_ref, bim_ref, abr_ref, abi_ref, bbr_ref, bbi_ref):
    a_re = are_ref[...]
    a_im = aim_ref[...]
    dt = jnp.exp(ldt_ref[...])
    mag = jnp.exp(a_re * dt)
    ang = a_im * dt
    ab_re = mag * jnp.cos(ang)
    ab_im = mag * jnp.sin(ang)
    n_re = ab_re - 1.0
    den = a_re * a_re + a_im * a_im
    c_re = (n_re * a_re + ab_im * a_im) / den
    c_im = (ab_im * a_re - n_re * a_im) / den
    abr_ref[...] = ab_re
    abi_ref[...] = ab_im
    b_re = bre_ref[...]
    b_im = bim_ref[...]
    bbr_ref[...] = c_re[:, None, :] * b_re - c_im[:, None, :] * b_im
    bbi_ref[...] = c_re[:, None, :] * b_im + c_im[:, None, :] * b_re


def _ssm_prep(a_re, a_im, log_dt, bt_re, bt_im):
    g, p = a_re.shape
    h = bt_re.shape[1]
    return pl.pallas_call(
        _ssm_prep_kernel,
        out_shape=(jax.ShapeDtypeStruct((g, p), F32), jax.ShapeDtypeStruct((g, p), F32),
                   jax.ShapeDtypeStruct((g, h, p), F32), jax.ShapeDtypeStruct((g, h, p), F32)),
        name="ssm_prep",
    )(a_re, a_im, log_dt.reshape(g, 1), bt_re, bt_im)


def _ssm_kernel(u_ref, h0_ref, are_ref, aim_ref, bre_ref, bim_ref, cre_ref, cim_ref, d_ref, wg_ref, bg_ref,
                gn_ref, o_ref, hlast_ref, hst_ref, bu_ref, y_ref, *, nb, tl, n_gt, width):
    sw = STATE_TILE

    @pl.when(pl.program_id(0) == 0)
    def _():
        hst_ref[...] = h0_ref[...]

    for gt in range(n_gt):
        cols = slice(gt * LANES, (gt + 1) * LANES)
        uf = u_ref[:, cols]
        ub = uf.astype(BF16)
        bu_ref[:, :sw] = jnp.dot(ub, bre_ref[gt], preferred_element_type=F32)
        bu_ref[:, sw:] = jnp.dot(ub, bim_ref[gt], preferred_element_type=F32)
        a_re = jnp.broadcast_to(are_ref[gt], (nb, sw))
        a_im = jnp.broadcast_to(aim_ref[gt], (nb, sw))

        def step(t, carry, a_re=a_re, a_im=a_im):
            h_re, h_im = carry
            r0 = pl.multiple_of(t * nb, nb)
            n_re = a_re * h_re - a_im * h_im + bu_ref[pl.ds(r0, nb), :sw]
            n_im = a_re * h_im + a_im * h_re + bu_ref[pl.ds(r0, nb), sw:]
            bu_ref[pl.ds(r0, nb), :sw] = n_re
            bu_ref[pl.ds(r0, nb), sw:] = n_im
            return n_re, n_im

        h_re, h_im = lax.fori_loop(0, tl, step, (hst_ref[gt, :, :sw], hst_ref[gt, :, sw:]), unroll=True)
        hst_ref[gt, :, :sw] = h_re
        hst_ref[gt, :, sw:] = h_im

        y = (jnp.dot(bu_ref[:, :sw].astype(BF16), cre_ref[gt], preferred_element_type=F32)
             - jnp.dot(bu_ref[:, sw:].astype(BF16), cim_ref[gt], preferred_element_type=F32)
             + d_ref[:, cols] * uf)
        y = _gelu(y)
        z = jnp.dot(y.astype(BF16), wg_ref[gt], preferred_element_type=F32) + bg_ref[:, cols]
        y_ref[:, cols] = y * jax.nn.sigmoid(z)

    yy = y_ref[...]
    ms = jnp.sum(yy * yy, axis=-1, keepdims=True) / width
    o_ref[...] = (yy * lax.rsqrt(ms + RMS_EPS) * gn_ref[...]).astype(BF16)
    hlast_ref[...] = hst_ref[...]


def _ssm(u_tm, h0, sp, nb, seq, width):
    n_gt = width // LANES
    tl = _tile(seq, 64)
    rows = tl * nb
    sw = STATE_TILE
    return pl.pallas_call(
        functools.partial(_ssm_kernel, nb=nb, tl=tl, n_gt=n_gt, width=width),
        out_shape=(jax.ShapeDtypeStruct((seq * nb, width), BF16),
                   jax.ShapeDtypeStruct((n_gt, nb, 2 * sw), F32)),
        grid=(seq // tl,),
        in_specs=[pl.BlockSpec((rows, width), lambda s: (s, 0)),
                  _full((n_gt, nb, 2 * sw)),
                  _full((n_gt, 1, sw)), _full((n_gt, 1, sw)),
                  _full((n_gt, LANES, sw)), _full((n_gt, LANES, sw)),
                  _full((n_gt, sw, LANES)), _full((n_gt, sw, LANES)),
                  _full((1, width)),
                  _full((n_gt, LANES, LANES)),
                  _full((1, width)), _full((1, width))],
        out_specs=(pl.BlockSpec((rows, width), lambda s: (s, 0)),
                   _full((n_gt, nb, 2 * sw))),
        scratch_shapes=[pltpu.VMEM((n_gt, nb, 2 * sw), F32),
                        pltpu.VMEM((rows, 2 * sw), F32),
                        pltpu.VMEM((rows, width), F32)],
        compiler_params=_params("arbitrary"),
        name="ssm",
    )(u_tm, h0, sp["a_re"], sp["a_im"], sp["b_re"], sp["b_im"], sp["c_re"], sp["c_im"], sp["d"],
      sp["wg"], sp["bg"], sp["gn"])


def _block_diag(m, n_gt):
    _, r, c = m.shape
    m4 = m.reshape(n_gt, GROUPS_PER_TILE, r, 1, c)
    eye = jnp.eye(GROUPS_PER_TILE, dtype=bool)[None, :, None, :, None]
    return jnp.where(eye, m4, 0).reshape(n_gt, GROUPS_PER_TILE * r, GROUPS_PER_TILE * c)


def _ssm_params(a_re, a_im, log_dt, b_re, b_im, c_re, c_im, d, glu_w, glu_b, gain, width):
    n_gt = width // LANES
    ab_re, ab_im, bb_re, bb_im = _ssm_prep(a_re, a_im, log_dt, jnp.swapaxes(b_re, 1, 2),
                                           jnp.swapaxes(b_im, 1, 2))
    return dict(
        a_re=ab_re.reshape(n_gt, 1, STATE_TILE), a_im=ab_im.reshape(n_gt, 1, STATE_TILE),
        b_re=_block_diag(bb_re, n_gt).astype(BF16), b_im=_block_diag(bb_im, n_gt).astype(BF16),
        c_re=_block_diag(jnp.swapaxes(c_re, 1, 2), n_gt).astype(BF16),
        c_im=_block_diag(jnp.swapaxes(c_im, 1, 2), n_gt).astype(BF16),
        d=d.reshape(1, width), wg=_block_diag(glu_w, n_gt).astype(BF16), bg=glu_b.reshape(1, width),
        gn=gain.reshape(1, width))


def _state_to_tiles(s_re, s_im, n_gt):
    nb = s_re.shape[0]
    def one(s):
        return jnp.swapaxes(s.reshape(nb, n_gt, STATE_TILE), 0, 1)
    return jnp.concatenate([one(s_re), one(s_im)], axis=-1)


def _tiles_to_state(h, n_groups):
    n_gt, nb, _ = h.shape
    def one(s):
        return jnp.swapaxes(s, 0, 1).reshape(nb, n_groups, SSM_STATE)
    return one(h[:, :, :STATE_TILE]), one(h[:, :, STATE_TILE:])


def _layer_norm(z, g, b):
    mu = jnp.mean(z, axis=-1, keepdims=True)
    zc = z - mu
    var = jnp.mean(zc * zc, axis=-1, keepdims=True)
    return zc * lax.rsqrt(var + LN_EPS) * g + b


def _mix_kernel(a_ref, s_ref, x_ref, wo_ref, g_ref, b_ref, x1_ref, x1b_ref, x1t_ref=None, *, alpha, width):
    tn_dims = (((0,), (0,)), ((), ()))
    tm = x_ref.shape[0]
    n_split = 2 if tm % (2 * LANES) == 0 else 1
    for r0 in range(0, tm, tm // n_split):
        rows = slice(r0, r0 + tm // n_split)
        mix = (lax.dot_general(a_ref[:, rows], wo_ref[:width], tn_dims, preferred_element_type=F32)
               + jnp.dot(s_ref[rows, :], wo_ref[width:], preferred_element_type=F32))
        x1 = _layer_norm(alpha * x_ref[rows, :] + mix, g_ref[...], b_ref[...])
        x1_ref[rows, :] = x1
        x1b_ref[rows, :] = x1.astype(BF16)
        if x1t_ref is not None:
            x1t_ref[:, rows] = x1.T.astype(BF16)


def _mix(att_t, ssm_tm, x2d, w_out_b, g, b, alpha, width):
    t, d = x2d.shape
    nb, _, seq = att_t.shape
    tm = _tile(seq, 512)
    per_seq = seq // tm
    row = lambda i: (i, 0)
    out_shape = [jax.ShapeDtypeStruct((t, d), F32), jax.ShapeDtypeStruct((t, d), BF16)]
    out_specs = [pl.BlockSpec((tm, d), row), pl.BlockSpec((tm, d), row)]
    if tm % LANES == 0:
        out_shape.append(jax.ShapeDtypeStruct((d, t), BF16))
        out_specs.append(pl.BlockSpec((d, tm), lambda i: (0, i)))
    outs = pl.pallas_call(
        functools.partial(_mix_kernel, alpha=alpha, width=width),
        out_shape=tuple(out_shape),
        grid=(t // tm,),
        in_specs=[pl.BlockSpec((None, width, tm), lambda i: (i // per_seq, 0, i % per_seq)),
                  pl.BlockSpec((tm, width), lambda i: (i % per_seq, i // per_seq)),
                  pl.BlockSpec((tm, d), row), _full((2 * width, d)), _full((1, d)), _full((1, d))],
        out_specs=tuple(out_specs),
        compiler_params=_params("parallel"),
        name="mix",
    )(att_t, ssm_tm, x2d, w_out_b, g.reshape(1, d), b.reshape(1, d))
    x1, x1b = outs[0], outs[1]
    return x1, x1b, (outs[2] if len(outs) == 3 else x1b.T)


def _sort16(v):
    v = list(v)
    for i, j in SORT16:
        v[i], v[j] = jnp.maximum(v[i], v[j]), jnp.minimum(v[i], v[j])
    return v


def _merge_top16(a, b):
    v = [jnp.maximum(a[j], b[PEER_TOPK - 1 - j]) for j in range(PEER_TOPK)]
    for dist in (8, 4, 2, 1):
        for i in range(PEER_TOPK):
            if not i & dist:
                v[i], v[i + dist] = jnp.maximum(v[i], v[i + dist]), jnp.minimum(v[i], v[i + dist])
    return v


def _fold_sublanes(v):
    for shift in (4, 2, 1):
        v = _merge_top16(v, [pltpu.roll(x, shift, 0) for x in v])
    return v


def _count_prefix(v, pred):
    assert len(v) == PEER_TOPK == 16
    full = pred(v[15])
    b3 = pred(v[7])
    b2 = pred(jnp.where(b3, v[11], v[3]))
    b1 = pred(jnp.where(b3, jnp.where(b2, v[13], v[9]), jnp.where(b2, v[5], v[1])))
    quad = [jnp.where(b1, v[4 * k + 2], v[4 * k]) for k in range(4)]
    b0 = pred(jnp.where(b3, jnp.where(b2, quad[3], quad[2]), jnp.where(b2, quad[1], quad[0])))
    n = (jnp.where(b3, 8.0, 0.0) + jnp.where(b2, 4.0, 0.0)) + (jnp.where(b1, 2.0, 0.0) + jnp.where(b0, 1.0, 0.0))
    return jnp.where(full, 16.0, n)


def _route_kernel(x_ref, wq_ref, keys_ref, lim_ref, f_ref, r2_ref, e2_ref, *, tt, tc):
    for hh in range(ROUTE_HEADS):
        _route_head(x_ref, wq_ref, keys_ref, lim_ref, f_ref, r2_ref, e2_ref, hh, tt, tc)


def _route_head(x_ref, wq_ref, keys_ref, lim_ref, f_ref, r2_ref, e2_ref, hh, tt, tc):
    dk = 2 * PEER_DHALF
    half = tt // 2
    qp = jnp.concatenate(
        [jnp.dot(x_ref[r0:r0 + half, :], wq_ref[:, hh * dk:(hh + 1) * dk], preferred_element_type=F32)
         for r0 in (0, half)], axis=0).astype(BF16)
    nt_dims = (((1,), (1,)), ((), ()))
    s_t = [lax.dot_general(keys_ref[hh, c], qp[:, c * PEER_DHALF:(c + 1) * PEER_DHALF], nt_dims,
                           preferred_element_type=F32) for c in range(2)]
    n_blocks = N_KEYS // SUBLANES
    for l0 in range(0, tt, tc):
        cols = slice(l0, l0 + tc)
        sub = lax.broadcasted_iota(jnp.int32, (SUBLANES, tc), 0)
        blocks = [[s_t[c][SUBLANES * j:SUBLANES * (j + 1), cols] for j in range(n_blocks)] for c in range(2)]
        v1, v2 = (_fold_sublanes(_sort16(blocks[c])) for c in range(2))
        v1_lo, v1_hi = v1[0], v1[SUBLANES]
        for r in range(1, SUBLANES):
            v1_lo = jnp.where(sub == r, v1[r], v1_lo)
            v1_hi = jnp.where(sub == r, v1[SUBLANES + r], v1_hi)
        cand_lo = [v1_lo + v2[b] for b in range(PEER_TOPK)]
        cand_hi = [v1_hi + v2[b] for b in range(PEER_TOPK)]
        top_s = _fold_sublanes(_merge_top16(cand_lo, cand_hi))
        tau = top_s[PEER_TOPK - 1]
        z = jnp.ones((SUBLANES, tc), F32)
        for k in range(1, PEER_TOPK):
            z = z + jnp.exp(top_s[k] - top_s[0])
        zinv = 1.0 / z
        r2, e2 = [], []
        for j in range(n_blocks):
            s1, s2 = blocks[0][j], blocks[1][j]
            cnt = _count_prefix(v2, lambda p, s1=s1: s1 + p >= tau)
            rank = _count_prefix(v2, lambda p, s2=s2: p > s2)
            lim_j = jnp.where(s1 >= v1[PEER_TOPK - 1], cnt, 0.0)
            f_j = jnp.exp(s1 - v1[0]) * zinv
            for g in range(SUBLANES // KEYS_PER_BLOCK):
                rows = slice(g * KEYS_PER_BLOCK, (g + 1) * KEYS_PER_BLOCK)
                lim_ref[hh, j * (SUBLANES // KEYS_PER_BLOCK) + g, :, cols] = lim_j[rows]
                f_ref[hh, j * (SUBLANES // KEYS_PER_BLOCK) + g, :, cols] = f_j[rows]
            r2.append(rank)
            e2.append(jnp.exp(s2 - v2[0]))
        r2_ref[hh, :, cols] = jnp.concatenate(r2, axis=0).astype(BF16)
        e2_ref[hh, :, cols] = jnp.concatenate(e2, axis=0).astype(BF16)


def _route(x1b, wq_b, keys_b, tt):
    t, d = x1b.shape
    dk = 2 * PEER_DHALF
    tc = _tile(tt, 256)
    hps = ROUTE_HEADS
    big_spec = pl.BlockSpec((hps, N_KEYS, tt), lambda i, h: (h, 0, i))
    n_eb = N_KEYS // KEYS_PER_BLOCK
    row_shape = jax.ShapeDtypeStruct((PEER_HEADS, n_eb, KEYS_PER_BLOCK, t), F32)
    row_spec = pl.BlockSpec((hps, n_eb, KEYS_PER_BLOCK, tt), lambda i, h: (h, 0, 0, i))
    return pl.pallas_call(
        functools.partial(_route_kernel, tt=tt, tc=tc),
        out_shape=(row_shape, row_shape,
                   jax.ShapeDtypeStruct((PEER_HEADS, N_KEYS, t), BF16),
                   jax.ShapeDtypeStruct((PEER_HEADS, N_KEYS, t), BF16)),
        grid=(t // tt, PEER_HEADS // hps),
        in_specs=[pl.BlockSpec((tt, d), lambda i, h: (i, 0)),
                  pl.BlockSpec((d, hps * dk), lambda i, h: (0, h)),
                  pl.BlockSpec((hps, 2, N_KEYS, PEER_DHALF), lambda i, h: (h, 0, 0, 0))],
        out_specs=(row_spec, row_spec, big_spec, big_spec),
        compiler_params=_params("parallel", "parallel"),
        name="route",
    )(x1b, wq_b, keys_b)


def _peer_kernel(xt_ref, u_ref, vt_ref, lim_ref, f_ref, r2_ref, e2_ref, y_ref, a_ref, hid_ref, bc_ref, w_ref,
                 *, n_blocks, tt):
    s = pl.program_id(0)
    slot = lax.rem(s, 2)
    rc = 2 * SUBLANES

    @pl.when(s == 0)
    def _():
        hid_ref[1] = jnp.zeros(hid_ref.shape[1:], BF16)

    @pl.when(lax.rem(jnp.maximum(s - 1, 0), n_blocks) == 0)
    def _():
        y_ref[...] = jnp.zeros_like(y_ref)

    a_ref[...] = jnp.dot(u_ref[...], xt_ref[...], preferred_element_type=F32)
    y_ref[...] += jnp.dot(vt_ref[...], hid_ref[1 - slot], preferred_element_type=F32)
    zero = jnp.zeros((rc, tt), BF16)
    for q in range(KEYS_PER_BLOCK):
        for h in range(PEER_HEADS):
            bc_ref[0, q, h] = jnp.broadcast_to(lim_ref[h, q:q + 1, :], (rc, tt)).astype(BF16)
            bc_ref[1, q, h] = jnp.broadcast_to(f_ref[h, q:q + 1, :], (rc, tt)).astype(BF16)
    never = r2_ref[0, 0:rc, :] < -1.0
    dep = zero
    for q in range(KEYS_PER_BLOCK):
        for k in range(N_KEYS // rc):
            rows = slice(k * rc, (k + 1) * rc)
            w = dep
            for h in range(PEER_HEADS):
                lim_t = bc_ref[0, q, h] + dep
                w = w + jnp.where(r2_ref[h, rows, :] < lim_t, e2_ref[h, rows, :], zero) * bc_ref[1, q, h]
                if h % CHAIN_EVERY == CHAIN_EVERY - 1:
                    dep = jnp.where(never, w, zero)
            w_ref[q * N_KEYS + k * rc:q * N_KEYS + (k + 1) * rc, :] = w
    hid_ref[slot] = _gelu(a_ref[...]).astype(BF16) * w_ref[...]


def _peer(x1t, u_b, vt_b, lim, f, r2, e2, tt):
    d, t = x1t.shape
    eb = EXPERT_BLOCK
    n_blocks = u_b.shape[0] // eb
    n_tiles = t // tt
    tile = lambda s: jnp.minimum(s // n_blocks, n_tiles - 1)
    prev = lambda s: jnp.maximum(s - 1, 0)
    row_spec = pl.BlockSpec((PEER_HEADS, None, KEYS_PER_BLOCK, tt), lambda s: (0, s % n_blocks, 0, tile(s)))
    big_spec = pl.BlockSpec((PEER_HEADS, N_KEYS, tt), lambda s: (0, 0, tile(s)))
    return pl.pallas_call(
        functools.partial(_peer_kernel, n_blocks=n_blocks, tt=tt),
        out_shape=jax.ShapeDtypeStruct((d, t), F32),
        grid=(n_tiles * n_blocks + 1,),
        in_specs=[pl.BlockSpec((d, tt), lambda s: (0, tile(s))),
                  pl.BlockSpec((eb, d), lambda s: (s % n_blocks, 0)),
                  pl.BlockSpec((None, d, eb), lambda s: (prev(s) % n_blocks, 0, 0)),
                  row_spec, row_spec, big_spec, big_spec],
        out_specs=pl.BlockSpec((d, tt), lambda s: (0, prev(s) // n_blocks)),
        scratch_shapes=[pltpu.VMEM((eb, tt), F32), pltpu.VMEM((2, eb, tt), BF16),
                        pltpu.VMEM((2, KEYS_PER_BLOCK, PEER_HEADS, 2 * SUBLANES, tt), BF16),
                        pltpu.VMEM((eb, tt), BF16)],
        compiler_params=_params("arbitrary"),
        name="peer",
    )(x1t, u_b, vt_b, lim, f, r2, e2)


def _ln2_kernel(x1_ref, yt_ref, g_ref, b_ref, o_ref, *, alpha):
    o_ref[...] = _layer_norm(alpha * x1_ref[...] + yt_ref[...].T, g_ref[...], b_ref[...])


def _ln2(x1, y_t, g, b, alpha):
    t, d = x1.shape
    tm = _tile(t, 512)
    row = lambda i: (i, 0)
    return pl.pallas_call(
        functools.partial(_ln2_kernel, alpha=alpha),
        out_shape=jax.ShapeDtypeStruct((t, d), F32),
        grid=(t // tm,),
        in_specs=[pl.BlockSpec((tm, d), row), pl.BlockSpec((d, tm), lambda i: (0, i)), _full((1, d)),
                  _full((1, d))],
        out_specs=pl.BlockSpec((tm, d), row),
        compiler_params=_params("parallel"),
        name="ln2",
    )(x1, y_t, g.reshape(1, d), b.reshape(1, d))


def _rel_bias_table(rel_bias):
    n_left = ATT_WINDOW - REL_CLIP + CHUNK - 1
    n_right = CHUNK - 1 - REL_CLIP
    ext = jnp.concatenate([jnp.repeat(rel_bias[:, :1], n_left, axis=1), rel_bias,
                           jnp.repeat(rel_bias[:, -1:], max(n_right, 0), axis=1)], axis=1)
    return jnp.stack([ext[:, CHUNK - 1 - i:CHUNK - 1 - i + BAND] for i in range(CHUNK)], axis=1)


def _step_bias(table, cpb):
    n_heads = table.shape[0]
    table = (table - table[:, :1, :1]) * LOG2E
    tab_t = jnp.swapaxes(table, 1, 2)
    per_chunk = [jnp.pad(tab_t, ((0, 0), (c * CHUNK, (cpb - 1 - c) * CHUNK), (0, 0)),
                         constant_values=MASK_VALUE) for c in range(cpb)]
    both = jnp.stack(per_chunk, axis=2)
    rows = both.shape[1]
    both = both.reshape(n_heads // 2, 2, rows, cpb * CHUNK)
    return jnp.swapaxes(both, 1, 2).reshape(n_heads // 2, rows, 2 * cpb * CHUNK)


def _encoder_layer(x, hist_k, hist_v, h0, wts, alpha):
    nb, seq, d = x.shape
    width = d // 2
    t = nb * seq
    n_gt = width // LANES
    assert seq % CHUNK == 0 and width % LANES == 0 and nb % SUBLANES == 0
    assert hist_k is None or seq == CHUNK
    x2d = x.reshape(t, d)

    n_hist = 0 if hist_k is None else hist_k.shape[1]
    n_invalid = ATT_WINDOW - n_hist
    def history(hist, n_new):
        if hist is None:
            return jnp.zeros((nb, ATT_WINDOW + n_new, width), BF16)
        return jnp.pad(hist.reshape(nb, n_hist, width).astype(BF16), ((0, 0), (n_invalid, n_new), (0, 0)))
    if _tile(seq, 256) % LANES == 0:
        q_b, kpad, vtpad, k_new, v_new, u_tm = _proj(
            x2d, wts["w_in"], history(hist_k, seq), jnp.swapaxes(history(hist_v, seq), 1, 2), nb, seq, width)
    else:
        q_b, k_b, v_b, k_new, v_new, u_tm = _proj_short(x2d, wts["w_in"], nb, seq, width)
        kpad = jnp.concatenate([history(hist_k, 0), k_b.reshape(nb, seq, width)], axis=1)
        vtpad = jnp.swapaxes(jnp.concatenate([history(hist_v, 0), v_b.reshape(nb, seq, width)], axis=1), 1, 2)
    n_chunks = seq // CHUNK
    cpb = 2 if n_chunks % 2 == 0 else 1
    assert cpb == 2 or n_chunks == 1
    att_t = _attention(q_b, kpad, vtpad, _step_bias(wts["bias"], cpb), wts["norm_attn_g"], nb, seq, width,
                       cpb, n_invalid)

    ssm_tm, h_last = _ssm(u_tm.reshape(t, width), h0, wts["ssm"], nb, seq, width)

    x1, x1b, x1t = _mix(att_t, ssm_tm.reshape(seq, nb * width), x2d, wts["w_out"], wts["ln1_g"],
                        wts["ln1_b"], alpha, width)

    tt = _tile(t, PEER_TOKENS)
    lim, f, r2, e2 = _route(x1b, wts["peer_wq"], wts["peer_keys"], tt)
    y_t = _peer(x1t, wts["peer_u"], wts["peer_vt"], lim, f, r2, e2, tt)
    x2 = _ln2(x1, y_t, wts["ln2_g"], wts["ln2_b"], alpha)

    n_heads = width // HEAD_DIM
    return (x2.reshape(nb, seq, d), k_new.reshape(nb, -1, n_heads, HEAD_DIM),
            v_new.reshape(nb, -1, n_heads, HEAD_DIM), h_last)


def kernel(x_prompt, x_sample, cache_attn_k, cache_attn_v, state_ssm_re, state_ssm_im, w_in, rel_bias, norm_attn_g, ssm_a_re, ssm_a_im, ssm_log_dt, ssm_b_re, ssm_b_im, ssm_c_re, ssm_c_im, ssm_d, ssm_glu_w, ssm_glu_b, norm_ssm_g, w_out, ln1_g, ln1_b, peer_wq, peer_keys, peer_u, peer_v, ln2_g, ln2_b):
    depth = w_in.shape[0]
    alpha = (2.0 * depth) ** 0.25
    d = x_prompt.shape[-1]
    width = d // 2
    n_gt = width // LANES
    n_groups = width // SSM_GROUP
    keep = min(ATT_WINDOW, x_prompt.shape[1])

    yp, ys = x_prompt, x_sample
    outs = [[] for _ in range(8)]
    for l in range(depth):
        wts = dict(
            w_in=w_in[l].astype(BF16),
            bias=_rel_bias_table(rel_bias[l]),
            norm_attn_g=jnp.broadcast_to(norm_attn_g[l][:, None], (width, LANES)),
            ssm=_ssm_params(ssm_a_re[l], ssm_a_im[l], ssm_log_dt[l], ssm_b_re[l], ssm_b_im[l], ssm_c_re[l],
                            ssm_c_im[l], ssm_d[l], ssm_glu_w[l], ssm_glu_b[l], norm_ssm_g[l], width),
            w_out=w_out[l].astype(BF16), ln1_g=ln1_g[l], ln1_b=ln1_b[l],
            peer_wq=peer_wq[l].astype(BF16), peer_keys=peer_keys[l].astype(BF16),
            peer_u=peer_u[l].astype(BF16),
            peer_vt=jnp.swapaxes(peer_v[l].astype(BF16).reshape(-1, EXPERT_BLOCK, d), 1, 2),
            ln2_g=ln2_g[l], ln2_b=ln2_b[l])
        h0p = jnp.zeros((n_gt, x_prompt.shape[0], 2 * STATE_TILE), F32)
        yp, kp, vp, hp = _encoder_layer(yp, None, None, h0p, wts, alpha)
        h0s = _state_to_tiles(state_ssm_re[l].astype(F32), state_ssm_im[l].astype(F32), n_gt)
        ys, kn, vn, hs = _encoder_layer(ys, cache_attn_k[l], cache_attn_v[l], h0s, wts, alpha)
        hp_re, hp_im = _tiles_to_state(hp, n_groups)
        hs_re, hs_im = _tiles_to_state(hs, n_groups)
        assert kp.shape[1] == keep
        for acc, val in zip(outs, (kp, vp, hp_re, hp_im, kn, vn, hs_re, hs_im)):
            acc.append(val)
    return (yp, ys) + tuple(jnp.stack(o) for o in outs)
```

```python
import functools

import jax
import jax.numpy as jnp
from jax import lax
from jax.experimental import pallas as pl
from jax.experimental.pallas import tpu as pltpu

F32 = jnp.float32
BF16 = jnp.bfloat16

CHUNK = 64
LEFT_CHUNKS = 8
ATT_WINDOW = LEFT_CHUNKS * CHUNK
BAND = ATT_WINDOW + CHUNK
HEAD_DIM = 64
REL_CLIP = 2 * CHUNK
SSM_GROUP = 16
SSM_STATE = 64
PEER_HEADS = 8
PEER_TOPK = 16
N_KEYS = 128
PEER_DHALF = 128
LN_EPS = 1e-5
RMS_EPS = 1e-6
MASK_VALUE = -1e30
LOG2E = 1.4426950408889634
Q_SCALE = HEAD_DIM ** -0.5 * LOG2E

LANES = 128
SUBLANES = 8
GROUPS_PER_TILE = LANES // SSM_GROUP
STATE_TILE = GROUPS_PER_TILE * SSM_STATE
VMEM_LIMIT = 56 * 1024 * 1024
EXPERT_BLOCK = 1024
KEYS_PER_BLOCK = EXPERT_BLOCK // N_KEYS
PEER_TOKENS = 512
ATTN_PAIR_GROUP = 8
CHAIN_EVERY = 4
ROUTE_HEADS = 1

SORT16 = (
    (0, 13), (1, 12), (2, 15), (3, 14), (4, 8), (5, 6), (7, 11), (9, 10),
    (0, 5), (1, 7), (2, 9), (3, 4), (6, 13), (8, 14), (10, 15), (11, 12),
    (0, 1), (2, 3), (4, 5), (6, 8), (7, 9), (10, 11), (12, 13), (14, 15),
    (0, 2), (1, 3), (4, 10), (5, 11), (6, 7), (8, 9), (12, 14), (13, 15),
    (1, 2), (3, 12), (4, 6), (5, 7), (8, 10), (9, 11), (13, 14),
    (1, 4), (2, 6), (5, 8), (7, 10), (9, 13), (11, 14),
    (2, 4), (3, 6), (9, 12), (11, 13),
    (3, 5), (6, 8), (7, 9), (10, 12),
    (3, 4), (5, 6), (7, 8), (9, 10), (11, 12),
    (6, 7), (8, 9),
)


def _params(*semantics):
    return pltpu.CompilerParams(dimension_semantics=semantics, vmem_limit_bytes=VMEM_LIMIT)


def _tile(n, pref):
    t = min(n, pref)
    while n % t:
        t //= 2
    return t


def _gelu(x):
    return 0.5 * x * (1.0 + lax.erf(x * 0.7071067811865476))


def _full(shape):
    return pl.BlockSpec(shape, lambda *_: (0,) * len(shape))


def _proj_kernel(x_ref, w_ref, kbase_ref, vtbase_ref, q_ref, kpad_ref, vtpad_ref, kf_ref, vf_ref, u_ref,
                 *, width):
    del kbase_ref, vtbase_ref
    xb = x_ref[...].astype(BF16)
    acc = [jnp.dot(xb, w_ref[:, c * width:(c + 1) * width], preferred_element_type=F32) for c in range(4)]
    q_ref[...] = (acc[0] * Q_SCALE).astype(BF16)
    kf_ref[...] = acc[1]
    kpad_ref[...] = acc[1].astype(BF16)
    vf_ref[...] = acc[2]
    vtpad_ref[...] = acc[2].T.astype(BF16)
    u_ref[...] = acc[3]


def _proj(x2d, w_in_b, kbase, vtbase, nb, seq, width):
    t, d = x2d.shape
    tm = _tile(seq, 256)
    assert tm % LANES == 0 and ATT_WINDOW % tm == 0
    per_seq = seq // tm
    skip = ATT_WINDOW // tm
    keep = min(ATT_WINDOW, seq)
    dropped = per_seq - keep // tm
    row = lambda i: (i, 0)
    tail = lambda i: (i // per_seq, jnp.maximum(i % per_seq - dropped, 0), 0)
    tail_f32 = jax.ShapeDtypeStruct((nb, keep, width), F32)
    return pl.pallas_call(
        functools.partial(_proj_kernel, width=width),
        out_shape=(jax.ShapeDtypeStruct((t, width), BF16),
                   jax.ShapeDtypeStruct(kbase.shape, BF16), jax.ShapeDtypeStruct(vtbase.shape, BF16),
                   tail_f32, tail_f32, jax.ShapeDtypeStruct((seq, nb * width), F32)),
        grid=(t // tm,),
        in_specs=[pl.BlockSpec((tm, d), row),
                  pl.BlockSpec((d, 4 * width), lambda i: (0, 0)),
                  pl.BlockSpec(memory_space=pl.ANY), pl.BlockSpec(memory_space=pl.ANY)],
        out_specs=(pl.BlockSpec((tm, width), row),
                   pl.BlockSpec((None, tm, width), lambda i: (i // per_seq, skip + i % per_seq, 0)),
                   pl.BlockSpec((None, width, tm), lambda i: (i // per_seq, 0, skip + i % per_seq)),
                   pl.BlockSpec((None, tm, width), tail), pl.BlockSpec((None, tm, width), tail),
                   pl.BlockSpec((tm, width), lambda i: (i % per_seq, i // per_seq))),
        input_output_aliases={2: 1, 3: 2},
        compiler_params=_params("arbitrary"),
        name="proj",
    )(x2d, w_in_b, kbase, vtbase)


def _proj_short_kernel(x_ref, w_ref, q_ref, k_ref, v_ref, kf_ref, vf_ref, u_ref, *, width):
    xb = x_ref[...].astype(BF16)
    acc = [jnp.dot(xb, w_ref[:, c * width:(c + 1) * width], preferred_element_type=F32) for c in range(4)]
    q_ref[...] = (acc[0] * Q_SCALE).astype(BF16)
    kf_ref[...] = acc[1]
    k_ref[...] = acc[1].astype(BF16)
    vf_ref[...] = acc[2]
    v_ref[...] = acc[2].astype(BF16)
    u_ref[...] = acc[3]


def _proj_short(x2d, w_in_b, nb, seq, width):
    t, d = x2d.shape
    tm = _tile(seq, 256)
    per_seq = seq // tm
    row = lambda i: (i, 0)
    rows_f32 = jax.ShapeDtypeStruct((t, width), F32)
    rows_b16 = jax.ShapeDtypeStruct((t, width), BF16)
    return pl.pallas_call(
        functools.partial(_proj_short_kernel, width=width),
        out_shape=(rows_b16, rows_b16, rows_b16, rows_f32, rows_f32,
                   jax.ShapeDtypeStruct((seq, nb * width), F32)),
        grid=(t // tm,),
        in_specs=[pl.BlockSpec((tm, d), row),
                  pl.BlockSpec((d, 4 * width), lambda i: (0, 0))],
        out_specs=tuple([pl.BlockSpec((tm, width), row)] * 5
                        + [pl.BlockSpec((tm, width), lambda i: (i % per_seq, i // per_seq))]),
        compiler_params=_params("parallel"),
        name="proj",
    )(x2d, w_in_b)


def _attn_kernel(q_ref, k_ref, vt_ref, bias_ref, g_ref, o_ref, *, n_steps, cpb, n_invalid, width):
    nq = cpb * CHUNK
    nc = 2 * nq
    kb = ATT_WINDOW + nq
    lo = lax.broadcasted_iota(jnp.int32, (nq, LANES), 1) < HEAD_DIM
    nt_dims = (((1,), (1,)), ((), ()))
    group = ATTN_PAIR_GROUP
    k_split = -(-(kb // 2) // LANES) * LANES
    halves = ((0, k_split), (k_split, kb - k_split))
    z0, z1 = (cpb - 1) * CHUNK, ATT_WINDOW - REL_CLIP

    def pair_scores(r0, hp):
        cols = slice(hp * LANES, (hp + 1) * LANES)
        q2 = q_ref[pl.ds(r0, nq), cols]
        qcat = jnp.concatenate([jnp.where(lo, q2, 0), jnp.where(lo, 0, q2)], axis=0)
        return jnp.concatenate(
            [lax.dot_general(k_ref[pl.ds(r0 + k0, kn), cols], qcat, nt_dims, preferred_element_type=F32)
             for k0, kn in halves], axis=0)

    def step(r0, masked):
        tiles = []
        ssq = jnp.zeros((1, nq), F32)
        n_pairs = width // LANES
        for g0 in range(0, n_pairs, group):
            pairs = range(g0, min(g0 + group, n_pairs))
            s = jnp.concatenate([pair_scores(r0, hp) for hp in pairs], axis=1)
            parts = [s[z0:z1], s[z1:] + jnp.concatenate([bias_ref[hp, z1:, :] for hp in pairs], axis=1)]
            if z0:
                parts.insert(0, s[:z0] + jnp.concatenate([bias_ref[hp, :z0, :] for hp in pairs], axis=1))
            s = jnp.concatenate(parts, axis=0)
            if masked:
                s = jnp.where(lax.broadcasted_iota(jnp.int32, s.shape, 0) + r0 >= n_invalid, s, MASK_VALUE)
            m = jnp.max(s, axis=0, keepdims=True)
            e = jnp.exp2(s - m)
            rl = 1.0 / jnp.sum(e, axis=0, keepdims=True)
            p = e.astype(BF16)
            for i, hp in enumerate(pairs):
                cols = slice(hp * LANES, (hp + 1) * LANES)
                pc = slice(i * nc, (i + 1) * nc)
                o_t = sum(jnp.dot(vt_ref[cols, pl.ds(r0 + k0, kn)], p[k0:k0 + kn, pc],
                                  preferred_element_type=F32) for k0, kn in halves) * rl[:, pc]
                tile = jnp.concatenate([o_t[:HEAD_DIM, :nq], o_t[HEAD_DIM:, nq:]], axis=0)
                ssq = ssq + jnp.sum(tile * tile, axis=0, keepdims=True)
                tiles.append(tile)
        rinv = lax.rsqrt(ssq / width + RMS_EPS)
        for hp, tile in enumerate(tiles):
            rows = slice(hp * LANES, (hp + 1) * LANES)
            o_ref[rows, pl.ds(r0, nq)] = (tile * rinv * g_ref[rows, :nq]).astype(BF16)

    n_masked = min(n_steps, -(-n_invalid // nq))
    if n_steps == 1:
        step(0, n_masked > 0)
    else:
        def body(i, carry, masked):
            step(pl.multiple_of(i * nq, nq), masked)
            return carry
        lax.fori_loop(0, n_masked, functools.partial(body, masked=True), 0)
        lax.fori_loop(n_masked, n_steps, functools.partial(body, masked=False), 0)


def _attention(qkv, kpad, vtpad, bias, gain_b, nb, seq, width, cpb, n_invalid):
    lp = kpad.shape[1]
    nq = cpb * CHUNK
    return pl.pallas_call(
        functools.partial(_attn_kernel, n_steps=seq // nq, cpb=cpb, n_invalid=n_invalid, width=width),
        out_shape=jax.ShapeDtypeStruct((nb, width, seq), BF16),
        grid=(nb,),
        in_specs=[pl.BlockSpec((seq, width), lambda b: (b, 0)),
                  pl.BlockSpec((None, lp, width), lambda b: (b, 0, 0)),
                  pl.BlockSpec((None, width, lp), lambda b: (b, 0, 0)),
                  _full(bias.shape),
                  _full((width, LANES))],
        out_specs=pl.BlockSpec((None, width, seq), lambda b: (b, 0, 0)),
        compiler_params=_params("parallel"),
        name="attn",
    )(qkv, kpad, vtpad, bias, gain_b)


def _ssm_prep_kernel(are_ref, aim_ref, ldt_ref, bre_ref, bim_ref, abr_ref, abi_ref, bbr_ref, bbi_ref):
    a_re = are_ref[...]
    a_im = aim_ref[...]
    dt = jnp.exp(ldt_ref[...])
    mag = jnp.exp(a_re * dt)
    ang = a_im * dt
    ab_re = mag * jnp.cos(ang)
    ab_im = mag * jnp.sin(ang)
    n_re = ab_re - 1.0
    den = a_re * a_re + a_im * a_im
    c_re = (n_re * a_re + ab_im * a_im) / den
    c_im = (ab_im * a_re - n_re * a_im) / den
    abr_ref[...] = ab_re
    abi_ref[...] = ab_im
    b_re = bre_ref[...]
    b_im = bim_ref[...]
    bbr_ref[...] = c_re[:, None, :] * b_re - c_im[:, None, :] * b_im
    bbi_ref[...] = c_re[:, None, :] * b_im + c_im[:, None, :] * b_re


def _ssm_prep(a_re, a_im, log_dt, bt_re, bt_im):
    g, p = a_re.shape
    h = bt_re.shape[1]
    return pl.pallas_call(
        _ssm_prep_kernel,
        out_shape=(jax.ShapeDtypeStruct((g, p), F32), jax.ShapeDtypeStruct((g, p), F32),
                   jax.ShapeDtypeStruct((g, h, p), F32), jax.ShapeDtypeStruct((g, h, p), F32)),
        name="ssm_prep",
    )(a_re, a_im, log_dt.reshape(g, 1), bt_re, bt_im)


def _ssm_kernel(u_ref, h0_ref, are_ref, aim_ref, bre_ref, bim_ref, cre_ref, cim_ref, d_ref, wg_ref, bg_ref,
                gn_ref, o_ref, hlast_ref, hst_ref, bu_ref, y_ref, *, nb, tl, n_gt, width):
    sw = STATE_TILE

    @pl.when(pl.program_id(0) == 0)
    def _():
        hst_ref[...] = h0_ref[...]

    for gt in range(n_gt):
        cols = slice(gt * LANES, (gt + 1) * LANES)
        uf = u_ref[:, cols]
        ub = uf.astype(BF16)
        bu_ref[:, :sw] = jnp.dot(ub, bre_ref[gt], preferred_element_type=F32)
        bu_ref[:, sw:] = jnp.dot(ub, bim_ref[gt], preferred_element_type=F32)
        a_re = jnp.broadcast_to(are_ref[gt], (nb, sw))
        a_im = jnp.broadcast_to(aim_ref[gt], (nb, sw))

        def step(t, carry, a_re=a_re, a_im=a_im):
            h_re, h_im = carry
            r0 = pl.multiple_of(t * nb, nb)
            n_re = a_re * h_re - a_im * h_im + bu_ref[pl.ds(r0, nb), :sw]
            n_im = a_re * h_im + a_im * h_re + bu_ref[pl.ds(r0, nb), sw:]
            bu_ref[pl.ds(r0, nb), :sw] = n_re
            bu_ref[pl.ds(r0, nb), sw:] = n_im
            return n_re, n_im

        h_re, h_im = lax.fori_loop(0, tl, step, (hst_ref[gt, :, :sw], hst_ref[gt, :, sw:]), unroll=True)
        hst_ref[gt, :, :sw] = h_re
        hst_ref[gt, :, sw:] = h_im

        y = (jnp.dot(bu_ref[:, :sw].astype(BF16), cre_ref[gt], preferred_element_type=F32)
             - jnp.dot(bu_ref[:, sw:].astype(BF16), cim_ref[gt], preferred_element_type=F32)
             + d_ref[:, cols] * uf)
        y = _gelu(y)
        z = jnp.dot(y.astype(BF16), wg_ref[gt], preferred_element_type=F32) + bg_ref[:, cols]
        y_ref[:, cols] = y * jax.nn.sigmoid(z)

    yy = y_ref[...]
    ms = jnp.sum(yy * yy, axis=-1, keepdims=True) / width
    o_ref[...] = (yy * lax.rsqrt(ms + RMS_EPS) * gn_ref[...]).astype(BF16)
    hlast_ref[...] = hst_ref[...]


def _ssm(u_tm, h0, sp, nb, seq, width):
    n_gt = width // LANES
    tl = _tile(seq, 64)
    rows = tl * nb
    sw = STATE_TILE
    return pl.pallas_call(
        functools.partial(_ssm_kernel, nb=nb, tl=tl, n_gt=n_gt, width=width),
        out_shape=(jax.ShapeDtypeStruct((seq * nb, width), BF16),
                   jax.ShapeDtypeStruct((n_gt, nb, 2 * sw), F32)),
        grid=(seq // tl,),
        in_specs=[pl.BlockSpec((rows, width), lambda s: (s, 0)),
                  _full((n_gt, nb, 2 * sw)),
                  _full((n_gt, 1, sw)), _full((n_gt, 1, sw)),
                  _full((n_gt, LANES, sw)), _full((n_gt, LANES, sw)),
                  _full((n_gt, sw, LANES)), _full((n_gt, sw, LANES)),
                  _full((1, width)),
                  _full((n_gt, LANES, LANES)),
                  _full((1, width)), _full((1, width))],
        out_specs=(pl.BlockSpec((rows, width), lambda s: (s, 0)),
                   _full((n_gt, nb, 2 * sw))),
        scratch_shapes=[pltpu.VMEM((n_gt, nb, 2 * sw), F32),
                        pltpu.VMEM((rows, 2 * sw), F32),
                        pltpu.VMEM((rows, width), F32)],
        compiler_params=_params("arbitrary"),
        name="ssm",
    )(u_tm, h0, sp["a_re"], sp["a_im"], sp["b_re"], sp["b_im"], sp["c_re"], sp["c_im"], sp["d"],
      sp["wg"], sp["bg"], sp["gn"])


def _block_diag(m, n_gt):
    _, r, c = m.shape
    m4 = m.reshape(n_gt, GROUPS_PER_TILE, r, 1, c)
    eye = jnp.eye(GROUPS_PER_TILE, dtype=bool)[None, :, None, :, None]
    return jnp.where(eye, m4, 0).reshape(n_gt, GROUPS_PER_TILE * r, GROUPS_PER_TILE * c)


def _ssm_params(a_re, a_im, log_dt, b_re, b_im, c_re, c_im, d, glu_w, glu_b, gain, width):
    n_gt = width // LANES
    ab_re, ab_im, bb_re, bb_im = _ssm_prep(a_re, a_im, log_dt, jnp.swapaxes(b_re, 1, 2),
                                           jnp.swapaxes(b_im, 1, 2))
    return dict(
        a_re=ab_re.reshape(n_gt, 1, STATE_TILE), a_im=ab_im.reshape(n_gt, 1, STATE_TILE),
        b_re=_block_diag(bb_re, n_gt).astype(BF16), b_im=_block_diag(bb_im, n_gt).astype(BF16),
        c_re=_block_diag(jnp.swapaxes(c_re, 1, 2), n_gt).astype(BF16),
        c_im=_block_diag(jnp.swapaxes(c_im, 1, 2), n_gt).astype(BF16),
        d=d.reshape(1, width), wg=_block_diag(glu_w, n_gt).astype(BF16), bg=glu_b.reshape(1, width),
        gn=gain.reshape(1, width))


def _state_to_tiles(s_re, s_im, n_gt):
    nb = s_re.shape[0]
    def one(s):
        return jnp.swapaxes(s.reshape(nb, n_gt, STATE_TILE), 0, 1)
    return jnp.concatenate([one(s_re), one(s_im)], axis=-1)


def _tiles_to_state(h, n_groups):
    n_gt, nb, _ = h.shape
    def one(s):
        return jnp.swapaxes(s, 0, 1).reshape(nb, n_groups, SSM_STATE)
    return one(h[:, :, :STATE_TILE]), one(h[:, :, STATE_TILE:])


def _layer_norm(z, g, b):
    mu = jnp.mean(z, axis=-1, keepdims=True)
    zc = z - mu
    var = jnp.mean(zc * zc, axis=-1, keepdims=True)
    return zc * lax.rsqrt(var + LN_EPS) * g + b


def _mix_kernel(a_ref, s_ref, x_ref, wo_ref, g_ref, b_ref, x1_ref, x1b_ref, x1t_ref=None, *, alpha, width):
    tn_dims = (((0,), (0,)), ((), ()))
    tm = x_ref.shape[0]
    n_split = 2 if tm % (2 * LANES) == 0 else 1
    for r0 in range(0, tm, tm // n_split):
        rows = slice(r0, r0 + tm // n_split)
        mix = (lax.dot_general(a_ref[:, rows], wo_ref[:width], tn_dims, preferred_element_type=F32)
               + jnp.dot(s_ref[rows, :], wo_ref[width:], preferred_element_type=F32))
        x1 = _layer_norm(alpha * x_ref[rows, :] + mix, g_ref[...], b_ref[...])
        x1_ref[rows, :] = x1
        x1b_ref[rows, :] = x1.astype(BF16)
        if x1t_ref is not None:
            x1t_ref[:, rows] = x1.T.astype(BF16)


def _mix(att_t, ssm_tm, x2d, w_out_b, g, b, alpha, width):
    t, d = x2d.shape
    nb, _, seq = att_t.shape
    tm = _tile(seq, 512)
    per_seq = seq // tm
    row = lambda i: (i, 0)
    out_shape = [jax.ShapeDtypeStruct((t, d), F32), jax.ShapeDtypeStruct((t, d), BF16)]
    out_specs = [pl.BlockSpec((tm, d), row), pl.BlockSpec((tm, d), row)]
    if tm % LANES == 0:
        out_shape.append(jax.ShapeDtypeStruct((d, t), BF16))
        out_specs.append(pl.BlockSpec((d, tm), lambda i: (0, i)))
    outs = pl.pallas_call(
        functools.partial(_mix_kernel, alpha=alpha, width=width),
        out_shape=tuple(out_shape),
        grid=(t // tm,),
        in_specs=[pl.BlockSpec((None, width, tm), lambda i: (i // per_seq, 0, i % per_seq)),
                  pl.BlockSpec((tm, width), lambda i: (i % per_seq, i // per_seq)),
                  pl.BlockSpec((tm, d), row), _full((2 * width, d)), _full((1, d)), _full((1, d))],
        out_specs=tuple(out_specs),
        compiler_params=_params("parallel"),
        name="mix",
    )(att_t, ssm_tm, x2d, w_out_b, g.reshape(1, d), b.reshape(1, d))
    x1, x1b = outs[0], outs[1]
    return x1, x1b, (outs[2] if len(outs) == 3 else x1b.T)


def _sort16(v):
    v = list(v)
    for i, j in SORT16:
        v[i], v[j] = jnp.maximum(v[i], v[j]), jnp.minimum(v[i], v[j])
    return v


def _merge_top16(a, b):
    v = [jnp.maximum(a[j], b[PEER_TOPK - 1 - j]) for j in range(PEER_TOPK)]
    for dist in (8, 4, 2, 1):
        for i in range(PEER_TOPK):
            if not i & dist:
                v[i], v[i + dist] = jnp.maximum(v[i], v[i + dist]), jnp.minimum(v[i], v[i + dist])
    return v


def _fold_sublanes(v):
    for shift in (4, 2, 1):
        v = _merge_top16(v, [pltpu.roll(x, shift, 0) for x in v])
    return v


def _count_prefix(v, pred):
    assert len(v) == PEER_TOPK == 16
    full = pred(v[15])
    b3 = pred(v[7])
    b2 = pred(jnp.where(b3, v[11], v[3]))
    b1 = pred(jnp.where(b3, jnp.where(b2, v[13], v[9]), jnp.where(b2, v[5], v[1])))
    quad = [jnp.where(b1, v[4 * k + 2], v[4 * k]) for k in range(4)]
    b0 = pred(jnp.where(b3, jnp.where(b2, quad[3], quad[2]), jnp.where(b2, quad[1], quad[0])))
    n = (jnp.where(b3, 8.0, 0.0) + jnp.where(b2, 4.0, 0.0)) + (jnp.where(b1, 2.0, 0.0) + jnp.where(b0, 1.0, 0.0))
    return jnp.where(full, 16.0, n)


def _route_kernel(x_ref, wq_ref, keys_ref, lim_ref, f_ref, r2_ref, e2_ref, *, tt, tc):
    for hh in range(ROUTE_HEADS):
        _route_head(x_ref, wq_ref, keys_ref, lim_ref, f_ref, r2_ref, e2_ref, hh, tt, tc)


def _route_head(x_ref, wq_ref, keys_ref, lim_ref, f_ref, r2_ref, e2_ref, hh, tt, tc):
    dk = 2 * PEER_DHALF
    half = tt // 2
    qp = jnp.concatenate(
        [jnp.dot(x_ref[r0:r0 + half, :], wq_ref[:, hh * dk:(hh + 1) * dk], preferred_element_type=F32)
         for r0 in (0, half)], axis=0).astype(BF16)
    nt_dims = (((1,), (1,)), ((), ()))
    s_t = [lax.dot_general(keys_ref[hh, c], qp[:, c * PEER_DHALF:(c + 1) * PEER_DHALF], nt_dims,
                           preferred_element_type=F32) for c in range(2)]
    n_blocks = N_KEYS // SUBLANES
    for l0 in range(0, tt, tc):
        cols = slice(l0, l0 + tc)
        sub = lax.broadcasted_iota(jnp.int32, (SUBLANES, tc), 0)
        blocks = [[s_t[c][SUBLANES * j:SUBLANES * (j + 1), cols] for j in range(n_blocks)] for c in range(2)]
        v1, v2 = (_fold_sublanes(_sort16(blocks[c])) for c in range(2))
        v1_lo, v1_hi = v1[0], v1[SUBLANES]
        for r in range(1, SUBLANES):
            v1_lo = jnp.where(sub == r, v1[r], v1_lo)
            v1_hi = jnp.where(sub == r, v1[SUBLANES + r], v1_hi)
        cand_lo = [v1_lo + v2[b] for b in range(PEER_TOPK)]
        cand_hi = [v1_hi + v2[b] for b in range(PEER_TOPK)]
        top_s = _fold_sublanes(_merge_top16(cand_lo, cand_hi))
        tau = top_s[PEER_TOPK - 1]
        z = jnp.ones((SUBLANES, tc), F32)
        for k in range(1, PEER_TOPK):
            z = z + jnp.exp(top_s[k] - top_s[0])
        zinv = 1.0 / z
        r2, e2 = [], []
        for j in range(n_blocks):
            s1, s2 = blocks[0][j], blocks[1][j]
            cnt = _count_prefix(v2, lambda p, s1=s1: s1 + p >= tau)
            rank = _count_prefix(v2, lambda p, s2=s2: p > s2)
            lim_j = jnp.where(s1 >= v1[PEER_TOPK - 1], cnt, 0.0)
            f_j = jnp.exp(s1 - v1[0]) * zinv
            for g in range(SUBLANES // KEYS_PER_BLOCK):
                rows = slice(g * KEYS_PER_BLOCK, (g + 1) * KEYS_PER_BLOCK)
                lim_ref[hh, j * (SUBLANES // KEYS_PER_BLOCK) + g, :, cols] = lim_j[rows]
                f_ref[hh, j * (SUBLANES // KEYS_PER_BLOCK) + g, :, cols] = f_j[rows]
            r2.append(rank)
            e2.append(jnp.exp(s2 - v2[0]))
        r2_ref[hh, :, cols] = jnp.concatenate(r2, axis=0).astype(BF16)
        e2_ref[hh, :, cols] = jnp.concatenate(e2, axis=0).astype(BF16)


def _route(x1b, wq_b, keys_b, tt):
    t, d = x1b.shape
    dk = 2 * PEER_DHALF
    tc = _tile(tt, 256)
    hps = ROUTE_HEADS
    big_spec = pl.BlockSpec((hps, N_KEYS, tt), lambda i, h: (h, 0, i))
    n_eb = N_KEYS // KEYS_PER_BLOCK
    row_shape = jax.ShapeDtypeStruct((PEER_HEADS, n_eb, KEYS_PER_BLOCK, t), F32)
    row_spec = pl.BlockSpec((hps, n_eb, KEYS_PER_BLOCK, tt), lambda i, h: (h, 0, 0, i))
    return pl.pallas_call(
        functools.partial(_route_kernel, tt=tt, tc=tc),
        out_shape=(row_shape, row_shape,
                   jax.ShapeDtypeStruct((PEER_HEADS, N_KEYS, t), BF16),
                   jax.ShapeDtypeStruct((PEER_HEADS, N_KEYS, t), BF16)),
        grid=(t // tt, PEER_HEADS // hps),
        in_specs=[pl.BlockSpec((tt, d), lambda i, h: (i, 0)),
                  pl.BlockSpec((d, hps * dk), lambda i, h: (0, h)),
                  pl.BlockSpec((hps, 2, N_KEYS, PEER_DHALF), lambda i, h: (h, 0, 0, 0))],
        out_specs=(row_spec, row_spec, big_spec, big_spec),
        compiler_params=_params("parallel", "parallel"),
        name="route",
    )(x1b, wq_b, keys_b)


def _peer_kernel(xt_ref, u_ref, vt_ref, lim_ref, f_ref, r2_ref, e2_ref, y_ref, a_ref, hid_ref, bc_ref, w_ref,
                 *, n_blocks, tt):
    s = pl.program_id(0)
    slot = lax.rem(s, 2)
    rc = 2 * SUBLANES

    @pl.when(s == 0)
    def _():
        hid_ref[1] = jnp.zeros(hid_ref.shape[1:], BF16)

    @pl.when(lax.rem(jnp.maximum(s - 1, 0), n_blocks) == 0)
    def _():
        y_ref[...] = jnp.zeros_like(y_ref)

    a_ref[...] = jnp.dot(u_ref[...], xt_ref[...], preferred_element_type=F32)
    y_ref[...] += jnp.dot(vt_ref[...], hid_ref[1 - slot], preferred_element_type=F32)
    zero = jnp.zeros((rc, tt), BF16)
    for q in range(KEYS_PER_BLOCK):
        for h in range(PEER_HEADS):
            bc_ref[0, q, h] = jnp.broadcast_to(lim_ref[h, q:q + 1, :], (rc, tt)).astype(BF16)
            bc_ref[1, q, h] = jnp.broadcast_to(f_ref[h, q:q + 1, :], (rc, tt)).astype(BF16)
    never = r2_ref[0, 0:rc, :] < -1.0
    dep = zero
    for q in range(KEYS_PER_BLOCK):
        for k in range(N_KEYS // rc):
            rows = slice(k * rc, (k + 1) * rc)
            w = dep
            for h in range(PEER_HEADS):
                lim_t = bc_ref[0, q, h]
                if h % CHAIN_EVERY == 0:
                    lim_t = lim_t + dep
                w = w + jnp.where(r2_ref[h, rows, :] < lim_t, e2_ref[h, rows, :], zero) * bc_ref[1, q, h]
                if h % CHAIN_EVERY == CHAIN_EVERY - 1:
                    dep = jnp.where(never, w, zero)
            w_ref[q * N_KEYS + k * rc:q * N_KEYS + (k + 1) * rc, :] = w
    hid_ref[slot] = _gelu(a_ref[...]).astype(BF16) * w_ref[...]


def _peer(x1t, u_b, vt_b, lim, f, r2, e2, tt):
    d, t = x1t.shape
    eb = EXPERT_BLOCK
    n_blocks = u_b.shape[0] // eb
    n_tiles = t // tt
    tile = lambda s: jnp.minimum(s // n_blocks, n_tiles - 1)
    prev = lambda s: jnp.maximum(s - 1, 0)
    row_spec = pl.BlockSpec((PEER_HEADS, None, KEYS_PER_BLOCK, tt), lambda s: (0, s % n_blocks, 0, tile(s)))
    big_spec = pl.BlockSpec((PEER_HEADS, N_KEYS, tt), lambda s: (0, 0, tile(s)))
    return pl.pallas_call(
        functools.partial(_peer_kernel, n_blocks=n_blocks, tt=tt),
        out_shape=jax.ShapeDtypeStruct((d, t), F32),
        grid=(n_tiles * n_blocks + 1,),
        in_specs=[pl.BlockSpec((d, tt), lambda s: (0, tile(s))),
                  pl.BlockSpec((eb, d), lambda s: (s % n_blocks, 0)),
                  pl.BlockSpec((None, d, eb), lambda s: (prev(s) % n_blocks, 0, 0)),
                  row_spec, row_spec, big_spec, big_spec],
        out_specs=pl.BlockSpec((d, tt), lambda s: (0, prev(s) // n_blocks)),
        scratch_shapes=[pltpu.VMEM((eb, tt), F32), pltpu.VMEM((2, eb, tt), BF16),
                        pltpu.VMEM((2, KEYS_PER_BLOCK, PEER_HEADS, 2 * SUBLANES, tt), BF16),
                        pltpu.VMEM((eb, tt), BF16)],
        compiler_params=_params("arbitrary"),
        name="peer",
    )(x1t, u_b, vt_b, lim, f, r2, e2)


def _ln2_kernel(x1_ref, yt_ref, g_ref, b_ref, o_ref, *, alpha):
    o_ref[...] = _layer_norm(alpha * x1_ref[...] + yt_ref[...].T, g_ref[...], b_ref[...])


def _ln2(x1, y_t, g, b, alpha):
    t, d = x1.shape
    tm = _tile(t, 512)
    row = lambda i: (i, 0)
    return pl.pallas_call(
        functools.partial(_ln2_kernel, alpha=alpha),
        out_shape=jax.ShapeDtypeStruct((t, d), F32),
        grid=(t // tm,),
        in_specs=[pl.BlockSpec((tm, d), row), pl.BlockSpec((d, tm), lambda i: (0, i)), _full((1, d)),
                  _full((1, d))],
        out_specs=pl.BlockSpec((tm, d), row),
        compiler_params=_params("parallel"),
        name="ln2",
    )(x1, y_t, g.reshape(1, d), b.reshape(1, d))


def _rel_bias_table(rel_bias):
    n_left = ATT_WINDOW - REL_CLIP + CHUNK - 1
    n_right = CHUNK - 1 - REL_CLIP
    ext = jnp.concatenate([jnp.repeat(rel_bias[:, :1], n_left, axis=1), rel_bias,
                           jnp.repeat(rel_bias[:, -1:], max(n_right, 0), axis=1)], axis=1)
    return jnp.stack([ext[:, CHUNK - 1 - i:CHUNK - 1 - i + BAND] for i in range(CHUNK)], axis=1)


def _step_bias(table, cpb):
    n_heads = table.shape[0]
    table = (table - table[:, :1, :1]) * LOG2E
    tab_t = jnp.swapaxes(table, 1, 2)
    per_chunk = [jnp.pad(tab_t, ((0, 0), (c * CHUNK, (cpb - 1 - c) * CHUNK), (0, 0)),
                         constant_values=MASK_VALUE) for c in range(cpb)]
    both = jnp.stack(per_chunk, axis=2)
    rows = both.shape[1]
    both = both.reshape(n_heads // 2, 2, rows, cpb * CHUNK)
    return jnp.swapaxes(both, 1, 2).reshape(n_heads // 2, rows, 2 * cpb * CHUNK)


def _encoder_layer(x, hist_k, hist_v, h0, wts, alpha):
    nb, seq, d = x.shape
    width = d // 2
    t = nb * seq
    n_gt = width // LANES
    assert seq % CHUNK == 0 and width % LANES == 0 and nb % SUBLANES == 0
    assert hist_k is None or seq == CHUNK
    x2d = x.reshape(t, d)

    n_hist = 0 if hist_k is None else hist_k.shape[1]
    n_invalid = ATT_WINDOW - n_hist
    def history(hist, n_new):
        if hist is None:
            return jnp.zeros((nb, ATT_WINDOW + n_new, width), BF16)
        return jnp.pad(hist.reshape(nb, n_hist, width).astype(BF16), ((0, 0), (n_invalid, n_new), (0, 0)))
    if _tile(seq, 256) % LANES == 0:
        q_b, kpad, vtpad, k_new, v_new, u_tm = _proj(
            x2d, wts["w_in"], history(hist_k, seq), jnp.swapaxes(history(hist_v, seq), 1, 2), nb, seq, width)
    else:
        q_b, k_b, v_b, k_new, v_new, u_tm = _proj_short(x2d, wts["w_in"], nb, seq, width)
        kpad = jnp.concatenate([history(hist_k, 0), k_b.reshape(nb, seq, width)], axis=1)
        vtpad = jnp.swapaxes(jnp.concatenate([history(hist_v, 0), v_b.reshape(nb, seq, width)], axis=1), 1, 2)
    n_chunks = seq // CHUNK
    cpb = 2 if n_chunks % 2 == 0 else 1
    assert cpb == 2 or n_chunks == 1
    att_t = _attention(q_b, kpad, vtpad, _step_bias(wts["bias"], cpb), wts["norm_attn_g"], nb, seq, width,
                       cpb, n_invalid)

    ssm_tm, h_last = _ssm(u_tm.reshape(t, width), h0, wts["ssm"], nb, seq, width)

    x1, x1b, x1t = _mix(att_t, ssm_tm.reshape(seq, nb * width), x2d, wts["w_out"], wts["ln1_g"],
                        wts["ln1_b"], alpha, width)

    tt = _tile(t, PEER_TOKENS)
    lim, f, r2, e2 = _route(x1b, wts["peer_wq"], wts["peer_keys"], tt)
    y_t = _peer(x1t, wts["peer_u"], wts["peer_vt"], lim, f, r2, e2, tt)
    x2 = _ln2(x1, y_t, wts["ln2_g"], wts["ln2_b"], alpha)

    n_heads = width // HEAD_DIM
    return (x2.reshape(nb, seq, d), k_new.reshape(nb, -1, n_heads, HEAD_DIM),
            v_new.reshape(nb, -1, n_heads, HEAD_DIM), h_last)


def kernel(x_prompt, x_sample, cache_attn_k, cache_attn_v, state_ssm_re, state_ssm_im, w_in, rel_bias, norm_attn_g, ssm_a_re, ssm_a_im, ssm_log_dt, ssm_b_re, ssm_b_im, ssm_c_re, ssm_c_im, ssm_d, ssm_glu_w, ssm_glu_b, norm_ssm_g, w_out, ln1_g, ln1_b, peer_wq, peer_keys, peer_u, peer_v, ln2_g, ln2_b):
    depth = w_in.shape[0]
    alpha = (2.0 * depth) ** 0.25
    d = x_prompt.shape[-1]
    width = d // 2
    n_gt = width // LANES
    n_groups = width // SSM_GROUP
    keep = min(ATT_WINDOW, x_prompt.shape[1])

    yp, ys = x_prompt, x_sample
    outs = [[] for _ in range(8)]
    for l in range(depth):
        wts = dict(
            w_in=w_in[l].astype(BF16),
            bias=_rel_bias_table(rel_bias[l]),
            norm_attn_g=jnp.broadcast_to(norm_attn_g[l][:, None], (width, LANES)),
            ssm=_ssm_params(ssm_a_re[l], ssm_a_im[l], ssm_log_dt[l], ssm_b_re[l], ssm_b_im[l], ssm_c_re[l],
                            ssm_c_im[l], ssm_d[l], ssm_glu_w[l], ssm_glu_b[l], norm_ssm_g[l], width),
            w_out=w_out[l].astype(BF16), ln1_g=ln1_g[l], ln1_b=ln1_b[l],
            peer_wq=peer_wq[l].astype(BF16), peer_keys=peer_keys[l].astype(BF16),
            peer_u=peer_u[l].astype(BF16),
            peer_vt=jnp.swapaxes(peer_v[l].astype(BF16).reshape(-1, EXPERT_BLOCK, d), 1, 2),
            ln2_g=ln2_g[l], ln2_b=ln2_b[l])
        h0p = jnp.zeros((n_gt, x_prompt.shape[0], 2 * STATE_TILE), F32)
        yp, kp, vp, hp = _encoder_layer(yp, None, None, h0p, wts, alpha)
        h0s = _state_to_tiles(state_ssm_re[l].astype(F32), state_ssm_im[l].astype(F32), n_gt)
        ys, kn, vn, hs = _encoder_layer(ys, cache_attn_k[l], cache_attn_v[l], h0s, wts, alpha)
        hp_re, hp_im = _tiles_to_state(hp, n_groups)
        hs_re, hs_im = _tiles_to_state(hs, n_groups)
        assert kp.shape[1] == keep
        for acc, val in zip(outs, (kp, vp, hp_re, hp_im, kn, vn, hs_re, hs_im)):
            acc.append(val)
    return (yp, ys) + tuple(jnp.stack(o) for o in outs)
```

```python
import functools

import jax
import jax.numpy as jnp
from jax import lax
from jax.experimental import pallas as pl
from jax.experimental.pallas import tpu as pltpu

F32 = jnp.float32
BF16 = jnp.bfloat16

CHUNK = 64
LEFT_CHUNKS = 8
ATT_WINDOW = LEFT_CHUNKS * CHUNK
BAND = ATT_WINDOW + CHUNK
HEAD_DIM = 64
REL_CLIP = 2 * CHUNK
SSM_GROUP = 16
SSM_STATE = 64
PEER_HEADS = 8
PEER_TOPK = 16
N_KEYS = 128
PEER_DHALF = 128
LN_EPS = 1e-5
RMS_EPS = 1e-6
MASK_VALUE = -1e30
LOG2E = 1.4426950408889634
Q_SCALE = HEAD_DIM ** -0.5 * LOG2E

LANES = 128
SUBLANES = 8
GROUPS_PER_TILE = LANES // SSM_GROUP
STATE_TILE = GROUPS_PER_TILE * SSM_STATE
VMEM_LIMIT = 56 * 1024 * 1024
EXPERT_BLOCK = 1024
KEYS_PER_BLOCK = EXPERT_BLOCK // N_KEYS
PEER_TOKENS = 512
ATTN_PAIR_GROUP = 8
CHAIN_EVERY = 2

SORT16 = (
    (0, 13), (1, 12), (2, 15), (3, 14), (4, 8), (5, 6), (7, 11), (9, 10),
    (0, 5), (1, 7), (2, 9), (3, 4), (6, 13), (8, 14), (10, 15), (11, 12),
    (0, 1), (2, 3), (4, 5), (6, 8), (7, 9), (10, 11), (12, 13), (14, 15),
    (0, 2), (1, 3), (4, 10), (5, 11), (6, 7), (8, 9), (12, 14), (13, 15),
    (1, 2), (3, 12), (4, 6), (5, 7), (8, 10), (9, 11), (13, 14),
    (1, 4), (2, 6), (5, 8), (7, 10), (9, 13), (11, 14),
    (2, 4), (3, 6), (9, 12), (11, 13),
    (3, 5), (6, 8), (7, 9), (10, 12),
    (3, 4), (5, 6), (7, 8), (9, 10), (11, 12),
    (6, 7), (8, 9),
)


def _params(*semantics):
    return pltpu.CompilerParams(dimension_semantics=semantics, vmem_limit_bytes=VMEM_LIMIT)


def _tile(n, pref):
    t = min(n, pref)
    while n % t:
        t //= 2
    return t


def _gelu(x):
    return 0.5 * x * (1.0 + lax.erf(x * 0.7071067811865476))


def _full(shape):
    return pl.BlockSpec(shape, lambda *_: (0,) * len(shape))


def _proj_kernel(x_ref, w_ref, kbase_ref, vtbase_ref, q_ref, kpad_ref, vtpad_ref, kf_ref, vf_ref, u_ref,
                 *, width):
    del kbase_ref, vtbase_ref
    xb = x_ref[...].astype(BF16)
    acc = [jnp.dot(xb, w_ref[:, c * width:(c + 1) * width], preferred_element_type=F32) for c in range(4)]
    q_ref[...] = (acc[0] * Q_SCALE).astype(BF16)
    kf_ref[...] = acc[1]
    kpad_ref[...] = acc[1].astype(BF16)
    vf_ref[...] = acc[2]
    vtpad_ref[...] = acc[2].T.astype(BF16)
    u_ref[...] = acc[3]


def _proj(x2d, w_in_b, kbase, vtbase, nb, seq, width):
    t, d = x2d.shape
    tm = _tile(seq, 256)
    assert tm % LANES == 0 and ATT_WINDOW % tm == 0
    per_seq = seq // tm
    skip = ATT_WINDOW // tm
    keep = min(ATT_WINDOW, seq)
    dropped = per_seq - keep // tm
    row = lambda i: (i, 0)
    tail = lambda i: (i // per_seq, jnp.maximum(i % per_seq - dropped, 0), 0)
    tail_f32 = jax.ShapeDtypeStruct((nb, keep, width), F32)
    return pl.pallas_call(
        functools.partial(_proj_kernel, width=width),
        out_shape=(jax.ShapeDtypeStruct((t, width), BF16),
                   jax.ShapeDtypeStruct(kbase.shape, BF16), jax.ShapeDtypeStruct(vtbase.shape, BF16),
                   tail_f32, tail_f32, jax.ShapeDtypeStruct((seq, nb * width), F32)),
        grid=(t // tm,),
        in_specs=[pl.BlockSpec((tm, d), row),
                  pl.BlockSpec((d, 4 * width), lambda i: (0, 0)),
                  pl.BlockSpec(memory_space=pl.ANY), pl.BlockSpec(memory_space=pl.ANY)],
        out_specs=(pl.BlockSpec((tm, width), row),
                   pl.BlockSpec((None, tm, width), lambda i: (i // per_seq, skip + i % per_seq, 0)),
                   pl.BlockSpec((None, width, tm), lambda i: (i // per_seq, 0, skip + i % per_seq)),
                   pl.BlockSpec((None, tm, width), tail), pl.BlockSpec((None, tm, width), tail),
                   pl.BlockSpec((tm, width), lambda i: (i % per_seq, i // per_seq))),
        input_output_aliases={2: 1, 3: 2},
        compiler_params=_params("arbitrary"),
        name="proj",
    )(x2d, w_in_b, kbase, vtbase)


def _proj_short_kernel(x_ref, w_ref, q_ref, k_ref, v_ref, kf_ref, vf_ref, u_ref, *, width):
    xb = x_ref[...].astype(BF16)
    acc = [jnp.dot(xb, w_ref[:, c * width:(c + 1) * width], preferred_element_type=F32) for c in range(4)]
    q_ref[...] = (acc[0] * Q_SCALE).astype(BF16)
    kf_ref[...] = acc[1]
    k_ref[...] = acc[1].astype(BF16)
    vf_ref[...] = acc[2]
    v_ref[...] = acc[2].astype(BF16)
    u_ref[...] = acc[3]


def _proj_short(x2d, w_in_b, nb, seq, width):
    t, d = x2d.shape
    tm = _tile(seq, 256)
    per_seq = seq // tm
    row = lambda i: (i, 0)
    rows_f32 = jax.ShapeDtypeStruct((t, width), F32)
    rows_b16 = jax.ShapeDtypeStruct((t, width), BF16)
    return pl.pallas_call(
        functools.partial(_proj_short_kernel, width=width),
        out_shape=(rows_b16, rows_b16, rows_b16, rows_f32, rows_f32,
                   jax.ShapeDtypeStruct((seq, nb * width), F32)),
        grid=(t // tm,),
        in_specs=[pl.BlockSpec((tm, d), row),
                  pl.BlockSpec((d, 4 * width), lambda i: (0, 0))],
        out_specs=tuple([pl.BlockSpec((tm, width), row)] * 5
                        + [pl.BlockSpec((tm, width), lambda i: (i % per_seq, i // per_seq))]),
        compiler_params=_params("parallel"),
        name="proj",
    )(x2d, w_in_b)


def _attn_kernel(q_ref, k_ref, vt_ref, bias_ref, g_ref, o_ref, *, n_steps, cpb, n_invalid, width):
    nq = cpb * CHUNK
    nc = 2 * nq
    kb = ATT_WINDOW + nq
    lo = lax.broadcasted_iota(jnp.int32, (nq, LANES), 1) < HEAD_DIM
    nt_dims = (((1,), (1,)), ((), ()))
    group = ATTN_PAIR_GROUP
    k_split = -(-(kb // 2) // LANES) * LANES
    halves = ((0, k_split), (k_split, kb - k_split))
    z0, z1 = (cpb - 1) * CHUNK, ATT_WINDOW - REL_CLIP

    def pair_scores(r0, hp):
        cols = slice(hp * LANES, (hp + 1) * LANES)
        q2 = q_ref[pl.ds(r0, nq), cols]
        qcat = jnp.concatenate([jnp.where(lo, q2, 0), jnp.where(lo, 0, q2)], axis=0)
        return jnp.concatenate(
            [lax.dot_general(k_ref[pl.ds(r0 + k0, kn), cols], qcat, nt_dims, preferred_element_type=F32)
             for k0, kn in halves], axis=0)

    def step(r0, masked):
        tiles = []
        ssq = jnp.zeros((1, nq), F32)
        n_pairs = width // LANES
        for g0 in range(0, n_pairs, group):
            pairs = range(g0, min(g0 + group, n_pairs))
            s = jnp.concatenate([pair_scores(r0, hp) for hp in pairs], axis=1)
            parts = [s[z0:z1], s[z1:] + jnp.concatenate([bias_ref[hp, z1:, :] for hp in pairs], axis=1)]
            if z0:
                parts.insert(0, s[:z0] + jnp.concatenate([bias_ref[hp, :z0, :] for hp in pairs], axis=1))
            s = jnp.concatenate(parts, axis=0)
            if masked:
                s = jnp.where(lax.broadcasted_iota(jnp.int32, s.shape, 0) + r0 >= n_invalid, s, MASK_VALUE)
            m = jnp.max(s, axis=0, keepdims=True)
            e = jnp.exp2(s - m)
            rl = 1.0 / jnp.sum(e, axis=0, keepdims=True)
            p = e.astype(BF16)
            for i, hp in enumerate(pairs):
                cols = slice(hp * LANES, (hp + 1) * LANES)
                pc = slice(i * nc, (i + 1) * nc)
                o_t = sum(jnp.dot(vt_ref[cols, pl.ds(r0 + k0, kn)], p[k0:k0 + kn, pc],
                                  preferred_element_type=F32) for k0, kn in halves) * rl[:, pc]
                tile = jnp.concatenate([o_t[:HEAD_DIM, :nq], o_t[HEAD_DIM:, nq:]], axis=0)
                ssq = ssq + jnp.sum(tile * tile, axis=0, keepdims=True)
                tiles.append(tile)
        rinv = lax.rsqrt(ssq / width + RMS_EPS)
        for hp, tile in enumerate(tiles):
            rows = slice(hp * LANES, (hp + 1) * LANES)
            o_ref[rows, pl.ds(r0, nq)] = (tile * rinv * g_ref[rows, :nq]).astype(BF16)

    n_masked = min(n_steps, -(-n_invalid // nq))
    if n_steps == 1:
        step(0, n_masked > 0)
    else:
        def body(i, carry, masked):
            step(pl.multiple_of(i * nq, nq), masked)
            return carry
        lax.fori_loop(0, n_masked, functools.partial(body, masked=True), 0)
        lax.fori_loop(n_masked, n_steps, functools.partial(body, masked=False), 0)


def _attention(qkv, kpad, vtpad, bias, gain_b, nb, seq, width, cpb, n_invalid):
    lp = kpad.shape[1]
    nq = cpb * CHUNK
    return pl.pallas_call(
        functools.partial(_attn_kernel, n_steps=seq // nq, cpb=cpb, n_invalid=n_invalid, width=width),
        out_shape=jax.ShapeDtypeStruct((nb, width, seq), BF16),
        grid=(nb,),
        in_specs=[pl.BlockSpec((seq, width), lambda b: (b, 0)),
                  pl.BlockSpec((None, lp, width), lambda b: (b, 0, 0)),
                  pl.BlockSpec((None, width, lp), lambda b: (b, 0, 0)),
                  _full(bias.shape),
                  _full((width, LANES))],
        out_specs=pl.BlockSpec((None, width, seq), lambda b: (b, 0, 0)),
        compiler_params=_params("parallel"),
        name="attn",
    )(qkv, kpad, vtpad, bias, gain_b)


def _ssm_prep_kernel(are_ref, aim_ref, ldt_ref, bre_ref, bim_ref, abr_ref, abi_ref, bbr_ref, bbi_ref):
    a_re = are_ref[...]
    a_im = aim_ref[...]
    dt = jnp.exp(ldt_ref[...])
    mag = jnp.exp(a_re * dt)
    ang = a_im * dt
    ab_re = mag * jnp.cos(ang)
    ab_im = mag * jnp.sin(ang)
    n_re = ab_re - 1.0
    den = a_re * a_re + a_im * a_im
    c_re = (n_re * a_re + ab_im * a_im) / den
    c_im = (ab_im * a_re - n_re * a_im) / den
    abr_ref[...] = ab_re
    abi_ref[...] = ab_im
    b_re = bre_ref[...]
    b_im = bim_ref[...]
    bbr_ref[...] = c_re[:, None, :] * b_re - c_im[:, None, :] * b_im
    bbi_ref[...] = c_re[:, None, :] * b_im + c_im[:, None, :] * b_re


def _ssm_prep(a_re, a_im, log_dt, bt_re, bt_im):
    g, p = a_re.shape
    h = bt_re.shape[1]
    return pl.pallas_call(
        _ssm_prep_kernel,
        out_shape=(jax.ShapeDtypeStruct((g, p), F32), jax.ShapeDtypeStruct((g, p), F32),
                   jax.ShapeDtypeStruct((g, h, p), F32), jax.ShapeDtypeStruct((g, h, p), F32)),
        name="ssm_prep",
    )(a_re, a_im, log_dt.reshape(g, 1), bt_re, bt_im)


def _ssm_kernel(u_ref, h0_ref, are_ref, aim_ref, bre_ref, bim_ref, cre_ref, cim_ref, d_ref, wg_ref, bg_ref,
                gn_ref, o_ref, hlast_ref, hst_ref, bu_ref, y_ref, *, nb, tl, n_gt, width):
    sw = STATE_TILE

    @pl.when(pl.program_id(0) == 0)
    def _():
        hst_ref[...] = h0_ref[...]

    for gt in range(n_gt):
        cols = slice(gt * LANES, (gt + 1) * LANES)
        uf = u_ref[:, cols]
        ub = uf.astype(BF16)
        bu_ref[:, :sw] = jnp.dot(ub, bre_ref[gt], preferred_element_type=F32)
        bu_ref[:, sw:] = jnp.dot(ub, bim_ref[gt], preferred_element_type=F32)
        a_re = jnp.broadcast_to(are_ref[gt], (nb, sw))
        a_im = jnp.broadcast_to(aim_ref[gt], (nb, sw))

        def step(t, carry, a_re=a_re, a_im=a_im):
            h_re, h_im = carry
            r0 = pl.multiple_of(t * nb, nb)
            n_re = a_re * h_re - a_im * h_im + bu_ref[pl.ds(r0, nb), :sw]
            n_im = a_re * h_im + a_im * h_re + bu_ref[pl.ds(r0, nb), sw:]
            bu_ref[pl.ds(r0, nb), :sw] = n_re
            bu_ref[pl.ds(r0, nb), sw:] = n_im
            return n_re, n_im

        h_re, h_im = lax.fori_loop(0, tl, step, (hst_ref[gt, :, :sw], hst_ref[gt, :, sw:]), unroll=True)
        hst_ref[gt, :, :sw] = h_re
        hst_ref[gt, :, sw:] = h_im

        y = (jnp.dot(bu_ref[:, :sw].astype(BF16), cre_ref[gt], preferred_element_type=F32)
             - jnp.dot(bu_ref[:, sw:].astype(BF16), cim_ref[gt], preferred_element_type=F32)
             + d_ref[:, cols] * uf)
        y = _gelu(y)
        z = jnp.dot(y.astype(BF16), wg_ref[gt], preferred_element_type=F32) + bg_ref[:, cols]
        y_ref[:, cols] = y * jax.nn.sigmoid(z)

    yy = y_ref[...]
    ms = jnp.sum(yy * yy, axis=-1, keepdims=True) / width
    o_ref[...] = (yy * lax.rsqrt(ms + RMS_EPS) * gn_ref[...]).astype(BF16)
    hlast_ref[...] = hst_ref[...]


def _ssm(u_tm, h0, sp, nb, seq, width):
    n_gt = width // LANES
    tl = _tile(seq, 64)
    rows = tl * nb
    sw = STATE_TILE
    return pl.pallas_call(
        functools.partial(_ssm_kernel, nb=nb, tl=tl, n_gt=n_gt, width=width),
        out_shape=(jax.ShapeDtypeStruct((seq * nb, width), BF16),
                   jax.ShapeDtypeStruct((n_gt, nb, 2 * sw), F32)),
        grid=(seq // tl,),
        in_specs=[pl.BlockSpec((rows, width), lambda s: (s, 0)),
                  _full((n_gt, nb, 2 * sw)),
                  _full((n_gt, 1, sw)), _full((n_gt, 1, sw)),
                  _full((n_gt, LANES, sw)), _full((n_gt, LANES, sw)),
                  _full((n_gt, sw, LANES)), _full((n_gt, sw, LANES)),
                  _full((1, width)),
                  _full((n_gt, LANES, LANES)),
                  _full((1, width)), _full((1, width))],
        out_specs=(pl.BlockSpec((rows, width), lambda s: (s, 0)),
                   _full((n_gt, nb, 2 * sw))),
        scratch_shapes=[pltpu.VMEM((n_gt, nb, 2 * sw), F32),
                        pltpu.VMEM((rows, 2 * sw), F32),
                        pltpu.VMEM((rows, width), F32)],
        compiler_params=_params("arbitrary"),
        name="ssm",
    )(u_tm, h0, sp["a_re"], sp["a_im"], sp["b_re"], sp["b_im"], sp["c_re"], sp["c_im"], sp["d"],
      sp["wg"], sp["bg"], sp["gn"])


def _block_diag(m, n_gt):
    _, r, c = m.shape
    m4 = m.reshape(n_gt, GROUPS_PER_TILE, r, 1, c)
    eye = jnp.eye(GROUPS_PER_TILE, dtype=bool)[None, :, None, :, None]
    return jnp.where(eye, m4, 0).reshape(n_gt, GROUPS_PER_TILE * r, GROUPS_PER_TILE * c)


def _ssm_params(a_re, a_im, log_dt, b_re, b_im, c_re, c_im, d, glu_w, glu_b, gain, width):
    n_gt = width // LANES
    ab_re, ab_im, bb_re, bb_im = _ssm_prep(a_re, a_im, log_dt, jnp.swapaxes(b_re, 1, 2),
                                           jnp.swapaxes(b_im, 1, 2))
    return dict(
        a_re=ab_re.reshape(n_gt, 1, STATE_TILE), a_im=ab_im.reshape(n_gt, 1, STATE_TILE),
        b_re=_block_diag(bb_re, n_gt).astype(BF16), b_im=_block_diag(bb_im, n_gt).astype(BF16),
        c_re=_block_diag(jnp.swapaxes(c_re, 1, 2), n_gt).astype(BF16),
        c_im=_block_diag(jnp.swapaxes(c_im, 1, 2), n_gt).astype(BF16),
        d=d.reshape(1, width), wg=_block_diag(glu_w, n_gt).astype(BF16), bg=glu_b.reshape(1, width),
        gn=gain.reshape(1, width))


def _state_to_tiles(s_re, s_im, n_gt):
    nb = s_re.shape[0]
    def one(s):
        return jnp.swapaxes(s.reshape(nb, n_gt, STATE_TILE), 0, 1)
    return jnp.concatenate([one(s_re), one(s_im)], axis=-1)


def _tiles_to_state(h, n_groups):
    n_gt, nb, _ = h.shape
    def one(s):
        return jnp.swapaxes(s, 0, 1).reshape(nb, n_groups, SSM_STATE)
    return one(h[:, :, :STATE_TILE]), one(h[:, :, STATE_TILE:])


def _layer_norm(z, g, b):
    mu = jnp.mean(z, axis=-1, keepdims=True)
    zc = z - mu
    var = jnp.mean(zc * zc, axis=-1, keepdims=True)
    return zc * lax.rsqrt(var + LN_EPS) * g + b


def _mix_kernel(a_ref, s_ref, x_ref, wo_ref, g_ref, b_ref, x1_ref, x1b_ref, x1t_ref=None, *, alpha, width):
    tn_dims = (((0,), (0,)), ((), ()))
    tm = x_ref.shape[0]
    n_split = 2 if tm % (2 * LANES) == 0 else 1
    for r0 in range(0, tm, tm // n_split):
        rows = slice(r0, r0 + tm // n_split)
        mix = (lax.dot_general(a_ref[:, rows], wo_ref[:width], tn_dims, preferred_element_type=F32)
               + jnp.dot(s_ref[rows, :], wo_ref[width:], preferred_element_type=F32))
        x1 = _layer_norm(alpha * x_ref[rows, :] + mix, g_ref[...], b_ref[...])
        x1_ref[rows, :] = x1
        x1b_ref[rows, :] = x1.astype(BF16)
        if x1t_ref is not None:
            x1t_ref[:, rows] = x1.T.astype(BF16)


def _mix(att_t, ssm_tm, x2d, w_out_b, g, b, alpha, width):
    t, d = x2d.shape
    nb, _, seq = att_t.shape
    tm = _tile(seq, 512)
    per_seq = seq // tm
    row = lambda i: (i, 0)
    out_shape = [jax.ShapeDtypeStruct((t, d), F32), jax.ShapeDtypeStruct((t, d), BF16)]
    out_specs = [pl.BlockSpec((tm, d), row), pl.BlockSpec((tm, d), row)]
    if tm % LANES == 0:
        out_shape.append(jax.ShapeDtypeStruct((d, t), BF16))
        out_specs.append(pl.BlockSpec((d, tm), lambda i: (0, i)))
    outs = pl.pallas_call(
        functools.partial(_mix_kernel, alpha=alpha, width=width),
        out_shape=tuple(out_shape),
        grid=(t // tm,),
        in_specs=[pl.BlockSpec((None, width, tm), lambda i: (i // per_seq, 0, i % per_seq)),
                  pl.BlockSpec((tm, width), lambda i: (i % per_seq, i // per_seq)),
                  pl.BlockSpec((tm, d), row), _full((2 * width, d)), _full((1, d)), _full((1, d))],
        out_specs=tuple(out_specs),
        compiler_params=_params("parallel"),
        name="mix",
    )(att_t, ssm_tm, x2d, w_out_b, g.reshape(1, d), b.reshape(1, d))
    x1, x1b = outs[0], outs[1]
    return x1, x1b, (outs[2] if len(outs) == 3 else x1b.T)


def _sort16(v):
    v = list(v)
    for i, j in SORT16:
        v[i], v[j] = jnp.maximum(v[i], v[j]), jnp.minimum(v[i], v[j])
    return v


def _merge_top16(a, b):
    v = [jnp.maximum(a[j], b[PEER_TOPK - 1 - j]) for j in range(PEER_TOPK)]
    for dist in (8, 4, 2, 1):
        for i in range(PEER_TOPK):
            if not i & dist:
                v[i], v[i + dist] = jnp.maximum(v[i], v[i + dist]), jnp.minimum(v[i], v[i + dist])
    return v


def _fold_sublanes(v):
    for shift in (4, 2, 1):
        v = _merge_top16(v, [pltpu.roll(x, shift, 0) for x in v])
    return v


def _count_prefix(v, pred):
    assert len(v) == PEER_TOPK == 16
    full = pred(v[15])
    b3 = pred(v[7])
    b2 = pred(jnp.where(b3, v[11], v[3]))
    b1 = pred(jnp.where(b3, jnp.where(b2, v[13], v[9]), jnp.where(b2, v[5], v[1])))
    quad = [jnp.where(b1, v[4 * k + 2], v[4 * k]) for k in range(4)]
    b0 = pred(jnp.where(b3, jnp.where(b2, quad[3], quad[2]), jnp.where(b2, quad[1], quad[0])))
    n = (jnp.where(b3, 8.0, 0.0) + jnp.where(b2, 4.0, 0.0)) + (jnp.where(b1, 2.0, 0.0) + jnp.where(b0, 1.0, 0.0))
    return jnp.where(full, 16.0, n)


def _route_kernel(x_ref, wq_ref, keys_ref, lim_ref, f_ref, r2_ref, e2_ref, *, tt, tc):
    half = tt // 2
    qp = jnp.concatenate(
        [jnp.dot(x_ref[r0:r0 + half, :], wq_ref[...], preferred_element_type=F32) for r0 in (0, half)],
        axis=0).astype(BF16)
    hh = 0
    nt_dims = (((1,), (1,)), ((), ()))
    s_t = [lax.dot_general(keys_ref[hh, c], qp[:, c * PEER_DHALF:(c + 1) * PEER_DHALF], nt_dims,
                           preferred_element_type=F32) for c in range(2)]
    n_blocks = N_KEYS // SUBLANES
    for l0 in range(0, tt, tc):
        cols = slice(l0, l0 + tc)
        sub = lax.broadcasted_iota(jnp.int32, (SUBLANES, tc), 0)
        blocks = [[s_t[c][SUBLANES * j:SUBLANES * (j + 1), cols] for j in range(n_blocks)] for c in range(2)]
        v1, v2 = (_fold_sublanes(_sort16(blocks[c])) for c in range(2))
        v1_lo, v1_hi = v1[0], v1[SUBLANES]
        for r in range(1, SUBLANES):
            v1_lo = jnp.where(sub == r, v1[r], v1_lo)
            v1_hi = jnp.where(sub == r, v1[SUBLANES + r], v1_hi)
        cand_lo = [v1_lo + v2[b] for b in range(PEER_TOPK)]
        cand_hi = [v1_hi + v2[b] for b in range(PEER_TOPK)]
        top_s = _fold_sublanes(_merge_top16(cand_lo, cand_hi))
        tau = top_s[PEER_TOPK - 1]
        z = jnp.ones((SUBLANES, tc), F32)
        for k in range(1, PEER_TOPK):
            z = z + jnp.exp(top_s[k] - top_s[0])
        zinv = 1.0 / z
        r2, e2 = [], []
        for j in range(n_blocks):
            s1, s2 = blocks[0][j], blocks[1][j]
            cnt = _count_prefix(v2, lambda p, s1=s1: s1 + p >= tau)
            rank = _count_prefix(v2, lambda p, s2=s2: p > s2)
            lim_j = jnp.where(s1 >= v1[PEER_TOPK - 1], cnt, 0.0)
            f_j = jnp.exp(s1 - v1[0]) * zinv
            for g in range(SUBLANES // KEYS_PER_BLOCK):
                rows = slice(g * KEYS_PER_BLOCK, (g + 1) * KEYS_PER_BLOCK)
                lim_ref[hh, j * (SUBLANES // KEYS_PER_BLOCK) + g, :, cols] = lim_j[rows]
                f_ref[hh, j * (SUBLANES // KEYS_PER_BLOCK) + g, :, cols] = f_j[rows]
            r2.append(rank)
            e2.append(jnp.exp(s2 - v2[0]))
        r2_ref[hh, :, cols] = jnp.concatenate(r2, axis=0).astype(BF16)
        e2_ref[hh, :, cols] = jnp.concatenate(e2, axis=0).astype(BF16)


def _route(x1b, wq_b, keys_b, tt):
    t, d = x1b.shape
    dk = 2 * PEER_DHALF
    tc = _tile(tt, 256)
    hps = 1
    big_spec = pl.BlockSpec((hps, N_KEYS, tt), lambda i, h: (h, 0, i))
    n_eb = N_KEYS // KEYS_PER_BLOCK
    row_shape = jax.ShapeDtypeStruct((PEER_HEADS, n_eb, KEYS_PER_BLOCK, t), F32)
    row_spec = pl.BlockSpec((hps, n_eb, KEYS_PER_BLOCK, tt), lambda i, h: (h, 0, 0, i))
    return pl.pallas_call(
        functools.partial(_route_kernel, tt=tt, tc=tc),
        out_shape=(row_shape, row_shape,
                   jax.ShapeDtypeStruct((PEER_HEADS, N_KEYS, t), BF16),
                   jax.ShapeDtypeStruct((PEER_HEADS, N_KEYS, t), BF16)),
        grid=(t // tt, PEER_HEADS // hps),
        in_specs=[pl.BlockSpec((tt, d), lambda i, h: (i, 0)),
                  pl.BlockSpec((d, hps * dk), lambda i, h: (0, h)),
                  pl.BlockSpec((hps, 2, N_KEYS, PEER_DHALF), lambda i, h: (h, 0, 0, 0))],
        out_specs=(row_spec, row_spec, big_spec, big_spec),
        compiler_params=_params("parallel", "parallel"),
        name="route",
    )(x1b, wq_b, keys_b)


def _peer_kernel(xt_ref, u_ref, vt_ref, lim_ref, f_ref, r2_ref, e2_ref, y_ref, a_ref, hid_ref, bc_ref, w_ref,
                 *, n_blocks, tt):
    s = pl.program_id(0)
    slot = lax.rem(s, 2)
    rc = 2 * SUBLANES

    @pl.when(s == 0)
    def _():
        hid_ref[1] = jnp.zeros(hid_ref.shape[1:], BF16)

    @pl.when(lax.rem(jnp.maximum(s - 1, 0), n_blocks) == 0)
    def _():
        y_ref[...] = jnp.zeros_like(y_ref)

    a_ref[...] = jnp.dot(u_ref[...], xt_ref[...], preferred_element_type=F32)
    y_ref[...] += jnp.dot(vt_ref[...], hid_ref[1 - slot], preferred_element_type=F32)
    zero = jnp.zeros((rc, tt), BF16)
    for q in range(KEYS_PER_BLOCK):
        for h in range(PEER_HEADS):
            bc_ref[0, q, h] = jnp.broadcast_to(lim_ref[h, q:q + 1, :], (rc, tt)).astype(BF16)
            bc_ref[1, q, h] = jnp.broadcast_to(f_ref[h, q:q + 1, :], (rc, tt)).astype(BF16)
    never = r2_ref[0, 0:rc, :] < -1.0
    dep = zero
    for q in range(KEYS_PER_BLOCK):
        for k in range(N_KEYS // rc):
            rows = slice(k * rc, (k + 1) * rc)
            w = dep
            for h in range(PEER_HEADS):
                lim_t = bc_ref[0, q, h]
                if h % CHAIN_EVERY == 0:
                    lim_t = lim_t + dep
                w = w + jnp.where(r2_ref[h, rows, :] < lim_t, e2_ref[h, rows, :], zero) * bc_ref[1, q, h]
                if h % CHAIN_EVERY == CHAIN_EVERY - 1:
                    dep = jnp.where(never, w, zero)
            w_ref[q * N_KEYS + k * rc:q * N_KEYS + (k + 1) * rc, :] = w
    hid_ref[slot] = _gelu(a_ref[...]).astype(BF16) * w_ref[...]


def _peer(x1t, u_b, vt_b, lim, f, r2, e2, tt):
    d, t = x1t.shape
    eb = EXPERT_BLOCK
    n_blocks = u_b.shape[0] // eb
    n_tiles = t // tt
    tile = lambda s: jnp.minimum(s // n_blocks, n_tiles - 1)
    prev = lambda s: jnp.maximum(s - 1, 0)
    row_spec = pl.BlockSpec((PEER_HEADS, None, KEYS_PER_BLOCK, tt), lambda s: (0, s % n_blocks, 0, tile(s)))
    big_spec = pl.BlockSpec((PEER_HEADS, N_KEYS, tt), lambda s: (0, 0, tile(s)))
    return pl.pallas_call(
        functools.partial(_peer_kernel, n_blocks=n_blocks, tt=tt),
        out_shape=jax.ShapeDtypeStruct((d, t), F32),
        grid=(n_tiles * n_blocks + 1,),
        in_specs=[pl.BlockSpec((d, tt), lambda s: (0, tile(s))),
                  pl.BlockSpec((eb, d), lambda s: (s % n_blocks, 0)),
                  pl.BlockSpec((None, d, eb), lambda s: (prev(s) % n_blocks, 0, 0)),
                  row_spec, row_spec, big_spec, big_spec],
        out_specs=pl.BlockSpec((d, tt), lambda s: (0, prev(s) // n_blocks)),
        scratch_shapes=[pltpu.VMEM((eb, tt), F32), pltpu.VMEM((2, eb, tt), BF16),
                        pltpu.VMEM((2, KEYS_PER_BLOCK, PEER_HEADS, 2 * SUBLANES, tt), BF16),
                        pltpu.VMEM((eb, tt), BF16)],
        compiler_params=_params("arbitrary"),
        name="peer",
    )(x1t, u_b, vt_b, lim, f, r2, e2)


def _ln2_kernel(x1_ref, yt_ref, g_ref, b_ref, o_ref, *, alpha):
    o_ref[...] = _layer_norm(alpha * x1_ref[...] + yt_ref[...].T, g_ref[...], b_ref[...])


def _ln2(x1, y_t, g, b, alpha):
    t, d = x1.shape
    tm = _tile(t, 512)
    row = lambda i: (i, 0)
    return pl.pallas_call(
        functools.partial(_ln2_kernel, alpha=alpha),
        out_shape=jax.ShapeDtypeStruct((t, d), F32),
        grid=(t // tm,),
        in_specs=[pl.BlockSpec((tm, d), row), pl.BlockSpec((d, tm), lambda i: (0, i)), _full((1, d)),
                  _full((1, d))],
        out_specs=pl.BlockSpec((tm, d), row),
        compiler_params=_params("parallel"),
        name="ln2",
    )(x1, y_t, g.reshape(1, d), b.reshape(1, d))


def _rel_bias_table(rel_bias):
    n_left = ATT_WINDOW - REL_CLIP + CHUNK - 1
    n_right = CHUNK - 1 - REL_CLIP
    ext = jnp.concatenate([jnp.repeat(rel_bias[:, :1], n_left, axis=1), rel_bias,
                           jnp.repeat(rel_bias[:, -1:], max(n_right, 0), axis=1)], axis=1)
    return jnp.stack([ext[:, CHUNK - 1 - i:CHUNK - 1 - i + BAND] for i in range(CHUNK)], axis=1)


def _step_bias(table, cpb):
    n_heads = table.shape[0]
    table = (table - table[:, :1, :1]) * LOG2E
    tab_t = jnp.swapaxes(table, 1, 2)
    per_chunk = [jnp.pad(tab_t, ((0, 0), (c * CHUNK, (cpb - 1 - c) * CHUNK), (0, 0)),
                         constant_values=MASK_VALUE) for c in range(cpb)]
    both = jnp.stack(per_chunk, axis=2)
    rows = both.shape[1]
    both = both.reshape(n_heads // 2, 2, rows, cpb * CHUNK)
    return jnp.swapaxes(both, 1, 2).reshape(n_heads // 2, rows, 2 * cpb * CHUNK)


def _encoder_layer(x, hist_k, hist_v, h0, wts, alpha):
    nb, seq, d = x.shape
    width = d // 2
    t = nb * seq
    n_gt = width // LANES
    assert seq % CHUNK == 0 and width % LANES == 0 and nb % SUBLANES == 0
    assert hist_k is None or seq == CHUNK
    x2d = x.reshape(t, d)

    n_hist = 0 if hist_k is None else hist_k.shape[1]
    n_invalid = ATT_WINDOW - n_hist
    def history(hist, n_new):
        if hist is None:
            return jnp.zeros((nb, ATT_WINDOW + n_new, width), BF16)
        return jnp.pad(hist.reshape(nb, n_hist, width).astype(BF16), ((0, 0), (n_invalid, n_new), (0, 0)))
    if _tile(seq, 256) % LANES == 0:
        q_b, kpad, vtpad, k_new, v_new, u_tm = _proj(
            x2d, wts["w_in"], history(hist_k, seq), jnp.swapaxes(history(hist_v, seq), 1, 2), nb, seq, width)
    else:
        q_b, k_b, v_b, k_new, v_new, u_tm = _proj_short(x2d, wts["w_in"], nb, seq, width)
        kpad = jnp.concatenate([history(hist_k, 0), k_b.reshape(nb, seq, width)], axis=1)
        vtpad = jnp.swapaxes(jnp.concatenate([history(hist_v, 0), v_b.reshape(nb, seq, width)], axis=1), 1, 2)
    n_chunks = seq // CHUNK
    cpb = 2 if n_chunks % 2 == 0 else 1
    assert cpb == 2 or n_chunks == 1
    att_t = _attention(q_b, kpad, vtpad, _step_bias(wts["bias"], cpb), wts["norm_attn_g"], nb, seq, width,
                       cpb, n_invalid)

    ssm_tm, h_last = _ssm(u_tm.reshape(t, width), h0, wts["ssm"], nb, seq, width)

    x1, x1b, x1t = _mix(att_t, ssm_tm.reshape(seq, nb * width), x2d, wts["w_out"], wts["ln1_g"],
                        wts["ln1_b"], alpha, width)

    tt = _tile(t, PEER_TOKENS)
    lim, f, r2, e2 = _route(x1b, wts["peer_wq"], wts["peer_keys"], tt)
    y_t = _peer(x1t, wts["peer_u"], wts["peer_vt"], lim, f, r2, e2, tt)
    x2 = _ln2(x1, y_t, wts["ln2_g"], wts["ln2_b"], alpha)

    n_heads = width // HEAD_DIM
    return (x2.reshape(nb, seq, d), k_new.reshape(nb, -1, n_heads, HEAD_DIM),
            v_new.reshape(nb, -1, n_heads, HEAD_DIM), h_last)


def kernel(x_prompt, x_sample, cache_attn_k, cache_attn_v, state_ssm_re, state_ssm_im, w_in, rel_bias, norm_attn_g, ssm_a_re, ssm_a_im, ssm_log_dt, ssm_b_re, ssm_b_im, ssm_c_re, ssm_c_im, ssm_d, ssm_glu_w, ssm_glu_b, norm_ssm_g, w_out, ln1_g, ln1_b, peer_wq, peer_keys, peer_u, peer_v, ln2_g, ln2_b):
    depth = w_in.shape[0]
    alpha = (2.0 * depth) ** 0.25
    d = x_prompt.shape[-1]
    width = d // 2
    n_gt = width // LANES
    n_groups = width // SSM_GROUP
    keep = min(ATT_WINDOW, x_prompt.shape[1])

    yp, ys = x_prompt, x_sample
    outs = [[] for _ in range(8)]
    for l in range(depth):
        wts = dict(
            w_in=w_in[l].astype(BF16),
            bias=_rel_bias_table(rel_bias[l]),
            norm_attn_g=jnp.broadcast_to(norm_attn_g[l][:, None], (width, LANES)),
            ssm=_ssm_params(ssm_a_re[l], ssm_a_im[l], ssm_log_dt[l], ssm_b_re[l], ssm_b_im[l], ssm_c_re[l],
                            ssm_c_im[l], ssm_d[l], ssm_glu_w[l], ssm_glu_b[l], norm_ssm_g[l], width),
            w_out=w_out[l].astype(BF16), ln1_g=ln1_g[l], ln1_b=ln1_b[l],
            peer_wq=peer_wq[l].astype(BF16), peer_keys=peer_keys[l].astype(BF16),
            peer_u=peer_u[l].astype(BF16),
            peer_vt=jnp.swapaxes(peer_v[l].astype(BF16).reshape(-1, EXPERT_BLOCK, d), 1, 2),
            ln2_g=ln2_g[l], ln2_b=ln2_b[l])
        h0p = jnp.zeros((n_gt, x_prompt.shape[0], 2 * STATE_TILE), F32)
        yp, kp, vp, hp = _encoder_layer(yp, None, None, h0p, wts, alpha)
        h0s = _state_to_tiles(state_ssm_re[l].astype(F32), state_ssm_im[l].astype(F32), n_gt)
        ys, kn, vn, hs = _encoder_layer(ys, cache_attn_k[l], cache_attn_v[l], h0s, wts, alpha)
        hp_re, hp_im = _tiles_to_state(hp, n_groups)
        hs_re, hs_im = _tiles_to_state(hs, n_groups)
        assert kp.shape[1] == keep
        for acc, val in zip(outs, (kp, vp, hp_re, hp_im, kn, vn, hs_re, hs_im)):
            acc.append(val)
    return (yp, ys) + tuple(jnp.stack(o) for o in outs)
```

```python
import functools

import jax
import jax.numpy as jnp
from jax import lax
from jax.experimental import pallas as pl
from jax.experimental.pallas import tpu as pltpu

F32 = jnp.float32
BF16 = jnp.bfloat16

CHUNK = 64
LEFT_CHUNKS = 8
ATT_WINDOW = LEFT_CHUNKS * CHUNK
BAND = ATT_WINDOW + CHUNK
HEAD_DIM = 64
REL_CLIP = 2 * CHUNK
SSM_GROUP = 16
SSM_STATE = 64
PEER_HEADS = 8
PEER_TOPK = 16
N_KEYS = 128
PEER_DHALF = 128
LN_EPS = 1e-5
RMS_EPS = 1e-6
MASK_VALUE = -1e30
LOG2E = 1.4426950408889634
Q_SCALE = HEAD_DIM ** -0.5 * LOG2E

LANES = 128
SUBLANES = 8
GROUPS_PER_TILE = LANES // SSM_GROUP
STATE_TILE = GROUPS_PER_TILE * SSM_STATE
VMEM_LIMIT = 56 * 1024 * 1024
EXPERT_BLOCK = 512
KEYS_PER_BLOCK = EXPERT_BLOCK // N_KEYS
PEER_TOKENS = 1024
ATTN_PAIR_GROUP = 8
CHAIN_EVERY = 2

SORT16 = (
    (0, 13), (1, 12), (2, 15), (3, 14), (4, 8), (5, 6), (7, 11), (9, 10),
    (0, 5), (1, 7), (2, 9), (3, 4), (6, 13), (8, 14), (10, 15), (11, 12),
    (0, 1), (2, 3), (4, 5), (6, 8), (7, 9), (10, 11), (12, 13), (14, 15),
    (0, 2), (1, 3), (4, 10), (5, 11), (6, 7), (8, 9), (12, 14), (13, 15),
    (1, 2), (3, 12), (4, 6), (5, 7), (8, 10), (9, 11), (13, 14),
    (1, 4), (2, 6), (5, 8), (7, 10), (9, 13), (11, 14),
    (2, 4), (3, 6), (9, 12), (11, 13),
    (3, 5), (6, 8), (7, 9), (10, 12),
    (3, 4), (5, 6), (7, 8), (9, 10), (11, 12),
    (6, 7), (8, 9),
)


def _params(*semantics):
    return pltpu.CompilerParams(dimension_semantics=semantics, vmem_limit_bytes=VMEM_LIMIT)


def _tile(n, pref):
    t = min(n, pref)
    while n % t:
        t //= 2
    return t


def _gelu(x):
    return 0.5 * x * (1.0 + lax.erf(x * 0.7071067811865476))


def _full(shape):
    return pl.BlockSpec(shape, lambda *_: (0,) * len(shape))


def _proj_kernel(x_ref, w_ref, kbase_ref, vtbase_ref, q_ref, kpad_ref, vtpad_ref, kf_ref, vf_ref, u_ref,
                 *, width):
    del kbase_ref, vtbase_ref
    xb = x_ref[...].astype(BF16)
    acc = [jnp.dot(xb, w_ref[:, c * width:(c + 1) * width], preferred_element_type=F32) for c in range(4)]
    q_ref[...] = (acc[0] * Q_SCALE).astype(BF16)
    kf_ref[...] = acc[1]
    kpad_ref[...] = acc[1].astype(BF16)
    vf_ref[...] = acc[2]
    vtpad_ref[...] = acc[2].T.astype(BF16)
    u_ref[...] = acc[3]


def _proj(x2d, w_in_b, kbase, vtbase, nb, seq, width):
    t, d = x2d.shape
    tm = _tile(seq, 256)
    assert tm % LANES == 0 and ATT_WINDOW % tm == 0
    per_seq = seq // tm
    skip = ATT_WINDOW // tm
    keep = min(ATT_WINDOW, seq)
    dropped = per_seq - keep // tm
    row = lambda i: (i, 0)
    tail = lambda i: (i // per_seq, jnp.maximum(i % per_seq - dropped, 0), 0)
    tail_f32 = jax.ShapeDtypeStruct((nb, keep, width), F32)
    return pl.pallas_call(
        functools.partial(_proj_kernel, width=width),
        out_shape=(jax.ShapeDtypeStruct((t, width), BF16),
                   jax.ShapeDtypeStruct(kbase.shape, BF16), jax.ShapeDtypeStruct(vtbase.shape, BF16),
                   tail_f32, tail_f32, jax.ShapeDtypeStruct((seq, nb * width), F32)),
        grid=(t // tm,),
        in_specs=[pl.BlockSpec((tm, d), row),
                  pl.BlockSpec((d, 4 * width), lambda i: (0, 0)),
                  pl.BlockSpec(memory_space=pl.ANY), pl.BlockSpec(memory_space=pl.ANY)],
        out_specs=(pl.BlockSpec((tm, width), row),
                   pl.BlockSpec((None, tm, width), lambda i: (i // per_seq, skip + i % per_seq, 0)),
                   pl.BlockSpec((None, width, tm), lambda i: (i // per_seq, 0, skip + i % per_seq)),
                   pl.BlockSpec((None, tm, width), tail), pl.BlockSpec((None, tm, width), tail),
                   pl.BlockSpec((tm, width), lambda i: (i % per_seq, i // per_seq))),
        input_output_aliases={2: 1, 3: 2},
        compiler_params=_params("arbitrary"),
        name="proj",
    )(x2d, w_in_b, kbase, vtbase)


def _proj_short_kernel(x_ref, w_ref, q_ref, k_ref, v_ref, kf_ref, vf_ref, u_ref, *, width):
    xb = x_ref[...].astype(BF16)
    acc = [jnp.dot(xb, w_ref[:, c * width:(c + 1) * width], preferred_element_type=F32) for c in range(4)]
    q_ref[...] = (acc[0] * Q_SCALE).astype(BF16)
    kf_ref[...] = acc[1]
    k_ref[...] = acc[1].astype(BF16)
    vf_ref[...] = acc[2]
    v_ref[...] = acc[2].astype(BF16)
    u_ref[...] = acc[3]


def _proj_short(x2d, w_in_b, nb, seq, width):
    t, d = x2d.shape
    tm = _tile(seq, 256)
    per_seq = seq // tm
    row = lambda i: (i, 0)
    rows_f32 = jax.ShapeDtypeStruct((t, width), F32)
    rows_b16 = jax.ShapeDtypeStruct((t, width), BF16)
    return pl.pallas_call(
        functools.partial(_proj_short_kernel, width=width),
        out_shape=(rows_b16, rows_b16, rows_b16, rows_f32, rows_f32,
                   jax.ShapeDtypeStruct((seq, nb * width), F32)),
        grid=(t // tm,),
        in_specs=[pl.BlockSpec((tm, d), row),
                  pl.BlockSpec((d, 4 * width), lambda i: (0, 0))],
        out_specs=tuple([pl.BlockSpec((tm, width), row)] * 5
                        + [pl.BlockSpec((tm, width), lambda i: (i % per_seq, i // per_seq))]),
        compiler_params=_params("parallel"),
        name="proj",
    )(x2d, w_in_b)


def _attn_kernel(q_ref, k_ref, vt_ref, bias_ref, g_ref, o_ref, *, n_steps, cpb, n_invalid, width):
    nq = cpb * CHUNK
    nc = 2 * nq
    kb = ATT_WINDOW + nq
    lo = lax.broadcasted_iota(jnp.int32, (nq, LANES), 1) < HEAD_DIM
    nt_dims = (((1,), (1,)), ((), ()))
    group = ATTN_PAIR_GROUP
    k_split = -(-(kb // 2) // LANES) * LANES
    halves = ((0, k_split), (k_split, kb - k_split))
    z0, z1 = (cpb - 1) * CHUNK, ATT_WINDOW - REL_CLIP

    def pair_scores(r0, hp):
        cols = slice(hp * LANES, (hp + 1) * LANES)
        q2 = q_ref[pl.ds(r0, nq), cols]
        qcat = jnp.concatenate([jnp.where(lo, q2, 0), jnp.where(lo, 0, q2)], axis=0)
        return jnp.concatenate(
            [lax.dot_general(k_ref[pl.ds(r0 + k0, kn), cols], qcat, nt_dims, preferred_element_type=F32)
             for k0, kn in halves], axis=0)

    def step(r0, masked):
        tiles = []
        ssq = jnp.zeros((1, nq), F32)
        n_pairs = width // LANES
        for g0 in range(0, n_pairs, group):
            pairs = range(g0, min(g0 + group, n_pairs))
            s = jnp.concatenate([pair_scores(r0, hp) for hp in pairs], axis=1)
            parts = [s[z0:z1], s[z1:] + jnp.concatenate([bias_ref[hp, z1:, :] for hp in pairs], axis=1)]
            if z0:
                parts.insert(0, s[:z0] + jnp.concatenate([bias_ref[hp, :z0, :] for hp in pairs], axis=1))
            s = jnp.concatenate(parts, axis=0)
            if masked:
                s = jnp.where(lax.broadcasted_iota(jnp.int32, s.shape, 0) + r0 >= n_invalid, s, MASK_VALUE)
            m = jnp.max(s, axis=0, keepdims=True)
            e = jnp.exp2(s - m)
            rl = 1.0 / jnp.sum(e, axis=0, keepdims=True)
            p = e.astype(BF16)
            for i, hp in enumerate(pairs):
                cols = slice(hp * LANES, (hp + 1) * LANES)
                pc = slice(i * nc, (i + 1) * nc)
                o_t = sum(jnp.dot(vt_ref[cols, pl.ds(r0 + k0, kn)], p[k0:k0 + kn, pc],
                                  preferred_element_type=F32) for k0, kn in halves) * rl[:, pc]
                tile = jnp.concatenate([o_t[:HEAD_DIM, :nq], o_t[HEAD_DIM:, nq:]], axis=0)
                ssq = ssq + jnp.sum(tile * tile, axis=0, keepdims=True)
                tiles.append(tile)
        rinv = lax.rsqrt(ssq / width + RMS_EPS)
        for hp, tile in enumerate(tiles):
            rows = slice(hp * LANES, (hp + 1) * LANES)
            o_ref[rows, pl.ds(r0, nq)] = (tile * rinv * g_ref[rows, :nq]).astype(BF16)

    n_masked = min(n_steps, -(-n_invalid // nq))
    if n_steps == 1:
        step(0, n_masked > 0)
    else:
        def body(i, carry, masked):
            step(pl.multiple_of(i * nq, nq), masked)
            return carry
        lax.fori_loop(0, n_masked, functools.partial(body, masked=True), 0)
        lax.fori_loop(n_masked, n_steps, functools.partial(body, masked=False), 0)


def _attention(qkv, kpad, vtpad, bias, gain_b, nb, seq, width, cpb, n_invalid):
    lp = kpad.shape[1]
    nq = cpb * CHUNK
    return pl.pallas_call(
        functools.partial(_attn_kernel, n_steps=seq // nq, cpb=cpb, n_invalid=n_invalid, width=width),
        out_shape=jax.ShapeDtypeStruct((nb, width, seq), BF16),
        grid=(nb,),
        in_specs=[pl.BlockSpec((seq, width), lambda b: (b, 0)),
                  pl.BlockSpec((None, lp, width), lambda b: (b, 0, 0)),
                  pl.BlockSpec((None, width, lp), lambda b: (b, 0, 0)),
                  _full(bias.shape),
                  _full((width, LANES))],
        out_specs=pl.BlockSpec((None, width, seq), lambda b: (b, 0, 0)),
        compiler_params=_params("parallel"),
        name="attn",
    )(qkv, kpad, vtpad, bias, gain_b)


def _ssm_prep_kernel(are_ref, aim_ref, ldt_ref, bre_ref, bim_ref, abr_ref, abi_ref, bbr_ref, bbi_ref):
    a_re = are_ref[...]
    a_im = aim_ref[...]
    dt = jnp.exp(ldt_ref[...])
    mag = jnp.exp(a_re * dt)
    ang = a_im * dt
    ab_re = mag * jnp.cos(ang)
    ab_im = mag * jnp.sin(ang)
    n_re = ab_re - 1.0
    den = a_re * a_re + a_im * a_im
    c_re = (n_re * a_re + ab_im * a_im) / den
    c_im = (ab_im * a_re - n_re * a_im) / den
    abr_ref[...] = ab_re
    abi_ref[...] = ab_im
    b_re = bre_ref[...]
    b_im = bim_ref[...]
    bbr_ref[...] = c_re[:, None, :] * b_re - c_im[:, None, :] * b_im
    bbi_ref[...] = c_re[:, None, :] * b_im + c_im[:, None, :] * b_re


def _ssm_prep(a_re, a_im, log_dt, bt_re, bt_im):
    g, p = a_re.shape
    h = bt_re.shape[1]
    return pl.pallas_call(
        _ssm_prep_kernel,
        out_shape=(jax.ShapeDtypeStruct((g, p), F32), jax.ShapeDtypeStruct((g, p), F32),
                   jax.ShapeDtypeStruct((g, h, p), F32), jax.ShapeDtypeStruct((g, h, p), F32)),
        name="ssm_prep",
    )(a_re, a_im, log_dt.reshape(g, 1), bt_re, bt_im)


def _ssm_kernel(u_ref, h0_ref, are_ref, aim_ref, bre_ref, bim_ref, cre_ref, cim_ref, d_ref, wg_ref, bg_ref,
                gn_ref, o_ref, hlast_ref, hst_ref, bu_ref, y_ref, *, nb, tl, n_gt, width):
    sw = STATE_TILE

    @pl.when(pl.program_id(0) == 0)
    def _():
        hst_ref[...] = h0_ref[...]

    for gt in range(n_gt):
        cols = slice(gt * LANES, (gt + 1) * LANES)
        uf = u_ref[:, cols]
        ub = uf.astype(BF16)
        bu_ref[:, :sw] = jnp.dot(ub, bre_ref[gt], preferred_element_type=F32)
        bu_ref[:, sw:] = jnp.dot(ub, bim_ref[gt], preferred_element_type=F32)
        a_re = jnp.broadcast_to(are_ref[gt], (nb, sw))
        a_im = jnp.broadcast_to(aim_ref[gt], (nb, sw))

        def step(t, carry, a_re=a_re, a_im=a_im):
            h_re, h_im = carry
            r0 = pl.multiple_of(t * nb, nb)
            n_re = a_re * h_re - a_im * h_im + bu_ref[pl.ds(r0, nb), :sw]
            n_im = a_re * h_im + a_im * h_re + bu_ref[pl.ds(r0, nb), sw:]
            bu_ref[pl.ds(r0, nb), :sw] = n_re
            bu_ref[pl.ds(r0, nb), sw:] = n_im
            return n_re, n_im

        h_re, h_im = lax.fori_loop(0, tl, step, (hst_ref[gt, :, :sw], hst_ref[gt, :, sw:]), unroll=True)
        hst_ref[gt, :, :sw] = h_re
        hst_ref[gt, :, sw:] = h_im

        y = (jnp.dot(bu_ref[:, :sw].astype(BF16), cre_ref[gt], preferred_element_type=F32)
             - jnp.dot(bu_ref[:, sw:].astype(BF16), cim_ref[gt], preferred_element_type=F32)
             + d_ref[:, cols] * uf)
        y = _gelu(y)
        z = jnp.dot(y.astype(BF16), wg_ref[gt], preferred_element_type=F32) + bg_ref[:, cols]
        y_ref[:, cols] = y * jax.nn.sigmoid(z)

    yy = y_ref[...]
    ms = jnp.sum(yy * yy, axis=-1, keepdims=True) / width
    o_ref[...] = (yy * lax.rsqrt(ms + RMS_EPS) * gn_ref[...]).astype(BF16)
    hlast_ref[...] = hst_ref[...]


def _ssm(u_tm, h0, sp, nb, seq, width):
    n_gt = width // LANES
    tl = _tile(seq, 64)
    rows = tl * nb
    sw = STATE_TILE
    return pl.pallas_call(
        functools.partial(_ssm_kernel, nb=nb, tl=tl, n_gt=n_gt, width=width),
        out_shape=(jax.ShapeDtypeStruct((seq * nb, width), BF16),
                   jax.ShapeDtypeStruct((n_gt, nb, 2 * sw), F32)),
        grid=(seq // tl,),
        in_specs=[pl.BlockSpec((rows, width), lambda s: (s, 0)),
                  _full((n_gt, nb, 2 * sw)),
                  _full((n_gt, 1, sw)), _full((n_gt, 1, sw)),
                  _full((n_gt, LANES, sw)), _full((n_gt, LANES, sw)),
                  _full((n_gt, sw, LANES)), _full((n_gt, sw, LANES)),
                  _full((1, width)),
                  _full((n_gt, LANES, LANES)),
                  _full((1, width)), _full((1, width))],
        out_specs=(pl.BlockSpec((rows, width), lambda s: (s, 0)),
                   _full((n_gt, nb, 2 * sw))),
        scratch_shapes=[pltpu.VMEM((n_gt, nb, 2 * sw), F32),
                        pltpu.VMEM((rows, 2 * sw), F32),
                        pltpu.VMEM((rows, width), F32)],
        compiler_params=_params("arbitrary"),
        name="ssm",
    )(u_tm, h0, sp["a_re"], sp["a_im"], sp["b_re"], sp["b_im"], sp["c_re"], sp["c_im"], sp["d"],
      sp["wg"], sp["bg"], sp["gn"])


def _block_diag(m, n_gt):
    _, r, c = m.shape
    m4 = m.reshape(n_gt, GROUPS_PER_TILE, r, 1, c)
    eye = jnp.eye(GROUPS_PER_TILE, dtype=bool)[None, :, None, :, None]
    return jnp.where(eye, m4, 0).reshape(n_gt, GROUPS_PER_TILE * r, GROUPS_PER_TILE * c)


def _ssm_params(a_re, a_im, log_dt, b_re, b_im, c_re, c_im, d, glu_w, glu_b, gain, width):
    n_gt = width // LANES
    ab_re, ab_im, bb_re, bb_im = _ssm_prep(a_re, a_im, log_dt, jnp.swapaxes(b_re, 1, 2),
                                           jnp.swapaxes(b_im, 1, 2))
    return dict(
        a_re=ab_re.reshape(n_gt, 1, STATE_TILE), a_im=ab_im.reshape(n_gt, 1, STATE_TILE),
        b_re=_block_diag(bb_re, n_gt).astype(BF16), b_im=_block_diag(bb_im, n_gt).astype(BF16),
        c_re=_block_diag(jnp.swapaxes(c_re, 1, 2), n_gt).astype(BF16),
        c_im=_block_diag(jnp.swapaxes(c_im, 1, 2), n_gt).astype(BF16),
        d=d.reshape(1, width), wg=_block_diag(glu_w, n_gt).astype(BF16), bg=glu_b.reshape(1, width),
        gn=gain.reshape(1, width))


def _state_to_tiles(s_re, s_im, n_gt):
    nb = s_re.shape[0]
    def one(s):
        return jnp.swapaxes(s.reshape(nb, n_gt, STATE_TILE), 0, 1)
    return jnp.concatenate([one(s_re), one(s_im)], axis=-1)


def _tiles_to_state(h, n_groups):
    n_gt, nb, _ = h.shape
    def one(s):
        return jnp.swapaxes(s, 0, 1).reshape(nb, n_groups, SSM_STATE)
    return one(h[:, :, :STATE_TILE]), one(h[:, :, STATE_TILE:])


def _layer_norm(z, g, b):
    mu = jnp.mean(z, axis=-1, keepdims=True)
    zc = z - mu
    var = jnp.mean(zc * zc, axis=-1, keepdims=True)
    return zc * lax.rsqrt(var + LN_EPS) * g + b


def _mix_kernel(a_ref, s_ref, x_ref, wo_ref, g_ref, b_ref, x1_ref, x1b_ref, x1t_ref=None, *, alpha, width):
    tn_dims = (((0,), (0,)), ((), ()))
    tm = x_ref.shape[0]
    n_split = 2 if tm % (2 * LANES) == 0 else 1
    for r0 in range(0, tm, tm // n_split):
        rows = slice(r0, r0 + tm // n_split)
        mix = (lax.dot_general(a_ref[:, rows], wo_ref[:width], tn_dims, preferred_element_type=F32)
               + jnp.dot(s_ref[rows, :], wo_ref[width:], preferred_element_type=F32))
        x1 = _layer_norm(alpha * x_ref[rows, :] + mix, g_ref[...], b_ref[...])
        x1_ref[rows, :] = x1
        x1b_ref[rows, :] = x1.astype(BF16)
        if x1t_ref is not None:
            x1t_ref[:, rows] = x1.T.astype(BF16)


def _mix(att_t, ssm_tm, x2d, w_out_b, g, b, alpha, width):
    t, d = x2d.shape
    nb, _, seq = att_t.shape
    tm = _tile(seq, 512)
    per_seq = seq // tm
    row = lambda i: (i, 0)
    out_shape = [jax.ShapeDtypeStruct((t, d), F32), jax.ShapeDtypeStruct((t, d), BF16)]
    out_specs = [pl.BlockSpec((tm, d), row), pl.BlockSpec((tm, d), row)]
    if tm % LANES == 0:
        out_shape.append(jax.ShapeDtypeStruct((d, t), BF16))
        out_specs.append(pl.BlockSpec((d, tm), lambda i: (0, i)))
    outs = pl.pallas_call(
        functools.partial(_mix_kernel, alpha=alpha, width=width),
        out_shape=tuple(out_shape),
        grid=(t // tm,),
        in_specs=[pl.BlockSpec((None, width, tm), lambda i: (i // per_seq, 0, i % per_seq)),
                  pl.BlockSpec((tm, width), lambda i: (i % per_seq, i // per_seq)),
                  pl.BlockSpec((tm, d), row), _full((2 * width, d)), _full((1, d)), _full((1, d))],
        out_specs=tuple(out_specs),
        compiler_params=_params("parallel"),
        name="mix",
    )(att_t, ssm_tm, x2d, w_out_b, g.reshape(1, d), b.reshape(1, d))
    x1, x1b = outs[0], outs[1]
    return x1, x1b, (outs[2] if len(outs) == 3 else x1b.T)


def _sort16(v):
    v = list(v)
    for i, j in SORT16:
        v[i], v[j] = jnp.maximum(v[i], v[j]), jnp.minimum(v[i], v[j])
    return v


def _merge_top16(a, b):
    v = [jnp.maximum(a[j], b[PEER_TOPK - 1 - j]) for j in range(PEER_TOPK)]
    for dist in (8, 4, 2, 1):
        for i in range(PEER_TOPK):
            if not i & dist:
                v[i], v[i + dist] = jnp.maximum(v[i], v[i + dist]), jnp.minimum(v[i], v[i + dist])
    return v


def _fold_sublanes(v):
    for shift in (4, 2, 1):
        v = _merge_top16(v, [pltpu.roll(x, shift, 0) for x in v])
    return v


def _count_prefix(v, pred):
    assert len(v) == PEER_TOPK == 16
    full = pred(v[15])
    b3 = pred(v[7])
    b2 = pred(jnp.where(b3, v[11], v[3]))
    b1 = pred(jnp.where(b3, jnp.where(b2, v[13], v[9]), jnp.where(b2, v[5], v[1])))
    quad = [jnp.where(b1, v[4 * k + 2], v[4 * k]) for k in range(4)]
    b0 = pred(jnp.where(b3, jnp.where(b2, quad[3], quad[2]), jnp.where(b2, quad[1], quad[0])))
    n = (jnp.where(b3, 8.0, 0.0) + jnp.where(b2, 4.0, 0.0)) + (jnp.where(b1, 2.0, 0.0) + jnp.where(b0, 1.0, 0.0))
    return jnp.where(full, 16.0, n)


def _route_kernel(x_ref, wq_ref, keys_ref, lim_ref, f_ref, r2_ref, e2_ref, *, tt, tc):
    half = tt // 2
    qp = jnp.concatenate(
        [jnp.dot(x_ref[r0:r0 + half, :], wq_ref[...], preferred_element_type=F32) for r0 in (0, half)],
        axis=0).astype(BF16)
    hh = 0
    nt_dims = (((1,), (1,)), ((), ()))
    s_t = [lax.dot_general(keys_ref[hh, c], qp[:, c * PEER_DHALF:(c + 1) * PEER_DHALF], nt_dims,
                           preferred_element_type=F32) for c in range(2)]
    n_blocks = N_KEYS // SUBLANES
    for l0 in range(0, tt, tc):
        cols = slice(l0, l0 + tc)
        sub = lax.broadcasted_iota(jnp.int32, (SUBLANES, tc), 0)
        blocks = [[s_t[c][SUBLANES * j:SUBLANES * (j + 1), cols] for j in range(n_blocks)] for c in range(2)]
        v1, v2 = (_fold_sublanes(_sort16(blocks[c])) for c in range(2))
        v1_lo, v1_hi = v1[0], v1[SUBLANES]
        for r in range(1, SUBLANES):
            v1_lo = jnp.where(sub == r, v1[r], v1_lo)
            v1_hi = jnp.where(sub == r, v1[SUBLANES + r], v1_hi)
        cand_lo = [v1_lo + v2[b] for b in range(PEER_TOPK)]
        cand_hi = [v1_hi + v2[b] for b in range(PEER_TOPK)]
        top_s = _fold_sublanes(_merge_top16(cand_lo, cand_hi))
        tau = top_s[PEER_TOPK - 1]
        z = jnp.ones((SUBLANES, tc), F32)
        for k in range(1, PEER_TOPK):
            z = z + jnp.exp(top_s[k] - top_s[0])
        zinv = 1.0 / z
        r2, e2 = [], []
        for j in range(n_blocks):
            s1, s2 = blocks[0][j], blocks[1][j]
            cnt = _count_prefix(v2, lambda p, s1=s1: s1 + p >= tau)
            rank = _count_prefix(v2, lambda p, s2=s2: p > s2)
            lim_j = jnp.where(s1 >= v1[PEER_TOPK - 1], cnt, 0.0)
            f_j = jnp.exp(s1 - v1[0]) * zinv
            for g in range(SUBLANES // KEYS_PER_BLOCK):
                rows = slice(g * KEYS_PER_BLOCK, (g + 1) * KEYS_PER_BLOCK)
                lim_ref[hh, j * (SUBLANES // KEYS_PER_BLOCK) + g, :, cols] = lim_j[rows]
                f_ref[hh, j * (SUBLANES // KEYS_PER_BLOCK) + g, :, cols] = f_j[rows]
            r2.append(rank)
            e2.append(jnp.exp(s2 - v2[0]))
        r2_ref[hh, :, cols] = jnp.concatenate(r2, axis=0).astype(BF16)
        e2_ref[hh, :, cols] = jnp.concatenate(e2, axis=0).astype(BF16)


def _route(x1b, wq_b, keys_b, tt):
    t, d = x1b.shape
    dk = 2 * PEER_DHALF
    tc = _tile(tt, 256)
    hps = 1
    big_spec = pl.BlockSpec((hps, N_KEYS, tt), lambda i, h: (h, 0, i))
    n_eb = N_KEYS // KEYS_PER_BLOCK
    row_shape = jax.ShapeDtypeStruct((PEER_HEADS, n_eb, KEYS_PER_BLOCK, t), F32)
    row_spec = pl.BlockSpec((hps, n_eb, KEYS_PER_BLOCK, tt), lambda i, h: (h, 0, 0, i))
    return pl.pallas_call(
        functools.partial(_route_kernel, tt=tt, tc=tc),
        out_shape=(row_shape, row_shape,
                   jax.ShapeDtypeStruct((PEER_HEADS, N_KEYS, t), BF16),
                   jax.ShapeDtypeStruct((PEER_HEADS, N_KEYS, t), BF16)),
        grid=(t // tt, PEER_HEADS // hps),
        in_specs=[pl.BlockSpec((tt, d), lambda i, h: (i, 0)),
                  pl.BlockSpec((d, hps * dk), lambda i, h: (0, h)),
                  pl.BlockSpec((hps, 2, N_KEYS, PEER_DHALF), lambda i, h: (h, 0, 0, 0))],
        out_specs=(row_spec, row_spec, big_spec, big_spec),
        compiler_params=_params("parallel", "parallel"),
        name="route",
    )(x1b, wq_b, keys_b)


def _peer_kernel(xt_ref, u_ref, vt_ref, lim_ref, f_ref, r2_ref, e2_ref, y_ref, a_ref, hid_ref, bc_ref, w_ref,
                 *, n_blocks, tt):
    s = pl.program_id(0)
    slot = lax.rem(s, 2)
    rc = 2 * SUBLANES

    @pl.when(s == 0)
    def _():
        hid_ref[1] = jnp.zeros(hid_ref.shape[1:], BF16)

    @pl.when(lax.rem(jnp.maximum(s - 1, 0), n_blocks) == 0)
    def _():
        y_ref[...] = jnp.zeros_like(y_ref)

    a_ref[...] = jnp.dot(u_ref[...], xt_ref[...], preferred_element_type=F32)
    y_ref[...] += jnp.dot(vt_ref[...], hid_ref[1 - slot], preferred_element_type=F32)
    zero = jnp.zeros((rc, tt), BF16)
    for q in range(KEYS_PER_BLOCK):
        for h in range(PEER_HEADS):
            bc_ref[0, q, h] = jnp.broadcast_to(lim_ref[h, q:q + 1, :], (rc, tt)).astype(BF16)
            bc_ref[1, q, h] = jnp.broadcast_to(f_ref[h, q:q + 1, :], (rc, tt)).astype(BF16)
    never = r2_ref[0, 0:rc, :] < -1.0
    dep = zero
    for q in range(KEYS_PER_BLOCK):
        for k in range(N_KEYS // rc):
            rows = slice(k * rc, (k + 1) * rc)
            w = dep
            for h in range(PEER_HEADS):
                lim_t = bc_ref[0, q, h]
                if h % CHAIN_EVERY == 0:
                    lim_t = lim_t + dep
                w = w + jnp.where(r2_ref[h, rows, :] < lim_t, e2_ref[h, rows, :], zero) * bc_ref[1, q, h]
                if h % CHAIN_EVERY == CHAIN_EVERY - 1:
                    dep = jnp.where(never, w, zero)
            w_ref[q * N_KEYS + k * rc:q * N_KEYS + (k + 1) * rc, :] = w
    hid_ref[slot] = _gelu(a_ref[...]).astype(BF16) * w_ref[...]


def _peer(x1t, u_b, vt_b, lim, f, r2, e2, tt):
    d, t = x1t.shape
    eb = EXPERT_BLOCK
    n_blocks = u_b.shape[0] // eb
    n_tiles = t // tt
    tile = lambda s: jnp.minimum(s // n_blocks, n_tiles - 1)
    prev = lambda s: jnp.maximum(s - 1, 0)
    row_spec = pl.BlockSpec((PEER_HEADS, None, KEYS_PER_BLOCK, tt), lambda s: (0, s % n_blocks, 0, tile(s)))
    big_spec = pl.BlockSpec((PEER_HEADS, N_KEYS, tt), lambda s: (0, 0, tile(s)))
    return pl.pallas_call(
        functools.partial(_peer_kernel, n_blocks=n_blocks, tt=tt),
        out_shape=jax.ShapeDtypeStruct((d, t), F32),
        grid=(n_tiles * n_blocks + 1,),
        in_specs=[pl.BlockSpec((d, tt), lambda s: (0, tile(s))),
                  pl.BlockSpec((eb, d), lambda s: (s % n_blocks, 0)),
                  pl.BlockSpec((None, d, eb), lambda s: (prev(s) % n_blocks, 0, 0)),
                  row_spec, row_spec, big_spec, big_spec],
        out_specs=pl.BlockSpec((d, tt), lambda s: (0, prev(s) // n_blocks)),
        scratch_shapes=[pltpu.VMEM((eb, tt), F32), pltpu.VMEM((2, eb, tt), BF16),
                        pltpu.VMEM((2, KEYS_PER_BLOCK, PEER_HEADS, 2 * SUBLANES, tt), BF16),
                        pltpu.VMEM((eb, tt), BF16)],
        compiler_params=_params("arbitrary"),
        name="peer",
    )(x1t, u_b, vt_b, lim, f, r2, e2)


def _ln2_kernel(x1_ref, yt_ref, g_ref, b_ref, o_ref, *, alpha):
    o_ref[...] = _layer_norm(alpha * x1_ref[...] + yt_ref[...].T, g_ref[...], b_ref[...])


def _ln2(x1, y_t, g, b, alpha):
    t, d = x1.shape
    tm = _tile(t, 512)
    row = lambda i: (i, 0)
    return pl.pallas_call(
        functools.partial(_ln2_kernel, alpha=alpha),
        out_shape=jax.ShapeDtypeStruct((t, d), F32),
        grid=(t // tm,),
        in_specs=[pl.BlockSpec((tm, d), row), pl.BlockSpec((d, tm), lambda i: (0, i)), _full((1, d)),
                  _full((1, d))],
        out_specs=pl.BlockSpec((tm, d), row),
        compiler_params=_params("parallel"),
        name="ln2",
    )(x1, y_t, g.reshape(1, d), b.reshape(1, d))


def _rel_bias_table(rel_bias):
    n_left = ATT_WINDOW - REL_CLIP + CHUNK - 1
    n_right = CHUNK - 1 - REL_CLIP
    ext = jnp.concatenate([jnp.repeat(rel_bias[:, :1], n_left, axis=1), rel_bias,
                           jnp.repeat(rel_bias[:, -1:], max(n_right, 0), axis=1)], axis=1)
    return jnp.stack([ext[:, CHUNK - 1 - i:CHUNK - 1 - i + BAND] for i in range(CHUNK)], axis=1)


def _step_bias(table, cpb):
    n_heads = table.shape[0]
    table = (table - table[:, :1, :1]) * LOG2E
    tab_t = jnp.swapaxes(table, 1, 2)
    per_chunk = [jnp.pad(tab_t, ((0, 0), (c * CHUNK, (cpb - 1 - c) * CHUNK), (0, 0)),
                         constant_values=MASK_VALUE) for c in range(cpb)]
    both = jnp.stack(per_chunk, axis=2)
    rows = both.shape[1]
    both = both.reshape(n_heads // 2, 2, rows, cpb * CHUNK)
    return jnp.swapaxes(both, 1, 2).reshape(n_heads // 2, rows, 2 * cpb * CHUNK)


def _encoder_layer(x, hist_k, hist_v, h0, wts, alpha):
    nb, seq, d = x.shape
    width = d // 2
    t = nb * seq
    n_gt = width // LANES
    assert seq % CHUNK == 0 and width % LANES == 0 and nb % SUBLANES == 0
    assert hist_k is None or seq == CHUNK
    x2d = x.reshape(t, d)

    n_hist = 0 if hist_k is None else hist_k.shape[1]
    n_invalid = ATT_WINDOW - n_hist
    def history(hist, n_new):
        if hist is None:
            return jnp.zeros((nb, ATT_WINDOW + n_new, width), BF16)
        return jnp.pad(hist.reshape(nb, n_hist, width).astype(BF16), ((0, 0), (n_invalid, n_new), (0, 0)))
    if _tile(seq, 256) % LANES == 0:
        q_b, kpad, vtpad, k_new, v_new, u_tm = _proj(
            x2d, wts["w_in"], history(hist_k, seq), jnp.swapaxes(history(hist_v, seq), 1, 2), nb, seq, width)
    else:
        q_b, k_b, v_b, k_new, v_new, u_tm = _proj_short(x2d, wts["w_in"], nb, seq, width)
        kpad = jnp.concatenate([history(hist_k, 0), k_b.reshape(nb, seq, width)], axis=1)
        vtpad = jnp.swapaxes(jnp.concatenate([history(hist_v, 0), v_b.reshape(nb, seq, width)], axis=1), 1, 2)
    n_chunks = seq // CHUNK
    cpb = 2 if n_chunks % 2 == 0 else 1
    assert cpb == 2 or n_chunks == 1
    att_t = _attention(q_b, kpad, vtpad, _step_bias(wts["bias"], cpb), wts["norm_attn_g"], nb, seq, width,
                       cpb, n_invalid)

    ssm_tm, h_last = _ssm(u_tm.reshape(t, width), h0, wts["ssm"], nb, seq, width)

    x1, x1b, x1t = _mix(att_t, ssm_tm.reshape(seq, nb * width), x2d, wts["w_out"], wts["ln1_g"],
                        wts["ln1_b"], alpha, width)

    tt = _tile(t, PEER_TOKENS)
    lim, f, r2, e2 = _route(x1b, wts["peer_wq"], wts["peer_keys"], tt)
    y_t = _peer(x1t, wts["peer_u"], wts["peer_vt"], lim, f, r2, e2, tt)
    x2 = _ln2(x1, y_t, wts["ln2_g"], wts["ln2_b"], alpha)

    n_heads = width // HEAD_DIM
    return (x2.reshape(nb, seq, d), k_new.reshape(nb, -1, n_heads, HEAD_DIM),
            v_new.reshape(nb, -1, n_heads, HEAD_DIM), h_last)


def kernel(x_prompt, x_sample, cache_attn_k, cache_attn_v, state_ssm_re, state_ssm_im, w_in, rel_bias, norm_attn_g, ssm_a_re, ssm_a_im, ssm_log_dt, ssm_b_re, ssm_b_im, ssm_c_re, ssm_c_im, ssm_d, ssm_glu_w, ssm_glu_b, norm_ssm_g, w_out, ln1_g, ln1_b, peer_wq, peer_keys, peer_u, peer_v, ln2_g, ln2_b):
    depth = w_in.shape[0]
    alpha = (2.0 * depth) ** 0.25
    d = x_prompt.shape[-1]
    width = d // 2
    n_gt = width // LANES
    n_groups = width // SSM_GROUP
    keep = min(ATT_WINDOW, x_prompt.shape[1])

    yp, ys = x_prompt, x_sample
    outs = [[] for _ in range(8)]
    for l in range(depth):
        wts = dict(
            w_in=w_in[l].astype(BF16),
            bias=_rel_bias_table(rel_bias[l]),
            norm_attn_g=jnp.broadcast_to(norm_attn_g[l][:, None], (width, LANES)),
            ssm=_ssm_params(ssm_a_re[l], ssm_a_im[l], ssm_log_dt[l], ssm_b_re[l], ssm_b_im[l], ssm_c_re[l],
                            ssm_c_im[l], ssm_d[l], ssm_glu_w[l], ssm_glu_b[l], norm_ssm_g[l], width),
            w_out=w_out[l].astype(BF16), ln1_g=ln1_g[l], ln1_b=ln1_b[l],
            peer_wq=peer_wq[l].astype(BF16), peer_keys=peer_keys[l].astype(BF16),
            peer_u=peer_u[l].astype(BF16),
            peer_vt=jnp.swapaxes(peer_v[l].astype(BF16).reshape(-1, EXPERT_BLOCK, d), 1, 2),
            ln2_g=ln2_g[l], ln2_b=ln2_b[l])
        h0p = jnp.zeros((n_gt, x_prompt.shape[0], 2 * STATE_TILE), F32)
        yp, kp, vp, hp = _encoder_layer(yp, None, None, h0p, wts, alpha)
        h0s = _state_to_tiles(state_ssm_re[l].astype(F32), state_ssm_im[l].astype(F32), n_gt)
        ys, kn, vn, hs = _encoder_layer(ys, cache_attn_k[l], cache_attn_v[l], h0s, wts, alpha)
        hp_re, hp_im = _tiles_to_state(hp, n_groups)
        hs_re, hs_im = _tiles_to_state(hs, n_groups)
        assert kp.shape[1] == keep
        for acc, val in zip(outs, (kp, vp, hp_re, hp_im, kn, vn, hs_re, hs_im)):
            acc.append(val)
    return (yp, ys) + tuple(jnp.stack(o) for o in outs)
```

```python
import functools

import jax
import jax.numpy as jnp
from jax import lax
from jax.experimental import pallas as pl
from jax.experimental.pallas import tpu as pltpu

F32 = jnp.float32
BF16 = jnp.bfloat16

CHUNK = 64
LEFT_CHUNKS = 8
ATT_WINDOW = LEFT_CHUNKS * CHUNK
BAND = ATT_WINDOW + CHUNK
HEAD_DIM = 64
REL_CLIP = 2 * CHUNK
SSM_GROUP = 16
SSM_STATE = 64
PEER_HEADS = 8
PEER_TOPK = 16
N_KEYS = 128
PEER_DHALF = 128
LN_EPS = 1e-5
RMS_EPS = 1e-6
MASK_VALUE = -1e30
LOG2E = 1.4426950408889634
Q_SCALE = HEAD_DIM ** -0.5 * LOG2E

LANES = 128
SUBLANES = 8
GROUPS_PER_TILE = LANES // SSM_GROUP
STATE_TILE = GROUPS_PER_TILE * SSM_STATE
VMEM_LIMIT = 56 * 1024 * 1024
EXPERT_BLOCK = 1024
KEYS_PER_BLOCK = EXPERT_BLOCK // N_KEYS
PEER_TOKENS = 512
ROUTE_TOKENS = 1024
ATTN_PAIR_GROUP = 8
CHAIN_EVERY = 2

SORT16 = (
    (0, 13), (1, 12), (2, 15), (3, 14), (4, 8), (5, 6), (7, 11), (9, 10),
    (0, 5), (1, 7), (2, 9), (3, 4), (6, 13), (8, 14), (10, 15), (11, 12),
    (0, 1), (2, 3), (4, 5), (6, 8), (7, 9), (10, 11), (12, 13), (14, 15),
    (0, 2), (1, 3), (4, 10), (5, 11), (6, 7), (8, 9), (12, 14), (13, 15),
    (1, 2), (3, 12), (4, 6), (5, 7), (8, 10), (9, 11), (13, 14),
    (1, 4), (2, 6), (5, 8), (7, 10), (9, 13), (11, 14),
    (2, 4), (3, 6), (9, 12), (11, 13),
    (3, 5), (6, 8), (7, 9), (10, 12),
    (3, 4), (5, 6), (7, 8), (9, 10), (11, 12),
    (6, 7), (8, 9),
)


def _params(*semantics):
    return pltpu.CompilerParams(dimension_semantics=semantics, vmem_limit_bytes=VMEM_LIMIT)


def _tile(n, pref):
    t = min(n, pref)
    while n % t:
        t //= 2
    return t


def _gelu(x):
    return 0.5 * x * (1.0 + lax.erf(x * 0.7071067811865476))


def _full(shape):
    return pl.BlockSpec(shape, lambda *_: (0,) * len(shape))


def _proj_kernel(x_ref, w_ref, kbase_ref, vtbase_ref, q_ref, kpad_ref, vtpad_ref, kf_ref, vf_ref, u_ref,
                 *, width):
    del kbase_ref, vtbase_ref
    xb = x_ref[...].astype(BF16)
    acc = [jnp.dot(xb, w_ref[:, c * width:(c + 1) * width], preferred_element_type=F32) for c in range(4)]
    q_ref[...] = (acc[0] * Q_SCALE).astype(BF16)
    kf_ref[...] = acc[1]
    kpad_ref[...] = acc[1].astype(BF16)
    vf_ref[...] = acc[2]
    vtpad_ref[...] = acc[2].T.astype(BF16)
    u_ref[...] = acc[3]


def _proj(x2d, w_in_b, kbase, vtbase, nb, seq, width):
    t, d = x2d.shape
    tm = _tile(seq, 256)
    assert tm % LANES == 0 and ATT_WINDOW % tm == 0
    per_seq = seq // tm
    skip = ATT_WINDOW // tm
    keep = min(ATT_WINDOW, seq)
    dropped = per_seq - keep // tm
    row = lambda i: (i, 0)
    tail = lambda i: (i // per_seq, jnp.maximum(i % per_seq - dropped, 0), 0)
    tail_f32 = jax.ShapeDtypeStruct((nb, keep, width), F32)
    return pl.pallas_call(
        functools.partial(_proj_kernel, width=width),
        out_shape=(jax.ShapeDtypeStruct((t, width), BF16),
                   jax.ShapeDtypeStruct(kbase.shape, BF16), jax.ShapeDtypeStruct(vtbase.shape, BF16),
                   tail_f32, tail_f32, jax.ShapeDtypeStruct((seq, nb * width), F32)),
        grid=(t // tm,),
        in_specs=[pl.BlockSpec((tm, d), row),
                  pl.BlockSpec((d, 4 * width), lambda i: (0, 0)),
                  pl.BlockSpec(memory_space=pl.ANY), pl.BlockSpec(memory_space=pl.ANY)],
        out_specs=(pl.BlockSpec((tm, width), row),
                   pl.BlockSpec((None, tm, width), lambda i: (i // per_seq, skip + i % per_seq, 0)),
                   pl.BlockSpec((None, width, tm), lambda i: (i // per_seq, 0, skip + i % per_seq)),
                   pl.BlockSpec((None, tm, width), tail), pl.BlockSpec((None, tm, width), tail),
                   pl.BlockSpec((tm, width), lambda i: (i % per_seq, i // per_seq))),
        input_output_aliases={2: 1, 3: 2},
        compiler_params=_params("arbitrary"),
        name="proj",
    )(x2d, w_in_b, kbase, vtbase)


def _proj_short_kernel(x_ref, w_ref, q_ref, k_ref, v_ref, kf_ref, vf_ref, u_ref, *, width):
    xb = x_ref[...].astype(BF16)
    acc = [jnp.dot(xb, w_ref[:, c * width:(c + 1) * width], preferred_element_type=F32) for c in range(4)]
    q_ref[...] = (acc[0] * Q_SCALE).astype(BF16)
    kf_ref[...] = acc[1]
    k_ref[...] = acc[1].astype(BF16)
    vf_ref[...] = acc[2]
    v_ref[...] = acc[2].astype(BF16)
    u_ref[...] = acc[3]


def _proj_short(x2d, w_in_b, nb, seq, width):
    t, d = x2d.shape
    tm = _tile(seq, 256)
    per_seq = seq // tm
    row = lambda i: (i, 0)
    rows_f32 = jax.ShapeDtypeStruct((t, width), F32)
    rows_b16 = jax.ShapeDtypeStruct((t, width), BF16)
    return pl.pallas_call(
        functools.partial(_proj_short_kernel, width=width),
        out_shape=(rows_b16, rows_b16, rows_b16, rows_f32, rows_f32,
                   jax.ShapeDtypeStruct((seq, nb * width), F32)),
        grid=(t // tm,),
        in_specs=[pl.BlockSpec((tm, d), row),
                  pl.BlockSpec((d, 4 * width), lambda i: (0, 0))],
        out_specs=tuple([pl.BlockSpec((tm, width), row)] * 5
                        + [pl.BlockSpec((tm, width), lambda i: (i % per_seq, i // per_seq))]),
        compiler_params=_params("parallel"),
        name="proj",
    )(x2d, w_in_b)


def _attn_kernel(q_ref, k_ref, vt_ref, bias_ref, g_ref, o_ref, *, n_steps, cpb, n_invalid, width):
    nq = cpb * CHUNK
    nc = 2 * nq
    kb = ATT_WINDOW + nq
    lo = lax.broadcasted_iota(jnp.int32, (nq, LANES), 1) < HEAD_DIM
    nt_dims = (((1,), (1,)), ((), ()))
    group = ATTN_PAIR_GROUP
    k_split = -(-(kb // 2) // LANES) * LANES
    halves = ((0, k_split), (k_split, kb - k_split))
    z0, z1 = (cpb - 1) * CHUNK, ATT_WINDOW - REL_CLIP

    def pair_scores(r0, hp):
        cols = slice(hp * LANES, (hp + 1) * LANES)
        q2 = q_ref[pl.ds(r0, nq), cols]
        qcat = jnp.concatenate([jnp.where(lo, q2, 0), jnp.where(lo, 0, q2)], axis=0)
        return jnp.concatenate(
            [lax.dot_general(k_ref[pl.ds(r0 + k0, kn), cols], qcat, nt_dims, preferred_element_type=F32)
             for k0, kn in halves], axis=0)

    def step(r0, masked):
        tiles = []
        ssq = jnp.zeros((1, nq), F32)
        n_pairs = width // LANES
        for g0 in range(0, n_pairs, group):
            pairs = range(g0, min(g0 + group, n_pairs))
            s = jnp.concatenate([pair_scores(r0, hp) for hp in pairs], axis=1)
            parts = [s[z0:z1], s[z1:] + jnp.concatenate([bias_ref[hp, z1:, :] for hp in pairs], axis=1)]
            if z0:
                parts.insert(0, s[:z0] + jnp.concatenate([bias_ref[hp, :z0, :] for hp in pairs], axis=1))
            s = jnp.concatenate(parts, axis=0)
            if masked:
                s = jnp.where(lax.broadcasted_iota(jnp.int32, s.shape, 0) + r0 >= n_invalid, s, MASK_VALUE)
            m = jnp.max(s, axis=0, keepdims=True)
            e = jnp.exp2(s - m)
            rl = 1.0 / jnp.sum(e, axis=0, keepdims=True)
            p = e.astype(BF16)
            for i, hp in enumerate(pairs):
                cols = slice(hp * LANES, (hp + 1) * LANES)
                pc = slice(i * nc, (i + 1) * nc)
                o_t = sum(jnp.dot(vt_ref[cols, pl.ds(r0 + k0, kn)], p[k0:k0 + kn, pc],
                                  preferred_element_type=F32) for k0, kn in halves) * rl[:, pc]
                tile = jnp.concatenate([o_t[:HEAD_DIM, :nq], o_t[HEAD_DIM:, nq:]], axis=0)
                ssq = ssq + jnp.sum(tile * tile, axis=0, keepdims=True)
                tiles.append(tile)
        rinv = lax.rsqrt(ssq / width + RMS_EPS)
        for hp, tile in enumerate(tiles):
            rows = slice(hp * LANES, (hp + 1) * LANES)
            o_ref[rows, pl.ds(r0, nq)] = (tile * rinv * g_ref[rows, :nq]).astype(BF16)

    n_masked = min(n_steps, -(-n_invalid // nq))
    if n_steps == 1:
        step(0, n_masked > 0)
    else:
        def body(i, carry, masked):
            step(pl.multiple_of(i * nq, nq), masked)
            return carry
        lax.fori_loop(0, n_masked, functools.partial(body, masked=True), 0)
        lax.fori_loop(n_masked, n_steps, functools.partial(body, masked=False), 0)


def _attention(qkv, kpad, vtpad, bias, gain_b, nb, seq, width, cpb, n_invalid):
    lp = kpad.shape[1]
    nq = cpb * CHUNK
    return pl.pallas_call(
        functools.partial(_attn_kernel, n_steps=seq // nq, cpb=cpb, n_invalid=n_invalid, width=width),
        out_shape=jax.ShapeDtypeStruct((nb, width, seq), BF16),
        grid=(nb,),
        in_specs=[pl.BlockSpec((seq, width), lambda b: (b, 0)),
                  pl.BlockSpec((None, lp, width), lambda b: (b, 0, 0)),
                  pl.BlockSpec((None, width, lp), lambda b: (b, 0, 0)),
                  _full(bias.shape),
                  _full((width, LANES))],
        out_specs=pl.BlockSpec((None, width, seq), lambda b: (b, 0, 0)),
        compiler_params=_params("parallel"),
        name="attn",
    )(qkv, kpad, vtpad, bias, gain_b)


def _ssm_prep_kernel(are_ref, aim_ref, ldt_ref, bre_ref, bim_ref, abr_ref, abi_ref, bbr_ref, bbi_ref):
    a_re = are_ref[...]
    a_im = aim_ref[...]
    dt = jnp.exp(ldt_ref[...])
    mag = jnp.exp(a_re * dt)
    ang = a_im * dt
    ab_re = mag * jnp.cos(ang)
    ab_im = mag * jnp.sin(ang)
    n_re = ab_re - 1.0
    den = a_re * a_re + a_im * a_im
    c_re = (n_re * a_re + ab_im * a_im) / den
    c_im = (ab_im * a_re - n_re * a_im) / den
    abr_ref[...] = ab_re
    abi_ref[...] = ab_im
    b_re = bre_ref[...]
    b_im = bim_ref[...]
    bbr_ref[...] = c_re[:, None, :] * b_re - c_im[:, None, :] * b_im
    bbi_ref[...] = c_re[:, None, :] * b_im + c_im[:, None, :] * b_re


def _ssm_prep(a_re, a_im, log_dt, bt_re, bt_im):
    g, p = a_re.shape
    h = bt_re.shape[1]
    return pl.pallas_call(
        _ssm_prep_kernel,
        out_shape=(jax.ShapeDtypeStruct((g, p), F32), jax.ShapeDtypeStruct((g, p), F32),
                   jax.ShapeDtypeStruct((g, h, p), F32), jax.ShapeDtypeStruct((g, h, p), F32)),
        name="ssm_prep",
    )(a_re, a_im, log_dt.reshape(g, 1), bt_re, bt_im)


def _ssm_kernel(u_ref, h0_ref, are_ref, aim_ref, bre_ref, bim_ref, cre_ref, cim_ref, d_ref, wg_ref, bg_ref,
                gn_ref, o_ref, hlast_ref, hst_ref, bu_ref, y_ref, *, nb, tl, n_gt, width):
    sw = STATE_TILE

    @pl.when(pl.program_id(0) == 0)
    def _():
        hst_ref[...] = h0_ref[...]

    for gt in range(n_gt):
        cols = slice(gt * LANES, (gt + 1) * LANES)
        uf = u_ref[:, cols]
        ub = uf.astype(BF16)
        bu_ref[:, :sw] = jnp.dot(ub, bre_ref[gt], preferred_element_type=F32)
        bu_ref[:, sw:] = jnp.dot(ub, bim_ref[gt], preferred_element_type=F32)
        a_re = jnp.broadcast_to(are_ref[gt], (nb, sw))
        a_im = jnp.broadcast_to(aim_ref[gt], (nb, sw))

        def step(t, carry, a_re=a_re, a_im=a_im):
            h_re, h_im = carry
            r0 = pl.multiple_of(t * nb, nb)
            n_re = a_re * h_re - a_im * h_im + bu_ref[pl.ds(r0, nb), :sw]
            n_im = a_re * h_im + a_im * h_re + bu_ref[pl.ds(r0, nb), sw:]
            bu_ref[pl.ds(r0, nb), :sw] = n_re
            bu_ref[pl.ds(r0, nb), sw:] = n_im
            return n_re, n_im

        h_re, h_im = lax.fori_loop(0, tl, step, (hst_ref[gt, :, :sw], hst_ref[gt, :, sw:]), unroll=True)
        hst_ref[gt, :, :sw] = h_re
        hst_ref[gt, :, sw:] = h_im

        y = (jnp.dot(bu_ref[:, :sw].astype(BF16), cre_ref[gt], preferred_element_type=F32)
             - jnp.dot(bu_ref[:, sw:].astype(BF16), cim_ref[gt], preferred_element_type=F32)
             + d_ref[:, cols] * uf)
        y = _gelu(y)
        z = jnp.dot(y.astype(BF16), wg_ref[gt], preferred_element_type=F32) + bg_ref[:, cols]
        y_ref[:, cols] = y * jax.nn.sigmoid(z)

    yy = y_ref[...]
    ms = jnp.sum(yy * yy, axis=-1, keepdims=True) / width
    o_ref[...] = (yy * lax.rsqrt(ms + RMS_EPS) * gn_ref[...]).astype(BF16)
    hlast_ref[...] = hst_ref[...]


def _ssm(u_tm, h0, sp, nb, seq, width):
    n_gt = width // LANES
    tl = _tile(seq, 64)
    rows = tl * nb
    sw = STATE_TILE
    return pl.pallas_call(
        functools.partial(_ssm_kernel, nb=nb, tl=tl, n_gt=n_gt, width=width),
        out_shape=(jax.ShapeDtypeStruct((seq * nb, width), BF16),
                   jax.ShapeDtypeStruct((n_gt, nb, 2 * sw), F32)),
        grid=(seq // tl,),
        in_specs=[pl.BlockSpec((rows, width), lambda s: (s, 0)),
                  _full((n_gt, nb, 2 * sw)),
                  _full((n_gt, 1, sw)), _full((n_gt, 1, sw)),
                  _full((n_gt, LANES, sw)), _full((n_gt, LANES, sw)),
                  _full((n_gt, sw, LANES)), _full((n_gt, sw, LANES)),
                  _full((1, width)),
                  _full((n_gt, LANES, LANES)),
                  _full((1, width)), _full((1, width))],
        out_specs=(pl.BlockSpec((rows, width), lambda s: (s, 0)),
                   _full((n_gt, nb, 2 * sw))),
        scratch_shapes=[pltpu.VMEM((n_gt, nb, 2 * sw), F32),
                        pltpu.VMEM((rows, 2 * sw), F32),
                        pltpu.VMEM((rows, width), F32)],
        compiler_params=_params("arbitrary"),
        name="ssm",
    )(u_tm, h0, sp["a_re"], sp["a_im"], sp["b_re"], sp["b_im"], sp["c_re"], sp["c_im"], sp["d"],
      sp["wg"], sp["bg"], sp["gn"])


def _block_diag(m, n_gt):
    _, r, c = m.shape
    m4 = m.reshape(n_gt, GROUPS_PER_TILE, r, 1, c)
    eye = jnp.eye(GROUPS_PER_TILE, dtype=bool)[None, :, None, :, None]
    return jnp.where(eye, m4, 0).reshape(n_gt, GROUPS_PER_TILE * r, GROUPS_PER_TILE * c)


def _ssm_params(a_re, a_im, log_dt, b_re, b_im, c_re, c_im, d, glu_w, glu_b, gain, width):
    n_gt = width // LANES
    ab_re, ab_im, bb_re, bb_im = _ssm_prep(a_re, a_im, log_dt, jnp.swapaxes(b_re, 1, 2),
                                           jnp.swapaxes(b_im, 1, 2))
    return dict(
        a_re=ab_re.reshape(n_gt, 1, STATE_TILE), a_im=ab_im.reshape(n_gt, 1, STATE_TILE),
        b_re=_block_diag(bb_re, n_gt).astype(BF16), b_im=_block_diag(bb_im, n_gt).astype(BF16),
        c_re=_block_diag(jnp.swapaxes(c_re, 1, 2), n_gt).astype(BF16),
        c_im=_block_diag(jnp.swapaxes(c_im, 1, 2), n_gt).astype(BF16),
        d=d.reshape(1, width), wg=_block_diag(glu_w, n_gt).astype(BF16), bg=glu_b.reshape(1, width),
        gn=gain.reshape(1, width))


def _state_to_tiles(s_re, s_im, n_gt):
    nb = s_re.shape[0]
    def one(s):
        return jnp.swapaxes(s.reshape(nb, n_gt, STATE_TILE), 0, 1)
    return jnp.concatenate([one(s_re), one(s_im)], axis=-1)


def _tiles_to_state(h, n_groups):
    n_gt, nb, _ = h.shape
    def one(s):
        return jnp.swapaxes(s, 0, 1).reshape(nb, n_groups, SSM_STATE)
    return one(h[:, :, :STATE_TILE]), one(h[:, :, STATE_TILE:])


def _layer_norm(z, g, b):
    mu = jnp.mean(z, axis=-1, keepdims=True)
    zc = z - mu
    var = jnp.mean(zc * zc, axis=-1, keepdims=True)
    return zc * lax.rsqrt(var + LN_EPS) * g + b


def _mix_kernel(a_ref, s_ref, x_ref, wo_ref, g_ref, b_ref, x1_ref, x1b_ref, x1t_ref=None, *, alpha, width):
    tn_dims = (((0,), (0,)), ((), ()))
    tm = x_ref.shape[0]
    n_split = 2 if tm % (2 * LANES) == 0 else 1
    for r0 in range(0, tm, tm // n_split):
        rows = slice(r0, r0 + tm // n_split)
        mix = (lax.dot_general(a_ref[:, rows], wo_ref[:width], tn_dims, preferred_element_type=F32)
               + jnp.dot(s_ref[rows, :], wo_ref[width:], preferred_element_type=F32))
        x1 = _layer_norm(alpha * x_ref[rows, :] + mix, g_ref[...], b_ref[...])
        x1_ref[rows, :] = x1
        x1b_ref[rows, :] = x1.astype(BF16)
        if x1t_ref is not None:
            x1t_ref[:, rows] = x1.T.astype(BF16)


def _mix(att_t, ssm_tm, x2d, w_out_b, g, b, alpha, width):
    t, d = x2d.shape
    nb, _, seq = att_t.shape
    tm = _tile(seq, 512)
    per_seq = seq // tm
    row = lambda i: (i, 0)
    out_shape = [jax.ShapeDtypeStruct((t, d), F32), jax.ShapeDtypeStruct((t, d), BF16)]
    out_specs = [pl.BlockSpec((tm, d), row), pl.BlockSpec((tm, d), row)]
    if tm % LANES == 0:
        out_shape.append(jax.ShapeDtypeStruct((d, t), BF16))
        out_specs.append(pl.BlockSpec((d, tm), lambda i: (0, i)))
    outs = pl.pallas_call(
        functools.partial(_mix_kernel, alpha=alpha, width=width),
        out_shape=tuple(out_shape),
        grid=(t // tm,),
        in_specs=[pl.BlockSpec((None, width, tm), lambda i: (i // per_seq, 0, i % per_seq)),
                  pl.BlockSpec((tm, width), lambda i: (i % per_seq, i // per_seq)),
                  pl.BlockSpec((tm, d), row), _full((2 * width, d)), _full((1, d)), _full((1, d))],
        out_specs=tuple(out_specs),
        compiler_params=_params("parallel"),
        name="mix",
    )(att_t, ssm_tm, x2d, w_out_b, g.reshape(1, d), b.reshape(1, d))
    x1, x1b = outs[0], outs[1]
    return x1, x1b, (outs[2] if len(outs) == 3 else x1b.T)


def _sort16(v):
    v = list(v)
    for i, j in SORT16:
        v[i], v[j] = jnp.maximum(v[i], v[j]), jnp.minimum(v[i], v[j])
    return v


def _merge_top16(a, b):
    v = [jnp.maximum(a[j], b[PEER_TOPK - 1 - j]) for j in range(PEER_TOPK)]
    for dist in (8, 4, 2, 1):
        for i in range(PEER_TOPK):
            if not i & dist:
                v[i], v[i + dist] = jnp.maximum(v[i], v[i + dist]), jnp.minimum(v[i], v[i + dist])
    return v


def _fold_sublanes(v):
    for shift in (4, 2, 1):
        v = _merge_top16(v, [pltpu.roll(x, shift, 0) for x in v])
    return v


def _count_prefix(v, pred):
    assert len(v) == PEER_TOPK == 16
    full = pred(v[15])
    b3 = pred(v[7])
    b2 = pred(jnp.where(b3, v[11], v[3]))
    b1 = pred(jnp.where(b3, jnp.where(b2, v[13], v[9]), jnp.where(b2, v[5], v[1])))
    quad = [jnp.where(b1, v[4 * k + 2], v[4 * k]) for k in range(4)]
    b0 = pred(jnp.where(b3, jnp.where(b2, quad[3], quad[2]), jnp.where(b2, quad[1], quad[0])))
    n = (jnp.where(b3, 8.0, 0.0) + jnp.where(b2, 4.0, 0.0)) + (jnp.where(b1, 2.0, 0.0) + jnp.where(b0, 1.0, 0.0))
    return jnp.where(full, 16.0, n)


def _route_kernel(x_ref, wq_ref, keys_ref, lim_ref, f_ref, r2_ref, e2_ref, *, tt, tc):
    half = tt // 2
    qp = jnp.concatenate(
        [jnp.dot(x_ref[r0:r0 + half, :], wq_ref[...], preferred_element_type=F32) for r0 in (0, half)],
        axis=0).astype(BF16)
    hh = 0
    nt_dims = (((1,), (1,)), ((), ()))
    s_t = [lax.dot_general(keys_ref[hh, c], qp[:, c * PEER_DHALF:(c + 1) * PEER_DHALF], nt_dims,
                           preferred_element_type=F32) for c in range(2)]
    n_blocks = N_KEYS // SUBLANES
    for l0 in range(0, tt, tc):
        cols = slice(l0, l0 + tc)
        sub = lax.broadcasted_iota(jnp.int32, (SUBLANES, tc), 0)
        blocks = [[s_t[c][SUBLANES * j:SUBLANES * (j + 1), cols] for j in range(n_blocks)] for c in range(2)]
        v1, v2 = (_fold_sublanes(_sort16(blocks[c])) for c in range(2))
        v1_lo, v1_hi = v1[0], v1[SUBLANES]
        for r in range(1, SUBLANES):
            v1_lo = jnp.where(sub == r, v1[r], v1_lo)
            v1_hi = jnp.where(sub == r, v1[SUBLANES + r], v1_hi)
        cand_lo = [v1_lo + v2[b] for b in range(PEER_TOPK)]
        cand_hi = [v1_hi + v2[b] for b in range(PEER_TOPK)]
        top_s = _fold_sublanes(_merge_top16(cand_lo, cand_hi))
        tau = top_s[PEER_TOPK - 1]
        z = jnp.ones((SUBLANES, tc), F32)
        for k in range(1, PEER_TOPK):
            z = z + jnp.exp(top_s[k] - top_s[0])
        zinv = 1.0 / z
        r2, e2 = [], []
        for j in range(n_blocks):
            s1, s2 = blocks[0][j], blocks[1][j]
            cnt = _count_prefix(v2, lambda p, s1=s1: s1 + p >= tau)
            rank = _count_prefix(v2, lambda p, s2=s2: p > s2)
            lim_j = jnp.where(s1 >= v1[PEER_TOPK - 1], cnt, 0.0)
            f_j = jnp.exp(s1 - v1[0]) * zinv
            for g in range(SUBLANES // KEYS_PER_BLOCK):
                rows = slice(g * KEYS_PER_BLOCK, (g + 1) * KEYS_PER_BLOCK)
                lim_ref[hh, j * (SUBLANES // KEYS_PER_BLOCK) + g, :, cols] = lim_j[rows]
                f_ref[hh, j * (SUBLANES // KEYS_PER_BLOCK) + g, :, cols] = f_j[rows]
            r2.append(rank)
            e2.append(jnp.exp(s2 - v2[0]))
        r2_ref[hh, :, cols] = jnp.concatenate(r2, axis=0).astype(BF16)
        e2_ref[hh, :, cols] = jnp.concatenate(e2, axis=0).astype(BF16)


def _route(x1b, wq_b, keys_b, tt):
    t, d = x1b.shape
    dk = 2 * PEER_DHALF
    tc = _tile(tt, 256)
    hps = 1
    big_spec = pl.BlockSpec((hps, N_KEYS, tt), lambda i, h: (h, 0, i))
    n_eb = N_KEYS // KEYS_PER_BLOCK
    row_shape = jax.ShapeDtypeStruct((PEER_HEADS, n_eb, KEYS_PER_BLOCK, t), F32)
    row_spec = pl.BlockSpec((hps, n_eb, KEYS_PER_BLOCK, tt), lambda i, h: (h, 0, 0, i))
    return pl.pallas_call(
        functools.partial(_route_kernel, tt=tt, tc=tc),
        out_shape=(row_shape, row_shape,
                   jax.ShapeDtypeStruct((PEER_HEADS, N_KEYS, t), BF16),
                   jax.ShapeDtypeStruct((PEER_HEADS, N_KEYS, t), BF16)),
        grid=(t // tt, PEER_HEADS // hps),
        in_specs=[pl.BlockSpec((tt, d), lambda i, h: (i, 0)),
                  pl.BlockSpec((d, hps * dk), lambda i, h: (0, h)),
                  pl.BlockSpec((hps, 2, N_KEYS, PEER_DHALF), lambda i, h: (h, 0, 0, 0))],
        out_specs=(row_spec, row_spec, big_spec, big_spec),
        compiler_params=_params("parallel", "parallel"),
        name="route",
    )(x1b, wq_b, keys_b)


def _peer_kernel(xt_ref, u_ref, vt_ref, lim_ref, f_ref, r2_ref, e2_ref, y_ref, a_ref, hid_ref, bc_ref, w_ref,
                 *, n_blocks, tt):
    s = pl.program_id(0)
    slot = lax.rem(s, 2)
    rc = 2 * SUBLANES

    @pl.when(s == 0)
    def _():
        hid_ref[1] = jnp.zeros(hid_ref.shape[1:], BF16)

    @pl.when(lax.rem(jnp.maximum(s - 1, 0), n_blocks) == 0)
    def _():
        y_ref[...] = jnp.zeros_like(y_ref)

    a_ref[...] = jnp.dot(u_ref[...], xt_ref[...], preferred_element_type=F32)
    y_ref[...] += jnp.dot(vt_ref[...], hid_ref[1 - slot], preferred_element_type=F32)
    zero = jnp.zeros((rc, tt), BF16)
    for q in range(KEYS_PER_BLOCK):
        for h in range(PEER_HEADS):
            bc_ref[0, q, h] = jnp.broadcast_to(lim_ref[h, q:q + 1, :], (rc, tt)).astype(BF16)
            bc_ref[1, q, h] = jnp.broadcast_to(f_ref[h, q:q + 1, :], (rc, tt)).astype(BF16)
    never = r2_ref[0, 0:rc, :] < -1.0
    dep = zero
    for q in range(KEYS_PER_BLOCK):
        for k in range(N_KEYS // rc):
            rows = slice(k * rc, (k + 1) * rc)
            w = dep
            for h in range(PEER_HEADS):
                lim_t = bc_ref[0, q, h]
                if h % CHAIN_EVERY == 0:
                    lim_t = lim_t + dep
                w = w + jnp.where(r2_ref[h, rows, :] < lim_t, e2_ref[h, rows, :], zero) * bc_ref[1, q, h]
                if h % CHAIN_EVERY == CHAIN_EVERY - 1:
                    dep = jnp.where(never, w, zero)
            w_ref[q * N_KEYS + k * rc:q * N_KEYS + (k + 1) * rc, :] = w
    hid_ref[slot] = _gelu(a_ref[...]).astype(BF16) * w_ref[...]


def _peer(x1t, u_b, vt_b, lim, f, r2, e2, tt):
    d, t = x1t.shape
    eb = EXPERT_BLOCK
    n_blocks = u_b.shape[0] // eb
    n_tiles = t // tt
    tile = lambda s: jnp.minimum(s // n_blocks, n_tiles - 1)
    prev = lambda s: jnp.maximum(s - 1, 0)
    row_spec = pl.BlockSpec((PEER_HEADS, None, KEYS_PER_BLOCK, tt), lambda s: (0, s % n_blocks, 0, tile(s)))
    big_spec = pl.BlockSpec((PEER_HEADS, N_KEYS, tt), lambda s: (0, 0, tile(s)))
    return pl.pallas_call(
        functools.partial(_peer_kernel, n_blocks=n_blocks, tt=tt),
        out_shape=jax.ShapeDtypeStruct((d, t), F32),
        grid=(n_tiles * n_blocks + 1,),
        in_specs=[pl.BlockSpec((d, tt), lambda s: (0, tile(s))),
                  pl.BlockSpec((eb, d), lambda s: (s % n_blocks, 0)),
                  pl.BlockSpec((None, d, eb), lambda s: (prev(s) % n_blocks, 0, 0)),
                  row_spec, row_spec, big_spec, big_spec],
        out_specs=pl.BlockSpec((d, tt), lambda s: (0, prev(s) // n_blocks)),
        scratch_shapes=[pltpu.VMEM((eb, tt), F32), pltpu.VMEM((2, eb, tt), BF16),
                        pltpu.VMEM((2, KEYS_PER_BLOCK, PEER_HEADS, 2 * SUBLANES, tt), BF16),
                        pltpu.VMEM((eb, tt), BF16)],
        compiler_params=_params("arbitrary"),
        name="peer",
    )(x1t, u_b, vt_b, lim, f, r2, e2)


def _ln2_kernel(x1_ref, yt_ref, g_ref, b_ref, o_ref, *, alpha):
    o_ref[...] = _layer_norm(alpha * x1_ref[...] + yt_ref[...].T, g_ref[...], b_ref[...])


def _ln2(x1, y_t, g, b, alpha):
    t, d = x1.shape
    tm = _tile(t, 512)
    row = lambda i: (i, 0)
    return pl.pallas_call(
        functools.partial(_ln2_kernel, alpha=alpha),
        out_shape=jax.ShapeDtypeStruct((t, d), F32),
        grid=(t // tm,),
        in_specs=[pl.BlockSpec((tm, d), row), pl.BlockSpec((d, tm), lambda i: (0, i)), _full((1, d)),
                  _full((1, d))],
        out_specs=pl.BlockSpec((tm, d), row),
        compiler_params=_params("parallel"),
        name="ln2",
    )(x1, y_t, g.reshape(1, d), b.reshape(1, d))


def _rel_bias_table(rel_bias):
    n_left = ATT_WINDOW - REL_CLIP + CHUNK - 1
    n_right = CHUNK - 1 - REL_CLIP
    ext = jnp.concatenate([jnp.repeat(rel_bias[:, :1], n_left, axis=1), rel_bias,
                           jnp.repeat(rel_bias[:, -1:], max(n_right, 0), axis=1)], axis=1)
    return jnp.stack([ext[:, CHUNK - 1 - i:CHUNK - 1 - i + BAND] for i in range(CHUNK)], axis=1)


def _step_bias(table, cpb):
    n_heads = table.shape[0]
    table = (table - table[:, :1, :1]) * LOG2E
    tab_t = jnp.swapaxes(table, 1, 2)
    per_chunk = [jnp.pad(tab_t, ((0, 0), (c * CHUNK, (cpb - 1 - c) * CHUNK), (0, 0)),
                         constant_values=MASK_VALUE) for c in range(cpb)]
    both = jnp.stack(per_chunk, axis=2)
    rows = both.shape[1]
    both = both.reshape(n_heads // 2, 2, rows, cpb * CHUNK)
    return jnp.swapaxes(both, 1, 2).reshape(n_heads // 2, rows, 2 * cpb * CHUNK)


def _encoder_layer(x, hist_k, hist_v, h0, wts, alpha):
    nb, seq, d = x.shape
    width = d // 2
    t = nb * seq
    n_gt = width // LANES
    assert seq % CHUNK == 0 and width % LANES == 0 and nb % SUBLANES == 0
    assert hist_k is None or seq == CHUNK
    x2d = x.reshape(t, d)

    n_hist = 0 if hist_k is None else hist_k.shape[1]
    n_invalid = ATT_WINDOW - n_hist
    def history(hist, n_new):
        if hist is None:
            return jnp.zeros((nb, ATT_WINDOW + n_new, width), BF16)
        return jnp.pad(hist.reshape(nb, n_hist, width).astype(BF16), ((0, 0), (n_invalid, n_new), (0, 0)))
    if _tile(seq, 256) % LANES == 0:
        q_b, kpad, vtpad, k_new, v_new, u_tm = _proj(
            x2d, wts["w_in"], history(hist_k, seq), jnp.swapaxes(history(hist_v, seq), 1, 2), nb, seq, width)
    else:
        q_b, k_b, v_b, k_new, v_new, u_tm = _proj_short(x2d, wts["w_in"], nb, seq, width)
        kpad = jnp.concatenate([history(hist_k, 0), k_b.reshape(nb, seq, width)], axis=1)
        vtpad = jnp.swapaxes(jnp.concatenate([history(hist_v, 0), v_b.reshape(nb, seq, width)], axis=1), 1, 2)
    n_chunks = seq // CHUNK
    cpb = 2 if n_chunks % 2 == 0 else 1
    assert cpb == 2 or n_chunks == 1
    att_t = _attention(q_b, kpad, vtpad, _step_bias(wts["bias"], cpb), wts["norm_attn_g"], nb, seq, width,
                       cpb, n_invalid)

    ssm_tm, h_last = _ssm(u_tm.reshape(t, width), h0, wts["ssm"], nb, seq, width)

    x1, x1b, x1t = _mix(att_t, ssm_tm.reshape(seq, nb * width), x2d, wts["w_out"], wts["ln1_g"],
                        wts["ln1_b"], alpha, width)

    lim, f, r2, e2 = _route(x1b, wts["peer_wq"], wts["peer_keys"], _tile(t, ROUTE_TOKENS))
    y_t = _peer(x1t, wts["peer_u"], wts["peer_vt"], lim, f, r2, e2, _tile(t, PEER_TOKENS))
    x2 = _ln2(x1, y_t, wts["ln2_g"], wts["ln2_b"], alpha)

    n_heads = width // HEAD_DIM
    return (x2.reshape(nb, seq, d), k_new.reshape(nb, -1, n_heads, HEAD_DIM),
            v_new.reshape(nb, -1, n_heads, HEAD_DIM), h_last)


def kernel(x_prompt, x_sample, cache_attn_k, cache_attn_v, state_ssm_re, state_ssm_im, w_in, rel_bias, norm_attn_g, ssm_a_re, ssm_a_im, ssm_log_dt, ssm_b_re, ssm_b_im, ssm_c_re, ssm_c_im, ssm_d, ssm_glu_w, ssm_glu_b, norm_ssm_g, w_out, ln1_g, ln1_b, peer_wq, peer_keys, peer_u, peer_v, ln2_g, ln2_b):
    depth = w_in.shape[0]
    alpha = (2.0 * depth) ** 0.25
    d = x_prompt.shape[-1]
    width = d // 2
    n_gt = width // LANES
    n_groups = width // SSM_GROUP
    keep = min(ATT_WINDOW, x_prompt.shape[1])

    yp, ys = x_prompt, x_sample
    outs = [[] for _ in range(8)]
    for l in range(depth):
        wts = dict(
            w_in=w_in[l].astype(BF16),
            bias=_rel_bias_table(rel_bias[l]),
            norm_attn_g=jnp.broadcast_to(norm_attn_g[l][:, None], (width, LANES)),
            ssm=_ssm_params(ssm_a_re[l], ssm_a_im[l], ssm_log_dt[l], ssm_b_re[l], ssm_b_im[l], ssm_c_re[l],
                            ssm_c_im[l], ssm_d[l], ssm_glu_w[l], ssm_glu_b[l], norm_ssm_g[l], width),
            w_out=w_out[l].astype(BF16), ln1_g=ln1_g[l], ln1_b=ln1_b[l],
            peer_wq=peer_wq[l].astype(BF16), peer_keys=peer_keys[l].astype(BF16),
            peer_u=peer_u[l].astype(BF16),
            peer_vt=jnp.swapaxes(peer_v[l].astype(BF16).reshape(-1, EXPERT_BLOCK, d), 1, 2),
            ln2_g=ln2_g[l], ln2_b=ln2_b[l])
        h0p = jnp.zeros((n_gt, x_prompt.shape[0], 2 * STATE_TILE), F32)
        yp, kp, vp, hp = _encoder_layer(yp, None, None, h0p, wts, alpha)
        h0s = _state_to_tiles(state_ssm_re[l].astype(F32), state_ssm_im[l].astype(F32), n_gt)
        ys, kn, vn, hs = _encoder_layer(ys, cache_attn_k[l], cache_attn_v[l], h0s, wts, alpha)
        hp_re, hp_im = _tiles_to_state(hp, n_groups)
        hs_re, hs_im = _tiles_to_state(hs, n_groups)
        assert kp.shape[1] == keep
        for acc, val in zip(outs, (kp, vp, hp_re, hp_im, kn, vn, hs_re, hs_im)):
            acc.append(val)
    return (yp, ys) + tuple(jnp.stack(o) for o in outs)
```

```python
import functools

import jax
import jax.numpy as jnp
from jax import lax
from jax.experimental import pallas as pl
from jax.experimental.pallas import tpu as pltpu

F32 = jnp.float32
BF16 = jnp.bfloat16

CHUNK = 64
LEFT_CHUNKS = 8
ATT_WINDOW = LEFT_CHUNKS * CHUNK
BAND = ATT_WINDOW + CHUNK
HEAD_DIM = 64
REL_CLIP = 2 * CHUNK
SSM_GROUP = 16
SSM_STATE = 64
PEER_HEADS = 8
PEER_TOPK = 16
N_KEYS = 128
PEER_DHALF = 128
LN_EPS = 1e-5
RMS_EPS = 1e-6
MASK_VALUE = -1e30
LOG2E = 1.4426950408889634
Q_SCALE = HEAD_DIM ** -0.5 * LOG2E

LANES = 128
SUBLANES = 8
GROUPS_PER_TILE = LANES // SSM_GROUP
STATE_TILE = GROUPS_PER_TILE * SSM_STATE
VMEM_LIMIT = 56 * 1024 * 1024
EXPERT_BLOCK = 1024
KEYS_PER_BLOCK = EXPERT_BLOCK // N_KEYS
PEER_TOKENS = 512
ROUTE_TOKENS = 2048
ATTN_PAIR_GROUP = 8
CHAIN_EVERY = 2

SORT16 = (
    (0, 13), (1, 12), (2, 15), (3, 14), (4, 8), (5, 6), (7, 11), (9, 10),
    (0, 5), (1, 7), (2, 9), (3, 4), (6, 13), (8, 14), (10, 15), (11, 12),
    (0, 1), (2, 3), (4, 5), (6, 8), (7, 9), (10, 11), (12, 13), (14, 15),
    (0, 2), (1, 3), (4, 10), (5, 11), (6, 7), (8, 9), (12, 14), (13, 15),
    (1, 2), (3, 12), (4, 6), (5, 7), (8, 10), (9, 11), (13, 14),
    (1, 4), (2, 6), (5, 8), (7, 10), (9, 13), (11, 14),
    (2, 4), (3, 6), (9, 12), (11, 13),
    (3, 5), (6, 8), (7, 9), (10, 12),
    (3, 4), (5, 6), (7, 8), (9, 10), (11, 12),
    (6, 7), (8, 9),
)


def _params(*semantics):
    return pltpu.CompilerParams(dimension_semantics=semantics, vmem_limit_bytes=VMEM_LIMIT)


def _tile(n, pref):
    t = min(n, pref)
    while n % t:
        t //= 2
    return t


def _gelu(x):
    return 0.5 * x * (1.0 + lax.erf(x * 0.7071067811865476))


def _full(shape):
    return pl.BlockSpec(shape, lambda *_: (0,) * len(shape))


def _proj_kernel(x_ref, w_ref, kbase_ref, vtbase_ref, q_ref, kpad_ref, vtpad_ref, kf_ref, vf_ref, u_ref,
                 *, width):
    del kbase_ref, vtbase_ref
    xb = x_ref[...].astype(BF16)
    acc = [jnp.dot(xb, w_ref[:, c * width:(c + 1) * width], preferred_element_type=F32) for c in range(4)]
    q_ref[...] = (acc[0] * Q_SCALE).astype(BF16)
    kf_ref[...] = acc[1]
    kpad_ref[...] = acc[1].astype(BF16)
    vf_ref[...] = acc[2]
    vtpad_ref[...] = acc[2].T.astype(BF16)
    u_ref[...] = acc[3]


def _proj(x2d, w_in_b, kbase, vtbase, nb, seq, width):
    t, d = x2d.shape
    tm = _tile(seq, 256)
    assert tm % LANES == 0 and ATT_WINDOW % tm == 0
    per_seq = seq // tm
    skip = ATT_WINDOW // tm
    keep = min(ATT_WINDOW, seq)
    dropped = per_seq - keep // tm
    row = lambda i: (i, 0)
    tail = lambda i: (i // per_seq, jnp.maximum(i % per_seq - dropped, 0), 0)
    tail_f32 = jax.ShapeDtypeStruct((nb, keep, width), F32)
    return pl.pallas_call(
        functools.partial(_proj_kernel, width=width),
        out_shape=(jax.ShapeDtypeStruct((t, width), BF16),
                   jax.ShapeDtypeStruct(kbase.shape, BF16), jax.ShapeDtypeStruct(vtbase.shape, BF16),
                   tail_f32, tail_f32, jax.ShapeDtypeStruct((seq, nb * width), F32)),
        grid=(t // tm,),
        in_specs=[pl.BlockSpec((tm, d), row),
                  pl.BlockSpec((d, 4 * width), lambda i: (0, 0)),
                  pl.BlockSpec(memory_space=pl.ANY), pl.BlockSpec(memory_space=pl.ANY)],
        out_specs=(pl.BlockSpec((tm, width), row),
                   pl.BlockSpec((None, tm, width), lambda i: (i // per_seq, skip + i % per_seq, 0)),
                   pl.BlockSpec((None, width, tm), lambda i: (i // per_seq, 0, skip + i % per_seq)),
                   pl.BlockSpec((None, tm, width), tail), pl.BlockSpec((None, tm, width), tail),
                   pl.BlockSpec((tm, width), lambda i: (i % per_seq, i // per_seq))),
        input_output_aliases={2: 1, 3: 2},
        compiler_params=_params("arbitrary"),
        name="proj",
    )(x2d, w_in_b, kbase, vtbase)


def _proj_short_kernel(x_ref, w_ref, q_ref, k_ref, v_ref, kf_ref, vf_ref, u_ref, *, width):
    xb = x_ref[...].astype(BF16)
    acc = [jnp.dot(xb, w_ref[:, c * width:(c + 1) * width], preferred_element_type=F32) for c in range(4)]
    q_ref[...] = (acc[0] * Q_SCALE).astype(BF16)
    kf_ref[...] = acc[1]
    k_ref[...] = acc[1].astype(BF16)
    vf_ref[...] = acc[2]
    v_ref[...] = acc[2].astype(BF16)
    u_ref[...] = acc[3]


def _proj_short(x2d, w_in_b, nb, seq, width):
    t, d = x2d.shape
    tm = _tile(seq, 256)
    per_seq = seq // tm
    row = lambda i: (i, 0)
    rows_f32 = jax.ShapeDtypeStruct((t, width), F32)
    rows_b16 = jax.ShapeDtypeStruct((t, width), BF16)
    return pl.pallas_call(
        functools.partial(_proj_short_kernel, width=width),
        out_shape=(rows_b16, rows_b16, rows_b16, rows_f32, rows_f32,
                   jax.ShapeDtypeStruct((seq, nb * width), F32)),
        grid=(t // tm,),
        in_specs=[pl.BlockSpec((tm, d), row),
                  pl.BlockSpec((d, 4 * width), lambda i: (0, 0))],
        out_specs=tuple([pl.BlockSpec((tm, width), row)] * 5
                        + [pl.BlockSpec((tm, width), lambda i: (i % per_seq, i // per_seq))]),
        compiler_params=_params("parallel"),
        name="proj",
    )(x2d, w_in_b)


def _attn_kernel(q_ref, k_ref, vt_ref, bias_ref, g_ref, o_ref, *, n_steps, cpb, n_invalid, width):
    nq = cpb * CHUNK
    nc = 2 * nq
    kb = ATT_WINDOW + nq
    lo = lax.broadcasted_iota(jnp.int32, (nq, LANES), 1) < HEAD_DIM
    nt_dims = (((1,), (1,)), ((), ()))
    group = ATTN_PAIR_GROUP
    k_split = -(-(kb // 2) // LANES) * LANES
    halves = ((0, k_split), (k_split, kb - k_split))
    z0, z1 = (cpb - 1) * CHUNK, ATT_WINDOW - REL_CLIP

    def pair_scores(r0, hp):
        cols = slice(hp * LANES, (hp + 1) * LANES)
        q2 = q_ref[pl.ds(r0, nq), cols]
        qcat = jnp.concatenate([jnp.where(lo, q2, 0), jnp.where(lo, 0, q2)], axis=0)
        return jnp.concatenate(
            [lax.dot_general(k_ref[pl.ds(r0 + k0, kn), cols], qcat, nt_dims, preferred_element_type=F32)
             for k0, kn in halves], axis=0)

    def step(r0, masked):
        tiles = []
        ssq = jnp.zeros((1, nq), F32)
        n_pairs = width // LANES
        for g0 in range(0, n_pairs, group):
            pairs = range(g0, min(g0 + group, n_pairs))
            s = jnp.concatenate([pair_scores(r0, hp) for hp in pairs], axis=1)
            parts = [s[z0:z1], s[z1:] + jnp.concatenate([bias_ref[hp, z1:, :] for hp in pairs], axis=1)]
            if z0:
                parts.insert(0, s[:z0] + jnp.concatenate([bias_ref[hp, :z0, :] for hp in pairs], axis=1))
            s = jnp.concatenate(parts, axis=0)
            if masked:
                s = jnp.where(lax.broadcasted_iota(jnp.int32, s.shape, 0) + r0 >= n_invalid, s, MASK_VALUE)
            m = jnp.max(s, axis=0, keepdims=True)
            e = jnp.exp2(s - m)
            rl = 1.0 / jnp.sum(e, axis=0, keepdims=True)
            p = e.astype(BF16)
            for i, hp in enumerate(pairs):
                cols = slice(hp * LANES, (hp + 1) * LANES)
                pc = slice(i * nc, (i + 1) * nc)
                o_t = sum(jnp.dot(vt_ref[cols, pl.ds(r0 + k0, kn)], p[k0:k0 + kn, pc],
                                  preferred_element_type=F32) for k0, kn in halves) * rl[:, pc]
                tile = jnp.concatenate([o_t[:HEAD_DIM, :nq], o_t[HEAD_DIM:, nq:]], axis=0)
                ssq = ssq + jnp.sum(tile * tile, axis=0, keepdims=True)
                tiles.append(tile)
        rinv = lax.rsqrt(ssq / width + RMS_EPS)
        for hp, tile in enumerate(tiles):
            rows = slice(hp * LANES, (hp + 1) * LANES)
            o_ref[rows, pl.ds(r0, nq)] = (tile * rinv * g_ref[rows, :nq]).astype(BF16)

    n_masked = min(n_steps, -(-n_invalid // nq))
    if n_steps == 1:
        step(0, n_masked > 0)
    else:
        def body(i, carry, masked):
            step(pl.multiple_of(i * nq, nq), masked)
            return carry
        lax.fori_loop(0, n_masked, functools.partial(body, masked=True), 0)
        lax.fori_loop(n_masked, n_steps, functools.partial(body, masked=False), 0)


def _attention(qkv, kpad, vtpad, bias, gain_b, nb, seq, width, cpb, n_invalid):
    lp = kpad.shape[1]
    nq = cpb * CHUNK
    return pl.pallas_call(
        functools.partial(_attn_kernel, n_steps=seq // nq, cpb=cpb, n_invalid=n_invalid, width=width),
        out_shape=jax.ShapeDtypeStruct((nb, width, seq), BF16),
        grid=(nb,),
        in_specs=[pl.BlockSpec((seq, width), lambda b: (b, 0)),
                  pl.BlockSpec((None, lp, width), lambda b: (b, 0, 0)),
                  pl.BlockSpec((None, width, lp), lambda b: (b, 0, 0)),
                  _full(bias.shape),
                  _full((width, LANES))],
        out_specs=pl.BlockSpec((None, width, seq), lambda b: (b, 0, 0)),
        compiler_params=_params("parallel"),
        name="attn",
    )(qkv, kpad, vtpad, bias, gain_b)


def _ssm_prep_kernel(are_ref, aim_ref, ldt_ref, bre_ref, bim_ref, abr_ref, abi_ref, bbr_ref, bbi_ref):
    a_re = are_ref[...]
    a_im = aim_ref[...]
    dt = jnp.exp(ldt_ref[...])
    mag = jnp.exp(a_re * dt)
    ang = a_im * dt
    ab_re = mag * jnp.cos(ang)
    ab_im = mag * jnp.sin(ang)
    n_re = ab_re - 1.0
    den = a_re * a_re + a_im * a_im
    c_re = (n_re * a_re + ab_im * a_im) / den
    c_im = (ab_im * a_re - n_re * a_im) / den
    abr_ref[...] = ab_re
    abi_ref[...] = ab_im
    b_re = bre_ref[...]
    b_im = bim_ref[...]
    bbr_ref[...] = c_re[:, None, :] * b_re - c_im[:, None, :] * b_im
    bbi_ref[...] = c_re[:, None, :] * b_im + c_im[:, None, :] * b_re


def _ssm_prep(a_re, a_im, log_dt, bt_re, bt_im):
    g, p = a_re.shape
    h = bt_re.shape[1]
    return pl.pallas_call(
        _ssm_prep_kernel,
        out_shape=(jax.ShapeDtypeStruct((g, p), F32), jax.ShapeDtypeStruct((g, p), F32),
                   jax.ShapeDtypeStruct((g, h, p), F32), jax.ShapeDtypeStruct((g, h, p), F32)),
        name="ssm_prep",
    )(a_re, a_im, log_dt.reshape(g, 1), bt_re, bt_im)


def _ssm_kernel(u_ref, h0_ref, are_ref, aim_ref, bre_ref, bim_ref, cre_ref, cim_ref, d_ref, wg_ref, bg_ref,
                gn_ref, o_ref, hlast_ref, hst_ref, bu_ref, y_ref, *, nb, tl, n_gt, width):
    sw = STATE_TILE

    @pl.when(pl.program_id(0) == 0)
    def _():
        hst_ref[...] = h0_ref[...]

    for gt in range(n_gt):
        cols = slice(gt * LANES, (gt + 1) * LANES)
        uf = u_ref[:, cols]
        ub = uf.astype(BF16)
        bu_ref[:, :sw] = jnp.dot(ub, bre_ref[gt], preferred_element_type=F32)
        bu_ref[:, sw:] = jnp.dot(ub, bim_ref[gt], preferred_element_type=F32)
        a_re = jnp.broadcast_to(are_ref[gt], (nb, sw))
        a_im = jnp.broadcast_to(aim_ref[gt], (nb, sw))

        def step(t, carry, a_re=a_re, a_im=a_im):
            h_re, h_im = carry
            r0 = pl.multiple_of(t * nb, nb)
            n_re = a_re * h_re - a_im * h_im + bu_ref[pl.ds(r0, nb), :sw]
            n_im = a_re * h_im + a_im * h_re + bu_ref[pl.ds(r0, nb), sw:]
            bu_ref[pl.ds(r0, nb), :sw] = n_re
            bu_ref[pl.ds(r0, nb), sw:] = n_im
            return n_re, n_im

        h_re, h_im = lax.fori_loop(0, tl, step, (hst_ref[gt, :, :sw], hst_ref[gt, :, sw:]), unroll=True)
        hst_ref[gt, :, :sw] = h_re
        hst_ref[gt, :, sw:] = h_im

        y = (jnp.dot(bu_ref[:, :sw].astype(BF16), cre_ref[gt], preferred_element_type=F32)
             - jnp.dot(bu_ref[:, sw:].astype(BF16), cim_ref[gt], preferred_element_type=F32)
             + d_ref[:, cols] * uf)
        y = _gelu(y)
        z = jnp.dot(y.astype(BF16), wg_ref[gt], preferred_element_type=F32) + bg_ref[:, cols]
        y_ref[:, cols] = y * jax.nn.sigmoid(z)

    yy = y_ref[...]
    ms = jnp.sum(yy * yy, axis=-1, keepdims=True) / width
    o_ref[...] = (yy * lax.rsqrt(ms + RMS_EPS) * gn_ref[...]).astype(BF16)
    hlast_ref[...] = hst_ref[...]


def _ssm(u_tm, h0, sp, nb, seq, width):
    n_gt = width // LANES
    tl = _tile(seq, 64)
    rows = tl * nb
    sw = STATE_TILE
    return pl.pallas_call(
        functools.partial(_ssm_kernel, nb=nb, tl=tl, n_gt=n_gt, width=width),
        out_shape=(jax.ShapeDtypeStruct((seq * nb, width), BF16),
                   jax.ShapeDtypeStruct((n_gt, nb, 2 * sw), F32)),
        grid=(seq // tl,),
        in_specs=[pl.BlockSpec((rows, width), lambda s: (s, 0)),
                  _full((n_gt, nb, 2 * sw)),
                  _full((n_gt, 1, sw)), _full((n_gt, 1, sw)),
                  _full((n_gt, LANES, sw)), _full((n_gt, LANES, sw)),
                  _full((n_gt, sw, LANES)), _full((n_gt, sw, LANES)),
                  _full((1, width)),
                  _full((n_gt, LANES, LANES)),
                  _full((1, width)), _full((1, width))],
        out_specs=(pl.BlockSpec((rows, width), lambda s: (s, 0)),
                   _full((n_gt, nb, 2 * sw))),
        scratch_shapes=[pltpu.VMEM((n_gt, nb, 2 * sw), F32),
                        pltpu.VMEM((rows, 2 * sw), F32),
                        pltpu.VMEM((rows, width), F32)],
        compiler_params=_params("arbitrary"),
        name="ssm",
    )(u_tm, h0, sp["a_re"], sp["a_im"], sp["b_re"], sp["b_im"], sp["c_re"], sp["c_im"], sp["d"],
      sp["wg"], sp["bg"], sp["gn"])


def _block_diag(m, n_gt):
    _, r, c = m.shape
    m4 = m.reshape(n_gt, GROUPS_PER_TILE, r, 1, c)
    eye = jnp.eye(GROUPS_PER_TILE, dtype=bool)[None, :, None, :, None]
    return jnp.where(eye, m4, 0).reshape(n_gt, GROUPS_PER_TILE * r, GROUPS_PER_TILE * c)


def _ssm_params(a_re, a_im, log_dt, b_re, b_im, c_re, c_im, d, glu_w, glu_b, gain, width):
    n_gt = width // LANES
    ab_re, ab_im, bb_re, bb_im = _ssm_prep(a_re, a_im, log_dt, jnp.swapaxes(b_re, 1, 2),
                                           jnp.swapaxes(b_im, 1, 2))
    return dict(
        a_re=ab_re.reshape(n_gt, 1, STATE_TILE), a_im=ab_im.reshape(n_gt, 1, STATE_TILE),
        b_re=_block_diag(bb_re, n_gt).astype(BF16), b_im=_block_diag(bb_im, n_gt).astype(BF16),
        c_re=_block_diag(jnp.swapaxes(c_re, 1, 2), n_gt).astype(BF16),
        c_im=_block_diag(jnp.swapaxes(c_im, 1, 2), n_gt).astype(BF16),
        d=d.reshape(1, width), wg=_block_diag(glu_w, n_gt).astype(BF16), bg=glu_b.reshape(1, width),
        gn=gain.reshape(1, width))


def _state_to_tiles(s_re, s_im, n_gt):
    nb = s_re.shape[0]
    def one(s):
        return jnp.swapaxes(s.reshape(nb, n_gt, STATE_TILE), 0, 1)
    return jnp.concatenate([one(s_re), one(s_im)], axis=-1)


def _tiles_to_state(h, n_groups):
    n_gt, nb, _ = h.shape
    def one(s):
        return jnp.swapaxes(s, 0, 1).reshape(nb, n_groups, SSM_STATE)
    return one(h[:, :, :STATE_TILE]), one(h[:, :, STATE_TILE:])


def _layer_norm(z, g, b):
    mu = jnp.mean(z, axis=-1, keepdims=True)
    zc = z - mu
    var = jnp.mean(zc * zc, axis=-1, keepdims=True)
    return zc * lax.rsqrt(var + LN_EPS) * g + b


def _mix_kernel(a_ref, s_ref, x_ref, wo_ref, g_ref, b_ref, x1_ref, x1b_ref, x1t_ref=None, *, alpha, width):
    tn_dims = (((0,), (0,)), ((), ()))
    tm = x_ref.shape[0]
    n_split = 2 if tm % (2 * LANES) == 0 else 1
    for r0 in range(0, tm, tm // n_split):
        rows = slice(r0, r0 + tm // n_split)
        mix = (lax.dot_general(a_ref[:, rows], wo_ref[:width], tn_dims, preferred_element_type=F32)
               + jnp.dot(s_ref[rows, :], wo_ref[width:], preferred_element_type=F32))
        x1 = _layer_norm(alpha * x_ref[rows, :] + mix, g_ref[...], b_ref[...])
        x1_ref[rows, :] = x1
        x1b_ref[rows, :] = x1.astype(BF16)
        if x1t_ref is not None:
            x1t_ref[:, rows] = x1.T.astype(BF16)


def _mix(att_t, ssm_tm, x2d, w_out_b, g, b, alpha, width):
    t, d = x2d.shape
    nb, _, seq = att_t.shape
    tm = _tile(seq, 512)
    per_seq = seq // tm
    row = lambda i: (i, 0)
    out_shape = [jax.ShapeDtypeStruct((t, d), F32), jax.ShapeDtypeStruct((t, d), BF16)]
    out_specs = [pl.BlockSpec((tm, d), row), pl.BlockSpec((tm, d), row)]
    if tm % LANES == 0:
        out_shape.append(jax.ShapeDtypeStruct((d, t), BF16))
        out_specs.append(pl.BlockSpec((d, tm), lambda i: (0, i)))
    outs = pl.pallas_call(
        functools.partial(_mix_kernel, alpha=alpha, width=width),
        out_shape=tuple(out_shape),
        grid=(t // tm,),
        in_specs=[pl.BlockSpec((None, width, tm), lambda i: (i // per_seq, 0, i % per_seq)),
                  pl.BlockSpec((tm, width), lambda i: (i % per_seq, i // per_seq)),
                  pl.BlockSpec((tm, d), row), _full((2 * width, d)), _full((1, d)), _full((1, d))],
        out_specs=tuple(out_specs),
        compiler_params=_params("parallel"),
        name="mix",
    )(att_t, ssm_tm, x2d, w_out_b, g.reshape(1, d), b.reshape(1, d))
    x1, x1b = outs[0], outs[1]
    return x1, x1b, (outs[2] if len(outs) == 3 else x1b.T)


def _sort16(v):
    v = list(v)
    for i, j in SORT16:
        v[i], v[j] = jnp.maximum(v[i], v[j]), jnp.minimum(v[i], v[j])
    return v


def _merge_top16(a, b):
    v = [jnp.maximum(a[j], b[PEER_TOPK - 1 - j]) for j in range(PEER_TOPK)]
    for dist in (8, 4, 2, 1):
        for i in range(PEER_TOPK):
            if not i & dist:
                v[i], v[i + dist] = jnp.maximum(v[i], v[i + dist]), jnp.minimum(v[i], v[i + dist])
    return v


def _fold_sublanes(v):
    for shift in (4, 2, 1):
        v = _merge_top16(v, [pltpu.roll(x, shift, 0) for x in v])
    return v


def _count_prefix(v, pred):
    assert len(v) == PEER_TOPK == 16
    full = pred(v[15])
    b3 = pred(v[7])
    b2 = pred(jnp.where(b3, v[11], v[3]))
    b1 = pred(jnp.where(b3, jnp.where(b2, v[13], v[9]), jnp.where(b2, v[5], v[1])))
    quad = [jnp.where(b1, v[4 * k + 2], v[4 * k]) for k in range(4)]
    b0 = pred(jnp.where(b3, jnp.where(b2, quad[3], quad[2]), jnp.where(b2, quad[1], quad[0])))
    n = (jnp.where(b3, 8.0, 0.0) + jnp.where(b2, 4.0, 0.0)) + (jnp.where(b1, 2.0, 0.0) + jnp.where(b0, 1.0, 0.0))
    return jnp.where(full, 16.0, n)


def _route_kernel(x_ref, wq_ref, keys_ref, lim_ref, f_ref, r2_ref, e2_ref, *, tt, tc):
    half = tt // 2
    qp = jnp.concatenate(
        [jnp.dot(x_ref[r0:r0 + half, :], wq_ref[...], preferred_element_type=F32) for r0 in (0, half)],
        axis=0).astype(BF16)
    hh = 0
    nt_dims = (((1,), (1,)), ((), ()))
    s_t = [lax.dot_general(keys_ref[hh, c], qp[:, c * PEER_DHALF:(c + 1) * PEER_DHALF], nt_dims,
                           preferred_element_type=F32) for c in range(2)]
    n_blocks = N_KEYS // SUBLANES
    for l0 in range(0, tt, tc):
        cols = slice(l0, l0 + tc)
        sub = lax.broadcasted_iota(jnp.int32, (SUBLANES, tc), 0)
        blocks = [[s_t[c][SUBLANES * j:SUBLANES * (j + 1), cols] for j in range(n_blocks)] for c in range(2)]
        v1, v2 = (_fold_sublanes(_sort16(blocks[c])) for c in range(2))
        v1_lo, v1_hi = v1[0], v1[SUBLANES]
        for r in range(1, SUBLANES):
            v1_lo = jnp.where(sub == r, v1[r], v1_lo)
            v1_hi = jnp.where(sub == r, v1[SUBLANES + r], v1_hi)
        cand_lo = [v1_lo + v2[b] for b in range(PEER_TOPK)]
        cand_hi = [v1_hi + v2[b] for b in range(PEER_TOPK)]
        top_s = _fold_sublanes(_merge_top16(cand_lo, cand_hi))
        tau = top_s[PEER_TOPK - 1]
        z = jnp.ones((SUBLANES, tc), F32)
        for k in range(1, PEER_TOPK):
            z = z + jnp.exp(top_s[k] - top_s[0])
        zinv = 1.0 / z
        r2, e2 = [], []
        for j in range(n_blocks):
            s1, s2 = blocks[0][j], blocks[1][j]
            cnt = _count_prefix(v2, lambda p, s1=s1: s1 + p >= tau)
            rank = _count_prefix(v2, lambda p, s2=s2: p > s2)
            lim_j = jnp.where(s1 >= v1[PEER_TOPK - 1], cnt, 0.0)
            f_j = jnp.exp(s1 - v1[0]) * zinv
            for g in range(SUBLANES // KEYS_PER_BLOCK):
                rows = slice(g * KEYS_PER_BLOCK, (g + 1) * KEYS_PER_BLOCK)
                lim_ref[hh, j * (SUBLANES // KEYS_PER_BLOCK) + g, :, cols] = lim_j[rows]
                f_ref[hh, j * (SUBLANES // KEYS_PER_BLOCK) + g, :, cols] = f_j[rows]
            r2.append(rank)
            e2.append(jnp.exp(s2 - v2[0]))
        r2_ref[hh, :, cols] = jnp.concatenate(r2, axis=0).astype(BF16)
        e2_ref[hh, :, cols] = jnp.concatenate(e2, axis=0).astype(BF16)


def _route(x1b, wq_b, keys_b, tt):
    t, d = x1b.shape
    dk = 2 * PEER_DHALF
    tc = _tile(tt, 256)
    hps = 1
    big_spec = pl.BlockSpec((hps, N_KEYS, tt), lambda i, h: (h, 0, i))
    n_eb = N_KEYS // KEYS_PER_BLOCK
    row_shape = jax.ShapeDtypeStruct((PEER_HEADS, n_eb, KEYS_PER_BLOCK, t), F32)
    row_spec = pl.BlockSpec((hps, n_eb, KEYS_PER_BLOCK, tt), lambda i, h: (h, 0, 0, i))
    return pl.pallas_call(
        functools.partial(_route_kernel, tt=tt, tc=tc),
        out_shape=(row_shape, row_shape,
                   jax.ShapeDtypeStruct((PEER_HEADS, N_KEYS, t), BF16),
                   jax.ShapeDtypeStruct((PEER_HEADS, N_KEYS, t), BF16)),
        grid=(t // tt, PEER_HEADS // hps),
        in_specs=[pl.BlockSpec((tt, d), lambda i, h: (i, 0)),
                  pl.BlockSpec((d, hps * dk), lambda i, h: (0, h)),
                  pl.BlockSpec((hps, 2, N_KEYS, PEER_DHALF), lambda i, h: (h, 0, 0, 0))],
        out_specs=(row_spec, row_spec, big_spec, big_spec),
        compiler_params=_params("parallel", "parallel"),
        name="route",
    )(x1b, wq_b, keys_b)


def _peer_kernel(xt_ref, u_ref, vt_ref, lim_ref, f_ref, r2_ref, e2_ref, y_ref, a_ref, hid_ref, bc_ref, w_ref,
                 *, n_blocks, tt):
    s = pl.program_id(0)
    slot = lax.rem(s, 2)
    rc = 2 * SUBLANES

    @pl.when(s == 0)
    def _():
        hid_ref[1] = jnp.zeros(hid_ref.shape[1:], BF16)

    @pl.when(lax.rem(jnp.maximum(s - 1, 0), n_blocks) == 0)
    def _():
        y_ref[...] = jnp.zeros_like(y_ref)

    a_ref[...] = jnp.dot(u_ref[...], xt_ref[...], preferred_element_type=F32)
    y_ref[...] += jnp.dot(vt_ref[...], hid_ref[1 - slot], preferred_element_type=F32)
    zero = jnp.zeros((rc, tt), BF16)
    for q in range(KEYS_PER_BLOCK):
        for h in range(PEER_HEADS):
            bc_ref[0, q, h] = jnp.broadcast_to(lim_ref[h, q:q + 1, :], (rc, tt)).astype(BF16)
            bc_ref[1, q, h] = jnp.broadcast_to(f_ref[h, q:q + 1, :], (rc, tt)).astype(BF16)
    never = r2_ref[0, 0:rc, :] < -1.0
    dep = zero
    for q in range(KEYS_PER_BLOCK):
        for k in range(N_KEYS // rc):
            rows = slice(k * rc, (k + 1) * rc)
            w = dep
            for h in range(PEER_HEADS):
                lim_t = bc_ref[0, q, h]
                if h % CHAIN_EVERY == 0:
                    lim_t = lim_t + dep
                w = w + jnp.where(r2_ref[h, rows, :] < lim_t, e2_ref[h, rows, :], zero) * bc_ref[1, q, h]
                if h % CHAIN_EVERY == CHAIN_EVERY - 1:
                    dep = jnp.where(never, w, zero)
            w_ref[q * N_KEYS + k * rc:q * N_KEYS + (k + 1) * rc, :] = w
    hid_ref[slot] = _gelu(a_ref[...]).astype(BF16) * w_ref[...]


def _peer(x1t, u_b, vt_b, lim, f, r2, e2, tt):
    d, t = x1t.shape
    eb = EXPERT_BLOCK
    n_blocks = u_b.shape[0] // eb
    n_tiles = t // tt
    tile = lambda s: jnp.minimum(s // n_blocks, n_tiles - 1)
    prev = lambda s: jnp.maximum(s - 1, 0)
    row_spec = pl.BlockSpec((PEER_HEADS, None, KEYS_PER_BLOCK, tt), lambda s: (0, s % n_blocks, 0, tile(s)))
    big_spec = pl.BlockSpec((PEER_HEADS, N_KEYS, tt), lambda s: (0, 0, tile(s)))
    return pl.pallas_call(
        functools.partial(_peer_kernel, n_blocks=n_blocks, tt=tt),
        out_shape=jax.ShapeDtypeStruct((d, t), F32),
        grid=(n_tiles * n_blocks + 1,),
        in_specs=[pl.BlockSpec((d, tt), lambda s: (0, tile(s))),
                  pl.BlockSpec((eb, d), lambda s: (s % n_blocks, 0)),
                  pl.BlockSpec((None, d, eb), lambda s: (prev(s) % n_blocks, 0, 0)),
                  row_spec, row_spec, big_spec, big_spec],
        out_specs=pl.BlockSpec((d, tt), lambda s: (0, prev(s) // n_blocks)),
        scratch_shapes=[pltpu.VMEM((eb, tt), F32), pltpu.VMEM((2, eb, tt), BF16),
                        pltpu.VMEM((2, KEYS_PER_BLOCK, PEER_HEADS, 2 * SUBLANES, tt), BF16),
                        pltpu.VMEM((eb, tt), BF16)],
        compiler_params=_params("arbitrary"),
        name="peer",
    )(x1t, u_b, vt_b, lim, f, r2, e2)


def _ln2_kernel(x1_ref, yt_ref, g_ref, b_ref, o_ref, *, alpha):
    o_ref[...] = _layer_norm(alpha * x1_ref[...] + yt_ref[...].T, g_ref[...], b_ref[...])


def _ln2(x1, y_t, g, b, alpha):
    t, d = x1.shape
    tm = _tile(t, 512)
    row = lambda i: (i, 0)
    return pl.pallas_call(
        functools.partial(_ln2_kernel, alpha=alpha),
        out_shape=jax.ShapeDtypeStruct((t, d), F32),
        grid=(t // tm,),
        in_specs=[pl.BlockSpec((tm, d), row), pl.BlockSpec((d, tm), lambda i: (0, i)), _full((1, d)),
                  _full((1, d))],
        out_specs=pl.BlockSpec((tm, d), row),
        compiler_params=_params("parallel"),
        name="ln2",
    )(x1, y_t, g.reshape(1, d), b.reshape(1, d))


def _rel_bias_table(rel_bias):
    n_left = ATT_WINDOW - REL_CLIP + CHUNK - 1
    n_right = CHUNK - 1 - REL_CLIP
    ext = jnp.concatenate([jnp.repeat(rel_bias[:, :1], n_left, axis=1), rel_bias,
                           jnp.repeat(rel_bias[:, -1:], max(n_right, 0), axis=1)], axis=1)
    return jnp.stack([ext[:, CHUNK - 1 - i:CHUNK - 1 - i + BAND] for i in range(CHUNK)], axis=1)


def _step_bias(table, cpb):
    n_heads = table.shape[0]
    table = (table - table[:, :1, :1]) * LOG2E
    tab_t = jnp.swapaxes(table, 1, 2)
    per_chunk = [jnp.pad(tab_t, ((0, 0), (c * CHUNK, (cpb - 1 - c) * CHUNK), (0, 0)),
                         constant_values=MASK_VALUE) for c in range(cpb)]
    both = jnp.stack(per_chunk, axis=2)
    rows = both.shape[1]
    both = both.reshape(n_heads // 2, 2, rows, cpb * CHUNK)
    return jnp.swapaxes(both, 1, 2).reshape(n_heads // 2, rows, 2 * cpb * CHUNK)


def _encoder_layer(x, hist_k, hist_v, h0, wts, alpha):
    nb, seq, d = x.shape
    width = d // 2
    t = nb * seq
    n_gt = width // LANES
    assert seq % CHUNK == 0 and width % LANES == 0 and nb % SUBLANES == 0
    assert hist_k is None or seq == CHUNK
    x2d = x.reshape(t, d)

    n_hist = 0 if hist_k is None else hist_k.shape[1]
    n_invalid = ATT_WINDOW - n_hist
    def history(hist, n_new):
        if hist is None:
            return jnp.zeros((nb, ATT_WINDOW + n_new, width), BF16)
        return jnp.pad(hist.reshape(nb, n_hist, width).astype(BF16), ((0, 0), (n_invalid, n_new), (0, 0)))
    if _tile(seq, 256) % LANES == 0:
        q_b, kpad, vtpad, k_new, v_new, u_tm = _proj(
            x2d, wts["w_in"], history(hist_k, seq), jnp.swapaxes(history(hist_v, seq), 1, 2), nb, seq, width)
    else:
        q_b, k_b, v_b, k_new, v_new, u_tm = _proj_short(x2d, wts["w_in"], nb, seq, width)
        kpad = jnp.concatenate([history(hist_k, 0), k_b.reshape(nb, seq, width)], axis=1)
        vtpad = jnp.swapaxes(jnp.concatenate([history(hist_v, 0), v_b.reshape(nb, seq, width)], axis=1), 1, 2)
    n_chunks = seq // CHUNK
    cpb = 2 if n_chunks % 2 == 0 else 1
    assert cpb == 2 or n_chunks == 1
    att_t = _attention(q_b, kpad, vtpad, _step_bias(wts["bias"], cpb), wts["norm_attn_g"], nb, seq, width,
                       cpb, n_invalid)

    ssm_tm, h_last = _ssm(u_tm.reshape(t, width), h0, wts["ssm"], nb, seq, width)

    x1, x1b, x1t = _mix(att_t, ssm_tm.reshape(seq, nb * width), x2d, wts["w_out"], wts["ln1_g"],
                        wts["ln1_b"], alpha, width)

    lim, f, r2, e2 = _route(x1b, wts["peer_wq"], wts["peer_keys"], _tile(t, ROUTE_TOKENS))
    y_t = _peer(x1t, wts["peer_u"], wts["peer_vt"], lim, f, r2, e2, _tile(t, PEER_TOKENS))
    x2 = _ln2(x1, y_t, wts["ln2_g"], wts["ln2_b"], alpha)

    n_heads = width // HEAD_DIM
    return (x2.reshape(nb, seq, d), k_new.reshape(nb, -1, n_heads, HEAD_DIM),
            v_new.reshape(nb, -1, n_heads, HEAD_DIM), h_last)


def kernel(x_prompt, x_sample, cache_attn_k, cache_attn_v, state_ssm_re, state_ssm_im, w_in, rel_bias, norm_attn_g, ssm_a_re, ssm_a_im, ssm_log_dt, ssm_b_re, ssm_b_im, ssm_c_re, ssm_c_im, ssm_d, ssm_glu_w, ssm_glu_b, norm_ssm_g, w_out, ln1_g, ln1_b, peer_wq, peer_keys, peer_u, peer_v, ln2_g, ln2_b):
    depth = w_in.shape[0]
    alpha = (2.0 * depth) ** 0.25
    d = x_prompt.shape[-1]
    width = d // 2
    n_gt = width // LANES
    n_groups = width // SSM_GROUP
    keep = min(ATT_WINDOW, x_prompt.shape[1])

    yp, ys = x_prompt, x_sample
    outs = [[] for _ in range(8)]
    for l in range(depth):
        wts = dict(
            w_in=w_in[l].astype(BF16),
            bias=_rel_bias_table(rel_bias[l]),
            norm_attn_g=jnp.broadcast_to(norm_attn_g[l][:, None], (width, LANES)),
            ssm=_ssm_params(ssm_a_re[l], ssm_a_im[l], ssm_log_dt[l], ssm_b_re[l], ssm_b_im[l], ssm_c_re[l],
                            ssm_c_im[l], ssm_d[l], ssm_glu_w[l], ssm_glu_b[l], norm_ssm_g[l], width),
            w_out=w_out[l].astype(BF16), ln1_g=ln1_g[l], ln1_b=ln1_b[l],
            peer_wq=peer_wq[l].astype(BF16), peer_keys=peer_keys[l].astype(BF16),
            peer_u=peer_u[l].astype(BF16),
            peer_vt=jnp.swapaxes(peer_v[l].astype(BF16).reshape(-1, EXPERT_BLOCK, d), 1, 2),
            ln2_g=ln2_g[l], ln2_b=ln2_b[l])
        h0p = jnp.zeros((n_gt, x_prompt.shape[0], 2 * STATE_TILE), F32)
        yp, kp, vp, hp = _encoder_layer(yp, None, None, h0p, wts, alpha)
        h0s = _state_to_tiles(state_ssm_re[l].astype(F32), state_ssm_im[l].astype(F32), n_gt)
        ys, kn, vn, hs = _encoder_layer(ys, cache_attn_k[l], cache_attn_v[l], h0s, wts, alpha)
        hp_re, hp_im = _tiles_to_state(hp, n_groups)
        hs_re, hs_im = _tiles_to_state(hs, n_groups)
        assert kp.shape[1] == keep
        for acc, val in zip(outs, (kp, vp, hp_re, hp_im, kn, vn, hs_re, hs_im)):
            acc.append(val)
    return (yp, ys) + tuple(jnp.stack(o) for o in outs)
```
